```python
import math
import jax, jax.numpy as jnp
from jax import lax
import numpy as np

D_MODEL = 1024
BATCH = 2
SEQ = 8192
DEPTH = 1
DEC_BATCH = 8
DEC_SEQ = 32
PAST_LEN = 1024

CHUNK = 64
Q_BLOCK = 128
T_BLOCK = 256
N_MEM = 256
EPS = 1e-6
NEG_INF = -1e30
MLA_HEADS = 8
MLA_Q_LORA = 384
MLA_KV_LORA = 256
MLA_NOPE = 64
MLA_ROPE = 32
MLA_V = 64
ROPE_THETA = 10000.0
MLA_SCALE = (MLA_NOPE + MLA_ROPE) ** -0.5
DIFF_HEADS = 8
DIFF_DH = 32
DIFF_V = 2 * DIFF_DH
DIFF_SCALE = DIFF_DH ** -0.5
MEM_HEADS = 4
MEM_DH = 128
MEM_SCALE = MEM_DH ** -0.5
PEER_HEADS = 8
PEER_NKEYS = 128
PEER_EXPERTS = PEER_NKEYS * PEER_NKEYS
PEER_DKEY = 256
PEER_HALF = PEER_DKEY // 2
PEER_TOPK = 16
BR_A = MLA_HEADS * MLA_V
BR_B = DIFF_HEADS * DIFF_V
BR_M = MEM_HEADS * MEM_DH
IN_SIZES = (MLA_Q_LORA, MLA_KV_LORA, MLA_ROPE, DIFF_HEADS * 2 * DIFF_DH, DIFF_HEADS * 2 * DIFF_DH, BR_B, BR_M)
IN_WIDTH = MLA_Q_LORA + MLA_KV_LORA + MLA_ROPE + 3 * DIFF_HEADS * 2 * DIFF_DH + BR_M
IN_OFFSETS = [MLA_Q_LORA,
              MLA_Q_LORA + MLA_KV_LORA,
              MLA_Q_LORA + MLA_KV_LORA + MLA_ROPE,
              MLA_Q_LORA + MLA_KV_LORA + MLA_ROPE + DIFF_HEADS * 2 * DIFF_DH,
              MLA_Q_LORA + MLA_KV_LORA + MLA_ROPE + 2 * DIFF_HEADS * 2 * DIFF_DH,
              MLA_Q_LORA + MLA_KV_LORA + MLA_ROPE + 3 * DIFF_HEADS * 2 * DIFF_DH]

kernel_name = "streaming_hybrid_mla_diffattn_peer_step"


def _rmsnorm(x, g):
    xf = x.astype(jnp.float32)
    y = xf * lax.rsqrt(jnp.mean(xf * xf, axis=-1, keepdims=True) + EPS)
    return (y * g.astype(jnp.float32)).astype(x.dtype)


def _rope(x, pos):
    half = MLA_ROPE // 2
    inv = 1.0 / (ROPE_THETA ** (jnp.arange(half, dtype=jnp.float32) / half))
    ang = pos.astype(jnp.float32)[:, None] * inv[None, :]
    shp = (1, ang.shape[0]) + (1,) * (x.ndim - 3) + (half,)
    cos = jnp.cos(ang).reshape(shp)
    sin = jnp.sin(ang).reshape(shp)
    x1 = x[..., :half].astype(jnp.float32)
    x2 = x[..., half:].astype(jnp.float32)
    return jnp.concatenate([x1 * cos - x2 * sin, x2 * cos + x1 * sin], axis=-1).astype(x.dtype)


def _chunk_mask(q_pos, k_pos):
    return (k_pos[None, :] // CHUNK) <= (q_pos[:, None] // CHUNK)


def _sweep_queries(fn, q_args, q_pos):
    s = q_pos.shape[0]
    if s <= Q_BLOCK or s % Q_BLOCK:
        return fn(*q_args, q_pos)
    nb = s // Q_BLOCK

    def split(a):
        return jnp.moveaxis(a.reshape((a.shape[0], nb, Q_BLOCK) + a.shape[2:]), 1, 0)

    blocks = tuple(split(a) for a in q_args) + (q_pos.reshape(nb, Q_BLOCK),)
    out = lax.map(lambda a: fn(*a), blocks)
    out = jnp.moveaxis(out, 0, 1)
    return out.reshape((out.shape[0], s) + out.shape[3:])


def _mla_attend(q_nope, q_rope, q_pos, k_nope, k_rope, v, k_pos):
    def blk(qn, qr, qp):
        s = (jnp.einsum("bqhd,bkhd->bhqk", qn, k_nope)
             + jnp.einsum("bqhd,bkd->bhqk", qr, k_rope)).astype(jnp.float32) * MLA_SCALE
        s = jnp.where(_chunk_mask(qp, k_pos)[None, None], s, NEG_INF)
        p = jax.nn.softmax(s, axis=-1).astype(v.dtype)
        return jnp.einsum("bhqk,bkhd->bqhd", p, v)
    return _sweep_queries(blk, (q_nope, q_rope), q_pos)


def _diff_attend(q, q_pos, k, v, k_pos, lam):
    slopes = jnp.exp2(-8.0 * jnp.arange(1, DIFF_HEADS + 1, dtype=jnp.float32) / DIFF_HEADS)

    def blk(qb, qp):
        s = jnp.einsum("bqhjd,bkhjd->bhjqk", qb, k).astype(jnp.float32) * DIFF_SCALE
        dist = jnp.abs(qp[:, None] - k_pos[None, :]).astype(jnp.float32)
        s = s - (slopes[:, None, None] * dist[None])[None, :, None]
        s = jnp.where(_chunk_mask(qp, k_pos)[None, None, None], s, NEG_INF)
        p = jax.nn.softmax(s, axis=-1)
        a = (p[:, :, 0] - lam * p[:, :, 1]).astype(v.dtype)
        return jnp.einsum("bhqk,bkhd->bqhd", a, v)
    return _sweep_queries(blk, (q,), q_pos)


def _mem_kv(mem, g, w):
    b, m, _ = mem.shape
    kv = jnp.einsum("bmd,de->bme", _rmsnorm(mem, g), w)
    k, v = jnp.split(kv, 2, axis=-1)
    return k.reshape(b, m, MEM_HEADS, MEM_DH), v.reshape(b, m, MEM_HEADS, MEM_DH)


def _mem_attend(q, mem_k, mem_v):
    s = jnp.einsum("bqhd,bmhd->bhqm", q, mem_k).astype(jnp.float32) * MEM_SCALE
    p = jax.nn.softmax(s, axis=-1).astype(mem_v.dtype)
    return jnp.einsum("bhqm,bmhd->bqhd", p, mem_v)


def _peer(h, wq, subkeys, u, v):
    b, s, d = h.shape
    n = b * s
    n_pad = -(-n // T_BLOCK) * T_BLOCK
    t = jnp.pad(h.reshape(n, d), ((0, n_pad - n), (0, 0)))

    def blk(tb):
        q = jnp.einsum("td,de->te", tb, wq).reshape(-1, PEER_HEADS, 2, PEER_HALF)
        sc = jnp.einsum("thjd,hjnd->thjn", q, subkeys).astype(jnp.float32)
        s1, i1 = lax.top_k(sc[:, :, 0], PEER_TOPK)
        s2, i2 = lax.top_k(sc[:, :, 1], PEER_TOPK)
        cand = (s1[..., :, None] + s2[..., None, :]).reshape(-1, PEER_HEADS, PEER_TOPK * PEER_TOPK)
        cidx = (i1[..., :, None] * PEER_NKEYS + i2[..., None, :]).reshape(-1, PEER_HEADS, PEER_TOPK * PEER_TOPK)
        best, sel = lax.top_k(cand, PEER_TOPK)
        idx = jnp.take_along_axis(cidx, sel, axis=-1)
        g = jax.nn.softmax(best, axis=-1).astype(tb.dtype)
        ue = jnp.take(u, idx, axis=0)
        ve = jnp.take(v, idx, axis=0)
        a = jax.nn.gelu(jnp.einsum("thkd,td->thk", ue, tb), approximate=False)
        return jnp.einsum("thk,thkd->td", g * a, ve)

    if n_pad == T_BLOCK:
        out = blk(t)
    else:
        out = lax.map(blk, t.reshape(-1, T_BLOCK, d)).reshape(n_pad, d)
    return out[:n].reshape(b, s, d)


def _layer(x, pos, past, mem_k, mem_v, p, layer):
    b, s, _ = x.shape
    h = _rmsnorm(x, p["norm_mix"])
    z = jnp.einsum("bsd,de->bse", h, p["w_in"])
    c_q, c_kv, k_r, d_q, d_k, d_v, m_q = jnp.split(z, IN_OFFSETS, axis=-1)

    q = jnp.einsum("bsr,re->bse", _rmsnorm(c_q, p["mla_q_norm"]), p["w_uq"])
    q = q.reshape(b, s, MLA_HEADS, MLA_NOPE + MLA_ROPE)
    q_nope = q[..., :MLA_NOPE]
    q_rope = _rope(q[..., MLA_NOPE:], pos)
    c_kv = _rmsnorm(c_kv, p["mla_kv_norm"])
    k_r = _rope(k_r, pos)
    d_k = d_k.reshape(b, s, DIFF_HEADS, DIFF_V)
    d_v = d_v.reshape(b, s, DIFF_HEADS, DIFF_V)
    new_rows = (c_kv, k_r, d_k, d_v)
    if past is None:
        ckv_all, kr_all, dk_all, dv_all = new_rows
        k_pos = pos
    else:
        ckv_all, kr_all, dk_all, dv_all = [jnp.concatenate([c, r], axis=1) for c, r in zip(past, new_rows)]
        k_pos = jnp.arange(past[0].shape[1] + s, dtype=jnp.int32)
    kk = ckv_all.shape[1]
    k_nope = jnp.einsum("bkr,re->bke", ckv_all, p["w_uk"]).reshape(b, kk, MLA_HEADS, MLA_NOPE)
    v_mla = jnp.einsum("bkr,re->bke", ckv_all, p["w_uv"]).reshape(b, kk, MLA_HEADS, MLA_V)
    o_a = _mla_attend(q_nope, q_rope, pos, k_nope, kr_all, v_mla, k_pos).reshape(b, s, BR_A)

    lam_init = 0.8 - 0.6 * math.exp(-0.3 * layer)
    lam = (jnp.exp(jnp.sum(p["diff_lq1"].astype(jnp.float32) * p["diff_lk1"].astype(jnp.float32)))
           - jnp.exp(jnp.sum(p["diff_lq2"].astype(jnp.float32) * p["diff_lk2"].astype(jnp.float32)))
           + lam_init)
    o_b = _diff_attend(d_q.reshape(b, s, DIFF_HEADS, 2, DIFF_DH), pos,
                       dk_all.reshape(b, kk, DIFF_HEADS, 2, DIFF_DH), dv_all, k_pos, lam)
    o_b = (_rmsnorm(o_b, p["diff_subln"]) * (1.0 - lam_init)).reshape(b, s, BR_B)

    o_m = _mem_attend(m_q.reshape(b, s, MEM_HEADS, MEM_DH), mem_k, mem_v).reshape(b, s, BR_M)

    gates = jax.nn.sigmoid((jnp.einsum("bsd,de->bse", h, p["w_gate"]) + p["b_gate"]).astype(jnp.float32)).astype(x.dtype)
    g_a, g_b, g_m = jnp.split(gates, 3, axis=-1)
    merged = (g_a * jnp.einsum("bse,ed->bsd", o_a, p["w_br_a"])
              + g_b * jnp.einsum("bse,ed->bsd", o_b, p["w_br_b"])
              + g_m * jnp.einsum("bse,ed->bsd", o_m, p["w_br_m"]))
    x = x + jnp.einsum("bsd,de->bse", merged, p["w_o"])

    x = x + _peer(_rmsnorm(x, p["norm_ffn"]), p["peer_wq"], p["peer_subkeys"], p["peer_u"], p["peer_v"])
    return x, new_rows


def setup_inputs(seed: int = 0) -> dict:
    key = jax.random.key(seed)
    ks = jax.random.split(key, 40)
    f32 = jnp.float32

    def nrm(k, shape, scale=1.0):
        return jax.random.normal(k, shape, f32) * scale

    def gain(k, shape):
        return 1.0 + 0.01 * jax.random.normal(k, shape, f32)

    L = DEPTH
    return {
        "x_prompt": nrm(ks[0], (BATCH, SEQ, D_MODEL)),
        "x_sample": nrm(ks[1], (DEC_BATCH, DEC_SEQ, D_MODEL)),
        "cache_mla_ckv": nrm(ks[2], (L, DEC_BATCH, PAST_LEN, MLA_KV_LORA)),
        "cache_mla_krope": nrm(ks[3], (L, DEC_BATCH, PAST_LEN, MLA_ROPE)),
        "cache_diff_k": nrm(ks[4], (L, DEC_BATCH, PAST_LEN, DIFF_HEADS, DIFF_V)),
        "cache_diff_v": nrm(ks[5], (L, DEC_BATCH, PAST_LEN, DIFF_HEADS, DIFF_V)),
        "cache_mem_k": nrm(ks[6], (L, DEC_BATCH, N_MEM, MEM_HEADS, MEM_DH)),
        "cache_mem_v": nrm(ks[7], (L, DEC_BATCH, N_MEM, MEM_HEADS, MEM_DH)),
        "mem_prompt": nrm(ks[8], (BATCH, N_MEM, D_MODEL)),
        "norm_mix": gain(ks[9], (L, D_MODEL)),
        "w_in": nrm(ks[10], (L, D_MODEL, IN_WIDTH), D_MODEL ** -0.5),
        "mla_q_norm": gain(ks[11], (L, MLA_Q_LORA)),
        "w_uq": nrm(ks[12], (L, MLA_Q_LORA, MLA_HEADS * (MLA_NOPE + MLA_ROPE)), MLA_Q_LORA ** -0.5),
        "mla_kv_norm": gain(ks[13], (L, MLA_KV_LORA)),
        "w_uk": nrm(ks[14], (L, MLA_KV_LORA, MLA_HEADS * MLA_NOPE), MLA_KV_LORA ** -0.5),
        "w_uv": nrm(ks[15], (L, MLA_KV_LORA, MLA_HEADS * MLA_V), MLA_KV_LORA ** -0.5),
        "diff_lq1": nrm(ks[16], (L, DIFF_DH), 0.1),
        "diff_lk1": nrm(ks[17], (L, DIFF_DH), 0.1),
        "diff_lq2": nrm(ks[18], (L, DIFF_DH), 0.1),
        "diff_lk2": nrm(ks[19], (L, DIFF_DH), 0.1),
        "diff_subln": gain(ks[20], (L, DIFF_V)),
        "norm_mem": gain(ks[21], (L, D_MODEL)),
        "w_mem_kv": nrm(ks[22], (L, D_MODEL, 2 * MEM_HEADS * MEM_DH), D_MODEL ** -0.5),
        "w_br_a": nrm(ks[23], (L, BR_A, D_MODEL), BR_A ** -0.5),
        "w_br_b": nrm(ks[24], (L, BR_B, D_MODEL), BR_B ** -0.5),
        "w_br_m": nrm(ks[25], (L, BR_M, D_MODEL), BR_M ** -0.5),
        "w_gate": nrm(ks[26], (L, D_MODEL, 3 * D_MODEL), D_MODEL ** -0.5),
        "b_gate": nrm(ks[27], (L, 3 * D_MODEL), 0.01),
        "w_o": nrm(ks[28], (L, D_MODEL, D_MODEL), D_MODEL ** -0.5),
        "norm_ffn": gain(ks[29], (L, D_MODEL)),
        "peer_wq": nrm(ks[30], (L, D_MODEL, PEER_HEADS * PEER_DKEY), D_MODEL ** -0.5),
        "peer_subkeys": nrm(ks[31], (L, PEER_HEADS, 2, PEER_NKEYS, PEER_HALF), PEER_HALF ** -0.5),
        "peer_u": nrm(ks[32], (L, PEER_EXPERTS, D_MODEL), D_MODEL ** -0.5),
        "peer_v": nrm(ks[33], (L, PEER_EXPERTS, D_MODEL), 0.5),
        "norm_final": gain(ks[34], (D_MODEL,)),
    }


def reference(x_prompt, x_sample, cache_mla_ckv, cache_mla_krope, cache_diff_k, cache_diff_v,
              cache_mem_k, cache_mem_v, mem_prompt, norm_mix, w_in, mla_q_norm, w_uq, mla_kv_norm,
              w_uk, w_uv, diff_lq1, diff_lk1, diff_lq2, diff_lk2, diff_subln, norm_mem, w_mem_kv,
              w_br_a, w_br_b, w_br_m, w_gate, b_gate, w_o, norm_ffn, peer_wq, peer_subkeys,
              peer_u, peer_v, norm_final):
    pos_p = jnp.arange(x_prompt.shape[1], dtype=jnp.int32)
    past_len = cache_mla_ckv.shape[2]
    pos_s = past_len + jnp.arange(x_sample.shape[1], dtype=jnp.int32)
    xp, xs = x_prompt, x_sample
    p_ckv, p_kr, p_dk, p_dv, p_mk, p_mv = [], [], [], [], [], []
    s_ckv, s_kr, s_dk, s_dv = [], [], [], []
    for l in range(DEPTH):
        prm = {
            "norm_mix": norm_mix[l], "w_in": w_in[l], "mla_q_norm": mla_q_norm[l], "w_uq": w_uq[l],
            "mla_kv_norm": mla_kv_norm[l], "w_uk": w_uk[l], "w_uv": w_uv[l],
            "diff_lq1": diff_lq1[l], "diff_lk1": diff_lk1[l], "diff_lq2": diff_lq2[l],
            "diff_lk2": diff_lk2[l], "diff_subln": diff_subln[l],
            "w_br_a": w_br_a[l], "w_br_b": w_br_b[l], "w_br_m": w_br_m[l],
            "w_gate": w_gate[l], "b_gate": b_gate[l], "w_o": w_o[l], "norm_ffn": norm_ffn[l],
            "peer_wq": peer_wq[l], "peer_subkeys": peer_subkeys[l], "peer_u": peer_u[l], "peer_v": peer_v[l],
        }
        mk, mv = _mem_kv(mem_prompt, norm_mem[l], w_mem_kv[l])
        xp, rows_p = _layer(xp, pos_p, None, mk, mv, prm, l)
        past = (cache_mla_ckv[l], cache_mla_krope[l], cache_diff_k[l], cache_diff_v[l])
        xs, rows_s = _layer(xs, pos_s, past, cache_mem_k[l], cache_mem_v[l], prm, l)
        p_ckv.append(rows_p[0]); p_kr.append(rows_p[1]); p_dk.append(rows_p[2]); p_dv.append(rows_p[3])
        p_mk.append(mk); p_mv.append(mv)
        s_ckv.append(rows_s[0]); s_kr.append(rows_s[1]); s_dk.append(rows_s[2]); s_dv.append(rows_s[3])
    y_prompt = _rmsnorm(xp, norm_final)
    y_sample = _rmsnorm(xs, norm_final)
    return (y_prompt, y_sample,
            jnp.stack(p_ckv), jnp.stack(p_kr), jnp.stack(p_dk), jnp.stack(p_dv),
            jnp.stack(p_mk), jnp.stack(p_mv),
            jnp.stack(s_ckv), jnp.stack(s_kr), jnp.stack(s_dk), jnp.stack(s_dv))
```

```python
import functools
import math

import jax
import jax.numpy as jnp
from jax import lax
from jax.experimental import pallas as pl
from jax.experimental.pallas import tpu as pltpu

F32 = jnp.float32
BF16 = jnp.bfloat16

CHUNK = 64
CHUNK_SHIFT = 6
EPS = 1e-6
NEG = -1e30
MLA_HEADS = 8
MLA_Q_LORA = 384
MLA_KV_LORA = 256
MLA_NOPE = 64
MLA_ROPE = 32
MLA_V = 64
ROPE_THETA = 10000.0
MLA_SCALE = (MLA_NOPE + MLA_ROPE) ** -0.5
DIFF_HEADS = 8
DIFF_DH = 32
DIFF_V = 2 * DIFF_DH
DIFF_SCALE = DIFF_DH ** -0.5
MEM_HEADS = 4
MEM_DH = 128
MEM_SCALE = MEM_DH ** -0.5
PEER_HEADS = 8
PEER_NKEYS = 128
PEER_HALF = 128
PEER_TOPK = 16
LANE = 128
HEAD_W = 128

VMEM_LIMIT = 56 * 1024 * 1024

_C_CQ = 0
_C_CKV = _C_CQ + MLA_Q_LORA
_C_KR = _C_CKV + MLA_KV_LORA
_C_KRR = _C_KR + LANE
_C_DQ = _C_KRR + LANE
_C_DK = _C_DQ + DIFF_HEADS * DIFF_V
_C_DV = _C_DK + DIFF_HEADS * DIFF_V
_C_MQ = _C_DV + DIFF_HEADS * DIFF_V
_C_END = _C_MQ + MEM_HEADS * MEM_DH


def _cparams(sem):
    return pltpu.CompilerParams(dimension_semantics=sem, vmem_limit_bytes=VMEM_LIMIT)


def _rms(x, g):
    return x * lax.rsqrt(jnp.mean(x * x, axis=-1, keepdims=True) + EPS) * g


def _dot(a, b):
    return jnp.dot(a, b, preferred_element_type=F32)


def _dot_nt(a, b):
    return lax.dot_general(a, b, (((1,), (1,)), ((), ())), preferred_element_type=F32)


def _full_spec(shape):
    nd = len(shape)
    return pl.BlockSpec(shape, lambda *_: (0,) * nd)


def _proj_in_kernel(x_ref, g_ref, win_ref, qn_ref, wuq_ref, kvn_ref, cq_ref, sq_ref, ck_ref, sk_ref,
                    ckv_ref, kr_ref, dk_ref, dv_ref, qm_ref, dqb_ref, dkb_ref, dvb_ref, mqb_ref):
    h = _rms(x_ref[...], g_ref[...]).astype(BF16)
    z = _dot(h, win_ref[...])
    cqn = _rms(z[:, _C_CQ:_C_CKV], qn_ref[...]).astype(BF16)
    q2 = _dot(cqn, wuq_ref[...])
    cq = cq_ref[...]
    sq = sq_ref[...]
    nq = MLA_HEADS * HEAD_W
    for hh in range(MLA_HEADS):
        lo = hh * HEAD_W
        qm_ref[:, lo:lo + HEAD_W] = (q2[:, lo:lo + HEAD_W] * cq
                                     + q2[:, nq + lo:nq + lo + HEAD_W] * sq).astype(BF16)
    ckv_ref[...] = _rms(z[:, _C_CKV:_C_KR], kvn_ref[...])
    kr = z[:, _C_KR:_C_KRR] * ck_ref[...] + z[:, _C_KRR:_C_DQ] * sk_ref[...]
    kr_ref[...] = kr[:, :MLA_ROPE]
    dk = z[:, _C_DK:_C_DV]
    dv = z[:, _C_DV:_C_MQ]
    dk_ref[...] = dk
    dv_ref[...] = dv
    dkb_ref[...] = dk.astype(BF16)
    dvb_ref[...] = dv.astype(BF16)
    dqb_ref[...] = (z[:, _C_DQ:_C_DK] * DIFF_SCALE).astype(BF16)
    mqb_ref[...] = (z[:, _C_MQ:_C_END] * MEM_SCALE).astype(BF16)


def _proj_in(x, tabs, wp, *, tm, pos_blocks):
    n, d = x.shape
    grid = (n // tm,)
    row = lambda w: pl.BlockSpec((tm, w), lambda i: (i, 0))
    tab = pl.BlockSpec((tm, LANE), lambda i: (i % pos_blocks, 0))
    dw = DIFF_HEADS * DIFF_V
    out_shape = (
        jax.ShapeDtypeStruct((n, MLA_KV_LORA), F32),
        jax.ShapeDtypeStruct((n, MLA_ROPE), F32),
        jax.ShapeDtypeStruct((n, dw), F32),
        jax.ShapeDtypeStruct((n, dw), F32),
        jax.ShapeDtypeStruct((n, MLA_HEADS * HEAD_W), BF16),
        jax.ShapeDtypeStruct((n, dw), BF16),
        jax.ShapeDtypeStruct((n, dw), BF16),
        jax.ShapeDtypeStruct((n, dw), BF16),
        jax.ShapeDtypeStruct((n, MEM_HEADS * MEM_DH), BF16),
    )
    return pl.pallas_call(
        _proj_in_kernel,
        grid=grid,
        in_specs=[row(d), _full_spec((1, d)), _full_spec(wp["w_in"].shape), _full_spec((1, MLA_Q_LORA)),
                  _full_spec(wp["w_uq"].shape), _full_spec((1, MLA_KV_LORA)), tab, tab, tab, tab],
        out_specs=(row(MLA_KV_LORA), row(MLA_ROPE), row(dw), row(dw), row(MLA_HEADS * HEAD_W),
                   row(dw), row(dw), row(dw), row(MEM_HEADS * MEM_DH)),
        out_shape=out_shape,
        compiler_params=_cparams(("parallel",)),
        name="proj_in",
    )(x, wp["norm_mix"], wp["w_in"], wp["mla_q_norm"], wp["w_uq"], wp["mla_kv_norm"], *tabs)


def _kv_up_kernel(ckv_ref, kr_ref, wuk_ref, pk_ref, wuv_ref, k_ref, v_ref):
    c = ckv_ref[...].astype(BF16)
    k = _dot(c, wuk_ref[...]) + _dot(kr_ref[...].astype(BF16), pk_ref[...])
    k_ref[...] = k.astype(BF16)
    v_ref[...] = _dot(c, wuv_ref[...]).astype(BF16)


def _kv_up(ckv, kr, wp, *, tm):
    n = ckv.shape[0]
    row = lambda w: pl.BlockSpec((tm, w), lambda i: (i, 0))
    return pl.pallas_call(
        _kv_up_kernel,
        grid=(n // tm,),
        in_specs=[row(MLA_KV_LORA), row(MLA_ROPE), _full_spec(wp["w_uk"].shape),
                  _full_spec(wp["p_kr"].shape), _full_spec(wp["w_uv"].shape)],
        out_specs=(row(MLA_HEADS * HEAD_W), row(MLA_HEADS * MLA_V)),
        out_shape=(jax.ShapeDtypeStruct((n, MLA_HEADS * HEAD_W), BF16),
                   jax.ShapeDtypeStruct((n, MLA_HEADS * MLA_V), BF16)),
        compiler_params=_cparams(("parallel",)),
        name="kv_up",
    )(ckv, kr, wp["w_uk"], wp["p_kr"], wp["w_uv"])


def _lanes(col, n):
    if n == LANE:
        return col
    if n % LANE == 0:
        return jnp.concatenate([col] * (n // LANE), axis=1)
    return jnp.broadcast_to(col[:, :1], (col.shape[0], n))


def _online_update(s, v, m_scr, l_scr, acc_scr, idx):
    tk = s.shape[1]
    m_prev = m_scr[idx]
    m_next = jnp.maximum(m_prev, jnp.max(s, axis=1, keepdims=True))
    p = jnp.exp(s - _lanes(m_next, tk))
    alpha = jnp.exp(m_prev - m_next)
    l_scr[idx] = alpha * l_scr[idx] + jnp.sum(p, axis=1, keepdims=True)
    acc_scr[idx] = acc_scr[idx] * alpha + _dot(p.astype(BF16), v)
    m_scr[idx] = m_next


def _block_flags(qi, ki, *, tq, tk, q_off):
    q_lo = q_off + qi * tq
    q_hi = q_lo + (tq - 1)
    k_lo = ki * tk
    k_hi = k_lo + (tk - 1)
    needed = (k_lo >> CHUNK_SHIFT) <= (q_hi >> CHUNK_SHIFT)
    behind = k_hi <= q_lo
    return q_lo, k_lo, needed, behind


def _last_kblock(qi, *, tq, tk, q_off):
    q_hi = q_off + qi * tq + (tq - 1)
    last_key = ((q_hi >> CHUNK_SHIFT) << CHUNK_SHIFT) + (CHUNK - 1)
    return last_key // tk


def _mla_kernel(q_ref, k_ref, v_ref, o_ref, m_scr, l_scr, acc_scr, *, tq, tk, q_off, nk):
    qi = pl.program_id(2)
    ki = pl.program_id(3)

    @pl.when(ki == 0)
    def _():
        m_scr[...] = jnp.full(m_scr.shape, NEG, F32)
        l_scr[...] = jnp.zeros(l_scr.shape, F32)
        acc_scr[...] = jnp.zeros(acc_scr.shape, F32)

    q_lo, k_lo, needed, behind = _block_flags(qi, ki, tq=tq, tk=tk, q_off=q_off)

    def body(masked):
        if masked:
            qp = q_lo + lax.broadcasted_iota(jnp.int32, (tq, tk), 0)
            kp = k_lo + lax.broadcasted_iota(jnp.int32, (tq, tk), 1)
            vis = (kp >> CHUNK_SHIFT) <= (qp >> CHUNK_SHIFT)
        v = v_ref[0]
        for hh in range(2):
            s = _dot_nt(q_ref[0, :, hh * HEAD_W:(hh + 1) * HEAD_W], k_ref[0, :, hh * HEAD_W:(hh + 1) * HEAD_W])
            if masked:
                s = jnp.where(vis, s, NEG)
            _online_update(s, v, m_scr, l_scr, acc_scr, hh)

    pl.when(needed & behind)(lambda: body(False))
    pl.when(needed & jnp.logical_not(behind))(lambda: body(True))

    @pl.when(ki == nk - 1)
    def _():
        lane = lax.broadcasted_iota(jnp.int32, (tq, LANE), 1)
        o0 = acc_scr[0] / l_scr[0]
        o1 = acc_scr[1] / l_scr[1]
        o_ref[0] = jnp.where(lane < MLA_V, o0, o1).astype(o_ref.dtype)


def _mla_attention(q, k, v, *, tq, tk, q_off):
    b, sq, _ = q.shape
    sk = k.shape[1]
    nq, nk = sq // tq, sk // tk
    pairs = MLA_HEADS // 2
    last = functools.partial(_last_kblock, tq=tq, tk=tk, q_off=q_off)
    kern = functools.partial(_mla_kernel, tq=tq, tk=tk, q_off=q_off, nk=nk)
    return pl.pallas_call(
        kern,
        grid=(b, pairs, nq, nk),
        in_specs=[pl.BlockSpec((1, tq, 2 * HEAD_W), lambda bi, p, qi, ki: (bi, qi, p)),
                  pl.BlockSpec((1, tk, 2 * HEAD_W), lambda bi, p, qi, ki: (bi, jnp.minimum(ki, last(qi)), p)),
                  pl.BlockSpec((1, tk, 2 * MLA_V), lambda bi, p, qi, ki: (bi, jnp.minimum(ki, last(qi)), p))],
        out_specs=pl.BlockSpec((1, tq, 2 * MLA_V), lambda bi, p, qi, ki: (bi, qi, p)),
        out_shape=jax.ShapeDtypeStruct((b, sq, MLA_HEADS * MLA_V), BF16),
        scratch_shapes=[pltpu.VMEM((2, tq, LANE), F32), pltpu.VMEM((2, tq, LANE), F32),
                        pltpu.VMEM((2, tq, LANE), F32)],
        compiler_params=_cparams(("parallel", "parallel", "parallel", "arbitrary")),
        name="mla_attn",
    )(q, k, v)


def _diff_kernel(slope_ref, lq1_ref, lk1_ref, lq2_ref, lk2_ref, sub_ref, q_ref, k_ref, v_ref, o_ref,
                 m_scr, l_scr, acc_scr, *, tq, tk, q_off, nk, lam_init):
    pr = pl.program_id(1)
    qi = pl.program_id(2)
    ki = pl.program_id(3)

    @pl.when(ki == 0)
    def _():
        m_scr[...] = jnp.full(m_scr.shape, NEG, F32)
        l_scr[...] = jnp.zeros(l_scr.shape, F32)
        acc_scr[...] = jnp.zeros(acc_scr.shape, F32)

    q_lo, k_lo, needed, behind = _block_flags(qi, ki, tq=tq, tk=tk, q_off=q_off)

    def body(masked):
        if masked:
            qp = q_lo + lax.broadcasted_iota(jnp.int32, (tq, tk), 0)
            kp = k_lo + lax.broadcasted_iota(jnp.int32, (tq, tk), 1)
            vis = (kp >> CHUNK_SHIFT) <= (qp >> CHUNK_SHIFT)
            pos = jnp.minimum(kp, 2 * qp - kp).astype(F32)
        else:
            pos = (k_lo + lax.broadcasted_iota(jnp.int32, (1, tk), 1)).astype(F32)
        q = q_ref[0]
        k = k_ref[0]
        v = v_ref[0]
        lane = lax.broadcasted_iota(jnp.int32, (tq, LANE), 1)
        for hh in range(2):
            slope = slope_ref[2 * pr + hh]
            bias = slope * pos
            for j in range(2):
                lo = hh * DIFF_V + j * DIFF_DH
                qj = jnp.where((lane >= lo) & (lane < lo + DIFF_DH), q, jnp.zeros_like(q))
                s = _dot_nt(qj, k) + bias
                if masked:
                    s = jnp.where(vis, s, NEG)
                _online_update(s, v, m_scr, l_scr, acc_scr, 2 * hh + j)

    pl.when(needed & behind)(lambda: body(False))
    pl.when(needed & jnp.logical_not(behind))(lambda: body(True))

    @pl.when(ki == nk - 1)
    def _():
        lam = (jnp.exp(jnp.sum(lq1_ref[...] * lk1_ref[...], axis=1, keepdims=True))
               - jnp.exp(jnp.sum(lq2_ref[...] * lk2_ref[...], axis=1, keepdims=True)) + lam_init)
        lane = lax.broadcasted_iota(jnp.int32, (tq, LANE), 1)
        first = lane < DIFF_V
        o0 = acc_scr[0] / l_scr[0] - lam * (acc_scr[1] / l_scr[1])
        o1 = acc_scr[2] / l_scr[2] - lam * (acc_scr[3] / l_scr[3])
        o = jnp.where(first, o0, o1)
        sq = o * o
        ms0 = jnp.sum(jnp.where(first, sq, 0.0), axis=1, keepdims=True) * (1.0 / DIFF_V)
        ms1 = jnp.sum(jnp.where(first, 0.0, sq), axis=1, keepdims=True) * (1.0 / DIFF_V)
        r = jnp.where(first, lax.rsqrt(ms0 + EPS), lax.rsqrt(ms1 + EPS))
        o_ref[0] = ((o * r * sub_ref[...]) * (1.0 - lam_init)).astype(o_ref.dtype)


def _diff_attention(q, k, v, wp, *, tq, tk, q_off, lam_init):
    b, sq, _ = q.shape
    sk = k.shape[1]
    nq, nk = sq // tq, sk // tk
    pairs = DIFF_HEADS // 2
    last = functools.partial(_last_kblock, tq=tq, tk=tk, q_off=q_off)
    kern = functools.partial(_diff_kernel, tq=tq, tk=tk, q_off=q_off, nk=nk, lam_init=lam_init)
    small = lambda w: pl.BlockSpec((1, w), lambda bi, p, qi, ki: (0, 0))
    kv_spec = pl.BlockSpec((1, tk, LANE), lambda bi, p, qi, ki: (bi, jnp.minimum(ki, last(qi)), p))
    return pl.pallas_call(
        kern,
        grid=(b, pairs, nq, nk),
        in_specs=[pl.BlockSpec(memory_space=pltpu.SMEM),
                  small(DIFF_DH), small(DIFF_DH), small(DIFF_DH), small(DIFF_DH), small(LANE),
                  pl.BlockSpec((1, tq, LANE), lambda bi, p, qi, ki: (bi, qi, p)), kv_spec, kv_spec],
        out_specs=pl.BlockSpec((1, tq, LANE), lambda bi, p, qi, ki: (bi, qi, p)),
        out_shape=jax.ShapeDtypeStruct((b, sq, DIFF_HEADS * DIFF_V), BF16),
        scratch_shapes=[pltpu.VMEM((4, tq, LANE), F32), pltpu.VMEM((4, tq, LANE), F32),
                        pltpu.VMEM((4, tq, LANE), F32)],
        compiler_params=_cparams(("parallel", "parallel", "parallel", "arbitrary")),
        name="diff_attn",
    )(wp["slopes"], wp["diff_lq1"], wp["diff_lk1"], wp["diff_lq2"], wp["diff_lk2"], wp["diff_subln2"],
      q, k, v)


def _mem_attn_kernel(q_ref, k_ref, v_ref, o_ref):
    for hh in range(MEM_HEADS):
        sl = slice(hh * MEM_DH, (hh + 1) * MEM_DH)
        s = _dot_nt(q_ref[0, :, sl], k_ref[0, :, sl])
        p = jnp.exp(s - jnp.max(s, axis=1, keepdims=True))
        o = _dot(p.astype(BF16), v_ref[0, :, sl]) / jnp.sum(p, axis=1, keepdims=True)
        o_ref[0, :, sl] = o.astype(o_ref.dtype)


def _mem_attention(q, k, v, *, tq):
    b, sq, w = q.shape
    nm = k.shape[1]
    return pl.pallas_call(
        _mem_attn_kernel,
        grid=(b, sq // tq),
        in_specs=[pl.BlockSpec((1, tq, w), lambda bi, qi: (bi, qi, 0)),
                  pl.BlockSpec((1, nm, w), lambda bi, qi: (bi, 0, 0)),
                  pl.BlockSpec((1, nm, w), lambda bi, qi: (bi, 0, 0))],
        out_specs=pl.BlockSpec((1, tq, w), lambda bi, qi: (bi, qi, 0)),
        out_shape=jax.ShapeDtypeStruct((b, sq, w), BF16),
        compiler_params=_cparams(("parallel", "parallel")),
        name="mem_attn",
    )(q, k, v)


def _mem_kv_kernel(x_ref, g_ref, w_ref, k_ref, v_ref, kb_ref, vb_ref):
    h = _rms(x_ref[...], g_ref[...]).astype(BF16)
    kv = _dot(h, w_ref[...])
    w = MEM_HEADS * MEM_DH
    k_ref[...] = kv[:, :w]
    v_ref[...] = kv[:, w:]
    kb_ref[...] = kv[:, :w].astype(BF16)
    vb_ref[...] = kv[:, w:].astype(BF16)


def _mem_kv(mem, wp, *, tm):
    n, d = mem.shape
    w = MEM_HEADS * MEM_DH
    row = lambda c: pl.BlockSpec((tm, c), lambda i: (i, 0))
    return pl.pallas_call(
        _mem_kv_kernel,
        grid=(n // tm,),
        in_specs=[row(d), _full_spec((1, d)), _full_spec(wp["w_mem_kv"].shape)],
        out_specs=(row(w), row(w), row(w), row(w)),
        out_shape=(jax.ShapeDtypeStruct((n, w), F32), jax.ShapeDtypeStruct((n, w), F32),
                   jax.ShapeDtypeStruct((n, w), BF16), jax.ShapeDtypeStruct((n, w), BF16)),
        compiler_params=_cparams(("parallel",)),
        name="mem_kv",
    )(mem, wp["norm_mem"], wp["w_mem_kv"])


def _merge_kernel(x_ref, oa_ref, ob_ref, om_ref, g_ref, wg_ref, bg_ref, wa_ref, wb_ref, wm_ref, wo_ref,
                  gf_ref, x1_ref, hn_ref):
    x = x_ref[...]
    d = x.shape[1]
    h = _rms(x, g_ref[...]).astype(BF16)
    gates = jax.nn.sigmoid(_dot(h, wg_ref[...]) + bg_ref[...])
    merged = (gates[:, :d] * _dot(oa_ref[...], wa_ref[...])
              + gates[:, d:2 * d] * _dot(ob_ref[...], wb_ref[...])
              + gates[:, 2 * d:] * _dot(om_ref[...], wm_ref[...]))
    x1 = x + _dot(merged.astype(BF16), wo_ref[...])
    x1_ref[...] = x1
    hn_ref[...] = _rms(x1, gf_ref[...]).astype(BF16)


def _merge(x, oa, ob, om, wp, *, tm):
    n, d = x.shape
    row = lambda c: pl.BlockSpec((tm, c), lambda i: (i, 0))
    names = ("norm_mix", "w_gate", "b_gate", "w_br_a", "w_br_b", "w_br_m", "w_o", "norm_ffn")
    return pl.pallas_call(
        _merge_kernel,
        grid=(n // tm,),
        in_specs=[row(d), row(oa.shape[1]), row(ob.shape[1]), row(om.shape[1])]
                 + [_full_spec(wp[k].shape) for k in names],
        out_specs=(row(d), row(d)),
        out_shape=(jax.ShapeDtypeStruct((n, d), F32), jax.ShapeDtypeStruct((n, d), BF16)),
        compiler_params=_cparams(("parallel",)),
        name="merge",
    )(x, oa, ob, om, *[wp[k] for k in names])


def _top_values(sc, k):
    vals = []
    cur = sc
    for r in range(k):
        m = jnp.max(cur, axis=0, keepdims=True)
        vals.append(m)
        if r + 1 < k:
            cur = jnp.where(cur == m, NEG, cur)
    return vals


def _stack_rows(rows, sub):
    out = jnp.zeros(sub.shape, F32)
    for r, row in enumerate(rows):
        out = jnp.where(sub == r, row, out)
    return out


def _route_kernel(hn_ref, wq_ref, sk_ref, s1_ref, s2_ref, st_ref):
    tt = hn_ref.shape[0]
    q = _dot(hn_ref[...], wq_ref[...]).astype(BF16)
    sub = lax.broadcasted_iota(jnp.int32, (PEER_TOPK, tt), 0)
    sub8 = lax.broadcasted_iota(jnp.int32, (8, tt), 0)
    for h in range(PEER_HEADS):
        sc1 = _dot_nt(sk_ref[2 * h], q[:, (2 * h) * PEER_HALF:(2 * h + 1) * PEER_HALF])
        sc2 = _dot_nt(sk_ref[2 * h + 1], q[:, (2 * h + 1) * PEER_HALF:(2 * h + 2) * PEER_HALF])
        s1_ref[h] = sc1
        s2_ref[h] = sc2
        t1 = _top_values(sc1, PEER_TOPK)
        t2 = _stack_rows(_top_values(sc2, PEER_TOPK), sub)
        cands = []
        for a in range(PEER_TOPK):
            nb = PEER_TOPK // (a + 1)
            rows = PEER_TOPK if nb > 8 else 8
            c = t1[a] + t2[:rows]
            if nb < rows:
                c = jnp.where((sub if rows == PEER_TOPK else sub8) < nb, c, NEG)
            cands.append(c)
        cand = jnp.concatenate(cands, axis=0)
        tau = _top_values(cand, PEER_TOPK)[-1]
        top = t1[0] + t2[0:1]
        z = jnp.sum(jnp.where(cand >= tau, jnp.exp(cand - top), 0.0), axis=0, keepdims=True)
        st_ref[h] = _stack_rows([tau, t1[0], t2[0:1], 1.0 / z], sub8)


def _route(hn, wp, *, tt):
    n, d = hn.shape
    big = pl.BlockSpec((PEER_HEADS, PEER_NKEYS, tt), lambda i: (0, 0, i))
    return pl.pallas_call(
        _route_kernel,
        grid=(n // tt,),
        in_specs=[pl.BlockSpec((tt, d), lambda i: (i, 0)), _full_spec(wp["peer_wq"].shape),
                  _full_spec(wp["peer_subkeys"].shape)],
        out_specs=(big, big, pl.BlockSpec((PEER_HEADS, 8, tt), lambda i: (0, 0, i))),
        out_shape=(jax.ShapeDtypeStruct((PEER_HEADS, PEER_NKEYS, n), F32),
                   jax.ShapeDtypeStruct((PEER_HEADS, PEER_NKEYS, n), F32),
                   jax.ShapeDtypeStruct((PEER_HEADS, 8, n), F32)),
        compiler_params=_cparams(("parallel",)),
        name="peer_route",
    )(hn, wp["peer_wq"], wp["peer_subkeys"])


def _peer_kernel(hn_ref, u_ref, vt_ref, s1_ref, s2_ref, st_ref, x1_ref, gn_ref, y_ref,
                 e1_scr, e2_scr, acc_scr, *, te, ne):
    j = pl.program_id(1)
    rows_per_step = te // PEER_NKEYS

    @pl.when(j == 0)
    def _():
        for h in range(PEER_HEADS):
            e1_scr[h] = jnp.exp(s1_ref[h] - st_ref[h, 1:2, :]) * st_ref[h, 3:4, :]
            e2_scr[h] = jnp.exp(s2_ref[h] - st_ref[h, 2:3, :])
        acc_scr[...] = jnp.zeros(acc_scr.shape, F32)

    a = _dot_nt(u_ref[...], hn_ref[...])
    act = 0.5 * a * (1.0 + lax.erf(a * math.sqrt(0.5)))
    parts = []
    for r in range(rows_per_step):
        i1 = j * rows_per_step + r
        w = None
        for h in range(PEER_HEADS):
            s1row = s1_ref[h, pl.ds(i1, 1), :]
            e1row = e1_scr[h, pl.ds(i1, 1), :]
            keep = (s1row + s2_ref[h]) >= st_ref[h, 0:1, :]
            c = jnp.where(keep, e1row * e2_scr[h], 0.0)
            w = c if w is None else w + c
        parts.append((w * act[r * PEER_NKEYS:(r + 1) * PEER_NKEYS]).astype(BF16))
    hw = jnp.concatenate(parts, axis=0)
    acc_scr[...] += _dot(vt_ref[...], hw)

    @pl.when(j == ne - 1)
    def _():
        xr = x1_ref[...] + acc_scr[...].T
        y_ref[...] = _rms(xr, gn_ref[...])


def _peer(hn, s1, s2, st, x1, wp, *, tt, te):
    n, d = hn.shape
    ne = wp["peer_u"].shape[0] // te
    big = pl.BlockSpec((PEER_HEADS, PEER_NKEYS, tt), lambda i, j: (0, 0, i))
    kern = functools.partial(_peer_kernel, te=te, ne=ne)
    return pl.pallas_call(
        kern,
        grid=(n // tt, ne),
        in_specs=[pl.BlockSpec((tt, d), lambda i, j: (i, 0)),
                  pl.BlockSpec((te, d), lambda i, j: (j, 0)),
                  pl.BlockSpec((d, te), lambda i, j: (0, j)),
                  big, big, pl.BlockSpec((PEER_HEADS, 8, tt), lambda i, j: (0, 0, i)),
                  pl.BlockSpec((tt, d), lambda i, j: (i, 0)),
                  pl.BlockSpec((1, d), lambda i, j: (0, 0))],
        out_specs=pl.BlockSpec((tt, d), lambda i, j: (i, 0)),
        out_shape=jax.ShapeDtypeStruct((n, d), F32),
        scratch_shapes=[pltpu.VMEM((PEER_HEADS, PEER_NKEYS, tt), F32),
                        pltpu.VMEM((PEER_HEADS, PEER_NKEYS, tt), F32),
                        pltpu.VMEM((d, tt), F32)],
        compiler_params=_cparams(("parallel", "arbitrary")),
        name="peer_experts",
    )(hn, wp["peer_u"], wp["peer_vt"], s1, s2, st, x1, wp["norm_final"])


def _prep_weights(l, p):
    f = lambda a: a.astype(F32)
    w_in = f(p["w_in"][l])
    o_cq, o_ckv, o_kr = 0, MLA_Q_LORA, MLA_Q_LORA + MLA_KV_LORA
    o_dq = o_kr + MLA_ROPE
    half = MLA_ROPE // 2
    d = w_in.shape[0]
    kr = w_in[:, o_kr:o_dq]
    pad = jnp.zeros((d, LANE - MLA_ROPE), F32)
    w_in2 = jnp.concatenate([
        w_in[:, o_cq:o_kr], kr, pad,
        -kr[:, half:], kr[:, :half], pad,
        w_in[:, o_dq:]], axis=1)
    assert w_in2.shape[1] == _C_END

    w_uq = f(p["w_uq"][l]).reshape(MLA_Q_LORA, MLA_HEADS, MLA_NOPE + MLA_ROPE)
    nope, x1, x2 = w_uq[..., :MLA_NOPE], w_uq[..., MLA_NOPE:MLA_NOPE + half], w_uq[..., MLA_NOPE + half:]
    zpad = jnp.zeros((MLA_Q_LORA, MLA_HEADS, HEAD_W - MLA_NOPE - MLA_ROPE), F32)
    q_slab = jnp.concatenate([nope, x1, x2, zpad], axis=-1).reshape(MLA_Q_LORA, -1)
    r_slab = jnp.concatenate([jnp.zeros_like(nope), -x2, x1, zpad], axis=-1).reshape(MLA_Q_LORA, -1)
    w_uq2 = jnp.concatenate([q_slab, r_slab], axis=1)

    w_uk = f(p["w_uk"][l]).reshape(MLA_KV_LORA, MLA_HEADS, MLA_NOPE)
    w_uk2 = jnp.concatenate([w_uk, jnp.zeros((MLA_KV_LORA, MLA_HEADS, HEAD_W - MLA_NOPE), F32)],
                            axis=-1).reshape(MLA_KV_LORA, -1)
    r_idx = jnp.arange(MLA_ROPE)
    cols = jnp.arange(MLA_HEADS * HEAD_W)
    p_kr = ((cols[None, :] % HEAD_W) == (MLA_NOPE + r_idx[:, None])).astype(F32)

    slopes = jnp.exp2(-8.0 * jnp.arange(1, DIFF_HEADS + 1, dtype=F32) / DIFF_HEADS)
    sub = f(p["diff_subln"][l])
    row = lambda a: f(a).reshape(1, -1)
    return {
        "norm_mix": row(p["norm_mix"][l]), "w_in": w_in2.astype(BF16),
        "mla_q_norm": row(p["mla_q_norm"][l]), "w_uq": w_uq2.astype(BF16),
        "mla_kv_norm": row(p["mla_kv_norm"][l]),
        "w_uk": w_uk2.astype(BF16), "p_kr": p_kr.astype(BF16), "w_uv": p["w_uv"][l].astype(BF16),
        "slopes": slopes,
        "diff_lq1": row(p["diff_lq1"][l]), "diff_lk1": row(p["diff_lk1"][l]),
        "diff_lq2": row(p["diff_lq2"][l]), "diff_lk2": row(p["diff_lk2"][l]),
        "diff_subln2": jnp.concatenate([sub, sub]).reshape(1, -1),
        "norm_mem": row(p["norm_mem"][l]), "w_mem_kv": p["w_mem_kv"][l].astype(BF16),
        "w_gate": p["w_gate"][l].astype(BF16), "b_gate": row(p["b_gate"][l]),
        "w_br_a": p["w_br_a"][l].astype(BF16), "w_br_b": p["w_br_b"][l].astype(BF16),
        "w_br_m": p["w_br_m"][l].astype(BF16), "w_o": p["w_o"][l].astype(BF16),
        "norm_ffn": row(p["norm_ffn"][l]),
        "peer_wq": p["peer_wq"][l].astype(BF16),
        "peer_subkeys": p["peer_subkeys"][l].reshape(PEER_HEADS * 2, PEER_NKEYS, PEER_HALF).astype(BF16),
        "peer_u": p["peer_u"][l].astype(BF16), "peer_vt": p["peer_v"][l].T.astype(BF16),
        "norm_final": row(p["norm_final"]),
    }


def _rope_tables(pos):
    half = MLA_ROPE // 2
    inv = 1.0 / (ROPE_THETA ** (jnp.arange(half, dtype=F32) / half))
    ang = pos.astype(F32)[:, None] * inv[None, :]
    cos, sin = jnp.cos(ang), jnp.sin(ang)
    n = pos.shape[0]
    one = jnp.ones((n, MLA_NOPE), F32)
    zq = jnp.zeros((n, HEAD_W - MLA_NOPE - MLA_ROPE), F32)
    cq = jnp.concatenate([one, cos, cos, zq], axis=1) * MLA_SCALE
    sq = jnp.concatenate([0.0 * one, sin, sin, zq], axis=1) * MLA_SCALE
    zk = jnp.zeros((n, LANE - MLA_ROPE), F32)
    ck = jnp.concatenate([cos, cos, zk], axis=1)
    sk = jnp.concatenate([sin, sin, zk], axis=1)
    return cq, sq, ck, sk


def _layer(x, pos, past, mem_k, mem_v, wp, layer, cfg):
    b, s, d = x.shape
    n = b * s
    xf = x.reshape(n, d)
    tabs = _rope_tables(pos)
    if s % cfg["tm"] == 0:
        pos_blocks = s // cfg["tm"]
    else:
        tabs = tuple(jnp.tile(t, (b, 1)) for t in tabs)
        pos_blocks = n // cfg["tm"]
    ckv, kr, dk, dv, qm, dqb, dkb, dvb, mqb = _proj_in(xf, tabs, wp, tm=cfg["tm"], pos_blocks=pos_blocks)
    new_rows = (ckv.reshape(b, s, -1), kr.reshape(b, s, -1), dk.reshape(b, s, DIFF_HEADS, DIFF_V),
                dv.reshape(b, s, DIFF_HEADS, DIFF_V))
    r3 = lambda a: a.reshape(b, s, -1)
    if past is None:
        ckv_all, kr_all = ckv, kr
        dk_all, dv_all = r3(dkb), r3(dvb)
        kk = s
        q_off = 0
    else:
        p_len = past[0].shape[1]
        kk = p_len + s
        q_off = p_len
        ckv_all = jnp.concatenate([past[0], r3(ckv)], axis=1).reshape(b * kk, -1)
        kr_all = jnp.concatenate([past[1], r3(kr)], axis=1).reshape(b * kk, -1)
        dk_all = jnp.concatenate([past[2].reshape(b, p_len, -1).astype(BF16), r3(dkb)], axis=1)
        dv_all = jnp.concatenate([past[3].reshape(b, p_len, -1).astype(BF16), r3(dvb)], axis=1)
    k_mla, v_mla = _kv_up(ckv_all, kr_all, wp, tm=cfg["tm_kv"])
    att = dict(tq=cfg["tq"], tk=cfg["tk"] if past is None else kk, q_off=q_off)
    o_a = _mla_attention(r3(qm), k_mla.reshape(b, kk, -1), v_mla.reshape(b, kk, -1), **att)
    lam_init = 0.8 - 0.6 * math.exp(-0.3 * layer)
    o_b = _diff_attention(r3(dqb), dk_all, dv_all, wp, lam_init=lam_init, **att)
    o_m = _mem_attention(r3(mqb), mem_k, mem_v, tq=cfg["tq"])
    x1, hn = _merge(xf, o_a.reshape(n, -1), o_b.reshape(n, -1), o_m.reshape(n, -1), wp, tm=cfg["tm"])
    return x1, hn, new_rows


def _peer_and_norm(x1, hn, wp, cfg):
    s1, s2, st = _route(hn, wp, tt=cfg["tt"])
    return _peer(hn, s1, s2, st, x1, wp, tt=cfg["tt"], te=cfg["te"])


_CFG_PROMPT = dict(tm=512, tm_kv=512, tq=512, tk=512, tt=512, te=1024)
_CFG_SAMPLE = dict(tm=256, tm_kv=256, tq=32, tk=None, tt=256, te=1024)


def kernel(x_prompt, x_sample, cache_mla_ckv, cache_mla_krope, cache_diff_k, cache_diff_v, cache_mem_k, cache_mem_v, mem_prompt, norm_mix, w_in, mla_q_norm, w_uq, mla_kv_norm, w_uk, w_uv, diff_lq1, diff_lk1, diff_lq2, diff_lk2, diff_subln, norm_mem, w_mem_kv, w_br_a, w_br_b, w_br_m, w_gate, b_gate, w_o, norm_ffn, peer_wq, peer_subkeys, peer_u, peer_v, norm_final):
    params = dict(norm_mix=norm_mix, w_in=w_in, mla_q_norm=mla_q_norm, w_uq=w_uq, mla_kv_norm=mla_kv_norm,
                  w_uk=w_uk, w_uv=w_uv, diff_lq1=diff_lq1, diff_lk1=diff_lk1, diff_lq2=diff_lq2,
                  diff_lk2=diff_lk2, diff_subln=diff_subln, norm_mem=norm_mem, w_mem_kv=w_mem_kv,
                  w_br_a=w_br_a, w_br_b=w_br_b, w_br_m=w_br_m, w_gate=w_gate, b_gate=b_gate, w_o=w_o,
                  norm_ffn=norm_ffn, peer_wq=peer_wq, peer_subkeys=peer_subkeys, peer_u=peer_u,
                  peer_v=peer_v, norm_final=norm_final)
    depth = w_in.shape[0]
    assert depth == 1, "the final norm is fused into the last PEER step of a single layer"
    bp, sp, d = x_prompt.shape
    bs, ss, _ = x_sample.shape
    n_mem = mem_prompt.shape[1]
    past_len = cache_mla_ckv.shape[2]
    pos_p = jnp.arange(sp, dtype=jnp.int32)
    pos_s = past_len + jnp.arange(ss, dtype=jnp.int32)

    l = 0
    wp = _prep_weights(l, params)
    mk, mv, mkb, mvb = _mem_kv(mem_prompt.reshape(bp * n_mem, d), wp, tm=n_mem)
    m3 = lambda a, b: a.reshape(b, n_mem, -1)
    x1p, hnp, rows_p = _layer(x_prompt, pos_p, None, m3(mkb, bp), m3(mvb, bp), wp, l, _CFG_PROMPT)
    past = (cache_mla_ckv[l], cache_mla_krope[l], cache_diff_k[l], cache_diff_v[l])
    x1s, hns, rows_s = _layer(x_sample, pos_s, past, m3(cache_mem_k[l].astype(BF16), bs),
                              m3(cache_mem_v[l].astype(BF16), bs), wp, l, _CFG_SAMPLE)
    y_prompt = _peer_and_norm(x1p, hnp, wp, _CFG_PROMPT).reshape(bp, sp, d)
    y_sample = _peer_and_norm(x1s, hns, wp, _CFG_SAMPLE).reshape(bs, ss, d)
    st = lambda a: a[None]
    mem4 = lambda a: a.reshape(bp, n_mem, MEM_HEADS, MEM_DH)[None]
    return (y_prompt, y_sample,
            st(rows_p[0]), st(rows_p[1]), st(rows_p[2]), st(rows_p[3]),
            mem4(mk), mem4(mv),
            st(rows_s[0]), st(rows_s[1]), st(rows_s[2]), st(rows_s[3]))
```

```python
import functools
import math

import jax
import jax.numpy as jnp
from jax import lax
from jax.experimental import pallas as pl
from jax.experimental.pallas import tpu as pltpu

F32 = jnp.float32
BF16 = jnp.bfloat16

CHUNK = 64
CHUNK_SHIFT = 6
EPS = 1e-6
NEG = -1e30
HUGE = 1e30
MLA_HEADS = 8
MLA_Q_LORA = 384
MLA_KV_LORA = 256
MLA_NOPE = 64
MLA_ROPE = 32
MLA_V = 64
ROPE_THETA = 10000.0
MLA_SCALE = (MLA_NOPE + MLA_ROPE) ** -0.5
DIFF_HEADS = 8
DIFF_DH = 32
DIFF_V = 2 * DIFF_DH
DIFF_SCALE = DIFF_DH ** -0.5
MEM_HEADS = 4
MEM_DH = 128
MEM_SCALE = MEM_DH ** -0.5
PEER_HEADS = 8
PEER_NKEYS = 128
PEER_HALF = 128
PEER_TOPK = 16
LANE = 128
HEAD_W = 128

VMEM_LIMIT = 56 * 1024 * 1024

_C_CQ = 0
_C_CKV = _C_CQ + MLA_Q_LORA
_C_KR = _C_CKV + MLA_KV_LORA
_C_KRR = _C_KR + LANE
_C_DQ = _C_KRR + LANE
_C_DK = _C_DQ + DIFF_HEADS * DIFF_V
_C_DV = _C_DK + DIFF_HEADS * DIFF_V
_C_MQ = _C_DV + DIFF_HEADS * DIFF_V
_C_END = _C_MQ + MEM_HEADS * MEM_DH


def _cparams(sem):
    return pltpu.CompilerParams(dimension_semantics=sem, vmem_limit_bytes=VMEM_LIMIT)


def _rms(x, g):
    return x * lax.rsqrt(jnp.mean(x * x, axis=-1, keepdims=True) + EPS) * g


def _dot(a, b):
    return jnp.dot(a, b, preferred_element_type=F32)


def _dot_nt(a, b):
    return lax.dot_general(a, b, (((1,), (1,)), ((), ())), preferred_element_type=F32)


def _full_spec(shape):
    nd = len(shape)
    return pl.BlockSpec(shape, lambda *_: (0,) * nd)


def _proj_in_kernel(x_ref, g_ref, win_ref, qn_ref, wuq_ref, kvn_ref, cq_ref, sq_ref, ck_ref, sk_ref,
                    ckv_ref, kr_ref, dk_ref, dv_ref, qm_ref, dqb_ref, dkb_ref, dvb_ref, mqb_ref):
    h = _rms(x_ref[...], g_ref[...]).astype(BF16)
    z = _dot(h, win_ref[...])
    cqn = _rms(z[:, _C_CQ:_C_CKV], qn_ref[...]).astype(BF16)
    q2 = _dot(cqn, wuq_ref[...])
    cq = cq_ref[...]
    sq = sq_ref[...]
    nq = MLA_HEADS * HEAD_W
    for hh in range(MLA_HEADS):
        lo = hh * HEAD_W
        qm_ref[:, lo:lo + HEAD_W] = (q2[:, lo:lo + HEAD_W] * cq
                                     + q2[:, nq + lo:nq + lo + HEAD_W] * sq).astype(BF16)
    ckv_ref[...] = _rms(z[:, _C_CKV:_C_KR], kvn_ref[...])
    kr = z[:, _C_KR:_C_KRR] * ck_ref[...] + z[:, _C_KRR:_C_DQ] * sk_ref[...]
    kr_ref[...] = kr[:, :MLA_ROPE]
    dk = z[:, _C_DK:_C_DV]
    dv = z[:, _C_DV:_C_MQ]
    dk_ref[...] = dk
    dv_ref[...] = dv
    dkb_ref[...] = dk.astype(BF16)
    dvb_ref[...] = dv.astype(BF16)
    dqb_ref[...] = (z[:, _C_DQ:_C_DK] * DIFF_SCALE).astype(BF16)
    mqb_ref[...] = (z[:, _C_MQ:_C_END] * MEM_SCALE).astype(BF16)


def _proj_in(x, tabs, wp, *, tm, pos_blocks):
    n, d = x.shape
    grid = (n // tm,)
    row = lambda w: pl.BlockSpec((tm, w), lambda i: (i, 0))
    tab = pl.BlockSpec((tm, LANE), lambda i: (i % pos_blocks, 0))
    dw = DIFF_HEADS * DIFF_V
    out_shape = (
        jax.ShapeDtypeStruct((n, MLA_KV_LORA), F32),
        jax.ShapeDtypeStruct((n, MLA_ROPE), F32),
        jax.ShapeDtypeStruct((n, dw), F32),
        jax.ShapeDtypeStruct((n, dw), F32),
        jax.ShapeDtypeStruct((n, MLA_HEADS * HEAD_W), BF16),
        jax.ShapeDtypeStruct((n, dw), BF16),
        jax.ShapeDtypeStruct((n, dw), BF16),
        jax.ShapeDtypeStruct((n, dw), BF16),
        jax.ShapeDtypeStruct((n, MEM_HEADS * MEM_DH), BF16),
    )
    return pl.pallas_call(
        _proj_in_kernel,
        grid=grid,
        in_specs=[row(d), _full_spec((1, d)), _full_spec(wp["w_in"].shape), _full_spec((1, MLA_Q_LORA)),
                  _full_spec(wp["w_uq"].shape), _full_spec((1, MLA_KV_LORA)), tab, tab, tab, tab],
        out_specs=(row(MLA_KV_LORA), row(MLA_ROPE), row(dw), row(dw), row(MLA_HEADS * HEAD_W),
                   row(dw), row(dw), row(dw), row(MEM_HEADS * MEM_DH)),
        out_shape=out_shape,
        compiler_params=_cparams(("parallel",)),
        name="proj_in",
    )(x, wp["norm_mix"], wp["w_in"], wp["mla_q_norm"], wp["w_uq"], wp["mla_kv_norm"], *tabs)


def _kv_up_kernel(ckv_ref, kr_ref, wuk_ref, pk_ref, wuv_ref, k_ref, v_ref):
    c = ckv_ref[...].astype(BF16)
    k = _dot(c, wuk_ref[...]) + _dot(kr_ref[...].astype(BF16), pk_ref[...])
    k_ref[...] = k.astype(BF16)
    v_ref[...] = _dot(c, wuv_ref[...]).astype(BF16)


def _kv_up(ckv, kr, wp, *, tm):
    n = ckv.shape[0]
    row = lambda w: pl.BlockSpec((tm, w), lambda i: (i, 0))
    return pl.pallas_call(
        _kv_up_kernel,
        grid=(n // tm,),
        in_specs=[row(MLA_KV_LORA), row(MLA_ROPE), _full_spec(wp["w_uk"].shape),
                  _full_spec(wp["p_kr"].shape), _full_spec(wp["w_uv"].shape)],
        out_specs=(row(MLA_HEADS * HEAD_W), row(MLA_HEADS * MLA_V)),
        out_shape=(jax.ShapeDtypeStruct((n, MLA_HEADS * HEAD_W), BF16),
                   jax.ShapeDtypeStruct((n, MLA_HEADS * MLA_V), BF16)),
        compiler_params=_cparams(("parallel",)),
        name="kv_up",
    )(ckv, kr, wp["w_uk"], wp["p_kr"], wp["w_uv"])


def _lanes(col, n):
    if n == LANE:
        return col
    if n % LANE == 0:
        return jnp.concatenate([col] * (n // LANE), axis=1)
    return jnp.broadcast_to(col[:, :1], (col.shape[0], n))


def _softmax_block(s, rows, m_scr, l_scr, al_scr, p_scr):
    tk = s.shape[1]
    m_prev = m_scr[rows]
    m_next = jnp.maximum(m_prev, jnp.max(s, axis=1, keepdims=True))
    p = jnp.exp(s - _lanes(m_next, tk))
    alpha = jnp.exp(m_prev - m_next)
    l_scr[rows] = alpha * l_scr[rows] + jnp.sum(p, axis=1, keepdims=True)
    al_scr[rows] = alpha
    m_scr[rows] = m_next
    p_scr[rows] = p.astype(BF16)


def _init_stats(m_scr, l_scr, acc_scr):
    m_scr[...] = jnp.full(m_scr.shape, NEG, F32)
    l_scr[...] = jnp.zeros(l_scr.shape, F32)
    acc_scr[...] = jnp.zeros(acc_scr.shape, F32)


def _block_range(qi, *, tq, tk, q_off, nk):
    q_lo = q_off + qi * tq
    q_hi = q_lo + (tq - 1)
    n_behind = jnp.minimum((q_lo + 1) // tk, nk)
    last_key = ((q_hi >> CHUNK_SHIFT) << CHUNK_SHIFT) + (CHUNK - 1)
    n_need = jnp.minimum(last_key // tk + 1, nk)
    return q_lo, n_behind, n_need


def _for_blocks(lo, hi, fn):
    lax.fori_loop(lo, hi, lambda i, c: (fn(i), c)[1], 0)


def _attn_specs(b, pairs, sq, sk, tq, qw, kw, vw):
    grid = (b, pairs, sq // tq)
    in_specs = [pl.BlockSpec((1, tq, qw), lambda bi, p, qi: (bi, qi, p)),
                pl.BlockSpec((1, sk, kw), lambda bi, p, qi: (bi, 0, p)),
                pl.BlockSpec((1, sk, vw), lambda bi, p, qi: (bi, 0, p))]
    out_spec = pl.BlockSpec((1, tq, vw), lambda bi, p, qi: (bi, qi, p))
    return grid, in_specs, out_spec


def _mla_kernel(q_ref, k_ref, v_ref, o_ref, m_scr, l_scr, al_scr, acc_scr, p_scr, *, tq, tk, q_off, nk):
    qi = pl.program_id(2)
    _init_stats(m_scr, l_scr, acc_scr)
    q_lo, n_behind, n_need = _block_range(qi, tq=tq, tk=tk, q_off=q_off, nk=nk)

    def block(i, masked):
        k_lo = pl.multiple_of(i * tk, tk)
        if masked:
            qp = q_lo + lax.broadcasted_iota(jnp.int32, (tq, tk), 0)
            kp = k_lo + lax.broadcasted_iota(jnp.int32, (tq, tk), 1)
            vis = (kp >> CHUNK_SHIFT) <= (qp >> CHUNK_SHIFT)
        for hh in range(2):
            hs = slice(hh * HEAD_W, (hh + 1) * HEAD_W)
            s = _dot_nt(q_ref[0, :, hs], k_ref[0, pl.ds(k_lo, tk), hs])
            if masked:
                s = jnp.where(vis, s, NEG)
            _softmax_block(s, slice(hh * tq, (hh + 1) * tq), m_scr, l_scr, al_scr, p_scr)
        acc_scr[...] = acc_scr[...] * al_scr[...] + _dot(p_scr[...], v_ref[0, pl.ds(k_lo, tk), :])

    _for_blocks(0, n_behind, lambda i: block(i, False))
    _for_blocks(n_behind, n_need, lambda i: block(i, True))

    lane = lax.broadcasted_iota(jnp.int32, (tq, LANE), 1)
    o0 = acc_scr[0:tq] / l_scr[0:tq]
    o1 = acc_scr[tq:2 * tq] / l_scr[tq:2 * tq]
    o_ref[0] = jnp.where(lane < MLA_V, o0, o1).astype(o_ref.dtype)


def _mla_attention(q, k, v, *, tq, tk, q_off):
    b, sq, _ = q.shape
    sk = k.shape[1]
    grid, in_specs, out_spec = _attn_specs(b, MLA_HEADS // 2, sq, sk, tq, 2 * HEAD_W, 2 * HEAD_W, 2 * MLA_V)
    kern = functools.partial(_mla_kernel, tq=tq, tk=tk, q_off=q_off, nk=sk // tk)
    stat = pltpu.VMEM((2 * tq, LANE), F32)
    return pl.pallas_call(
        kern,
        grid=grid,
        in_specs=in_specs,
        out_specs=out_spec,
        out_shape=jax.ShapeDtypeStruct((b, sq, MLA_HEADS * MLA_V), BF16),
        scratch_shapes=[stat, stat, stat, stat, pltpu.VMEM((2 * tq, tk), BF16)],
        compiler_params=_cparams(("parallel", "parallel", "parallel")),
        name="mla_attn",
    )(q, k, v)


def _diff_kernel(slope_ref, lq1_ref, lk1_ref, lq2_ref, lk2_ref, sub_ref, q_ref, k_ref, v_ref, o_ref,
                 qs_scr, m_scr, l_scr, al_scr, acc_scr, p_scr, *, tq, tk, q_off, nk, lam_init):
    pr = pl.program_id(1)
    qi = pl.program_id(2)
    _init_stats(m_scr, l_scr, acc_scr)
    q_lo, n_behind, n_need = _block_range(qi, tq=tq, tk=tk, q_off=q_off, nk=nk)

    q = q_ref[0]
    lane = lax.broadcasted_iota(jnp.int32, (tq, LANE), 1)
    for mi in range(4):
        lo = mi * DIFF_DH
        qs_scr[mi * tq:(mi + 1) * tq] = jnp.where((lane >= lo) & (lane < lo + DIFF_DH), q, jnp.zeros_like(q))

    def block(i, masked):
        k_lo = pl.multiple_of(i * tk, tk)
        if masked:
            qp = q_lo + lax.broadcasted_iota(jnp.int32, (tq, tk), 0)
            kp = k_lo + lax.broadcasted_iota(jnp.int32, (tq, tk), 1)
            vis = (kp >> CHUNK_SHIFT) <= (qp >> CHUNK_SHIFT)
            pos = jnp.minimum(kp, 2 * qp - kp).astype(F32)
        else:
            pos = (k_lo + lax.broadcasted_iota(jnp.int32, (1, tk), 1)).astype(F32)
        s_all = _dot_nt(qs_scr[...], k_ref[0, pl.ds(k_lo, tk), :])
        for hh in range(2):
            bias = slope_ref[2 * pr + hh] * pos
            for j in range(2):
                rows = slice((2 * hh + j) * tq, (2 * hh + j + 1) * tq)
                s = s_all[rows] + bias
                if masked:
                    s = jnp.where(vis, s, NEG)
                _softmax_block(s, rows, m_scr, l_scr, al_scr, p_scr)
        acc_scr[...] = acc_scr[...] * al_scr[...] + _dot(p_scr[...], v_ref[0, pl.ds(k_lo, tk), :])

    _for_blocks(0, n_behind, lambda i: block(i, False))
    _for_blocks(n_behind, n_need, lambda i: block(i, True))

    lam = (jnp.exp(jnp.sum(lq1_ref[...] * lk1_ref[...], axis=1, keepdims=True))
           - jnp.exp(jnp.sum(lq2_ref[...] * lk2_ref[...], axis=1, keepdims=True)) + lam_init)
    first = lane < DIFF_V
    on = [acc_scr[mi * tq:(mi + 1) * tq] / l_scr[mi * tq:(mi + 1) * tq] for mi in range(4)]
    o = jnp.where(first, on[0] - lam * on[1], on[2] - lam * on[3])
    sq = o * o
    ms0 = jnp.sum(jnp.where(first, sq, 0.0), axis=1, keepdims=True) * (1.0 / DIFF_V)
    ms1 = jnp.sum(jnp.where(first, 0.0, sq), axis=1, keepdims=True) * (1.0 / DIFF_V)
    r = jnp.where(first, lax.rsqrt(ms0 + EPS), lax.rsqrt(ms1 + EPS))
    o_ref[0] = ((o * r * sub_ref[...]) * (1.0 - lam_init)).astype(o_ref.dtype)


def _diff_attention(q, k, v, wp, *, tq, tk, q_off, lam_init):
    b, sq, _ = q.shape
    sk = k.shape[1]
    grid, in_specs, out_spec = _attn_specs(b, DIFF_HEADS // 2, sq, sk, tq, LANE, LANE, LANE)
    kern = functools.partial(_diff_kernel, tq=tq, tk=tk, q_off=q_off, nk=sk // tk, lam_init=lam_init)
    small = lambda w: pl.BlockSpec((1, w), lambda bi, p, qi: (0, 0))
    stat = pltpu.VMEM((4 * tq, LANE), F32)
    return pl.pallas_call(
        kern,
        grid=grid,
        in_specs=[pl.BlockSpec(memory_space=pltpu.SMEM),
                  small(DIFF_DH), small(DIFF_DH), small(DIFF_DH), small(DIFF_DH), small(LANE)] + in_specs,
        out_specs=out_spec,
        out_shape=jax.ShapeDtypeStruct((b, sq, DIFF_HEADS * DIFF_V), BF16),
        scratch_shapes=[pltpu.VMEM((4 * tq, LANE), BF16), stat, stat, stat, stat,
                        pltpu.VMEM((4 * tq, tk), BF16)],
        compiler_params=_cparams(("parallel", "parallel", "parallel")),
        name="diff_attn",
    )(wp["slopes"], wp["diff_lq1"], wp["diff_lk1"], wp["diff_lq2"], wp["diff_lk2"], wp["diff_subln2"],
      q, k, v)


def _mem_attn_kernel(q_ref, k_ref, v_ref, o_ref):
    for hh in range(MEM_HEADS):
        sl = slice(hh * MEM_DH, (hh + 1) * MEM_DH)
        s = _dot_nt(q_ref[0, :, sl], k_ref[0, :, sl])
        p = jnp.exp(s - jnp.max(s, axis=1, keepdims=True))
        o = _dot(p.astype(BF16), v_ref[0, :, sl]) / jnp.sum(p, axis=1, keepdims=True)
        o_ref[0, :, sl] = o.astype(o_ref.dtype)


def _mem_attention(q, k, v, *, tq):
    b, sq, w = q.shape
    nm = k.shape[1]
    return pl.pallas_call(
        _mem_attn_kernel,
        grid=(b, sq // tq),
        in_specs=[pl.BlockSpec((1, tq, w), lambda bi, qi: (bi, qi, 0)),
                  pl.BlockSpec((1, nm, w), lambda bi, qi: (bi, 0, 0)),
                  pl.BlockSpec((1, nm, w), lambda bi, qi: (bi, 0, 0))],
        out_specs=pl.BlockSpec((1, tq, w), lambda bi, qi: (bi, qi, 0)),
        out_shape=jax.ShapeDtypeStruct((b, sq, w), BF16),
        compiler_params=_cparams(("parallel", "parallel")),
        name="mem_attn",
    )(q, k, v)


def _mem_kv_kernel(x_ref, g_ref, w_ref, k_ref, v_ref, kb_ref, vb_ref):
    h = _rms(x_ref[...], g_ref[...]).astype(BF16)
    kv = _dot(h, w_ref[...])
    w = MEM_HEADS * MEM_DH
    k_ref[...] = kv[:, :w]
    v_ref[...] = kv[:, w:]
    kb_ref[...] = kv[:, :w].astype(BF16)
    vb_ref[...] = kv[:, w:].astype(BF16)


def _mem_kv(mem, wp, *, tm):
    n, d = mem.shape
    w = MEM_HEADS * MEM_DH
    row = lambda c: pl.BlockSpec((tm, c), lambda i: (i, 0))
    return pl.pallas_call(
        _mem_kv_kernel,
        grid=(n // tm,),
        in_specs=[row(d), _full_spec((1, d)), _full_spec(wp["w_mem_kv"].shape)],
        out_specs=(row(w), row(w), row(w), row(w)),
        out_shape=(jax.ShapeDtypeStruct((n, w), F32), jax.ShapeDtypeStruct((n, w), F32),
                   jax.ShapeDtypeStruct((n, w), BF16), jax.ShapeDtypeStruct((n, w), BF16)),
        compiler_params=_cparams(("parallel",)),
        name="mem_kv",
    )(mem, wp["norm_mem"], wp["w_mem_kv"])


def _merge_kernel(x_ref, oa_ref, ob_ref, om_ref, g_ref, wg_ref, bg_ref, wa_ref, wb_ref, wm_ref, wo_ref,
                  gf_ref, x1_ref, hn_ref):
    x = x_ref[...]
    d = x.shape[1]
    h = _rms(x, g_ref[...]).astype(BF16)
    gates = jax.nn.sigmoid(_dot(h, wg_ref[...]) + bg_ref[...])
    merged = (gates[:, :d] * _dot(oa_ref[...], wa_ref[...])
              + gates[:, d:2 * d] * _dot(ob_ref[...], wb_ref[...])
              + gates[:, 2 * d:] * _dot(om_ref[...], wm_ref[...]))
    x1 = x + _dot(merged.astype(BF16), wo_ref[...])
    x1_ref[...] = x1
    hn_ref[...] = _rms(x1, gf_ref[...]).astype(BF16)


def _merge(x, oa, ob, om, wp, *, tm):
    n, d = x.shape
    row = lambda c: pl.BlockSpec((tm, c), lambda i: (i, 0))
    names = ("norm_mix", "w_gate", "b_gate", "w_br_a", "w_br_b", "w_br_m", "w_o", "norm_ffn")
    return pl.pallas_call(
        _merge_kernel,
        grid=(n // tm,),
        in_specs=[row(d), row(oa.shape[1]), row(ob.shape[1]), row(om.shape[1])]
                 + [_full_spec(wp[k].shape) for k in names],
        out_specs=(row(d), row(d)),
        out_shape=(jax.ShapeDtypeStruct((n, d), F32), jax.ShapeDtypeStruct((n, d), BF16)),
        compiler_params=_cparams(("parallel",)),
        name="merge",
    )(x, oa, ob, om, *[wp[k] for k in names])


def _top_values(sc, k):
    vals = []
    cur = sc
    for r in range(k):
        m = jnp.max(cur, axis=0, keepdims=True)
        vals.append(m)
        if r + 1 < k:
            cur = jnp.where(cur == m, NEG, cur)
    return vals


def _stack_rows(rows, sub):
    out = jnp.zeros(sub.shape, F32)
    for r, row in enumerate(rows):
        out = jnp.where(sub == r, row, out)
    return out


def _route_kernel(hn_ref, wq_ref, sk_ref, thr_ref, e1_ref, s2_ref, e2_ref):
    tt = hn_ref.shape[0]
    q = _dot(hn_ref[...], wq_ref[...]).astype(BF16)
    sub = lax.broadcasted_iota(jnp.int32, (PEER_TOPK, tt), 0)
    sub8 = lax.broadcasted_iota(jnp.int32, (8, tt), 0)
    for h in range(PEER_HEADS):
        sc1 = _dot_nt(sk_ref[2 * h], q[:, (2 * h) * PEER_HALF:(2 * h + 1) * PEER_HALF])
        sc2 = _dot_nt(sk_ref[2 * h + 1], q[:, (2 * h + 1) * PEER_HALF:(2 * h + 2) * PEER_HALF])
        t1 = _top_values(sc1, PEER_TOPK)
        t2 = _stack_rows(_top_values(sc2, PEER_TOPK), sub)
        t2h = t2[:8]
        cands = []
        for a in range(PEER_TOPK):
            nb = PEER_TOPK // (a + 1)
            c = t1[a] + (t2 if nb > 8 else t2h)
            if nb < c.shape[0]:
                c = jnp.where((sub if nb > 8 else sub8) < nb, c, NEG)
            cands.append(c)
        cand = jnp.concatenate(cands, axis=0)
        tau = _top_values(cand, PEER_TOPK)[-1]
        top = t1[0] + t2[0:1]
        z = jnp.sum(jnp.where(cand >= tau, jnp.exp(cand - top), 0.0), axis=0, keepdims=True)
        thr = jnp.full(sc1.shape, HUGE, F32)
        for a in range(PEER_TOPK):
            t2a = t2 if cands[a].shape[0] == PEER_TOPK else t2h
            ga = jnp.min(jnp.where(cands[a] >= tau, t2a, HUGE), axis=0, keepdims=True)
            thr = jnp.where(sc1 == t1[a], ga, thr)
        thr_ref[h] = thr
        e1_ref[h] = jnp.exp(sc1 - t1[0]) * (1.0 / z)
        s2_ref[h] = sc2
        e2_ref[h] = jnp.exp(sc2 - t2[0:1])


def _route(hn, wp, *, tt):
    n, d = hn.shape
    big = pl.BlockSpec((PEER_HEADS, PEER_NKEYS, tt), lambda i: (0, 0, i))
    big_shape = jax.ShapeDtypeStruct((PEER_HEADS, PEER_NKEYS, n), F32)
    return pl.pallas_call(
        _route_kernel,
        grid=(n // tt,),
        in_specs=[pl.BlockSpec((tt, d), lambda i: (i, 0)), _full_spec(wp["peer_wq"].shape),
                  _full_spec(wp["peer_subkeys"].shape)],
        out_specs=(big, big, big, big),
        out_shape=(big_shape, big_shape, big_shape, big_shape),
        compiler_params=_cparams(("parallel",)),
        name="peer_route",
    )(hn, wp["peer_wq"], wp["peer_subkeys"])


def _peer_kernel(hn_ref, u_ref, vt_ref, thr_ref, e1_ref, s2_ref, e2_ref, x1_ref, gn_ref, y_ref,
                 acc_scr, hw_scr, *, te, ne, ce):
    j = pl.program_id(1)
    tt = hn_ref.shape[0]
    rows_per_chunk = ce // PEER_NKEYS

    @pl.when(j == 0)
    def _():
        acc_scr[...] = jnp.zeros(acc_scr.shape, F32)

    hn = hn_ref[...]
    tot = None
    for c in range(te // ce):
        es = slice(c * ce, (c + 1) * ce)
        a = _dot_nt(u_ref[es, :], hn)
        act = 0.5 * a * (1.0 + lax.erf(a * math.sqrt(0.5)))
        for r in range(rows_per_chunk):
            i1 = c * rows_per_chunk + r
            rs = slice(r * PEER_NKEYS, (r + 1) * PEER_NKEYS)
            for tc in range(tt // LANE):
                ls = slice(tc * LANE, (tc + 1) * LANE)
                w = None
                for h in range(PEER_HEADS):
                    keep = s2_ref[h, :, ls] >= thr_ref[h, i1:i1 + 1, ls]
                    cw = jnp.where(keep, e2_ref[h, :, ls], 0.0) * e1_ref[h, i1:i1 + 1, ls]
                    w = cw if w is None else w + cw
                hw_scr[c * ce + r * PEER_NKEYS:c * ce + (r + 1) * PEER_NKEYS, ls] = (w * act[rs, ls]).astype(BF16)
        part = _dot(vt_ref[:, es], hw_scr[es, :])
        tot = part if tot is None else tot + part
    acc_scr[...] += tot

    @pl.when(j == ne - 1)
    def _():
        xr = x1_ref[...] + acc_scr[...].T
        y_ref[...] = _rms(xr, gn_ref[...])


def _peer(hn, thr, e1, s2, e2, x1, wp, *, tt, te, ce):
    n, d = hn.shape
    ne = wp["peer_u"].shape[0] // te
    big = pl.BlockSpec((PEER_HEADS, PEER_NKEYS, tt), lambda i, j: (0, 0, i))
    rows = pl.BlockSpec((PEER_HEADS, te // PEER_NKEYS, tt), lambda i, j: (0, j, i))
    kern = functools.partial(_peer_kernel, te=te, ne=ne, ce=ce)
    return pl.pallas_call(
        kern,
        grid=(n // tt, ne),
        in_specs=[pl.BlockSpec((tt, d), lambda i, j: (i, 0)),
                  pl.BlockSpec((te, d), lambda i, j: (j, 0)),
                  pl.BlockSpec((d, te), lambda i, j: (0, j)),
                  rows, rows, big, big,
                  pl.BlockSpec((tt, d), lambda i, j: (i, 0)),
                  pl.BlockSpec((1, d), lambda i, j: (0, 0))],
        out_specs=pl.BlockSpec((tt, d), lambda i, j: (i, 0)),
        out_shape=jax.ShapeDtypeStruct((n, d), F32),
        scratch_shapes=[pltpu.VMEM((d, tt), F32), pltpu.VMEM((te, tt), BF16)],
        compiler_params=_cparams(("parallel", "arbitrary")),
        name="peer_experts",
    )(hn, wp["peer_u"], wp["peer_vt"], thr, e1, s2, e2, x1, wp["norm_final"])


def _prep_weights(l, p):
    f = lambda a: a.astype(F32)
    w_in = f(p["w_in"][l])
    o_cq, o_ckv, o_kr = 0, MLA_Q_LORA, MLA_Q_LORA + MLA_KV_LORA
    o_dq = o_kr + MLA_ROPE
    half = MLA_ROPE // 2
    d = w_in.shape[0]
    kr = w_in[:, o_kr:o_dq]
    pad = jnp.zeros((d, LANE - MLA_ROPE), F32)
    w_in2 = jnp.concatenate([
        w_in[:, o_cq:o_kr], kr, pad,
        -kr[:, half:], kr[:, :half], pad,
        w_in[:, o_dq:]], axis=1)
    assert w_in2.shape[1] == _C_END

    w_uq = f(p["w_uq"][l]).reshape(MLA_Q_LORA, MLA_HEADS, MLA_NOPE + MLA_ROPE)
    nope, x1, x2 = w_uq[..., :MLA_NOPE], w_uq[..., MLA_NOPE:MLA_NOPE + half], w_uq[..., MLA_NOPE + half:]
    zpad = jnp.zeros((MLA_Q_LORA, MLA_HEADS, HEAD_W - MLA_NOPE - MLA_ROPE), F32)
    q_slab = jnp.concatenate([nope, x1, x2, zpad], axis=-1).reshape(MLA_Q_LORA, -1)
    r_slab = jnp.concatenate([jnp.zeros_like(nope), -x2, x1, zpad], axis=-1).reshape(MLA_Q_LORA, -1)
    w_uq2 = jnp.concatenate([q_slab, r_slab], axis=1)

    w_uk = f(p["w_uk"][l]).reshape(MLA_KV_LORA, MLA_HEADS, MLA_NOPE)
    w_uk2 = jnp.concatenate([w_uk, jnp.zeros((MLA_KV_LORA, MLA_HEADS, HEAD_W - MLA_NOPE), F32)],
                            axis=-1).reshape(MLA_KV_LORA, -1)
    r_idx = jnp.arange(MLA_ROPE)
    cols = jnp.arange(MLA_HEADS * HEAD_W)
    p_kr = ((cols[None, :] % HEAD_W) == (MLA_NOPE + r_idx[:, None])).astype(F32)

    slopes = jnp.exp2(-8.0 * jnp.arange(1, DIFF_HEADS + 1, dtype=F32) / DIFF_HEADS)
    sub = f(p["diff_subln"][l])
    row = lambda a: f(a).reshape(1, -1)
    return {
        "norm_mix": row(p["norm_mix"][l]), "w_in": w_in2.astype(BF16),
        "mla_q_norm": row(p["mla_q_norm"][l]), "w_uq": w_uq2.astype(BF16),
        "mla_kv_norm": row(p["mla_kv_norm"][l]),
        "w_uk": w_uk2.astype(BF16), "p_kr": p_kr.astype(BF16), "w_uv": p["w_uv"][l].astype(BF16),
        "slopes": slopes,
        "diff_lq1": row(p["diff_lq1"][l]), "diff_lk1": row(p["diff_lk1"][l]),
        "diff_lq2": row(p["diff_lq2"][l]), "diff_lk2": row(p["diff_lk2"][l]),
        "diff_subln2": jnp.concatenate([sub, sub]).reshape(1, -1),
        "norm_mem": row(p["norm_mem"][l]), "w_mem_kv": p["w_mem_kv"][l].astype(BF16),
        "w_gate": p["w_gate"][l].astype(BF16), "b_gate": row(p["b_gate"][l]),
        "w_br_a": p["w_br_a"][l].astype(BF16), "w_br_b": p["w_br_b"][l].astype(BF16),
        "w_br_m": p["w_br_m"][l].astype(BF16), "w_o": p["w_o"][l].astype(BF16),
        "norm_ffn": row(p["norm_ffn"][l]),
        "peer_wq": p["peer_wq"][l].astype(BF16),
        "peer_subkeys": p["peer_subkeys"][l].reshape(PEER_HEADS * 2, PEER_NKEYS, PEER_HALF).astype(BF16),
        "peer_u": p["peer_u"][l].astype(BF16), "peer_vt": p["peer_v"][l].T.astype(BF16),
        "norm_final": row(p["norm_final"]),
    }


def _rope_tables(pos):
    half = MLA_ROPE // 2
    inv = 1.0 / (ROPE_THETA ** (jnp.arange(half, dtype=F32) / half))
    ang = pos.astype(F32)[:, None] * inv[None, :]
    cos, sin = jnp.cos(ang), jnp.sin(ang)
    n = pos.shape[0]
    one = jnp.ones((n, MLA_NOPE), F32)
    zq = jnp.zeros((n, HEAD_W - MLA_NOPE - MLA_ROPE), F32)
    cq = jnp.concatenate([one, cos, cos, zq], axis=1) * MLA_SCALE
    sq = jnp.concatenate([0.0 * one, sin, sin, zq], axis=1) * MLA_SCALE
    zk = jnp.zeros((n, LANE - MLA_ROPE), F32)
    ck = jnp.concatenate([cos, cos, zk], axis=1)
    sk = jnp.concatenate([sin, sin, zk], axis=1)
    return cq, sq, ck, sk


def _layer(x, pos, past, mem_k, mem_v, wp, layer, cfg):
    b, s, d = x.shape
    n = b * s
    xf = x.reshape(n, d)
    tabs = _rope_tables(pos)
    if s % cfg["tm"] == 0:
        pos_blocks = s // cfg["tm"]
    else:
        tabs = tuple(jnp.tile(t, (b, 1)) for t in tabs)
        pos_blocks = n // cfg["tm"]
    ckv, kr, dk, dv, qm, dqb, dkb, dvb, mqb = _proj_in(xf, tabs, wp, tm=cfg["tm"], pos_blocks=pos_blocks)
    new_rows = (ckv.reshape(b, s, -1), kr.reshape(b, s, -1), dk.reshape(b, s, DIFF_HEADS, DIFF_V),
                dv.reshape(b, s, DIFF_HEADS, DIFF_V))
    r3 = lambda a: a.reshape(b, s, -1)
    if past is None:
        ckv_all, kr_all = ckv, kr
        dk_all, dv_all = r3(dkb), r3(dvb)
        kk = s
        q_off = 0
    else:
        p_len = past[0].shape[1]
        kk = p_len + s
        q_off = p_len
        ckv_all = jnp.concatenate([past[0], r3(ckv)], axis=1).reshape(b * kk, -1)
        kr_all = jnp.concatenate([past[1], r3(kr)], axis=1).reshape(b * kk, -1)
        dk_all = jnp.concatenate([past[2].reshape(b, p_len, -1).astype(BF16), r3(dkb)], axis=1)
        dv_all = jnp.concatenate([past[3].reshape(b, p_len, -1).astype(BF16), r3(dvb)], axis=1)
    k_mla, v_mla = _kv_up(ckv_all, kr_all, wp, tm=cfg["tm_kv"])
    att = dict(tq=cfg["tq"], tk=cfg["tk"] if past is None else kk, q_off=q_off)
    o_a = _mla_attention(r3(qm), k_mla.reshape(b, kk, -1), v_mla.reshape(b, kk, -1), **att)
    lam_init = 0.8 - 0.6 * math.exp(-0.3 * layer)
    o_b = _diff_attention(r3(dqb), dk_all, dv_all, wp, lam_init=lam_init, **att)
    o_m = _mem_attention(r3(mqb), mem_k, mem_v, tq=cfg["tq"])
    x1, hn = _merge(xf, o_a.reshape(n, -1), o_b.reshape(n, -1), o_m.reshape(n, -1), wp, tm=cfg["tm"])
    return x1, hn, new_rows


def _peer_and_norm(x1, hn, wp, cfg):
    thr, e1, s2, e2 = _route(hn, wp, tt=cfg["tt"])
    return _peer(hn, thr, e1, s2, e2, x1, wp, tt=cfg["tt"], te=cfg["te"], ce=cfg["ce"])


_CFG_PROMPT = dict(tm=512, tm_kv=512, tq=512, tk=512, tt=512, te=1024, ce=256)
_CFG_SAMPLE = dict(tm=256, tm_kv=256, tq=32, tk=None, tt=256, te=1024, ce=256)


def kernel(x_prompt, x_sample, cache_mla_ckv, cache_mla_krope, cache_diff_k, cache_diff_v, cache_mem_k, cache_mem_v, mem_prompt, norm_mix, w_in, mla_q_norm, w_uq, mla_kv_norm, w_uk, w_uv, diff_lq1, diff_lk1, diff_lq2, diff_lk2, diff_subln, norm_mem, w_mem_kv, w_br_a, w_br_b, w_br_m, w_gate, b_gate, w_o, norm_ffn, peer_wq, peer_subkeys, peer_u, peer_v, norm_final):
    params = dict(norm_mix=norm_mix, w_in=w_in, mla_q_norm=mla_q_norm, w_uq=w_uq, mla_kv_norm=mla_kv_norm,
                  w_uk=w_uk, w_uv=w_uv, diff_lq1=diff_lq1, diff_lk1=diff_lk1, diff_lq2=diff_lq2,
                  diff_lk2=diff_lk2, diff_subln=diff_subln, norm_mem=norm_mem, w_mem_kv=w_mem_kv,
                  w_br_a=w_br_a, w_br_b=w_br_b, w_br_m=w_br_m, w_gate=w_gate, b_gate=b_gate, w_o=w_o,
                  norm_ffn=norm_ffn, peer_wq=peer_wq, peer_subkeys=peer_subkeys, peer_u=peer_u,
                  peer_v=peer_v, norm_final=norm_final)
    depth = w_in.shape[0]
    assert depth == 1, "the final norm is fused into the last PEER step of a single layer"
    bp, sp, d = x_prompt.shape
    bs, ss, _ = x_sample.shape
    n_mem = mem_prompt.shape[1]
    past_len = cache_mla_ckv.shape[2]
    pos_p = jnp.arange(sp, dtype=jnp.int32)
    pos_s = past_len + jnp.arange(ss, dtype=jnp.int32)

    l = 0
    wp = _prep_weights(l, params)
    mk, mv, mkb, mvb = _mem_kv(mem_prompt.reshape(bp * n_mem, d), wp, tm=n_mem)
    m3 = lambda a, b: a.reshape(b, n_mem, -1)
    x1p, hnp, rows_p = _layer(x_prompt, pos_p, None, m3(mkb, bp), m3(mvb, bp), wp, l, _CFG_PROMPT)
    past = (cache_mla_ckv[l], cache_mla_krope[l], cache_diff_k[l], cache_diff_v[l])
    x1s, hns, rows_s = _layer(x_sample, pos_s, past, m3(cache_mem_k[l].astype(BF16), bs),
                              m3(cache_mem_v[l].astype(BF16), bs), wp, l, _CFG_SAMPLE)
    y_prompt = _peer_and_norm(x1p, hnp, wp, _CFG_PROMPT).reshape(bp, sp, d)
    y_sample = _peer_and_norm(x1s, hns, wp, _CFG_SAMPLE).reshape(bs, ss, d)
    st = lambda a: a[None]
    mem4 = lambda a: a.reshape(bp, n_mem, MEM_HEADS, MEM_DH)[None]
    return (y_prompt, y_sample,
            st(rows_p[0]), st(rows_p[1]), st(rows_p[2]), st(rows_p[3]),
            mem4(mk), mem4(mv),
            st(rows_s[0]), st(rows_s[1]), st(rows_s[2]), st(rows_s[3]))
```

```python
import functools
import math

import jax
import jax.numpy as jnp
from jax import lax
from jax.experimental import pallas as pl
from jax.experimental.pallas import tpu as pltpu

F32 = jnp.float32
BF16 = jnp.bfloat16

CHUNK = 64
CHUNK_SHIFT = 6
EPS = 1e-6
NEG = -1e30
HUGE = 1e30
MLA_HEADS = 8
MLA_Q_LORA = 384
MLA_KV_LORA = 256
MLA_NOPE = 64
MLA_ROPE = 32
MLA_V = 64
ROPE_THETA = 10000.0
LOG2E = math.log2(math.e)
MLA_SCALE = (MLA_NOPE + MLA_ROPE) ** -0.5 * LOG2E
DIFF_HEADS = 8
DIFF_DH = 32
DIFF_V = 2 * DIFF_DH
DIFF_SCALE = DIFF_DH ** -0.5 * LOG2E
MEM_HEADS = 4
MEM_DH = 128
MEM_SCALE = MEM_DH ** -0.5 * LOG2E
PEER_HEADS = 8
PEER_NKEYS = 128
PEER_HALF = 128
PEER_TOPK = 16
LANE = 128
HEAD_W = 128

VMEM_LIMIT = 56 * 1024 * 1024

_C_CQ = 0
_C_CKV = _C_CQ + MLA_Q_LORA
_C_KR = _C_CKV + MLA_KV_LORA
_C_KRR = _C_KR + LANE
_C_DQ = _C_KRR + LANE
_C_DK = _C_DQ + DIFF_HEADS * DIFF_V
_C_DV = _C_DK + DIFF_HEADS * DIFF_V
_C_MQ = _C_DV + DIFF_HEADS * DIFF_V
_C_END = _C_MQ + MEM_HEADS * MEM_DH


def _cparams(sem):
    return pltpu.CompilerParams(dimension_semantics=sem, vmem_limit_bytes=VMEM_LIMIT)


def _rms(x, g):
    return x * lax.rsqrt(jnp.mean(x * x, axis=-1, keepdims=True) + EPS) * g


def _dot(a, b):
    return jnp.dot(a, b, preferred_element_type=F32)


def _dot_nt(a, b):
    return lax.dot_general(a, b, (((1,), (1,)), ((), ())), preferred_element_type=F32)


def _full_spec(shape):
    nd = len(shape)
    return pl.BlockSpec(shape, lambda *_: (0,) * nd)


def _proj_in_kernel(x_ref, g_ref, win_ref, qn_ref, wuq_ref, kvn_ref, cq_ref, sq_ref, ck_ref, sk_ref,
                    ckv_ref, kr_ref, dk_ref, dv_ref, qm_ref, dqb_ref, dkb_ref, dvb_ref, mqb_ref):
    h = _rms(x_ref[...], g_ref[...]).astype(BF16)
    z = _dot(h, win_ref[...])
    cqn = _rms(z[:, _C_CQ:_C_CKV], qn_ref[...]).astype(BF16)
    q2 = _dot(cqn, wuq_ref[...])
    cq = cq_ref[...]
    sq = sq_ref[...]
    nq = MLA_HEADS * HEAD_W
    for hh in range(MLA_HEADS):
        lo = hh * HEAD_W
        qm_ref[:, lo:lo + HEAD_W] = (q2[:, lo:lo + HEAD_W] * cq
                                     + q2[:, nq + lo:nq + lo + HEAD_W] * sq).astype(BF16)
    ckv_ref[...] = _rms(z[:, _C_CKV:_C_KR], kvn_ref[...])
    kr = z[:, _C_KR:_C_KRR] * ck_ref[...] + z[:, _C_KRR:_C_DQ] * sk_ref[...]
    kr_ref[...] = kr[:, :MLA_ROPE]
    dk = z[:, _C_DK:_C_DV]
    dv = z[:, _C_DV:_C_MQ]
    dk_ref[...] = dk
    dv_ref[...] = dv
    dkb_ref[...] = dk.astype(BF16)
    dvb_ref[...] = dv.astype(BF16)
    dqb_ref[...] = (z[:, _C_DQ:_C_DK] * DIFF_SCALE).astype(BF16)
    mqb_ref[...] = (z[:, _C_MQ:_C_END] * MEM_SCALE).astype(BF16)


def _proj_in(x, tabs, wp, *, tm, pos_blocks):
    n, d = x.shape
    grid = (n // tm,)
    row = lambda w: pl.BlockSpec((tm, w), lambda i: (i, 0))
    tab = pl.BlockSpec((tm, LANE), lambda i: (i % pos_blocks, 0))
    dw = DIFF_HEADS * DIFF_V
    out_shape = (
        jax.ShapeDtypeStruct((n, MLA_KV_LORA), F32),
        jax.ShapeDtypeStruct((n, MLA_ROPE), F32),
        jax.ShapeDtypeStruct((n, dw), F32),
        jax.ShapeDtypeStruct((n, dw), F32),
        jax.ShapeDtypeStruct((n, MLA_HEADS * HEAD_W), BF16),
        jax.ShapeDtypeStruct((n, dw), BF16),
        jax.ShapeDtypeStruct((n, dw), BF16),
        jax.ShapeDtypeStruct((n, dw), BF16),
        jax.ShapeDtypeStruct((n, MEM_HEADS * MEM_DH), BF16),
    )
    return pl.pallas_call(
        _proj_in_kernel,
        grid=grid,
        in_specs=[row(d), _full_spec((1, d)), _full_spec(wp["w_in"].shape), _full_spec((1, MLA_Q_LORA)),
                  _full_spec(wp["w_uq"].shape), _full_spec((1, MLA_KV_LORA)), tab, tab, tab, tab],
        out_specs=(row(MLA_KV_LORA), row(MLA_ROPE), row(dw), row(dw), row(MLA_HEADS * HEAD_W),
                   row(dw), row(dw), row(dw), row(MEM_HEADS * MEM_DH)),
        out_shape=out_shape,
        compiler_params=_cparams(("parallel",)),
        name="proj_in",
    )(x, wp["norm_mix"], wp["w_in"], wp["mla_q_norm"], wp["w_uq"], wp["mla_kv_norm"], *tabs)


def _kv_up_kernel(ckv_ref, kr_ref, wuk_ref, pk_ref, wuv_ref, k_ref, v_ref):
    c = ckv_ref[...].astype(BF16)
    k = _dot(c, wuk_ref[...]) + _dot(kr_ref[...].astype(BF16), pk_ref[...])
    k_ref[...] = k.astype(BF16)
    v_ref[...] = _dot(c, wuv_ref[...]).astype(BF16)


def _kv_up(ckv, kr, wp, *, tm):
    n = ckv.shape[0]
    row = lambda w: pl.BlockSpec((tm, w), lambda i: (i, 0))
    return pl.pallas_call(
        _kv_up_kernel,
        grid=(n // tm,),
        in_specs=[row(MLA_KV_LORA), row(MLA_ROPE), _full_spec(wp["w_uk"].shape),
                  _full_spec(wp["p_kr"].shape), _full_spec(wp["w_uv"].shape)],
        out_specs=(row(MLA_HEADS * HEAD_W), row(MLA_HEADS * MLA_V)),
        out_shape=(jax.ShapeDtypeStruct((n, MLA_HEADS * HEAD_W), BF16),
                   jax.ShapeDtypeStruct((n, MLA_HEADS * MLA_V), BF16)),
        compiler_params=_cparams(("parallel",)),
        name="kv_up",
    )(ckv, kr, wp["w_uk"], wp["p_kr"], wp["w_uv"])


def _lanes(col, n):
    if n == LANE:
        return col
    if n % LANE == 0:
        return jnp.concatenate([col] * (n // LANE), axis=1)
    return jnp.broadcast_to(col[:, :1], (col.shape[0], n))


GROUP_VREGS = 16


def _softmax_block(score_group, tq, tk, row0, m_scr, al_scr, p_scr, l_scr=None):
    rg = min(tq, max(16, (GROUP_VREGS * 8 * LANE // tk) // 16 * 16))
    for g in range(tq // rg):
        r = slice(row0 + g * rg, row0 + (g + 1) * rg)
        m_prev = m_scr[r]
        m_next = jnp.maximum(m_prev, jnp.max(score_group(g, rg), axis=1, keepdims=True))
        al_scr[r] = jnp.exp2(m_prev - m_next)
        m_scr[r] = m_next
    for g in range(tq // rg):
        r = slice(row0 + g * rg, row0 + (g + 1) * rg)
        e = jnp.exp2(score_group(g, rg) - _lanes(m_scr[r], tk))
        p_scr[r] = e.astype(BF16)
        if l_scr is not None:
            l_scr[r] = al_scr[r] * l_scr[r] + jnp.sum(e, axis=1, keepdims=True)


def _accumulate(rows, v, al_scr, l_scr, acc_scr, p_scr):
    pv = _dot(p_scr[rows], jnp.concatenate([v, jnp.ones(v.shape, v.dtype)], axis=1))
    w = v.shape[1]
    acc_scr[rows] = acc_scr[rows] * al_scr[rows] + pv[:, :w]
    l_scr[rows] = l_scr[rows] * al_scr[rows] + pv[:, w:]


def _chunk_visible(q_lo, k_lo, g, rg, tk):
    qp = q_lo + g * rg + lax.broadcasted_iota(jnp.int32, (rg, tk), 0)
    kp = k_lo + lax.broadcasted_iota(jnp.int32, (rg, tk), 1)
    return qp, kp, (kp >> CHUNK_SHIFT) <= (qp >> CHUNK_SHIFT)


def _init_stats(m_scr, l_scr, acc_scr):
    m_scr[...] = jnp.full(m_scr.shape, NEG, F32)
    l_scr[...] = jnp.zeros(l_scr.shape, F32)
    acc_scr[...] = jnp.zeros(acc_scr.shape, F32)


def _block_range(qi, *, tq, tk, q_off, nk):
    q_lo = q_off + qi * tq
    q_hi = q_lo + (tq - 1)
    n_behind = jnp.minimum((q_lo + 1) // tk, nk)
    last_key = ((q_hi >> CHUNK_SHIFT) << CHUNK_SHIFT) + (CHUNK - 1)
    n_need = jnp.minimum(last_key // tk + 1, nk)
    return q_lo, n_behind, n_need


def _for_blocks(lo, hi, fn):
    lax.fori_loop(lo, hi, lambda i, c: (fn(i), c)[1], 0)


def _attn_specs(b, pairs, sq, sk, tq, qw, kw, vw):
    grid = (b, pairs, sq // tq)
    in_specs = [pl.BlockSpec((1, tq, qw), lambda bi, p, qi: (bi, qi, p)),
                pl.BlockSpec((1, sk, kw), lambda bi, p, qi: (bi, 0, p)),
                pl.BlockSpec((1, sk, vw), lambda bi, p, qi: (bi, 0, p))]
    out_spec = pl.BlockSpec((1, tq, vw), lambda bi, p, qi: (bi, qi, p))
    return grid, in_specs, out_spec


def _mla_kernel(q_ref, k_ref, v_ref, o_ref, m_scr, l_scr, al_scr, acc_scr, p_scr, *, tq, tk, q_off, nk):
    qi = pl.program_id(2)
    _init_stats(m_scr, l_scr, acc_scr)
    q_lo, n_behind, n_need = _block_range(qi, tq=tq, tk=tk, q_off=q_off, nk=nk)

    def block(i, masked):
        k_lo = pl.multiple_of(i * tk, tk)
        for hh in range(2):
            hs = slice(hh * HEAD_W, (hh + 1) * HEAD_W)
            s = _dot_nt(q_ref[0, :, hs], k_ref[0, pl.ds(k_lo, tk), hs])

            def score_group(g, rg, s=s):
                sg = s[g * rg:(g + 1) * rg]
                if masked:
                    sg = jnp.where(_chunk_visible(q_lo, k_lo, g, rg, tk)[2], sg, NEG)
                return sg

            _softmax_block(score_group, tq, tk, hh * tq, m_scr, al_scr, p_scr)
        _accumulate(slice(0, 2 * tq), v_ref[0, pl.ds(k_lo, tk), :], al_scr, l_scr, acc_scr, p_scr)

    _for_blocks(0, n_behind, lambda i: block(i, False))
    _for_blocks(n_behind, n_need, lambda i: block(i, True))

    lane = lax.broadcasted_iota(jnp.int32, (tq, LANE), 1)
    o0 = acc_scr[0:tq] / l_scr[0:tq]
    o1 = acc_scr[tq:2 * tq] / l_scr[tq:2 * tq]
    o_ref[0] = jnp.where(lane < MLA_V, o0, o1).astype(o_ref.dtype)


def _mla_attention(q, k, v, *, tq, tk, q_off):
    b, sq, _ = q.shape
    sk = k.shape[1]
    grid, in_specs, out_spec = _attn_specs(b, MLA_HEADS // 2, sq, sk, tq, 2 * HEAD_W, 2 * HEAD_W, 2 * MLA_V)
    kern = functools.partial(_mla_kernel, tq=tq, tk=tk, q_off=q_off, nk=sk // tk)
    stat = pltpu.VMEM((2 * tq, LANE), F32)
    return pl.pallas_call(
        kern,
        grid=grid,
        in_specs=in_specs,
        out_specs=out_spec,
        out_shape=jax.ShapeDtypeStruct((b, sq, MLA_HEADS * MLA_V), BF16),
        scratch_shapes=[stat, stat, stat, stat, pltpu.VMEM((2 * tq, tk), BF16)],
        compiler_params=_cparams(("parallel", "parallel", "parallel")),
        name="mla_attn",
    )(q, k, v)


def _diff_kernel(slope_ref, lq1_ref, lk1_ref, lq2_ref, lk2_ref, sub_ref, q_ref, k_ref, v_ref, o_ref,
                 qs_scr, m_scr, l_scr, al_scr, acc_scr, p_scr, *, tq, tk, q_off, nk, lam_init):
    pr = pl.program_id(1)
    qi = pl.program_id(2)
    _init_stats(m_scr, l_scr, acc_scr)
    q_lo, n_behind, n_need = _block_range(qi, tq=tq, tk=tk, q_off=q_off, nk=nk)

    q = q_ref[0]
    lane = lax.broadcasted_iota(jnp.int32, (tq, LANE), 1)
    for mi in range(4):
        lo = mi * DIFF_DH
        qs_scr[mi * tq:(mi + 1) * tq] = jnp.where((lane >= lo) & (lane < lo + DIFF_DH), q, jnp.zeros_like(q))

    def block(i, masked):
        k_lo = pl.multiple_of(i * tk, tk)
        s_all = _dot_nt(qs_scr[...], k_ref[0, pl.ds(k_lo, tk), :])
        kpos = (k_lo + lax.broadcasted_iota(jnp.int32, (1, tk), 1)).astype(F32)
        for hh in range(2):
            slope = slope_ref[2 * pr + hh] * LOG2E
            for j in range(2):
                row0 = (2 * hh + j) * tq

                def score_group(g, rg, row0=row0, slope=slope):
                    sg = s_all[row0 + g * rg:row0 + (g + 1) * rg]
                    if masked:
                        qp, kp, vis = _chunk_visible(q_lo, k_lo, g, rg, tk)
                        return jnp.where(vis, sg + slope * jnp.minimum(kp, 2 * qp - kp).astype(F32), NEG)
                    return sg + slope * kpos

                _softmax_block(score_group, tq, tk, row0, m_scr, al_scr, p_scr, l_scr)
            rows = slice(2 * hh * tq, (2 * hh + 2) * tq)
            acc_scr[rows] = acc_scr[rows] * al_scr[rows] + _dot(p_scr[rows], v_ref[0, pl.ds(k_lo, tk), :])

    _for_blocks(0, n_behind, lambda i: block(i, False))
    _for_blocks(n_behind, n_need, lambda i: block(i, True))

    lam = (jnp.exp(jnp.sum(lq1_ref[...] * lk1_ref[...], axis=1, keepdims=True))
           - jnp.exp(jnp.sum(lq2_ref[...] * lk2_ref[...], axis=1, keepdims=True)) + lam_init)
    first = lane < DIFF_V
    on = [acc_scr[mi * tq:(mi + 1) * tq] / l_scr[mi * tq:(mi + 1) * tq] for mi in range(4)]
    o = jnp.where(first, on[0] - lam * on[1], on[2] - lam * on[3])
    sq = o * o
    ms0 = jnp.sum(jnp.where(first, sq, 0.0), axis=1, keepdims=True) * (1.0 / DIFF_V)
    ms1 = jnp.sum(jnp.where(first, 0.0, sq), axis=1, keepdims=True) * (1.0 / DIFF_V)
    r = jnp.where(first, lax.rsqrt(ms0 + EPS), lax.rsqrt(ms1 + EPS))
    o_ref[0] = ((o * r * sub_ref[...]) * (1.0 - lam_init)).astype(o_ref.dtype)


def _diff_attention(q, k, v, wp, *, tq, tk, q_off, lam_init):
    b, sq, _ = q.shape
    sk = k.shape[1]
    grid, in_specs, out_spec = _attn_specs(b, DIFF_HEADS // 2, sq, sk, tq, LANE, LANE, LANE)
    kern = functools.partial(_diff_kernel, tq=tq, tk=tk, q_off=q_off, nk=sk // tk, lam_init=lam_init)
    small = lambda w: pl.BlockSpec((1, w), lambda bi, p, qi: (0, 0))
    stat = pltpu.VMEM((4 * tq, LANE), F32)
    return pl.pallas_call(
        kern,
        grid=grid,
        in_specs=[pl.BlockSpec(memory_space=pltpu.SMEM),
                  small(DIFF_DH), small(DIFF_DH), small(DIFF_DH), small(DIFF_DH), small(LANE)] + in_specs,
        out_specs=out_spec,
        out_shape=jax.ShapeDtypeStruct((b, sq, DIFF_HEADS * DIFF_V), BF16),
        scratch_shapes=[pltpu.VMEM((4 * tq, LANE), BF16), stat, stat, stat, stat,
                        pltpu.VMEM((4 * tq, tk), BF16)],
        compiler_params=_cparams(("parallel", "parallel", "parallel")),
        name="diff_attn",
    )(wp["slopes"], wp["diff_lq1"], wp["diff_lk1"], wp["diff_lq2"], wp["diff_lk2"], wp["diff_subln2"],
      q, k, v)


def _mem_attn_kernel(q_ref, k_ref, v_ref, o_ref):
    for hh in range(MEM_HEADS):
        sl = slice(hh * MEM_DH, (hh + 1) * MEM_DH)
        s = _dot_nt(q_ref[0, :, sl], k_ref[0, :, sl])
        p = jnp.exp2(s - jnp.max(s, axis=1, keepdims=True))
        o = _dot(p.astype(BF16), v_ref[0, :, sl]) / jnp.sum(p, axis=1, keepdims=True)
        o_ref[0, :, sl] = o.astype(o_ref.dtype)


def _mem_attention(q, k, v, *, tq):
    b, sq, w = q.shape
    nm = k.shape[1]
    return pl.pallas_call(
        _mem_attn_kernel,
        grid=(b, sq // tq),
        in_specs=[pl.BlockSpec((1, tq, w), lambda bi, qi: (bi, qi, 0)),
                  pl.BlockSpec((1, nm, w), lambda bi, qi: (bi, 0, 0)),
                  pl.BlockSpec((1, nm, w), lambda bi, qi: (bi, 0, 0))],
        out_specs=pl.BlockSpec((1, tq, w), lambda bi, qi: (bi, qi, 0)),
        out_shape=jax.ShapeDtypeStruct((b, sq, w), BF16),
        compiler_params=_cparams(("parallel", "parallel")),
        name="mem_attn",
    )(q, k, v)


def _mem_kv_kernel(x_ref, g_ref, w_ref, k_ref, v_ref, kb_ref, vb_ref):
    h = _rms(x_ref[...], g_ref[...]).astype(BF16)
    kv = _dot(h, w_ref[...])
    w = MEM_HEADS * MEM_DH
    k_ref[...] = kv[:, :w]
    v_ref[...] = kv[:, w:]
    kb_ref[...] = kv[:, :w].astype(BF16)
    vb_ref[...] = kv[:, w:].astype(BF16)


def _mem_kv(mem, wp, *, tm):
    n, d = mem.shape
    w = MEM_HEADS * MEM_DH
    row = lambda c: pl.BlockSpec((tm, c), lambda i: (i, 0))
    return pl.pallas_call(
        _mem_kv_kernel,
        grid=(n // tm,),
        in_specs=[row(d), _full_spec((1, d)), _full_spec(wp["w_mem_kv"].shape)],
        out_specs=(row(w), row(w), row(w), row(w)),
        out_shape=(jax.ShapeDtypeStruct((n, w), F32), jax.ShapeDtypeStruct((n, w), F32),
                   jax.ShapeDtypeStruct((n, w), BF16), jax.ShapeDtypeStruct((n, w), BF16)),
        compiler_params=_cparams(("parallel",)),
        name="mem_kv",
    )(mem, wp["norm_mem"], wp["w_mem_kv"])


def _merge_kernel(x_ref, oa_ref, ob_ref, om_ref, g_ref, wg_ref, bg_ref, wa_ref, wb_ref, wm_ref, wo_ref,
                  gf_ref, x1_ref, hn_ref):
    x = x_ref[...]
    d = x.shape[1]
    h = _rms(x, g_ref[...]).astype(BF16)
    gates = jax.nn.sigmoid(_dot(h, wg_ref[...]) + bg_ref[...])
    merged = (gates[:, :d] * _dot(oa_ref[...], wa_ref[...])
              + gates[:, d:2 * d] * _dot(ob_ref[...], wb_ref[...])
              + gates[:, 2 * d:] * _dot(om_ref[...], wm_ref[...]))
    x1 = x + _dot(merged.astype(BF16), wo_ref[...])
    x1_ref[...] = x1
    hn_ref[...] = _rms(x1, gf_ref[...]).astype(BF16)


def _merge(x, oa, ob, om, wp, *, tm):
    n, d = x.shape
    row = lambda c: pl.BlockSpec((tm, c), lambda i: (i, 0))
    names = ("norm_mix", "w_gate", "b_gate", "w_br_a", "w_br_b", "w_br_m", "w_o", "norm_ffn")
    return pl.pallas_call(
        _merge_kernel,
        grid=(n // tm,),
        in_specs=[row(d), row(oa.shape[1]), row(ob.shape[1]), row(om.shape[1])]
                 + [_full_spec(wp[k].shape) for k in names],
        out_specs=(row(d), row(d)),
        out_shape=(jax.ShapeDtypeStruct((n, d), F32), jax.ShapeDtypeStruct((n, d), BF16)),
        compiler_params=_cparams(("parallel",)),
        name="merge",
    )(x, oa, ob, om, *[wp[k] for k in names])


def _top_values(sc, k):
    vals = []
    cur = sc
    for r in range(k):
        m = jnp.max(cur, axis=0, keepdims=True)
        vals.append(m)
        if r + 1 < k:
            cur = jnp.where(cur == m, NEG, cur)
    return vals


def _stack_rows(rows, sub):
    out = jnp.zeros(sub.shape, F32)
    for r, row in enumerate(rows):
        out = jnp.where(sub == r, row, out)
    return out


def _route_kernel(hn_ref, wq_ref, sk_ref, thr_ref, e1_ref, s2_ref, e2_ref):
    tt = hn_ref.shape[0]
    q = _dot(hn_ref[...], wq_ref[...]).astype(BF16)
    sub = lax.broadcasted_iota(jnp.int32, (PEER_TOPK, tt), 0)
    sub8 = lax.broadcasted_iota(jnp.int32, (8, tt), 0)
    for h in range(PEER_HEADS):
        sc1 = _dot_nt(sk_ref[2 * h], q[:, (2 * h) * PEER_HALF:(2 * h + 1) * PEER_HALF])
        sc2 = _dot_nt(sk_ref[2 * h + 1], q[:, (2 * h + 1) * PEER_HALF:(2 * h + 2) * PEER_HALF])
        t1 = _top_values(sc1, PEER_TOPK)
        t2 = _stack_rows(_top_values(sc2, PEER_TOPK), sub)
        t2h = t2[:8]
        cands = []
        for a in range(PEER_TOPK):
            nb = PEER_TOPK // (a + 1)
            c = t1[a] + (t2 if nb > 8 else t2h)
            if nb < c.shape[0]:
                c = jnp.where((sub if nb > 8 else sub8) < nb, c, NEG)
            cands.append(c)
        cand = jnp.concatenate(cands, axis=0)
        tau = _top_values(cand, PEER_TOPK)[-1]
        top = t1[0] + t2[0:1]
        z = jnp.sum(jnp.where(cand >= tau, jnp.exp(cand - top), 0.0), axis=0, keepdims=True)
        thr = jnp.full(sc1.shape, HUGE, F32)
        for a in range(PEER_TOPK):
            t2a = t2 if cands[a].shape[0] == PEER_TOPK else t2h
            ga = jnp.min(jnp.where(cands[a] >= tau, t2a, HUGE), axis=0, keepdims=True)
            thr = jnp.where(sc1 == t1[a], ga, thr)
        thr_ref[h] = thr
        e1_ref[h] = jnp.exp(sc1 - t1[0]) * (1.0 / z)
        s2_ref[h] = sc2
        e2_ref[h] = jnp.exp(sc2 - t2[0:1])


def _route(hn, wp, *, tt):
    n, d = hn.shape
    big = pl.BlockSpec((PEER_HEADS, PEER_NKEYS, tt), lambda i: (0, 0, i))
    big_shape = jax.ShapeDtypeStruct((PEER_HEADS, PEER_NKEYS, n), F32)
    return pl.pallas_call(
        _route_kernel,
        grid=(n // tt,),
        in_specs=[pl.BlockSpec((tt, d), lambda i: (i, 0)), _full_spec(wp["peer_wq"].shape),
                  _full_spec(wp["peer_subkeys"].shape)],
        out_specs=(big, big, big, big),
        out_shape=(big_shape, big_shape, big_shape, big_shape),
        compiler_params=_cparams(("parallel",)),
        name="peer_route",
    )(hn, wp["peer_wq"], wp["peer_subkeys"])


PAIR = 2 * PEER_NKEYS
SUBROWS = 64


def _peer_kernel(hn_ref, u_ref, vt_ref, thr_ref, e1_ref, s2_ref, e2_ref, x1_ref, gn_ref, y_ref,
                 acc_scr, hw_scr, *, te, ne):
    j = pl.program_id(1)
    tt = hn_ref.shape[0]
    npair = te // PAIR

    def pre_act(p):
        a = _dot_nt(u_ref[p * PAIR:(p + 1) * PAIR, :], hn_ref[...])
        return 0.5 * a * (1.0 + lax.erf(a * math.sqrt(0.5)))

    def down(p):
        return _dot(vt_ref[:, p * PAIR:(p + 1) * PAIR], hw_scr[p * PAIR:(p + 1) * PAIR, :])

    groups = [(tc, sb) for tc in range(tt // LANE) for sb in range(PEER_NKEYS // SUBROWS)]
    act = pre_act(0)
    tot = None
    carry = None
    for p in range(npair):
        nxt = None
        for gi, (tc, sb) in enumerate(groups):
            if gi == 1 and p + 1 < npair:
                nxt = pre_act(p + 1)
            if gi == len(groups) // 2 and p >= 1:
                part = down(p - 1)
                tot = part if tot is None else tot + part
            ls = slice(tc * LANE, (tc + 1) * LANE)
            i2 = slice(sb * SUBROWS, (sb + 1) * SUBROWS)
            w = [None, None] if carry is None else [carry * 0.0, carry * 0.0]
            for h in range(PEER_HEADS):
                s2b = s2_ref[h, i2, ls]
                e2b = e2_ref[h, i2, ls]
                for rr in range(2):
                    i1 = 2 * p + rr
                    cw = jnp.where(s2b >= thr_ref[h, i1:i1 + 1, ls], e2b, 0.0) * e1_ref[h, i1:i1 + 1, ls]
                    w[rr] = cw if w[rr] is None else w[rr] + cw
            carry = w[1]
            for rr in range(2):
                r0 = rr * PEER_NKEYS + sb * SUBROWS
                hw_scr[p * PAIR + r0:p * PAIR + r0 + SUBROWS, ls] = (w[rr] * act[r0:r0 + SUBROWS, ls]).astype(BF16)
        act = nxt
    part = down(npair - 1)
    tot = part if tot is None else tot + part

    @pl.when(j == 0)
    def _():
        acc_scr[...] = tot

    @pl.when(j > 0)
    def _():
        acc_scr[...] += tot

    @pl.when(j == ne - 1)
    def _():
        xr = x1_ref[...] + acc_scr[...].T
        y_ref[...] = _rms(xr, gn_ref[...])


def _peer(hn, thr, e1, s2, e2, x1, wp, *, tt, te):
    n, d = hn.shape
    ne = wp["peer_u"].shape[0] // te
    big = pl.BlockSpec((PEER_HEADS, PEER_NKEYS, tt), lambda i, j: (0, 0, i))
    rows = pl.BlockSpec((PEER_HEADS, te // PEER_NKEYS, tt), lambda i, j: (0, j, i))
    kern = functools.partial(_peer_kernel, te=te, ne=ne)
    return pl.pallas_call(
        kern,
        grid=(n // tt, ne),
        in_specs=[pl.BlockSpec((tt, d), lambda i, j: (i, 0)),
                  pl.BlockSpec((te, d), lambda i, j: (j, 0)),
                  pl.BlockSpec((d, te), lambda i, j: (0, j)),
                  rows, rows, big, big,
                  pl.BlockSpec((tt, d), lambda i, j: (i, 0)),
                  pl.BlockSpec((1, d), lambda i, j: (0, 0))],
        out_specs=pl.BlockSpec((tt, d), lambda i, j: (i, 0)),
        out_shape=jax.ShapeDtypeStruct((n, d), F32),
        scratch_shapes=[pltpu.VMEM((d, tt), F32), pltpu.VMEM((te, tt), BF16)],
        compiler_params=_cparams(("parallel", "arbitrary")),
        name="peer_experts",
    )(hn, wp["peer_u"], wp["peer_vt"], thr, e1, s2, e2, x1, wp["norm_final"])


def _prep_weights(l, p):
    f = lambda a: a.astype(F32)
    w_in = f(p["w_in"][l])
    o_cq, o_ckv, o_kr = 0, MLA_Q_LORA, MLA_Q_LORA + MLA_KV_LORA
    o_dq = o_kr + MLA_ROPE
    half = MLA_ROPE // 2
    d = w_in.shape[0]
    kr = w_in[:, o_kr:o_dq]
    pad = jnp.zeros((d, LANE - MLA_ROPE), F32)
    w_in2 = jnp.concatenate([
        w_in[:, o_cq:o_kr], kr, pad,
        -kr[:, half:], kr[:, :half], pad,
        w_in[:, o_dq:]], axis=1)
    assert w_in2.shape[1] == _C_END

    w_uq = f(p["w_uq"][l]).reshape(MLA_Q_LORA, MLA_HEADS, MLA_NOPE + MLA_ROPE)
    nope, x1, x2 = w_uq[..., :MLA_NOPE], w_uq[..., MLA_NOPE:MLA_NOPE + half], w_uq[..., MLA_NOPE + half:]
    zpad = jnp.zeros((MLA_Q_LORA, MLA_HEADS, HEAD_W - MLA_NOPE - MLA_ROPE), F32)
    q_slab = jnp.concatenate([nope, x1, x2, zpad], axis=-1).reshape(MLA_Q_LORA, -1)
    r_slab = jnp.concatenate([jnp.zeros_like(nope), -x2, x1, zpad], axis=-1).reshape(MLA_Q_LORA, -1)
    w_uq2 = jnp.concatenate([q_slab, r_slab], axis=1)

    w_uk = f(p["w_uk"][l]).reshape(MLA_KV_LORA, MLA_HEADS, MLA_NOPE)
    w_uk2 = jnp.concatenate([w_uk, jnp.zeros((MLA_KV_LORA, MLA_HEADS, HEAD_W - MLA_NOPE), F32)],
                            axis=-1).reshape(MLA_KV_LORA, -1)
    r_idx = jnp.arange(MLA_ROPE)
    cols = jnp.arange(MLA_HEADS * HEAD_W)
    p_kr = ((cols[None, :] % HEAD_W) == (MLA_NOPE + r_idx[:, None])).astype(F32)

    slopes = jnp.exp2(-8.0 * jnp.arange(1, DIFF_HEADS + 1, dtype=F32) / DIFF_HEADS)
    sub = f(p["diff_subln"][l])
    row = lambda a: f(a).reshape(1, -1)
    return {
        "norm_mix": row(p["norm_mix"][l]), "w_in": w_in2.astype(BF16),
        "mla_q_norm": row(p["mla_q_norm"][l]), "w_uq": w_uq2.astype(BF16),
        "mla_kv_norm": row(p["mla_kv_norm"][l]),
        "w_uk": w_uk2.astype(BF16), "p_kr": p_kr.astype(BF16), "w_uv": p["w_uv"][l].astype(BF16),
        "slopes": slopes,
        "diff_lq1": row(p["diff_lq1"][l]), "diff_lk1": row(p["diff_lk1"][l]),
        "diff_lq2": row(p["diff_lq2"][l]), "diff_lk2": row(p["diff_lk2"][l]),
        "diff_subln2": jnp.concatenate([sub, sub]).reshape(1, -1),
        "norm_mem": row(p["norm_mem"][l]), "w_mem_kv": p["w_mem_kv"][l].astype(BF16),
        "w_gate": p["w_gate"][l].astype(BF16), "b_gate": row(p["b_gate"][l]),
        "w_br_a": p["w_br_a"][l].astype(BF16), "w_br_b": p["w_br_b"][l].astype(BF16),
        "w_br_m": p["w_br_m"][l].astype(BF16), "w_o": p["w_o"][l].astype(BF16),
        "norm_ffn": row(p["norm_ffn"][l]),
        "peer_wq": p["peer_wq"][l].astype(BF16),
        "peer_subkeys": p["peer_subkeys"][l].reshape(PEER_HEADS * 2, PEER_NKEYS, PEER_HALF).astype(BF16),
        "peer_u": p["peer_u"][l].astype(BF16), "peer_vt": p["peer_v"][l].T.astype(BF16),
        "norm_final": row(p["norm_final"]),
    }


def _rope_tables(pos):
    half = MLA_ROPE // 2
    inv = 1.0 / (ROPE_THETA ** (jnp.arange(half, dtype=F32) / half))
    ang = pos.astype(F32)[:, None] * inv[None, :]
    cos, sin = jnp.cos(ang), jnp.sin(ang)
    n = pos.shape[0]
    one = jnp.ones((n, MLA_NOPE), F32)
    zq = jnp.zeros((n, HEAD_W - MLA_NOPE - MLA_ROPE), F32)
    cq = jnp.concatenate([one, cos, cos, zq], axis=1) * MLA_SCALE
    sq = jnp.concatenate([0.0 * one, sin, sin, zq], axis=1) * MLA_SCALE
    zk = jnp.zeros((n, LANE - MLA_ROPE), F32)
    ck = jnp.concatenate([cos, cos, zk], axis=1)
    sk = jnp.concatenate([sin, sin, zk], axis=1)
    return cq, sq, ck, sk


def _layer(x, pos, past, mem_k, mem_v, wp, layer, cfg):
    b, s, d = x.shape
    n = b * s
    xf = x.reshape(n, d)
    tabs = _rope_tables(pos)
    if s % cfg["tm"] == 0:
        pos_blocks = s // cfg["tm"]
    else:
        tabs = tuple(jnp.tile(t, (b, 1)) for t in tabs)
        pos_blocks = n // cfg["tm"]
    ckv, kr, dk, dv, qm, dqb, dkb, dvb, mqb = _proj_in(xf, tabs, wp, tm=cfg["tm"], pos_blocks=pos_blocks)
    new_rows = (ckv.reshape(b, s, -1), kr.reshape(b, s, -1), dk.reshape(b, s, DIFF_HEADS, DIFF_V),
                dv.reshape(b, s, DIFF_HEADS, DIFF_V))
    r3 = lambda a: a.reshape(b, s, -1)
    if past is None:
        ckv_all, kr_all = ckv, kr
        dk_all, dv_all = r3(dkb), r3(dvb)
        kk = s
        q_off = 0
    else:
        p_len = past[0].shape[1]
        kk = p_len + s
        q_off = p_len
        ckv_all = jnp.concatenate([past[0], r3(ckv)], axis=1).reshape(b * kk, -1)
        kr_all = jnp.concatenate([past[1], r3(kr)], axis=1).reshape(b * kk, -1)
        dk_all = jnp.concatenate([past[2].reshape(b, p_len, -1).astype(BF16), r3(dkb)], axis=1)
        dv_all = jnp.concatenate([past[3].reshape(b, p_len, -1).astype(BF16), r3(dvb)], axis=1)
    k_mla, v_mla = _kv_up(ckv_all, kr_all, wp, tm=cfg["tm_kv"])
    att = dict(tq=cfg["tq"], tk=cfg["tk"] if past is None else kk, q_off=q_off)
    o_a = _mla_attention(r3(qm), k_mla.reshape(b, kk, -1), v_mla.reshape(b, kk, -1), **att)
    lam_init = 0.8 - 0.6 * math.exp(-0.3 * layer)
    o_b = _diff_attention(r3(dqb), dk_all, dv_all, wp, lam_init=lam_init, **att)
    o_m = _mem_attention(r3(mqb), mem_k, mem_v, tq=cfg["tq"])
    x1, hn = _merge(xf, o_a.reshape(n, -1), o_b.reshape(n, -1), o_m.reshape(n, -1), wp, tm=cfg["tm"])
    return x1, hn, new_rows


def _peer_and_norm(x1, hn, wp, cfg):
    thr, e1, s2, e2 = _route(hn, wp, tt=cfg["tt"])
    return _peer(hn, thr, e1, s2, e2, x1, wp, tt=cfg["tt"], te=cfg["te"])


_CFG_PROMPT = dict(tm=512, tm_kv=512, tq=512, tk=512, tt=512, te=2048)
_CFG_SAMPLE = dict(tm=256, tm_kv=256, tq=32, tk=None, tt=256, te=2048)


def kernel(x_prompt, x_sample, cache_mla_ckv, cache_mla_krope, cache_diff_k, cache_diff_v, cache_mem_k, cache_mem_v, mem_prompt, norm_mix, w_in, mla_q_norm, w_uq, mla_kv_norm, w_uk, w_uv, diff_lq1, diff_lk1, diff_lq2, diff_lk2, diff_subln, norm_mem, w_mem_kv, w_br_a, w_br_b, w_br_m, w_gate, b_gate, w_o, norm_ffn, peer_wq, peer_subkeys, peer_u, peer_v, norm_final):
    params = dict(norm_mix=norm_mix, w_in=w_in, mla_q_norm=mla_q_norm, w_uq=w_uq, mla_kv_norm=mla_kv_norm,
                  w_uk=w_uk, w_uv=w_uv, diff_lq1=diff_lq1, diff_lk1=diff_lk1, diff_lq2=diff_lq2,
                  diff_lk2=diff_lk2, diff_subln=diff_subln, norm_mem=norm_mem, w_mem_kv=w_mem_kv,
                  w_br_a=w_br_a, w_br_b=w_br_b, w_br_m=w_br_m, w_gate=w_gate, b_gate=b_gate, w_o=w_o,
                  norm_ffn=norm_ffn, peer_wq=peer_wq, peer_subkeys=peer_subkeys, peer_u=peer_u,
                  peer_v=peer_v, norm_final=norm_final)
    depth = w_in.shape[0]
    assert depth == 1, "the final norm is fused into the last PEER step of a single layer"
    bp, sp, d = x_prompt.shape
    bs, ss, _ = x_sample.shape
    n_mem = mem_prompt.shape[1]
    past_len = cache_mla_ckv.shape[2]
    pos_p = jnp.arange(sp, dtype=jnp.int32)
    pos_s = past_len + jnp.arange(ss, dtype=jnp.int32)

    l = 0
    wp = _prep_weights(l, params)
    mk, mv, mkb, mvb = _mem_kv(mem_prompt.reshape(bp * n_mem, d), wp, tm=n_mem)
    m3 = lambda a, b: a.reshape(b, n_mem, -1)
    x1p, hnp, rows_p = _layer(x_prompt, pos_p, None, m3(mkb, bp), m3(mvb, bp), wp, l, _CFG_PROMPT)
    past = (cache_mla_ckv[l], cache_mla_krope[l], cache_diff_k[l], cache_diff_v[l])
    x1s, hns, rows_s = _layer(x_sample, pos_s, past, m3(cache_mem_k[l].astype(BF16), bs),
                              m3(cache_mem_v[l].astype(BF16), bs), wp, l, _CFG_SAMPLE)
    y_prompt = _peer_and_norm(x1p, hnp, wp, _CFG_PROMPT).reshape(bp, sp, d)
    y_sample = _peer_and_norm(x1s, hns, wp, _CFG_SAMPLE).reshape(bs, ss, d)
    st = lambda a: a[None]
    mem4 = lambda a: a.reshape(bp, n_mem, MEM_HEADS, MEM_DH)[None]
    return (y_prompt, y_sample,
            st(rows_p[0]), st(rows_p[1]), st(rows_p[2]), st(rows_p[3]),
            mem4(mk), mem4(mv),
            st(rows_s[0]), st(rows_s[1]), st(rows_s[2]), st(rows_s[3]))
```

```python
import functools
import math

import jax
import jax.numpy as jnp
from jax import lax
from jax.experimental import pallas as pl
from jax.experimental.pallas import tpu as pltpu

F32 = jnp.float32
BF16 = jnp.bfloat16

CHUNK = 64
CHUNK_SHIFT = 6
EPS = 1e-6
NEG = -1e30
HUGE = 1e30
MLA_HEADS = 8
MLA_Q_LORA = 384
MLA_KV_LORA = 256
MLA_NOPE = 64
MLA_ROPE = 32
MLA_V = 64
ROPE_THETA = 10000.0
LOG2E = math.log2(math.e)
MLA_SCALE = (MLA_NOPE + MLA_ROPE) ** -0.5 * LOG2E
DIFF_HEADS = 8
DIFF_DH = 32
DIFF_V = 2 * DIFF_DH
DIFF_SCALE = DIFF_DH ** -0.5 * LOG2E
MEM_HEADS = 4
MEM_DH = 128
MEM_SCALE = MEM_DH ** -0.5 * LOG2E
PEER_HEADS = 8
PEER_NKEYS = 128
PEER_HALF = 128
PEER_TOPK = 16
LANE = 128
HEAD_W = 128

VMEM_LIMIT = 56 * 1024 * 1024

_C_CQ = 0
_C_CKV = _C_CQ + MLA_Q_LORA
_C_KR = _C_CKV + MLA_KV_LORA
_C_KRR = _C_KR + LANE
_C_DQ = _C_KRR + LANE
_C_DK = _C_DQ + DIFF_HEADS * DIFF_V
_C_DV = _C_DK + DIFF_HEADS * DIFF_V
_C_MQ = _C_DV + DIFF_HEADS * DIFF_V
_C_END = _C_MQ + MEM_HEADS * MEM_DH


def _cparams(sem):
    return pltpu.CompilerParams(dimension_semantics=sem, vmem_limit_bytes=VMEM_LIMIT)


def _rms(x, g):
    return x * lax.rsqrt(jnp.mean(x * x, axis=-1, keepdims=True) + EPS) * g


def _dot(a, b):
    return jnp.dot(a, b, preferred_element_type=F32)


def _dot_nt(a, b):
    return lax.dot_general(a, b, (((1,), (1,)), ((), ())), preferred_element_type=F32)


def _full_spec(shape):
    nd = len(shape)
    return pl.BlockSpec(shape, lambda *_: (0,) * nd)


def _proj_in_kernel(x_ref, g_ref, win_ref, qn_ref, wuq_ref, kvn_ref, cq_ref, sq_ref, ck_ref, sk_ref,
                    ckv_ref, kr_ref, dk_ref, dv_ref, qm_ref, dqb_ref, dkb_ref, dvb_ref, mqb_ref):
    h = _rms(x_ref[...], g_ref[...]).astype(BF16)
    z = _dot(h, win_ref[...])
    cqn = _rms(z[:, _C_CQ:_C_CKV], qn_ref[...]).astype(BF16)
    q2 = _dot(cqn, wuq_ref[...])
    cq = cq_ref[...]
    sq = sq_ref[...]
    nq = MLA_HEADS * HEAD_W
    for hh in range(MLA_HEADS):
        lo = hh * HEAD_W
        qm_ref[:, lo:lo + HEAD_W] = (q2[:, lo:lo + HEAD_W] * cq
                                     + q2[:, nq + lo:nq + lo + HEAD_W] * sq).astype(BF16)
    ckv_ref[...] = _rms(z[:, _C_CKV:_C_KR], kvn_ref[...])
    kr = z[:, _C_KR:_C_KRR] * ck_ref[...] + z[:, _C_KRR:_C_DQ] * sk_ref[...]
    kr_ref[...] = kr[:, :MLA_ROPE]
    dk = z[:, _C_DK:_C_DV]
    dv = z[:, _C_DV:_C_MQ]
    dk_ref[...] = dk
    dv_ref[...] = dv
    dkb_ref[...] = dk.astype(BF16)
    dvb_ref[...] = dv.astype(BF16)
    dqb_ref[...] = (z[:, _C_DQ:_C_DK] * DIFF_SCALE).astype(BF16)
    mqb_ref[...] = (z[:, _C_MQ:_C_END] * MEM_SCALE).astype(BF16)


def _proj_in(x, tabs, wp, *, tm, pos_blocks):
    n, d = x.shape
    grid = (n // tm,)
    row = lambda w: pl.BlockSpec((tm, w), lambda i: (i, 0))
    tab = pl.BlockSpec((tm, LANE), lambda i: (i % pos_blocks, 0))
    dw = DIFF_HEADS * DIFF_V
    out_shape = (
        jax.ShapeDtypeStruct((n, MLA_KV_LORA), F32),
        jax.ShapeDtypeStruct((n, MLA_ROPE), F32),
        jax.ShapeDtypeStruct((n, dw), F32),
        jax.ShapeDtypeStruct((n, dw), F32),
        jax.ShapeDtypeStruct((n, MLA_HEADS * HEAD_W), BF16),
        jax.ShapeDtypeStruct((n, dw), BF16),
        jax.ShapeDtypeStruct((n, dw), BF16),
        jax.ShapeDtypeStruct((n, dw), BF16),
        jax.ShapeDtypeStruct((n, MEM_HEADS * MEM_DH), BF16),
    )
    return pl.pallas_call(
        _proj_in_kernel,
        grid=grid,
        in_specs=[row(d), _full_spec((1, d)), _full_spec(wp["w_in"].shape), _full_spec((1, MLA_Q_LORA)),
                  _full_spec(wp["w_uq"].shape), _full_spec((1, MLA_KV_LORA)), tab, tab, tab, tab],
        out_specs=(row(MLA_KV_LORA), row(MLA_ROPE), row(dw), row(dw), row(MLA_HEADS * HEAD_W),
                   row(dw), row(dw), row(dw), row(MEM_HEADS * MEM_DH)),
        out_shape=out_shape,
        compiler_params=_cparams(("parallel",)),
        name="proj_in",
    )(x, wp["norm_mix"], wp["w_in"], wp["mla_q_norm"], wp["w_uq"], wp["mla_kv_norm"], *tabs)


def _kv_up_kernel(ckv_ref, kr_ref, wuk_ref, pk_ref, wuv_ref, k_ref, v_ref):
    c = ckv_ref[...].astype(BF16)
    k = _dot(c, wuk_ref[...]) + _dot(kr_ref[...].astype(BF16), pk_ref[...])
    k_ref[...] = k.astype(BF16)
    v_ref[...] = _dot(c, wuv_ref[...]).astype(BF16)


def _kv_up(ckv, kr, wp, *, tm):
    n = ckv.shape[0]
    row = lambda w: pl.BlockSpec((tm, w), lambda i: (i, 0))
    return pl.pallas_call(
        _kv_up_kernel,
        grid=(n // tm,),
        in_specs=[row(MLA_KV_LORA), row(MLA_ROPE), _full_spec(wp["w_uk"].shape),
                  _full_spec(wp["p_kr"].shape), _full_spec(wp["w_uv"].shape)],
        out_specs=(row(MLA_HEADS * HEAD_W), row(MLA_HEADS * MLA_V)),
        out_shape=(jax.ShapeDtypeStruct((n, MLA_HEADS * HEAD_W), BF16),
                   jax.ShapeDtypeStruct((n, MLA_HEADS * MLA_V), BF16)),
        compiler_params=_cparams(("parallel",)),
        name="kv_up",
    )(ckv, kr, wp["w_uk"], wp["p_kr"], wp["w_uv"])


def _lanes(col, n):
    if n == LANE:
        return col
    if n % LANE == 0:
        return jnp.concatenate([col] * (n // LANE), axis=1)
    return jnp.broadcast_to(col[:, :1], (col.shape[0], n))


GROUP_VREGS = 16


def _softmax_block(score_group, tq, tk, row0, m_scr, al_scr, p_scr, l_scr=None):
    rg = min(tq, max(16, (GROUP_VREGS * 8 * LANE // tk) // 16 * 16))
    for g in range(tq // rg):
        r = slice(row0 + g * rg, row0 + (g + 1) * rg)
        m_prev = m_scr[r]
        m_next = jnp.maximum(m_prev, jnp.max(score_group(g, rg), axis=1, keepdims=True))
        al_scr[r] = jnp.exp2(m_prev - m_next)
        m_scr[r] = m_next
    for g in range(tq // rg):
        r = slice(row0 + g * rg, row0 + (g + 1) * rg)
        e = jnp.exp2(score_group(g, rg) - _lanes(m_scr[r], tk))
        p_scr[r] = e.astype(BF16)
        if l_scr is not None:
            l_scr[r] = al_scr[r] * l_scr[r] + jnp.sum(e, axis=1, keepdims=True)


def _accumulate(rows, v, al_scr, l_scr, acc_scr, p_scr):
    pv = _dot(p_scr[rows], jnp.concatenate([v, jnp.ones(v.shape, v.dtype)], axis=1))
    w = v.shape[1]
    acc_scr[rows] = acc_scr[rows] * al_scr[rows] + pv[:, :w]
    l_scr[rows] = l_scr[rows] * al_scr[rows] + pv[:, w:]


def _chunk_visible(q_lo, k_lo, g, rg, tk):
    qp = q_lo + g * rg + lax.broadcasted_iota(jnp.int32, (rg, tk), 0)
    kp = k_lo + lax.broadcasted_iota(jnp.int32, (rg, tk), 1)
    return qp, kp, (kp >> CHUNK_SHIFT) <= (qp >> CHUNK_SHIFT)


def _init_stats(m_scr, l_scr, acc_scr):
    m_scr[...] = jnp.full(m_scr.shape, NEG, F32)
    l_scr[...] = jnp.zeros(l_scr.shape, F32)
    acc_scr[...] = jnp.zeros(acc_scr.shape, F32)


def _block_range(qi, *, tq, tk, q_off, nk):
    q_lo = q_off + qi * tq
    q_hi = q_lo + (tq - 1)
    n_behind = jnp.minimum((q_lo + 1) // tk, nk)
    last_key = ((q_hi >> CHUNK_SHIFT) << CHUNK_SHIFT) + (CHUNK - 1)
    n_need = jnp.minimum(last_key // tk + 1, nk)
    return q_lo, n_behind, n_need


def _for_blocks(lo, hi, fn):
    lax.fori_loop(lo, hi, lambda i, c: (fn(i), c)[1], 0)


def _attn_specs(b, pairs, sq, sk, tq, qw, kw, vw):
    grid = (b, pairs, sq // tq)
    in_specs = [pl.BlockSpec((1, tq, qw), lambda bi, p, qi: (bi, qi, p)),
                pl.BlockSpec((1, sk, kw), lambda bi, p, qi: (bi, 0, p)),
                pl.BlockSpec((1, sk, vw), lambda bi, p, qi: (bi, 0, p))]
    out_spec = pl.BlockSpec((1, tq, vw), lambda bi, p, qi: (bi, qi, p))
    return grid, in_specs, out_spec


def _mla_kernel(q_ref, k_ref, v_ref, o_ref, m_scr, l_scr, al_scr, acc_scr, p_scr, *, tq, tk, q_off, nk):
    qi = pl.program_id(2)
    _init_stats(m_scr, l_scr, acc_scr)
    q_lo, n_behind, n_need = _block_range(qi, tq=tq, tk=tk, q_off=q_off, nk=nk)

    def block(i, masked):
        k_lo = pl.multiple_of(i * tk, tk)
        for hh in range(2):
            hs = slice(hh * HEAD_W, (hh + 1) * HEAD_W)
            s = _dot_nt(q_ref[0, :, hs], k_ref[0, pl.ds(k_lo, tk), hs])

            def score_group(g, rg, s=s):
                sg = s[g * rg:(g + 1) * rg]
                if masked:
                    sg = jnp.where(_chunk_visible(q_lo, k_lo, g, rg, tk)[2], sg, NEG)
                return sg

            _softmax_block(score_group, tq, tk, hh * tq, m_scr, al_scr, p_scr)
        _accumulate(slice(0, 2 * tq), v_ref[0, pl.ds(k_lo, tk), :], al_scr, l_scr, acc_scr, p_scr)

    _for_blocks(0, n_behind, lambda i: block(i, False))
    _for_blocks(n_behind, n_need, lambda i: block(i, True))

    lane = lax.broadcasted_iota(jnp.int32, (tq, LANE), 1)
    o0 = acc_scr[0:tq] / l_scr[0:tq]
    o1 = acc_scr[tq:2 * tq] / l_scr[tq:2 * tq]
    o_ref[0] = jnp.where(lane < MLA_V, o0, o1).astype(o_ref.dtype)


def _mla_attention(q, k, v, *, tq, tk, q_off):
    b, sq, _ = q.shape
    sk = k.shape[1]
    grid, in_specs, out_spec = _attn_specs(b, MLA_HEADS // 2, sq, sk, tq, 2 * HEAD_W, 2 * HEAD_W, 2 * MLA_V)
    kern = functools.partial(_mla_kernel, tq=tq, tk=tk, q_off=q_off, nk=sk // tk)
    stat = pltpu.VMEM((2 * tq, LANE), F32)
    return pl.pallas_call(
        kern,
        grid=grid,
        in_specs=in_specs,
        out_specs=out_spec,
        out_shape=jax.ShapeDtypeStruct((b, sq, MLA_HEADS * MLA_V), BF16),
        scratch_shapes=[stat, stat, stat, stat, pltpu.VMEM((2 * tq, tk), BF16)],
        compiler_params=_cparams(("parallel", "parallel", "parallel")),
        name="mla_attn",
    )(q, k, v)


def _diff_kernel(slope_ref, lq1_ref, lk1_ref, lq2_ref, lk2_ref, sub_ref, q_ref, k_ref, v_ref, o_ref,
                 qs_scr, m_scr, l_scr, al_scr, acc_scr, p_scr, *, tq, tk, q_off, nk, lam_init):
    pr = pl.program_id(1)
    qi = pl.program_id(2)
    _init_stats(m_scr, l_scr, acc_scr)
    q_lo, n_behind, n_need = _block_range(qi, tq=tq, tk=tk, q_off=q_off, nk=nk)

    q = q_ref[0]
    lane = lax.broadcasted_iota(jnp.int32, (tq, LANE), 1)
    for mi in range(4):
        lo = mi * DIFF_DH
        qs_scr[mi * tq:(mi + 1) * tq] = jnp.where((lane >= lo) & (lane < lo + DIFF_DH), q, jnp.zeros_like(q))

    def block(i, masked):
        k_lo = pl.multiple_of(i * tk, tk)
        s_all = _dot_nt(qs_scr[...], k_ref[0, pl.ds(k_lo, tk), :])
        kpos = (k_lo + lax.broadcasted_iota(jnp.int32, (1, tk), 1)).astype(F32)
        for hh in range(2):
            slope = slope_ref[2 * pr + hh] * LOG2E
            for j in range(2):
                row0 = (2 * hh + j) * tq

                def score_group(g, rg, row0=row0, slope=slope):
                    sg = s_all[row0 + g * rg:row0 + (g + 1) * rg]
                    if masked:
                        qp, kp, vis = _chunk_visible(q_lo, k_lo, g, rg, tk)
                        return jnp.where(vis, sg + slope * jnp.minimum(kp, 2 * qp - kp).astype(F32), NEG)
                    return sg + slope * kpos

                _softmax_block(score_group, tq, tk, row0, m_scr, al_scr, p_scr, l_scr)
            rows = slice(2 * hh * tq, (2 * hh + 2) * tq)
            acc_scr[rows] = acc_scr[rows] * al_scr[rows] + _dot(p_scr[rows], v_ref[0, pl.ds(k_lo, tk), :])

    _for_blocks(0, n_behind, lambda i: block(i, False))
    _for_blocks(n_behind, n_need, lambda i: block(i, True))

    lam = (jnp.exp(jnp.sum(lq1_ref[...] * lk1_ref[...], axis=1, keepdims=True))
           - jnp.exp(jnp.sum(lq2_ref[...] * lk2_ref[...], axis=1, keepdims=True)) + lam_init)
    first = lane < DIFF_V
    on = [acc_scr[mi * tq:(mi + 1) * tq] / l_scr[mi * tq:(mi + 1) * tq] for mi in range(4)]
    o = jnp.where(first, on[0] - lam * on[1], on[2] - lam * on[3])
    sq = o * o
    ms0 = jnp.sum(jnp.where(first, sq, 0.0), axis=1, keepdims=True) * (1.0 / DIFF_V)
    ms1 = jnp.sum(jnp.where(first, 0.0, sq), axis=1, keepdims=True) * (1.0 / DIFF_V)
    r = jnp.where(first, lax.rsqrt(ms0 + EPS), lax.rsqrt(ms1 + EPS))
    o_ref[0] = ((o * r * sub_ref[...]) * (1.0 - lam_init)).astype(o_ref.dtype)


def _diff_attention(q, k, v, wp, *, tq, tk, q_off, lam_init):
    b, sq, _ = q.shape
    sk = k.shape[1]
    grid, in_specs, out_spec = _attn_specs(b, DIFF_HEADS // 2, sq, sk, tq, LANE, LANE, LANE)
    kern = functools.partial(_diff_kernel, tq=tq, tk=tk, q_off=q_off, nk=sk // tk, lam_init=lam_init)
    small = lambda w: pl.BlockSpec((1, w), lambda bi, p, qi: (0, 0))
    stat = pltpu.VMEM((4 * tq, LANE), F32)
    return pl.pallas_call(
        kern,
        grid=grid,
        in_specs=[pl.BlockSpec(memory_space=pltpu.SMEM),
                  small(DIFF_DH), small(DIFF_DH), small(DIFF_DH), small(DIFF_DH), small(LANE)] + in_specs,
        out_specs=out_spec,
        out_shape=jax.ShapeDtypeStruct((b, sq, DIFF_HEADS * DIFF_V), BF16),
        scratch_shapes=[pltpu.VMEM((4 * tq, LANE), BF16), stat, stat, stat, stat,
                        pltpu.VMEM((4 * tq, tk), BF16)],
        compiler_params=_cparams(("parallel", "parallel", "parallel")),
        name="diff_attn",
    )(wp["slopes"], wp["diff_lq1"], wp["diff_lk1"], wp["diff_lq2"], wp["diff_lk2"], wp["diff_subln2"],
      q, k, v)


def _mem_attn_kernel(q_ref, k_ref, v_ref, o_ref):
    for hh in range(MEM_HEADS):
        sl = slice(hh * MEM_DH, (hh + 1) * MEM_DH)
        s = _dot_nt(q_ref[0, :, sl], k_ref[0, :, sl])
        p = jnp.exp2(s - jnp.max(s, axis=1, keepdims=True))
        o = _dot(p.astype(BF16), v_ref[0, :, sl]) / jnp.sum(p, axis=1, keepdims=True)
        o_ref[0, :, sl] = o.astype(o_ref.dtype)


def _mem_attention(q, k, v, *, tq):
    b, sq, w = q.shape
    nm = k.shape[1]
    return pl.pallas_call(
        _mem_attn_kernel,
        grid=(b, sq // tq),
        in_specs=[pl.BlockSpec((1, tq, w), lambda bi, qi: (bi, qi, 0)),
                  pl.BlockSpec((1, nm, w), lambda bi, qi: (bi, 0, 0)),
                  pl.BlockSpec((1, nm, w), lambda bi, qi: (bi, 0, 0))],
        out_specs=pl.BlockSpec((1, tq, w), lambda bi, qi: (bi, qi, 0)),
        out_shape=jax.ShapeDtypeStruct((b, sq, w), BF16),
        compiler_params=_cparams(("parallel", "parallel")),
        name="mem_attn",
    )(q, k, v)


def _mem_kv_kernel(x_ref, g_ref, w_ref, k_ref, v_ref, kb_ref, vb_ref):
    h = _rms(x_ref[...], g_ref[...]).astype(BF16)
    kv = _dot(h, w_ref[...])
    w = MEM_HEADS * MEM_DH
    k_ref[...] = kv[:, :w]
    v_ref[...] = kv[:, w:]
    kb_ref[...] = kv[:, :w].astype(BF16)
    vb_ref[...] = kv[:, w:].astype(BF16)


def _mem_kv(mem, wp, *, tm):
    n, d = mem.shape
    w = MEM_HEADS * MEM_DH
    row = lambda c: pl.BlockSpec((tm, c), lambda i: (i, 0))
    return pl.pallas_call(
        _mem_kv_kernel,
        grid=(n // tm,),
        in_specs=[row(d), _full_spec((1, d)), _full_spec(wp["w_mem_kv"].shape)],
        out_specs=(row(w), row(w), row(w), row(w)),
        out_shape=(jax.ShapeDtypeStruct((n, w), F32), jax.ShapeDtypeStruct((n, w), F32),
                   jax.ShapeDtypeStruct((n, w), BF16), jax.ShapeDtypeStruct((n, w), BF16)),
        compiler_params=_cparams(("parallel",)),
        name="mem_kv",
    )(mem, wp["norm_mem"], wp["w_mem_kv"])


def _merge_kernel(x_ref, oa_ref, ob_ref, om_ref, g_ref, wg_ref, bg_ref, wa_ref, wb_ref, wm_ref, wo_ref,
                  gf_ref, x1_ref, hn_ref):
    x = x_ref[...]
    d = x.shape[1]
    h = _rms(x, g_ref[...]).astype(BF16)
    gates = jax.nn.sigmoid(_dot(h, wg_ref[...]) + bg_ref[...])
    merged = (gates[:, :d] * _dot(oa_ref[...], wa_ref[...])
              + gates[:, d:2 * d] * _dot(ob_ref[...], wb_ref[...])
              + gates[:, 2 * d:] * _dot(om_ref[...], wm_ref[...]))
    x1 = x + _dot(merged.astype(BF16), wo_ref[...])
    x1_ref[...] = x1
    hn_ref[...] = _rms(x1, gf_ref[...]).astype(BF16)


def _merge(x, oa, ob, om, wp, *, tm):
    n, d = x.shape
    row = lambda c: pl.BlockSpec((tm, c), lambda i: (i, 0))
    names = ("norm_mix", "w_gate", "b_gate", "w_br_a", "w_br_b", "w_br_m", "w_o", "norm_ffn")
    return pl.pallas_call(
        _merge_kernel,
        grid=(n // tm,),
        in_specs=[row(d), row(oa.shape[1]), row(ob.shape[1]), row(om.shape[1])]
                 + [_full_spec(wp[k].shape) for k in names],
        out_specs=(row(d), row(d)),
        out_shape=(jax.ShapeDtypeStruct((n, d), F32), jax.ShapeDtypeStruct((n, d), BF16)),
        compiler_params=_cparams(("parallel",)),
        name="merge",
    )(x, oa, ob, om, *[wp[k] for k in names])


def _top_values(sc, k, with_rank=False):
    vals = []
    cur = sc
    rank = jnp.full(sc.shape, float(k), F32) if with_rank else None
    for r in range(k):
        m = jnp.max(cur, axis=0, keepdims=True)
        vals.append(m)
        hit = cur == m
        if with_rank:
            rank = jnp.where(hit, float(r), rank)
        if r + 1 < k:
            cur = jnp.where(hit, NEG, cur)
    return (vals, rank) if with_rank else vals


def _stack_rows(rows, sub):
    out = jnp.zeros(sub.shape, F32)
    for r, row in enumerate(rows):
        out = jnp.where(sub == r, row, out)
    return out


def _route_kernel(hn_ref, wq_ref, sk_ref, cnt_ref, e1_ref, rank_ref, e2_ref):
    tt = hn_ref.shape[0]
    q = _dot(hn_ref[...], wq_ref[...]).astype(BF16)
    sub = lax.broadcasted_iota(jnp.int32, (PEER_TOPK, tt), 0)
    sub8 = lax.broadcasted_iota(jnp.int32, (8, tt), 0)
    for h in range(PEER_HEADS):
        sc1 = _dot_nt(sk_ref[2 * h], q[:, (2 * h) * PEER_HALF:(2 * h + 1) * PEER_HALF])
        sc2 = _dot_nt(sk_ref[2 * h + 1], q[:, (2 * h + 1) * PEER_HALF:(2 * h + 2) * PEER_HALF])
        t1 = _top_values(sc1, PEER_TOPK)
        t2_rows, rank2 = _top_values(sc2, PEER_TOPK, with_rank=True)
        t2 = _stack_rows(t2_rows, sub)
        t2h = t2[:8]
        cands = []
        for a in range(PEER_TOPK):
            nb = PEER_TOPK // (a + 1)
            c = t1[a] + (t2 if nb > 8 else t2h)
            if nb < c.shape[0]:
                c = jnp.where((sub if nb > 8 else sub8) < nb, c, NEG)
            cands.append(c)
        cand = jnp.concatenate(cands, axis=0)
        tau = _top_values(cand, PEER_TOPK)[-1]
        top = t1[0] + t2[0:1]
        z = jnp.sum(jnp.where(cand >= tau, jnp.exp(cand - top), 0.0), axis=0, keepdims=True)
        cnt = jnp.zeros(sc1.shape, F32)
        for a in range(PEER_TOPK):
            ca = jnp.sum(jnp.where(cands[a] >= tau, 1.0, 0.0), axis=0, keepdims=True)
            cnt = jnp.where(sc1 == t1[a], ca, cnt)
        cnt_ref[h] = cnt
        e1_ref[h] = jnp.exp(sc1 - t1[0]) * (1.0 / z)
        rank_ref[h] = rank2.astype(BF16)
        e2_ref[h] = jnp.exp(sc2 - t2[0:1]).astype(BF16)


def _route(hn, wp, *, tt):
    n, d = hn.shape
    big = pl.BlockSpec((PEER_HEADS, PEER_NKEYS, tt), lambda i: (0, 0, i))
    big_shape = jax.ShapeDtypeStruct((PEER_HEADS, PEER_NKEYS, n), F32)
    big_half = jax.ShapeDtypeStruct((PEER_HEADS, PEER_NKEYS, n), BF16)
    return pl.pallas_call(
        _route_kernel,
        grid=(n // tt,),
        in_specs=[pl.BlockSpec((tt, d), lambda i: (i, 0)), _full_spec(wp["peer_wq"].shape),
                  _full_spec(wp["peer_subkeys"].shape)],
        out_specs=(big, big, big, big),
        out_shape=(big_shape, big_shape, big_half, big_half),
        compiler_params=_cparams(("parallel",)),
        name="peer_route",
    )(hn, wp["peer_wq"], wp["peer_subkeys"])


PAIR = 2 * PEER_NKEYS
DOWN_PAIRS = 4
SUBROWS = 128


def _peer_kernel(hn_ref, u_ref, vt_ref, cnt_ref, e1_ref, rank_in, e2_in, x1_ref, gn_ref, y_ref,
                 acc_scr, hw_scr, rank_ref, e2_ref, *, te, ne):
    j = pl.program_id(1)
    tt = hn_ref.shape[0]
    npair = te // PAIR

    @pl.when(j == 0)
    def _():
        rank_ref[...] = rank_in[...]
        e2_ref[...] = e2_in[...]

    def pre_act(p):
        a = _dot_nt(u_ref[p * PAIR:(p + 1) * PAIR, :], hn_ref[...])
        return 0.5 * a * (1.0 + lax.erf(a * math.sqrt(0.5)))

    def down(p):
        lo = (p + 1 - DOWN_PAIRS) * PAIR
        return _dot(vt_ref[:, lo:(p + 1) * PAIR], hw_scr[lo:(p + 1) * PAIR, :])

    groups = [(tc, sb) for tc in range(tt // LANE) for sb in range(PEER_NKEYS // SUBROWS)]
    zero = jnp.zeros((), BF16)
    act = pre_act(0)
    tot = None
    carry = None
    for p in range(npair):
        nxt = None
        for gi, (tc, sb) in enumerate(groups):
            if gi == 1 and p + 1 < npair:
                nxt = pre_act(p + 1)
            if gi == len(groups) // 2 and p >= 1 and p % DOWN_PAIRS == 0:
                part = down(p - 1)
                tot = part if tot is None else tot + part
            ls = slice(tc * LANE, (tc + 1) * LANE)
            i2 = slice(sb * SUBROWS, (sb + 1) * SUBROWS)
            w = [None, None] if carry is None else [carry * zero, carry * zero]
            for h in range(PEER_HEADS):
                rk = rank_ref[h, i2, ls]
                e2b = e2_ref[h, i2, ls]
                for rr in range(2):
                    i1 = 2 * p + rr
                    cnt = jnp.broadcast_to(cnt_ref[h, i1:i1 + 1, ls], (SUBROWS, LANE)).astype(BF16)
                    e1b = jnp.broadcast_to(e1_ref[h, i1:i1 + 1, ls], (SUBROWS, LANE)).astype(BF16)
                    cw = jnp.where(rk < cnt, e2b, zero) * e1b
                    w[rr] = cw if w[rr] is None else w[rr] + cw
            carry = w[1]
            for rr in range(2):
                r0 = rr * PEER_NKEYS + sb * SUBROWS
                hw_scr[p * PAIR + r0:p * PAIR + r0 + SUBROWS, ls] = w[rr] * act[r0:r0 + SUBROWS, ls].astype(BF16)
        act = nxt
    part = down(npair - 1)
    tot = part if tot is None else tot + part

    @pl.when(j == 0)
    def _():
        acc_scr[...] = tot

    @pl.when(j > 0)
    def _():
        acc_scr[...] += tot

    @pl.when(j == ne - 1)
    def _():
        xr = x1_ref[...] + acc_scr[...].T
        y_ref[...] = _rms(xr, gn_ref[...])


def _peer(hn, cnt, e1, rank2, e2, x1, wp, *, tt, te):
    n, d = hn.shape
    ne = wp["peer_u"].shape[0] // te
    big = pl.BlockSpec((PEER_HEADS, PEER_NKEYS, tt), lambda i, j: (0, 0, i))
    rows = pl.BlockSpec((PEER_HEADS, te // PEER_NKEYS, tt), lambda i, j: (0, j, i))
    kern = functools.partial(_peer_kernel, te=te, ne=ne)
    return pl.pallas_call(
        kern,
        grid=(n // tt, ne),
        in_specs=[pl.BlockSpec((tt, d), lambda i, j: (i, 0)),
                  pl.BlockSpec((te, d), lambda i, j: (j, 0)),
                  pl.BlockSpec((d, te), lambda i, j: (0, j)),
                  rows, rows, big, big,
                  pl.BlockSpec((tt, d), lambda i, j: (i, 0)),
                  pl.BlockSpec((1, d), lambda i, j: (0, 0))],
        out_specs=pl.BlockSpec((tt, d), lambda i, j: (i, 0)),
        out_shape=jax.ShapeDtypeStruct((n, d), F32),
        scratch_shapes=[pltpu.VMEM((d, tt), F32), pltpu.VMEM((te, tt), BF16),
                        pltpu.VMEM((PEER_HEADS, PEER_NKEYS, tt), BF16),
                        pltpu.VMEM((PEER_HEADS, PEER_NKEYS, tt), BF16)],
        compiler_params=_cparams(("parallel", "arbitrary")),
        name="peer_experts",
    )(hn, wp["peer_u"], wp["peer_vt"], cnt, e1, rank2, e2, x1, wp["norm_final"])


def _prep_weights(l, p):
    f = lambda a: a.astype(F32)
    w_in = f(p["w_in"][l])
    o_cq, o_ckv, o_kr = 0, MLA_Q_LORA, MLA_Q_LORA + MLA_KV_LORA
    o_dq = o_kr + MLA_ROPE
    half = MLA_ROPE // 2
    d = w_in.shape[0]
    kr = w_in[:, o_kr:o_dq]
    pad = jnp.zeros((d, LANE - MLA_ROPE), F32)
    w_in2 = jnp.concatenate([
        w_in[:, o_cq:o_kr], kr, pad,
        -kr[:, half:], kr[:, :half], pad,
        w_in[:, o_dq:]], axis=1)
    assert w_in2.shape[1] == _C_END

    w_uq = f(p["w_uq"][l]).reshape(MLA_Q_LORA, MLA_HEADS, MLA_NOPE + MLA_ROPE)
    nope, x1, x2 = w_uq[..., :MLA_NOPE], w_uq[..., MLA_NOPE:MLA_NOPE + half], w_uq[..., MLA_NOPE + half:]
    zpad = jnp.zeros((MLA_Q_LORA, MLA_HEADS, HEAD_W - MLA_NOPE - MLA_ROPE), F32)
    q_slab = jnp.concatenate([nope, x1, x2, zpad], axis=-1).reshape(MLA_Q_LORA, -1)
    r_slab = jnp.concatenate([jnp.zeros_like(nope), -x2, x1, zpad], axis=-1).reshape(MLA_Q_LORA, -1)
    w_uq2 = jnp.concatenate([q_slab, r_slab], axis=1)

    w_uk = f(p["w_uk"][l]).reshape(MLA_KV_LORA, MLA_HEADS, MLA_NOPE)
    w_uk2 = jnp.concatenate([w_uk, jnp.zeros((MLA_KV_LORA, MLA_HEADS, HEAD_W - MLA_NOPE), F32)],
                            axis=-1).reshape(MLA_KV_LORA, -1)
    r_idx = jnp.arange(MLA_ROPE)
    cols = jnp.arange(MLA_HEADS * HEAD_W)
    p_kr = ((cols[None, :] % HEAD_W) == (MLA_NOPE + r_idx[:, None])).astype(F32)

    slopes = jnp.exp2(-8.0 * jnp.arange(1, DIFF_HEADS + 1, dtype=F32) / DIFF_HEADS)
    sub = f(p["diff_subln"][l])
    row = lambda a: f(a).reshape(1, -1)
    return {
        "norm_mix": row(p["norm_mix"][l]), "w_in": w_in2.astype(BF16),
        "mla_q_norm": row(p["mla_q_norm"][l]), "w_uq": w_uq2.astype(BF16),
        "mla_kv_norm": row(p["mla_kv_norm"][l]),
        "w_uk": w_uk2.astype(BF16), "p_kr": p_kr.astype(BF16), "w_uv": p["w_uv"][l].astype(BF16),
        "slopes": slopes,
        "diff_lq1": row(p["diff_lq1"][l]), "diff_lk1": row(p["diff_lk1"][l]),
        "diff_lq2": row(p["diff_lq2"][l]), "diff_lk2": row(p["diff_lk2"][l]),
        "diff_subln2": jnp.concatenate([sub, sub]).reshape(1, -1),
        "norm_mem": row(p["norm_mem"][l]), "w_mem_kv": p["w_mem_kv"][l].astype(BF16),
        "w_gate": p["w_gate"][l].astype(BF16), "b_gate": row(p["b_gate"][l]),
        "w_br_a": p["w_br_a"][l].astype(BF16), "w_br_b": p["w_br_b"][l].astype(BF16),
        "w_br_m": p["w_br_m"][l].astype(BF16), "w_o": p["w_o"][l].astype(BF16),
        "norm_ffn": row(p["norm_ffn"][l]),
        "peer_wq": p["peer_wq"][l].astype(BF16),
        "peer_subkeys": p["peer_subkeys"][l].reshape(PEER_HEADS * 2, PEER_NKEYS, PEER_HALF).astype(BF16),
        "peer_u": p["peer_u"][l].astype(BF16), "peer_vt": p["peer_v"][l].T.astype(BF16),
        "norm_final": row(p["norm_final"]),
    }


def _rope_tables(pos):
    half = MLA_ROPE // 2
    inv = 1.0 / (ROPE_THETA ** (jnp.arange(half, dtype=F32) / half))
    ang = pos.astype(F32)[:, None] * inv[None, :]
    cos, sin = jnp.cos(ang), jnp.sin(ang)
    n = pos.shape[0]
    one = jnp.ones((n, MLA_NOPE), F32)
    zq = jnp.zeros((n, HEAD_W - MLA_NOPE - MLA_ROPE), F32)
    cq = jnp.concatenate([one, cos, cos, zq], axis=1) * MLA_SCALE
    sq = jnp.concatenate([0.0 * one, sin, sin, zq], axis=1) * MLA_SCALE
    zk = jnp.zeros((n, LANE - MLA_ROPE), F32)
    ck = jnp.concatenate([cos, cos, zk], axis=1)
    sk = jnp.concatenate([sin, sin, zk], axis=1)
    return cq, sq, ck, sk


def _layer(x, pos, past, mem_k, mem_v, wp, layer, cfg):
    b, s, d = x.shape
    n = b * s
    xf = x.reshape(n, d)
    tabs = _rope_tables(pos)
    if s % cfg["tm"] == 0:
        pos_blocks = s // cfg["tm"]
    else:
        tabs = tuple(jnp.tile(t, (b, 1)) for t in tabs)
        pos_blocks = n // cfg["tm"]
    ckv, kr, dk, dv, qm, dqb, dkb, dvb, mqb = _proj_in(xf, tabs, wp, tm=cfg["tm"], pos_blocks=pos_blocks)
    new_rows = (ckv.reshape(b, s, -1), kr.reshape(b, s, -1), dk.reshape(b, s, DIFF_HEADS, DIFF_V),
                dv.reshape(b, s, DIFF_HEADS, DIFF_V))
    r3 = lambda a: a.reshape(b, s, -1)
    if past is None:
        ckv_all, kr_all = ckv, kr
        dk_all, dv_all = r3(dkb), r3(dvb)
        kk = s
        q_off = 0
    else:
        p_len = past[0].shape[1]
        kk = p_len + s
        q_off = p_len
        ckv_all = jnp.concatenate([past[0], r3(ckv)], axis=1).reshape(b * kk, -1)
        kr_all = jnp.concatenate([past[1], r3(kr)], axis=1).reshape(b * kk, -1)
        dk_all = jnp.concatenate([past[2].reshape(b, p_len, -1).astype(BF16), r3(dkb)], axis=1)
        dv_all = jnp.concatenate([past[3].reshape(b, p_len, -1).astype(BF16), r3(dvb)], axis=1)
    k_mla, v_mla = _kv_up(ckv_all, kr_all, wp, tm=cfg["tm_kv"])
    att = dict(tq=cfg["tq"], tk=cfg["tk"] if past is None else kk, q_off=q_off)
    o_a = _mla_attention(r3(qm), k_mla.reshape(b, kk, -1), v_mla.reshape(b, kk, -1), **att)
    lam_init = 0.8 - 0.6 * math.exp(-0.3 * layer)
    o_b = _diff_attention(r3(dqb), dk_all, dv_all, wp, lam_init=lam_init, **att)
    o_m = _mem_attention(r3(mqb), mem_k, mem_v, tq=cfg["tq"])
    x1, hn = _merge(xf, o_a.reshape(n, -1), o_b.reshape(n, -1), o_m.reshape(n, -1), wp, tm=cfg["tm"])
    return x1, hn, new_rows


def _peer_and_norm(x1, hn, wp, cfg):
    cnt, e1, rank2, e2 = _route(hn, wp, tt=cfg["tt"])
    return _peer(hn, cnt, e1, rank2, e2, x1, wp, tt=cfg["tt"], te=cfg["te"])


_CFG_PROMPT = dict(tm=512, tm_kv=512, tq=512, tk=512, tt=512, te=2048)
_CFG_SAMPLE = dict(tm=256, tm_kv=256, tq=32, tk=None, tt=256, te=2048)


def kernel(x_prompt, x_sample, cache_mla_ckv, cache_mla_krope, cache_diff_k, cache_diff_v, cache_mem_k, cache_mem_v, mem_prompt, norm_mix, w_in, mla_q_norm, w_uq, mla_kv_norm, w_uk, w_uv, diff_lq1, diff_lk1, diff_lq2, diff_lk2, diff_subln, norm_mem, w_mem_kv, w_br_a, w_br_b, w_br_m, w_gate, b_gate, w_o, norm_ffn, peer_wq, peer_subkeys, peer_u, peer_v, norm_final):
    params = dict(norm_mix=norm_mix, w_in=w_in, mla_q_norm=mla_q_norm, w_uq=w_uq, mla_kv_norm=mla_kv_norm,
                  w_uk=w_uk, w_uv=w_uv, diff_lq1=diff_lq1, diff_lk1=diff_lk1, diff_lq2=diff_lq2,
                  diff_lk2=diff_lk2, diff_subln=diff_subln, norm_mem=norm_mem, w_mem_kv=w_mem_kv,
                  w_br_a=w_br_a, w_br_b=w_br_b, w_br_m=w_br_m, w_gate=w_gate, b_gate=b_gate, w_o=w_o,
                  norm_ffn=norm_ffn, peer_wq=peer_wq, peer_subkeys=peer_subkeys, peer_u=peer_u,
                  peer_v=peer_v, norm_final=norm_final)
    depth = w_in.shape[0]
    assert depth == 1, "the final norm is fused into the last PEER step of a single layer"
    bp, sp, d = x_prompt.shape
    bs, ss, _ = x_sample.shape
    n_mem = mem_prompt.shape[1]
    past_len = cache_mla_ckv.shape[2]
    pos_p = jnp.arange(sp, dtype=jnp.int32)
    pos_s = past_len + jnp.arange(ss, dtype=jnp.int32)

    l = 0
    wp = _prep_weights(l, params)
    mk, mv, mkb, mvb = _mem_kv(mem_prompt.reshape(bp * n_mem, d), wp, tm=n_mem)
    m3 = lambda a, b: a.reshape(b, n_mem, -1)
    x1p, hnp, rows_p = _layer(x_prompt, pos_p, None, m3(mkb, bp), m3(mvb, bp), wp, l, _CFG_PROMPT)
    past = (cache_mla_ckv[l], cache_mla_krope[l], cache_diff_k[l], cache_diff_v[l])
    x1s, hns, rows_s = _layer(x_sample, pos_s, past, m3(cache_mem_k[l].astype(BF16), bs),
                              m3(cache_mem_v[l].astype(BF16), bs), wp, l, _CFG_SAMPLE)
    y_prompt = _peer_and_norm(x1p, hnp, wp, _CFG_PROMPT).reshape(bp, sp, d)
    y_sample = _peer_and_norm(x1s, hns, wp, _CFG_SAMPLE).reshape(bs, ss, d)
    st = lambda a: a[None]
    mem4 = lambda a: a.reshape(bp, n_mem, MEM_HEADS, MEM_DH)[None]
    return (y_prompt, y_sample,
            st(rows_p[0]), st(rows_p[1]), st(rows_p[2]), st(rows_p[3]),
            mem4(mk), mem4(mv),
            st(rows_s[0]), st(rows_s[1]), st(rows_s[2]), st(rows_s[3]))
```

```python
import functools
import math

import jax
import jax.numpy as jnp
from jax import lax
from jax.experimental import pallas as pl
from jax.experimental.pallas import tpu as pltpu

F32 = jnp.float32
BF16 = jnp.bfloat16

CHUNK = 64
CHUNK_SHIFT = 6
EPS = 1e-6
NEG = -1e30
MLA_HEADS = 8
MLA_Q_LORA = 384
MLA_KV_LORA = 256
MLA_NOPE = 64
MLA_ROPE = 32
MLA_V = 64
ROPE_THETA = 10000.0
LOG2E = math.log2(math.e)
MLA_SCALE = (MLA_NOPE + MLA_ROPE) ** -0.5 * LOG2E
DIFF_HEADS = 8
DIFF_DH = 32
DIFF_V = 2 * DIFF_DH
DIFF_SCALE = DIFF_DH ** -0.5 * LOG2E
MEM_HEADS = 4
MEM_DH = 128
MEM_SCALE = MEM_DH ** -0.5 * LOG2E
PEER_HEADS = 8
PEER_NKEYS = 128
PEER_HALF = 128
PEER_TOPK = 16
LANE = 128
HEAD_W = 128

VMEM_LIMIT = 56 * 1024 * 1024

_C_CQ = 0
_C_CKV = _C_CQ + MLA_Q_LORA
_C_KR = _C_CKV + MLA_KV_LORA
_C_KRR = _C_KR + LANE
_C_DQ = _C_KRR + LANE
_C_DK = _C_DQ + DIFF_HEADS * DIFF_V
_C_DV = _C_DK + DIFF_HEADS * DIFF_V
_C_MQ = _C_DV + DIFF_HEADS * DIFF_V
_C_END = _C_MQ + MEM_HEADS * MEM_DH


def _cparams(sem):
    return pltpu.CompilerParams(dimension_semantics=sem, vmem_limit_bytes=VMEM_LIMIT)


def _rms(x, g):
    return x * lax.rsqrt(jnp.mean(x * x, axis=-1, keepdims=True) + EPS) * g


def _dot(a, b):
    return jnp.dot(a, b, preferred_element_type=F32)


def _dot_nt(a, b):
    return lax.dot_general(a, b, (((1,), (1,)), ((), ())), preferred_element_type=F32)


def _full_spec(shape):
    nd = len(shape)
    return pl.BlockSpec(shape, lambda *_: (0,) * nd)


def _proj_in_kernel(x_ref, g_ref, win_ref, qn_ref, wuq_ref, kvn_ref, cq_ref, sq_ref, ck_ref, sk_ref,
                    ckv_ref, kr_ref, dk_ref, dv_ref, qm_ref, dqb_ref, dkb_ref, dvb_ref, mqb_ref):
    h = _rms(x_ref[...], g_ref[...]).astype(BF16)
    z = _dot(h, win_ref[...])
    cqn = _rms(z[:, _C_CQ:_C_CKV], qn_ref[...]).astype(BF16)
    q2 = _dot(cqn, wuq_ref[...])
    cq = cq_ref[...]
    sq = sq_ref[...]
    nq = MLA_HEADS * HEAD_W
    for hh in range(MLA_HEADS):
        lo = hh * HEAD_W
        qm_ref[:, lo:lo + HEAD_W] = (q2[:, lo:lo + HEAD_W] * cq
                                     + q2[:, nq + lo:nq + lo + HEAD_W] * sq).astype(BF16)
    ckv_ref[...] = _rms(z[:, _C_CKV:_C_KR], kvn_ref[...])
    kr = z[:, _C_KR:_C_KRR] * ck_ref[...] + z[:, _C_KRR:_C_DQ] * sk_ref[...]
    kr_ref[...] = kr[:, :MLA_ROPE]
    dk = z[:, _C_DK:_C_DV]
    dv = z[:, _C_DV:_C_MQ]
    dk_ref[...] = dk
    dv_ref[...] = dv
    dkb_ref[...] = dk.astype(BF16)
    dvb_ref[...] = dv.astype(BF16)
    dqb_ref[...] = (z[:, _C_DQ:_C_DK] * DIFF_SCALE).astype(BF16)
    mqb_ref[...] = (z[:, _C_MQ:_C_END] * MEM_SCALE).astype(BF16)


def _proj_in(x, tabs, wp, *, tm, pos_blocks):
    n, d = x.shape
    grid = (n // tm,)
    row = lambda w: pl.BlockSpec((tm, w), lambda i: (i, 0))
    tab = pl.BlockSpec((tm, LANE), lambda i: (i % pos_blocks, 0))
    dw = DIFF_HEADS * DIFF_V
    out_shape = (
        jax.ShapeDtypeStruct((n, MLA_KV_LORA), F32),
        jax.ShapeDtypeStruct((n, MLA_ROPE), F32),
        jax.ShapeDtypeStruct((n, dw), F32),
        jax.ShapeDtypeStruct((n, dw), F32),
        jax.ShapeDtypeStruct((n, MLA_HEADS * HEAD_W), BF16),
        jax.ShapeDtypeStruct((n, dw), BF16),
        jax.ShapeDtypeStruct((n, dw), BF16),
        jax.ShapeDtypeStruct((n, dw), BF16),
        jax.ShapeDtypeStruct((n, MEM_HEADS * MEM_DH), BF16),
    )
    return pl.pallas_call(
        _proj_in_kernel,
        grid=grid,
        in_specs=[row(d), _full_spec((1, d)), _full_spec(wp["w_in"].shape), _full_spec((1, MLA_Q_LORA)),
                  _full_spec(wp["w_uq"].shape), _full_spec((1, MLA_KV_LORA)), tab, tab, tab, tab],
        out_specs=(row(MLA_KV_LORA), row(MLA_ROPE), row(dw), row(dw), row(MLA_HEADS * HEAD_W),
                   row(dw), row(dw), row(dw), row(MEM_HEADS * MEM_DH)),
        out_shape=out_shape,
        compiler_params=_cparams(("parallel",)),
        name="proj_in",
    )(x, wp["norm_mix"], wp["w_in"], wp["mla_q_norm"], wp["w_uq"], wp["mla_kv_norm"], *tabs)


def _kv_up_kernel(ckv_ref, kr_ref, wuk_ref, pk_ref, wuv_ref, k_ref, v_ref):
    c = ckv_ref[...].astype(BF16)
    k = _dot(c, wuk_ref[...]) + _dot(kr_ref[...].astype(BF16), pk_ref[...])
    k_ref[...] = k.astype(BF16)
    v_ref[...] = _dot(c, wuv_ref[...]).astype(BF16)


def _kv_up(ckv, kr, wp, *, tm):
    n = ckv.shape[0]
    row = lambda w: pl.BlockSpec((tm, w), lambda i: (i, 0))
    return pl.pallas_call(
        _kv_up_kernel,
        grid=(n // tm,),
        in_specs=[row(MLA_KV_LORA), row(MLA_ROPE), _full_spec(wp["w_uk"].shape),
                  _full_spec(wp["p_kr"].shape), _full_spec(wp["w_uv"].shape)],
        out_specs=(row(MLA_HEADS * HEAD_W), row(MLA_HEADS * MLA_V)),
        out_shape=(jax.ShapeDtypeStruct((n, MLA_HEADS * HEAD_W), BF16),
                   jax.ShapeDtypeStruct((n, MLA_HEADS * MLA_V), BF16)),
        compiler_params=_cparams(("parallel",)),
        name="kv_up",
    )(ckv, kr, wp["w_uk"], wp["p_kr"], wp["w_uv"])


def _lanes(col, n):
    if n == LANE:
        return col
    if n % LANE == 0:
        return jnp.concatenate([col] * (n // LANE), axis=1)
    return jnp.broadcast_to(col[:, :1], (col.shape[0], n))


GROUP_VREGS = 16


def _softmax_block(score_group, tq, tk, row0, m_scr, al_scr, p_scr, l_scr=None):
    rg = min(tq, max(16, (GROUP_VREGS * 8 * LANE // tk) // 16 * 16))
    for g in range(tq // rg):
        r = slice(row0 + g * rg, row0 + (g + 1) * rg)
        m_prev = m_scr[r]
        m_next = jnp.maximum(m_prev, jnp.max(score_group(g, rg), axis=1, keepdims=True))
        al_scr[r] = jnp.exp2(m_prev - m_next)
        m_scr[r] = m_next
    for g in range(tq // rg):
        r = slice(row0 + g * rg, row0 + (g + 1) * rg)
        e = jnp.exp2(score_group(g, rg) - _lanes(m_scr[r], tk))
        p_scr[r] = e.astype(BF16)
        if l_scr is not None:
            l_scr[r] = al_scr[r] * l_scr[r] + jnp.sum(e, axis=1, keepdims=True)


def _accumulate(rows, v, al_scr, l_scr, acc_scr, p_scr):
    pv = _dot(p_scr[rows], jnp.concatenate([v, jnp.ones(v.shape, v.dtype)], axis=1))
    w = v.shape[1]
    acc_scr[rows] = acc_scr[rows] * al_scr[rows] + pv[:, :w]
    l_scr[rows] = l_scr[rows] * al_scr[rows] + pv[:, w:]


def _chunk_visible(q_lo, k_lo, g, rg, tk):
    qp = q_lo + g * rg + lax.broadcasted_iota(jnp.int32, (rg, tk), 0)
    kp = k_lo + lax.broadcasted_iota(jnp.int32, (rg, tk), 1)
    return qp, kp, (kp >> CHUNK_SHIFT) <= (qp >> CHUNK_SHIFT)


def _init_stats(m_scr, l_scr, acc_scr):
    m_scr[...] = jnp.full(m_scr.shape, NEG, F32)
    l_scr[...] = jnp.zeros(l_scr.shape, F32)
    acc_scr[...] = jnp.zeros(acc_scr.shape, F32)


def _block_range(qi, *, tq, tk, q_off, nk):
    q_lo = q_off + qi * tq
    q_hi = q_lo + (tq - 1)
    n_behind = jnp.minimum((q_lo + 1) // tk, nk)
    last_key = ((q_hi >> CHUNK_SHIFT) << CHUNK_SHIFT) + (CHUNK - 1)
    n_need = jnp.minimum(last_key // tk + 1, nk)
    return q_lo, n_behind, n_need


def _for_blocks(lo, hi, fn):
    lax.fori_loop(lo, hi, lambda i, c: (fn(i), c)[1], 0)


def _attn_specs(b, pairs, sq, sk, tq, qw, kw, vw):
    grid = (b, pairs, sq // tq)
    in_specs = [pl.BlockSpec((1, tq, qw), lambda bi, p, qi: (bi, qi, p)),
                pl.BlockSpec((1, sk, kw), lambda bi, p, qi: (bi, 0, p)),
                pl.BlockSpec((1, sk, vw), lambda bi, p, qi: (bi, 0, p))]
    out_spec = pl.BlockSpec((1, tq, vw), lambda bi, p, qi: (bi, qi, p))
    return grid, in_specs, out_spec


def _mla_kernel(q_ref, k_ref, v_ref, o_ref, m_scr, l_scr, al_scr, acc_scr, p_scr, *, tq, tk, q_off, nk):
    qi = pl.program_id(2)
    _init_stats(m_scr, l_scr, acc_scr)
    q_lo, n_behind, n_need = _block_range(qi, tq=tq, tk=tk, q_off=q_off, nk=nk)

    def block(i, masked):
        k_lo = pl.multiple_of(i * tk, tk)
        for hh in range(2):
            hs = slice(hh * HEAD_W, (hh + 1) * HEAD_W)
            s = _dot_nt(q_ref[0, :, hs], k_ref[0, pl.ds(k_lo, tk), hs])

            def score_group(g, rg, s=s):
                sg = s[g * rg:(g + 1) * rg]
                if masked:
                    sg = jnp.where(_chunk_visible(q_lo, k_lo, g, rg, tk)[2], sg, NEG)
                return sg

            _softmax_block(score_group, tq, tk, hh * tq, m_scr, al_scr, p_scr)
        _accumulate(slice(0, 2 * tq), v_ref[0, pl.ds(k_lo, tk), :], al_scr, l_scr, acc_scr, p_scr)

    _for_blocks(0, n_behind, lambda i: block(i, False))
    _for_blocks(n_behind, n_need, lambda i: block(i, True))

    lane = lax.broadcasted_iota(jnp.int32, (tq, LANE), 1)
    o0 = acc_scr[0:tq] / l_scr[0:tq]
    o1 = acc_scr[tq:2 * tq] / l_scr[tq:2 * tq]
    o_ref[0] = jnp.where(lane < MLA_V, o0, o1).astype(o_ref.dtype)


def _mla_attention(q, k, v, *, tq, tk, q_off):
    b, sq, _ = q.shape
    sk = k.shape[1]
    grid, in_specs, out_spec = _attn_specs(b, MLA_HEADS // 2, sq, sk, tq, 2 * HEAD_W, 2 * HEAD_W, 2 * MLA_V)
    kern = functools.partial(_mla_kernel, tq=tq, tk=tk, q_off=q_off, nk=sk // tk)
    stat = pltpu.VMEM((2 * tq, LANE), F32)
    return pl.pallas_call(
        kern,
        grid=grid,
        in_specs=in_specs,
        out_specs=out_spec,
        out_shape=jax.ShapeDtypeStruct((b, sq, MLA_HEADS * MLA_V), BF16),
        scratch_shapes=[stat, stat, stat, stat, pltpu.VMEM((2 * tq, tk), BF16)],
        compiler_params=_cparams(("parallel", "parallel", "parallel")),
        name="mla_attn",
    )(q, k, v)


def _diff_kernel(slope_ref, lq1_ref, lk1_ref, lq2_ref, lk2_ref, sub_ref, q_ref, k_ref, v_ref, o_ref,
                 qs_scr, m_scr, l_scr, al_scr, acc_scr, p_scr, *, tq, tk, q_off, nk, lam_init):
    pr = pl.program_id(1)
    qi = pl.program_id(2)
    _init_stats(m_scr, l_scr, acc_scr)
    q_lo, n_behind, n_need = _block_range(qi, tq=tq, tk=tk, q_off=q_off, nk=nk)

    q = q_ref[0]
    lane = lax.broadcasted_iota(jnp.int32, (tq, LANE), 1)
    for mi in range(4):
        lo = mi * DIFF_DH
        qs_scr[mi * tq:(mi + 1) * tq] = jnp.where((lane >= lo) & (lane < lo + DIFF_DH), q, jnp.zeros_like(q))

    def block(i, masked):
        k_lo = pl.multiple_of(i * tk, tk)
        kb = k_ref[0, pl.ds(k_lo, tk), :]
        kpos = (k_lo + lax.broadcasted_iota(jnp.int32, (1, tk), 1)).astype(F32)

        def scores(mi):
            return _dot_nt(qs_scr[mi * tq:(mi + 1) * tq], kb)

        def softmax(mi, s_map):
            slope = slope_ref[2 * pr + mi // 2] * LOG2E

            def score_group(g, rg):
                sg = s_map[g * rg:(g + 1) * rg]
                if masked:
                    qp, kp, vis = _chunk_visible(q_lo, k_lo, g, rg, tk)
                    return jnp.where(vis, sg + slope * jnp.minimum(kp, 2 * qp - kp).astype(F32), NEG)
                return sg + slope * kpos

            _softmax_block(score_group, tq, tk, mi * tq, m_scr, al_scr, p_scr, l_scr)

        def values(mi):
            rows = slice(mi * tq, (mi + 1) * tq)
            acc_scr[rows] = acc_scr[rows] * al_scr[rows] + _dot(p_scr[rows], v_ref[0, pl.ds(k_lo, tk), :])

        s0 = scores(0)
        s1 = scores(1)
        softmax(0, s0)
        s2 = scores(2)
        values(0)
        softmax(1, s1)
        s3 = scores(3)
        values(1)
        softmax(2, s2)
        values(2)
        softmax(3, s3)
        values(3)

    _for_blocks(0, n_behind, lambda i: block(i, False))
    _for_blocks(n_behind, n_need, lambda i: block(i, True))

    lam = (jnp.exp(jnp.sum(lq1_ref[...] * lk1_ref[...], axis=1, keepdims=True))
           - jnp.exp(jnp.sum(lq2_ref[...] * lk2_ref[...], axis=1, keepdims=True)) + lam_init)
    first = lane < DIFF_V
    on = [acc_scr[mi * tq:(mi + 1) * tq] / l_scr[mi * tq:(mi + 1) * tq] for mi in range(4)]
    o = jnp.where(first, on[0] - lam * on[1], on[2] - lam * on[3])
    sq = o * o
    ms0 = jnp.sum(jnp.where(first, sq, 0.0), axis=1, keepdims=True) * (1.0 / DIFF_V)
    ms1 = jnp.sum(jnp.where(first, 0.0, sq), axis=1, keepdims=True) * (1.0 / DIFF_V)
    r = jnp.where(first, lax.rsqrt(ms0 + EPS), lax.rsqrt(ms1 + EPS))
    o_ref[0] = ((o * r * sub_ref[...]) * (1.0 - lam_init)).astype(o_ref.dtype)


def _diff_attention(q, k, v, wp, *, tq, tk, q_off, lam_init):
    b, sq, _ = q.shape
    sk = k.shape[1]
    grid, in_specs, out_spec = _attn_specs(b, DIFF_HEADS // 2, sq, sk, tq, LANE, LANE, LANE)
    kern = functools.partial(_diff_kernel, tq=tq, tk=tk, q_off=q_off, nk=sk // tk, lam_init=lam_init)
    small = lambda w: pl.BlockSpec((1, w), lambda bi, p, qi: (0, 0))
    stat = pltpu.VMEM((4 * tq, LANE), F32)
    return pl.pallas_call(
        kern,
        grid=grid,
        in_specs=[pl.BlockSpec(memory_space=pltpu.SMEM),
                  small(DIFF_DH), small(DIFF_DH), small(DIFF_DH), small(DIFF_DH), small(LANE)] + in_specs,
        out_specs=out_spec,
        out_shape=jax.ShapeDtypeStruct((b, sq, DIFF_HEADS * DIFF_V), BF16),
        scratch_shapes=[pltpu.VMEM((4 * tq, LANE), BF16), stat, stat, stat, stat,
                        pltpu.VMEM((4 * tq, tk), BF16)],
        compiler_params=_cparams(("parallel", "parallel", "parallel")),
        name="diff_attn",
    )(wp["slopes"], wp["diff_lq1"], wp["diff_lk1"], wp["diff_lq2"], wp["diff_lk2"], wp["diff_subln2"],
      q, k, v)


def _mem_attn_kernel(q_ref, k_ref, v_ref, o_ref):
    for hh in range(MEM_HEADS):
        sl = slice(hh * MEM_DH, (hh + 1) * MEM_DH)
        s = _dot_nt(q_ref[0, :, sl], k_ref[0, :, sl])
        p = jnp.exp2(s - jnp.max(s, axis=1, keepdims=True))
        o = _dot(p.astype(BF16), v_ref[0, :, sl]) / jnp.sum(p, axis=1, keepdims=True)
        o_ref[0, :, sl] = o.astype(o_ref.dtype)


def _mem_attention(q, k, v, *, tq):
    b, sq, w = q.shape
    nm = k.shape[1]
    return pl.pallas_call(
        _mem_attn_kernel,
        grid=(b, sq // tq),
        in_specs=[pl.BlockSpec((1, tq, w), lambda bi, qi: (bi, qi, 0)),
                  pl.BlockSpec((1, nm, w), lambda bi, qi: (bi, 0, 0)),
                  pl.BlockSpec((1, nm, w), lambda bi, qi: (bi, 0, 0))],
        out_specs=pl.BlockSpec((1, tq, w), lambda bi, qi: (bi, qi, 0)),
        out_shape=jax.ShapeDtypeStruct((b, sq, w), BF16),
        compiler_params=_cparams(("parallel", "parallel")),
        name="mem_attn",
    )(q, k, v)


def _mem_kv_kernel(x_ref, g_ref, w_ref, k_ref, v_ref, kb_ref, vb_ref):
    h = _rms(x_ref[...], g_ref[...]).astype(BF16)
    kv = _dot(h, w_ref[...])
    w = MEM_HEADS * MEM_DH
    k_ref[...] = kv[:, :w]
    v_ref[...] = kv[:, w:]
    kb_ref[...] = kv[:, :w].astype(BF16)
    vb_ref[...] = kv[:, w:].astype(BF16)


def _mem_kv(mem, wp, *, tm):
    n, d = mem.shape
    w = MEM_HEADS * MEM_DH
    row = lambda c: pl.BlockSpec((tm, c), lambda i: (i, 0))
    return pl.pallas_call(
        _mem_kv_kernel,
        grid=(n // tm,),
        in_specs=[row(d), _full_spec((1, d)), _full_spec(wp["w_mem_kv"].shape)],
        out_specs=(row(w), row(w), row(w), row(w)),
        out_shape=(jax.ShapeDtypeStruct((n, w), F32), jax.ShapeDtypeStruct((n, w), F32),
                   jax.ShapeDtypeStruct((n, w), BF16), jax.ShapeDtypeStruct((n, w), BF16)),
        compiler_params=_cparams(("parallel",)),
        name="mem_kv",
    )(mem, wp["norm_mem"], wp["w_mem_kv"])


def _merge_kernel(x_ref, oa_ref, ob_ref, om_ref, g_ref, wg_ref, bg_ref, wa_ref, wb_ref, wm_ref, wo_ref,
                  gf_ref, x1_ref, hn_ref):
    x = x_ref[...]
    d = x.shape[1]
    h = _rms(x, g_ref[...]).astype(BF16)
    gates = jax.nn.sigmoid(_dot(h, wg_ref[...]) + bg_ref[...])
    merged = (gates[:, :d] * _dot(oa_ref[...], wa_ref[...])
              + gates[:, d:2 * d] * _dot(ob_ref[...], wb_ref[...])
              + gates[:, 2 * d:] * _dot(om_ref[...], wm_ref[...]))
    x1 = x + _dot(merged.astype(BF16), wo_ref[...])
    x1_ref[...] = x1
    hn_ref[...] = _rms(x1, gf_ref[...]).astype(BF16)


def _merge(x, oa, ob, om, wp, *, tm):
    n, d = x.shape
    row = lambda c: pl.BlockSpec((tm, c), lambda i: (i, 0))
    names = ("norm_mix", "w_gate", "b_gate", "w_br_a", "w_br_b", "w_br_m", "w_o", "norm_ffn")
    return pl.pallas_call(
        _merge_kernel,
        grid=(n // tm,),
        in_specs=[row(d), row(oa.shape[1]), row(ob.shape[1]), row(om.shape[1])]
                 + [_full_spec(wp[k].shape) for k in names],
        out_specs=(row(d), row(d)),
        out_shape=(jax.ShapeDtypeStruct((n, d), F32), jax.ShapeDtypeStruct((n, d), BF16)),
        compiler_params=_cparams(("parallel",)),
        name="merge",
    )(x, oa, ob, om, *[wp[k] for k in names])


def _top_values(sc, k, with_rank=False):
    vals = []
    cur = sc
    rank = jnp.full(sc.shape, float(k), F32) if with_rank else None
    for r in range(k):
        m = jnp.max(cur, axis=0, keepdims=True)
        vals.append(m)
        hit = cur == m
        if with_rank:
            rank = jnp.where(hit, float(r), rank)
        if r + 1 < k:
            cur = jnp.where(hit, NEG, cur)
    return (vals, rank) if with_rank else vals


def _stack_rows(rows, sub):
    out = jnp.zeros(sub.shape, F32)
    for r, row in enumerate(rows):
        out = jnp.where(sub == r, row, out)
    return out


def _route_kernel(hn_ref, wq_ref, sk_ref, cnt_ref, e1_ref, rank_ref, e2_ref):
    tt = hn_ref.shape[0]
    q = _dot(hn_ref[...], wq_ref[...]).astype(BF16)
    sub = lax.broadcasted_iota(jnp.int32, (PEER_TOPK, tt), 0)
    sub8 = lax.broadcasted_iota(jnp.int32, (8, tt), 0)
    for h in range(PEER_HEADS):
        sc1 = _dot_nt(sk_ref[2 * h], q[:, (2 * h) * PEER_HALF:(2 * h + 1) * PEER_HALF])
        sc2 = _dot_nt(sk_ref[2 * h + 1], q[:, (2 * h + 1) * PEER_HALF:(2 * h + 2) * PEER_HALF])
        t1 = _top_values(sc1, PEER_TOPK)
        t2_rows, rank2 = _top_values(sc2, PEER_TOPK, with_rank=True)
        t2 = _stack_rows(t2_rows, sub)
        t2h = t2[:8]
        cands = []
        for a in range(PEER_TOPK):
            nb = PEER_TOPK // (a + 1)
            c = t1[a] + (t2 if nb > 8 else t2h)
            if nb < c.shape[0]:
                c = jnp.where((sub if nb > 8 else sub8) < nb, c, NEG)
            cands.append(c)
        cand = jnp.concatenate(cands, axis=0)
        tau = _top_values(cand, PEER_TOPK)[-1]
        top = t1[0] + t2[0:1]
        z = jnp.sum(jnp.where(cand >= tau, jnp.exp(cand - top), 0.0), axis=0, keepdims=True)
        cnt = jnp.zeros(sc1.shape, F32)
        for a in range(PEER_TOPK):
            ca = jnp.sum(jnp.where(cands[a] >= tau, 1.0, 0.0), axis=0, keepdims=True)
            cnt = jnp.where(sc1 == t1[a], ca, cnt)
        cnt_ref[h] = cnt
        e1_ref[h] = jnp.exp(sc1 - t1[0]) * (1.0 / z)
        rank_ref[h] = rank2.astype(BF16)
        e2_ref[h] = jnp.exp(sc2 - t2[0:1]).astype(BF16)


def _route(hn, wp, *, tt):
    n, d = hn.shape
    big = pl.BlockSpec((PEER_HEADS, PEER_NKEYS, tt), lambda i: (0, 0, i))
    big_shape = jax.ShapeDtypeStruct((PEER_HEADS, PEER_NKEYS, n), F32)
    big_half = jax.ShapeDtypeStruct((PEER_HEADS, PEER_NKEYS, n), BF16)
    return pl.pallas_call(
        _route_kernel,
        grid=(n // tt,),
        in_specs=[pl.BlockSpec((tt, d), lambda i: (i, 0)), _full_spec(wp["peer_wq"].shape),
                  _full_spec(wp["peer_subkeys"].shape)],
        out_specs=(big, big, big, big),
        out_shape=(big_shape, big_shape, big_half, big_half),
        compiler_params=_cparams(("parallel",)),
        name="peer_route",
    )(hn, wp["peer_wq"], wp["peer_subkeys"])


PAIR = 2 * PEER_NKEYS
DOWN_PAIRS = 4
SUBROWS = 128


def _peer_kernel(hn_in, u_ref, vt_ref, cnt_ref, e1_ref, rank_in, e2_in, x1_ref, gn_ref, y_ref,
                 acc_scr, hw_scr, rank_ref, e2_ref, hn_ref, *, te, ne):
    j = pl.program_id(1)
    tt = hn_in.shape[0]
    npair = te // PAIR

    @pl.when(j == 0)
    def _():
        rank_ref[...] = rank_in[...]
        e2_ref[...] = e2_in[...]
        hn_ref[...] = hn_in[...]

    def pre_act(p):
        a = _dot_nt(u_ref[p * PAIR:(p + 1) * PAIR, :], hn_ref[...])
        return 0.5 * a * (1.0 + lax.erf(a * math.sqrt(0.5)))

    def down(p):
        lo = (p + 1 - DOWN_PAIRS) * PAIR
        return _dot(vt_ref[:, lo:(p + 1) * PAIR], hw_scr[lo:(p + 1) * PAIR, :])

    groups = [(tc, sb) for tc in range(tt // LANE) for sb in range(PEER_NKEYS // SUBROWS)]
    zero = jnp.zeros((), BF16)
    act = pre_act(0)
    tot = None
    carry = None
    for p in range(npair):
        nxt = None
        for gi, (tc, sb) in enumerate(groups):
            if gi == 1 and p + 1 < npair:
                nxt = pre_act(p + 1)
            if gi == len(groups) // 2 and p >= 1 and p % DOWN_PAIRS == 0:
                part = down(p - 1)
                tot = part if tot is None else tot + part
            ls = slice(tc * LANE, (tc + 1) * LANE)
            i2 = slice(sb * SUBROWS, (sb + 1) * SUBROWS)
            w = [None, None] if carry is None else [carry * zero, carry * zero]
            for h in range(PEER_HEADS):
                rk = rank_ref[h, i2, ls]
                e2b = e2_ref[h, i2, ls]
                for rr in range(2):
                    i1 = 2 * p + rr
                    cnt = jnp.broadcast_to(cnt_ref[h, i1:i1 + 1, ls], (SUBROWS, LANE)).astype(BF16)
                    e1b = jnp.broadcast_to(e1_ref[h, i1:i1 + 1, ls], (SUBROWS, LANE)).astype(BF16)
                    cw = jnp.where(rk < cnt, e2b, zero) * e1b
                    w[rr] = cw if w[rr] is None else w[rr] + cw
            carry = w[1]
            for rr in range(2):
                r0 = rr * PEER_NKEYS + sb * SUBROWS
                hw_scr[p * PAIR + r0:p * PAIR + r0 + SUBROWS, ls] = w[rr] * act[r0:r0 + SUBROWS, ls].astype(BF16)
        act = nxt
    part = down(npair - 1)
    tot = part if tot is None else tot + part

    @pl.when(j == 0)
    def _():
        acc_scr[...] = tot

    @pl.when(j > 0)
    def _():
        acc_scr[...] += tot

    @pl.when(j == ne - 1)
    def _():
        xr = x1_ref[...] + acc_scr[...].T
        y_ref[...] = _rms(xr, gn_ref[...])


def _peer(hn, cnt, e1, rank2, e2, x1, wp, *, tt, te):
    n, d = hn.shape
    ne = wp["peer_u"].shape[0] // te
    big = pl.BlockSpec((PEER_HEADS, PEER_NKEYS, tt), lambda i, j: (0, 0, i))
    rows = pl.BlockSpec((PEER_HEADS, te // PEER_NKEYS, tt), lambda i, j: (0, j, i))
    kern = functools.partial(_peer_kernel, te=te, ne=ne)
    return pl.pallas_call(
        kern,
        grid=(n // tt, ne),
        in_specs=[pl.BlockSpec((tt, d), lambda i, j: (i, 0)),
                  pl.BlockSpec((te, d), lambda i, j: (j, 0)),
                  pl.BlockSpec((d, te), lambda i, j: (0, j)),
                  rows, rows, big, big,
                  pl.BlockSpec((tt, d), lambda i, j: (i, 0)),
                  pl.BlockSpec((1, d), lambda i, j: (0, 0))],
        out_specs=pl.BlockSpec((tt, d), lambda i, j: (i, 0)),
        out_shape=jax.ShapeDtypeStruct((n, d), F32),
        scratch_shapes=[pltpu.VMEM((d, tt), F32), pltpu.VMEM((te, tt), BF16),
                        pltpu.VMEM((PEER_HEADS, PEER_NKEYS, tt), BF16),
                        pltpu.VMEM((PEER_HEADS, PEER_NKEYS, tt), BF16),
                        pltpu.VMEM((tt, d), BF16)],
        compiler_params=_cparams(("parallel", "arbitrary")),
        name="peer_experts",
    )(hn, wp["peer_u"], wp["peer_vt"], cnt, e1, rank2, e2, x1, wp["norm_final"])


def _prep_weights(l, p):
    f = lambda a: a.astype(F32)
    w_in = f(p["w_in"][l])
    o_cq, o_ckv, o_kr = 0, MLA_Q_LORA, MLA_Q_LORA + MLA_KV_LORA
    o_dq = o_kr + MLA_ROPE
    half = MLA_ROPE // 2
    d = w_in.shape[0]
    kr = w_in[:, o_kr:o_dq]
    pad = jnp.zeros((d, LANE - MLA_ROPE), F32)
    w_in2 = jnp.concatenate([
        w_in[:, o_cq:o_kr], kr, pad,
        -kr[:, half:], kr[:, :half], pad,
        w_in[:, o_dq:]], axis=1)
    assert w_in2.shape[1] == _C_END

    w_uq = f(p["w_uq"][l]).reshape(MLA_Q_LORA, MLA_HEADS, MLA_NOPE + MLA_ROPE)
    nope, x1, x2 = w_uq[..., :MLA_NOPE], w_uq[..., MLA_NOPE:MLA_NOPE + half], w_uq[..., MLA_NOPE + half:]
    zpad = jnp.zeros((MLA_Q_LORA, MLA_HEADS, HEAD_W - MLA_NOPE - MLA_ROPE), F32)
    q_slab = jnp.concatenate([nope, x1, x2, zpad], axis=-1).reshape(MLA_Q_LORA, -1)
    r_slab = jnp.concatenate([jnp.zeros_like(nope), -x2, x1, zpad], axis=-1).reshape(MLA_Q_LORA, -1)
    w_uq2 = jnp.concatenate([q_slab, r_slab], axis=1)

    w_uk = f(p["w_uk"][l]).reshape(MLA_KV_LORA, MLA_HEADS, MLA_NOPE)
    w_uk2 = jnp.concatenate([w_uk, jnp.zeros((MLA_KV_LORA, MLA_HEADS, HEAD_W - MLA_NOPE), F32)],
                            axis=-1).reshape(MLA_KV_LORA, -1)
    r_idx = jnp.arange(MLA_ROPE)
    cols = jnp.arange(MLA_HEADS * HEAD_W)
    p_kr = ((cols[None, :] % HEAD_W) == (MLA_NOPE + r_idx[:, None])).astype(F32)

    slopes = jnp.exp2(-8.0 * jnp.arange(1, DIFF_HEADS + 1, dtype=F32) / DIFF_HEADS)
    sub = f(p["diff_subln"][l])
    row = lambda a: f(a).reshape(1, -1)
    return {
        "norm_mix": row(p["norm_mix"][l]), "w_in": w_in2.astype(BF16),
        "mla_q_norm": row(p["mla_q_norm"][l]), "w_uq": w_uq2.astype(BF16),
        "mla_kv_norm": row(p["mla_kv_norm"][l]),
        "w_uk": w_uk2.astype(BF16), "p_kr": p_kr.astype(BF16), "w_uv": p["w_uv"][l].astype(BF16),
        "slopes": slopes,
        "diff_lq1": row(p["diff_lq1"][l]), "diff_lk1": row(p["diff_lk1"][l]),
        "diff_lq2": row(p["diff_lq2"][l]), "diff_lk2": row(p["diff_lk2"][l]),
        "diff_subln2": jnp.concatenate([sub, sub]).reshape(1, -1),
        "norm_mem": row(p["norm_mem"][l]), "w_mem_kv": p["w_mem_kv"][l].astype(BF16),
        "w_gate": p["w_gate"][l].astype(BF16), "b_gate": row(p["b_gate"][l]),
        "w_br_a": p["w_br_a"][l].astype(BF16), "w_br_b": p["w_br_b"][l].astype(BF16),
        "w_br_m": p["w_br_m"][l].astype(BF16), "w_o": p["w_o"][l].astype(BF16),
        "norm_ffn": row(p["norm_ffn"][l]),
        "peer_wq": p["peer_wq"][l].astype(BF16),
        "peer_subkeys": p["peer_subkeys"][l].reshape(PEER_HEADS * 2, PEER_NKEYS, PEER_HALF).astype(BF16),
        "peer_u": p["peer_u"][l].astype(BF16), "peer_vt": p["peer_v"][l].T.astype(BF16),
        "norm_final": row(p["norm_final"]),
    }


def _rope_tables(pos):
    half = MLA_ROPE // 2
    inv = 1.0 / (ROPE_THETA ** (jnp.arange(half, dtype=F32) / half))
    ang = pos.astype(F32)[:, None] * inv[None, :]
    cos, sin = jnp.cos(ang), jnp.sin(ang)
    n = pos.shape[0]
    one = jnp.ones((n, MLA_NOPE), F32)
    zq = jnp.zeros((n, HEAD_W - MLA_NOPE - MLA_ROPE), F32)
    cq = jnp.concatenate([one, cos, cos, zq], axis=1) * MLA_SCALE
    sq = jnp.concatenate([0.0 * one, sin, sin, zq], axis=1) * MLA_SCALE
    zk = jnp.zeros((n, LANE - MLA_ROPE), F32)
    ck = jnp.concatenate([cos, cos, zk], axis=1)
    sk = jnp.concatenate([sin, sin, zk], axis=1)
    return cq, sq, ck, sk


def _layer(x, pos, past, mem_k, mem_v, wp, layer, cfg):
    b, s, d = x.shape
    n = b * s
    xf = x.reshape(n, d)
    tabs = _rope_tables(pos)
    if s % cfg["tm"] == 0:
        pos_blocks = s // cfg["tm"]
    else:
        tabs = tuple(jnp.tile(t, (b, 1)) for t in tabs)
        pos_blocks = n // cfg["tm"]
    ckv, kr, dk, dv, qm, dqb, dkb, dvb, mqb = _proj_in(xf, tabs, wp, tm=cfg["tm"], pos_blocks=pos_blocks)
    new_rows = (ckv.reshape(b, s, -1), kr.reshape(b, s, -1), dk.reshape(b, s, DIFF_HEADS, DIFF_V),
                dv.reshape(b, s, DIFF_HEADS, DIFF_V))
    r3 = lambda a: a.reshape(b, s, -1)
    if past is None:
        ckv_all, kr_all = ckv, kr
        dk_all, dv_all = r3(dkb), r3(dvb)
        kk = s
        q_off = 0
    else:
        p_len = past[0].shape[1]
        kk = p_len + s
        q_off = p_len
        ckv_all = jnp.concatenate([past[0], r3(ckv)], axis=1).reshape(b * kk, -1)
        kr_all = jnp.concatenate([past[1], r3(kr)], axis=1).reshape(b * kk, -1)
        dk_all = jnp.concatenate([past[2].reshape(b, p_len, -1).astype(BF16), r3(dkb)], axis=1)
        dv_all = jnp.concatenate([past[3].reshape(b, p_len, -1).astype(BF16), r3(dvb)], axis=1)
    k_mla, v_mla = _kv_up(ckv_all, kr_all, wp, tm=cfg["tm_kv"])
    att = dict(tq=cfg["tq"], tk=cfg["tk"] if past is None else kk, q_off=q_off)
    o_a = _mla_attention(r3(qm), k_mla.reshape(b, kk, -1), v_mla.reshape(b, kk, -1), **att)
    lam_init = 0.8 - 0.6 * math.exp(-0.3 * layer)
    o_b = _diff_attention(r3(dqb), dk_all, dv_all, wp, lam_init=lam_init, **att)
    o_m = _mem_attention(r3(mqb), mem_k, mem_v, tq=cfg["tq"])
    x1, hn = _merge(xf, o_a.reshape(n, -1), o_b.reshape(n, -1), o_m.reshape(n, -1), wp, tm=cfg["tm"])
    return x1, hn, new_rows


def _peer_and_norm(x1, hn, wp, cfg):
    cnt, e1, rank2, e2 = _route(hn, wp, tt=cfg["tt"])
    return _peer(hn, cnt, e1, rank2, e2, x1, wp, tt=cfg["tt"], te=cfg["te"])


_CFG_PROMPT = dict(tm=512, tm_kv=512, tq=512, tk=512, tt=512, te=2048)
_CFG_SAMPLE = dict(tm=256, tm_kv=256, tq=32, tk=None, tt=256, te=2048)


def kernel(x_prompt, x_sample, cache_mla_ckv, cache_mla_krope, cache_diff_k, cache_diff_v, cache_mem_k, cache_mem_v, mem_prompt, norm_mix, w_in, mla_q_norm, w_uq, mla_kv_norm, w_uk, w_uv, diff_lq1, diff_lk1, diff_lq2, diff_lk2, diff_subln, norm_mem, w_mem_kv, w_br_a, w_br_b, w_br_m, w_gate, b_gate, w_o, norm_ffn, peer_wq, peer_subkeys, peer_u, peer_v, norm_final):
    params = dict(norm_mix=norm_mix, w_in=w_in, mla_q_norm=mla_q_norm, w_uq=w_uq, mla_kv_norm=mla_kv_norm,
                  w_uk=w_uk, w_uv=w_uv, diff_lq1=diff_lq1, diff_lk1=diff_lk1, diff_lq2=diff_lq2,
                  diff_lk2=diff_lk2, diff_subln=diff_subln, norm_mem=norm_mem, w_mem_kv=w_mem_kv,
                  w_br_a=w_br_a, w_br_b=w_br_b, w_br_m=w_br_m, w_gate=w_gate, b_gate=b_gate, w_o=w_o,
                  norm_ffn=norm_ffn, peer_wq=peer_wq, peer_subkeys=peer_subkeys, peer_u=peer_u,
                  peer_v=peer_v, norm_final=norm_final)
    depth = w_in.shape[0]
    assert depth == 1, "the final norm is fused into the last PEER step of a single layer"
    bp, sp, d = x_prompt.shape
    bs, ss, _ = x_sample.shape
    n_mem = mem_prompt.shape[1]
    past_len = cache_mla_ckv.shape[2]
    pos_p = jnp.arange(sp, dtype=jnp.int32)
    pos_s = past_len + jnp.arange(ss, dtype=jnp.int32)

    l = 0
    wp = _prep_weights(l, params)
    mk, mv, mkb, mvb = _mem_kv(mem_prompt.reshape(bp * n_mem, d), wp, tm=n_mem)
    m3 = lambda a, b: a.reshape(b, n_mem, -1)
    x1p, hnp, rows_p = _layer(x_prompt, pos_p, None, m3(mkb, bp), m3(mvb, bp), wp, l, _CFG_PROMPT)
    past = (cache_mla_ckv[l], cache_mla_krope[l], cache_diff_k[l], cache_diff_v[l])
    x1s, hns, rows_s = _layer(x_sample, pos_s, past, m3(cache_mem_k[l].astype(BF16), bs),
                              m3(cache_mem_v[l].astype(BF16), bs), wp, l, _CFG_SAMPLE)
    y_prompt = _peer_and_norm(x1p, hnp, wp, _CFG_PROMPT).reshape(bp, sp, d)
    y_sample = _peer_and_norm(x1s, hns, wp, _CFG_SAMPLE).reshape(bs, ss, d)
    st = lambda a: a[None]
    mem4 = lambda a: a.reshape(bp, n_mem, MEM_HEADS, MEM_DH)[None]
    return (y_prompt, y_sample,
            st(rows_p[0]), st(rows_p[1]), st(rows_p[2]), st(rows_p[3]),
            mem4(mk), mem4(mv),
            st(rows_s[0]), st(rows_s[1]), st(rows_s[2]), st(rows_s[3]))
```

```python
import functools
import math

import jax
import jax.numpy as jnp
from jax import lax
from jax.experimental import pallas as pl
from jax.experimental.pallas import tpu as pltpu

F32 = jnp.float32
BF16 = jnp.bfloat16

CHUNK = 64
CHUNK_SHIFT = 6
EPS = 1e-6
NEG = -1e30
MLA_HEADS = 8
MLA_Q_LORA = 384
MLA_KV_LORA = 256
MLA_NOPE = 64
MLA_ROPE = 32
MLA_V = 64
ROPE_THETA = 10000.0
LOG2E = math.log2(math.e)
MLA_SCALE = (MLA_NOPE + MLA_ROPE) ** -0.5 * LOG2E
DIFF_HEADS = 8
DIFF_DH = 32
DIFF_V = 2 * DIFF_DH
DIFF_SCALE = DIFF_DH ** -0.5 * LOG2E
MEM_HEADS = 4
MEM_DH = 128
MEM_SCALE = MEM_DH ** -0.5 * LOG2E
PEER_HEADS = 8
PEER_NKEYS = 128
PEER_HALF = 128
PEER_TOPK = 16
LANE = 128
HEAD_W = 128

VMEM_LIMIT = 56 * 1024 * 1024

_C_CQ = 0
_C_CKV = _C_CQ + MLA_Q_LORA
_C_KR = _C_CKV + MLA_KV_LORA
_C_KRR = _C_KR + LANE
_C_DQ = _C_KRR + LANE
_C_DK = _C_DQ + DIFF_HEADS * DIFF_V
_C_DV = _C_DK + DIFF_HEADS * DIFF_V
_C_MQ = _C_DV + DIFF_HEADS * DIFF_V
_C_END = _C_MQ + MEM_HEADS * MEM_DH


def _cparams(sem):
    return pltpu.CompilerParams(dimension_semantics=sem, vmem_limit_bytes=VMEM_LIMIT)


def _rms(x, g):
    return x * lax.rsqrt(jnp.mean(x * x, axis=-1, keepdims=True) + EPS) * g


def _dot(a, b):
    return jnp.dot(a, b, preferred_element_type=F32)


def _dot_nt(a, b):
    return lax.dot_general(a, b, (((1,), (1,)), ((), ())), preferred_element_type=F32)


def _full_spec(shape):
    nd = len(shape)
    return pl.BlockSpec(shape, lambda *_: (0,) * nd)


def _proj_in_kernel(x_ref, g_ref, win_ref, qn_ref, wuq_ref, kvn_ref, cq_ref, sq_ref, ck_ref, sk_ref,
                    ckv_ref, kr_ref, dk_ref, dv_ref, qm_ref, dqb_ref, dkb_ref, dvb_ref, mqb_ref):
    h = _rms(x_ref[...], g_ref[...]).astype(BF16)
    z = _dot(h, win_ref[...])
    cqn = _rms(z[:, _C_CQ:_C_CKV], qn_ref[...]).astype(BF16)
    q2 = _dot(cqn, wuq_ref[...])
    cq = cq_ref[...]
    sq = sq_ref[...]
    nq = MLA_HEADS * HEAD_W
    for hh in range(MLA_HEADS):
        lo = hh * HEAD_W
        qm_ref[:, lo:lo + HEAD_W] = (q2[:, lo:lo + HEAD_W] * cq
                                     + q2[:, nq + lo:nq + lo + HEAD_W] * sq).astype(BF16)
    ckv_ref[...] = _rms(z[:, _C_CKV:_C_KR], kvn_ref[...])
    kr = z[:, _C_KR:_C_KRR] * ck_ref[...] + z[:, _C_KRR:_C_DQ] * sk_ref[...]
    kr_ref[...] = kr[:, :MLA_ROPE]
    dk = z[:, _C_DK:_C_DV]
    dv = z[:, _C_DV:_C_MQ]
    dk_ref[...] = dk
    dv_ref[...] = dv
    dkb_ref[...] = dk.astype(BF16)
    dvb_ref[...] = dv.astype(BF16)
    dqb_ref[...] = (z[:, _C_DQ:_C_DK] * DIFF_SCALE).astype(BF16)
    mqb_ref[...] = (z[:, _C_MQ:_C_END] * MEM_SCALE).astype(BF16)


def _proj_in(x, tabs, wp, *, tm, pos_blocks):
    n, d = x.shape
    grid = (n // tm,)
    row = lambda w: pl.BlockSpec((tm, w), lambda i: (i, 0))
    tab = pl.BlockSpec((tm, LANE), lambda i: (i % pos_blocks, 0))
    dw = DIFF_HEADS * DIFF_V
    out_shape = (
        jax.ShapeDtypeStruct((n, MLA_KV_LORA), F32),
        jax.ShapeDtypeStruct((n, MLA_ROPE), F32),
        jax.ShapeDtypeStruct((n, dw), F32),
        jax.ShapeDtypeStruct((n, dw), F32),
        jax.ShapeDtypeStruct((n, MLA_HEADS * HEAD_W), BF16),
        jax.ShapeDtypeStruct((n, dw), BF16),
        jax.ShapeDtypeStruct((n, dw), BF16),
        jax.ShapeDtypeStruct((n, dw), BF16),
        jax.ShapeDtypeStruct((n, MEM_HEADS * MEM_DH), BF16),
    )
    return pl.pallas_call(
        _proj_in_kernel,
        grid=grid,
        in_specs=[row(d), _full_spec((1, d)), _full_spec(wp["w_in"].shape), _full_spec((1, MLA_Q_LORA)),
                  _full_spec(wp["w_uq"].shape), _full_spec((1, MLA_KV_LORA)), tab, tab, tab, tab],
        out_specs=(row(MLA_KV_LORA), row(MLA_ROPE), row(dw), row(dw), row(MLA_HEADS * HEAD_W),
                   row(dw), row(dw), row(dw), row(MEM_HEADS * MEM_DH)),
        out_shape=out_shape,
        compiler_params=_cparams(("parallel",)),
        name="proj_in",
    )(x, wp["norm_mix"], wp["w_in"], wp["mla_q_norm"], wp["w_uq"], wp["mla_kv_norm"], *tabs)


def _kv_up_kernel(ckv_ref, kr_ref, wuk_ref, pk_ref, wuv_ref, k_ref, v_ref):
    c = ckv_ref[...].astype(BF16)
    k = _dot(c, wuk_ref[...]) + _dot(kr_ref[...].astype(BF16), pk_ref[...])
    k_ref[...] = k.astype(BF16)
    v_ref[...] = _dot(c, wuv_ref[...]).astype(BF16)


def _kv_up(ckv, kr, wp, *, tm):
    n = ckv.shape[0]
    row = lambda w: pl.BlockSpec((tm, w), lambda i: (i, 0))
    return pl.pallas_call(
        _kv_up_kernel,
        grid=(n // tm,),
        in_specs=[row(MLA_KV_LORA), row(MLA_ROPE), _full_spec(wp["w_uk"].shape),
                  _full_spec(wp["p_kr"].shape), _full_spec(wp["w_uv"].shape)],
        out_specs=(row(MLA_HEADS * HEAD_W), row(MLA_HEADS * MLA_V)),
        out_shape=(jax.ShapeDtypeStruct((n, MLA_HEADS * HEAD_W), BF16),
                   jax.ShapeDtypeStruct((n, MLA_HEADS * MLA_V), BF16)),
        compiler_params=_cparams(("parallel",)),
        name="kv_up",
    )(ckv, kr, wp["w_uk"], wp["p_kr"], wp["w_uv"])


def _lanes(col, n):
    if n == LANE:
        return col
    if n % LANE == 0:
        return jnp.concatenate([col] * (n // LANE), axis=1)
    return jnp.broadcast_to(col[:, :1], (col.shape[0], n))


GROUP_VREGS = 16


def _softmax_block(score_group, tq, tk, row0, m_scr, al_scr, p_scr, l_scr=None):
    rg = min(tq, max(16, (GROUP_VREGS * 8 * LANE // tk) // 16 * 16))
    for g in range(tq // rg):
        r = slice(row0 + g * rg, row0 + (g + 1) * rg)
        m_prev = m_scr[r]
        m_next = jnp.maximum(m_prev, jnp.max(score_group(g, rg), axis=1, keepdims=True))
        al_scr[r] = jnp.exp2(m_prev - m_next)
        m_scr[r] = m_next
    for g in range(tq // rg):
        r = slice(row0 + g * rg, row0 + (g + 1) * rg)
        e = jnp.exp2(score_group(g, rg) - _lanes(m_scr[r], tk))
        p_scr[r] = e.astype(BF16)
        if l_scr is not None:
            l_scr[r] = al_scr[r] * l_scr[r] + jnp.sum(e, axis=1, keepdims=True)


def _accumulate(rows, v, al_scr, l_scr, acc_scr, p_scr):
    pv = _dot(p_scr[rows], jnp.concatenate([v, jnp.ones(v.shape, v.dtype)], axis=1))
    w = v.shape[1]
    acc_scr[rows] = acc_scr[rows] * al_scr[rows] + pv[:, :w]
    l_scr[rows] = l_scr[rows] * al_scr[rows] + pv[:, w:]


def _chunk_visible(q_lo, k_lo, g, rg, tk):
    qp = q_lo + g * rg + lax.broadcasted_iota(jnp.int32, (rg, tk), 0)
    kp = k_lo + lax.broadcasted_iota(jnp.int32, (rg, tk), 1)
    return qp, kp, (kp >> CHUNK_SHIFT) <= (qp >> CHUNK_SHIFT)


def _init_stats(m_scr, l_scr, acc_scr):
    m_scr[...] = jnp.full(m_scr.shape, NEG, F32)
    l_scr[...] = jnp.zeros(l_scr.shape, F32)
    acc_scr[...] = jnp.zeros(acc_scr.shape, F32)


def _block_range(qi, *, tq, tk, q_off, nk):
    q_lo = q_off + qi * tq
    q_hi = q_lo + (tq - 1)
    n_behind = jnp.minimum((q_lo + 1) // tk, nk)
    last_key = ((q_hi >> CHUNK_SHIFT) << CHUNK_SHIFT) + (CHUNK - 1)
    n_need = jnp.minimum(last_key // tk + 1, nk)
    return q_lo, n_behind, n_need


def _for_blocks(lo, hi, fn):
    lax.fori_loop(lo, hi, lambda i, c: (fn(i), c)[1], 0)


def _attn_specs(b, pairs, sq, sk, tq, qw, kw, vw):
    grid = (b, pairs, sq // tq)
    in_specs = [pl.BlockSpec((1, tq, qw), lambda bi, p, qi: (bi, qi, p)),
                pl.BlockSpec((1, sk, kw), lambda bi, p, qi: (bi, 0, p)),
                pl.BlockSpec((1, sk, vw), lambda bi, p, qi: (bi, 0, p))]
    out_spec = pl.BlockSpec((1, tq, vw), lambda bi, p, qi: (bi, qi, p))
    return grid, in_specs, out_spec


def _mla_kernel(q_ref, k_ref, v_ref, o_ref, m_scr, l_scr, al_scr, acc_scr, p_scr, *, tq, tk, q_off, nk):
    qi = pl.program_id(2)
    _init_stats(m_scr, l_scr, acc_scr)
    q_lo, n_behind, n_need = _block_range(qi, tq=tq, tk=tk, q_off=q_off, nk=nk)

    def block(i, masked):
        k_lo = pl.multiple_of(i * tk, tk)
        for hh in range(2):
            hs = slice(hh * HEAD_W, (hh + 1) * HEAD_W)
            s = _dot_nt(q_ref[0, :, hs], k_ref[0, pl.ds(k_lo, tk), hs])

            def score_group(g, rg, s=s):
                sg = s[g * rg:(g + 1) * rg]
                if masked:
                    sg = jnp.where(_chunk_visible(q_lo, k_lo, g, rg, tk)[2], sg, NEG)
                return sg

            _softmax_block(score_group, tq, tk, hh * tq, m_scr, al_scr, p_scr)
        _accumulate(slice(0, 2 * tq), v_ref[0, pl.ds(k_lo, tk), :], al_scr, l_scr, acc_scr, p_scr)

    _for_blocks(0, n_behind, lambda i: block(i, False))
    _for_blocks(n_behind, n_need, lambda i: block(i, True))

    lane = lax.broadcasted_iota(jnp.int32, (tq, LANE), 1)
    o0 = acc_scr[0:tq] / l_scr[0:tq]
    o1 = acc_scr[tq:2 * tq] / l_scr[tq:2 * tq]
    o_ref[0] = jnp.where(lane < MLA_V, o0, o1).astype(o_ref.dtype)


def _mla_attention(q, k, v, *, tq, tk, q_off):
    b, sq, _ = q.shape
    sk = k.shape[1]
    grid, in_specs, out_spec = _attn_specs(b, MLA_HEADS // 2, sq, sk, tq, 2 * HEAD_W, 2 * HEAD_W, 2 * MLA_V)
    kern = functools.partial(_mla_kernel, tq=tq, tk=tk, q_off=q_off, nk=sk // tk)
    stat = pltpu.VMEM((2 * tq, LANE), F32)
    return pl.pallas_call(
        kern,
        grid=grid,
        in_specs=in_specs,
        out_specs=out_spec,
        out_shape=jax.ShapeDtypeStruct((b, sq, MLA_HEADS * MLA_V), BF16),
        scratch_shapes=[stat, stat, stat, stat, pltpu.VMEM((2 * tq, tk), BF16)],
        compiler_params=_cparams(("parallel", "parallel", "parallel")),
        name="mla_attn",
    )(q, k, v)


def _diff_kernel(slope_ref, lq1_ref, lk1_ref, lq2_ref, lk2_ref, sub_ref, q_ref, k_ref, v_ref, o_ref,
                 qs_scr, m_scr, l_scr, al_scr, acc_scr, p_scr, *, tq, tk, q_off, nk, lam_init):
    pr = pl.program_id(1)
    qi = pl.program_id(2)
    _init_stats(m_scr, l_scr, acc_scr)
    q_lo, n_behind, n_need = _block_range(qi, tq=tq, tk=tk, q_off=q_off, nk=nk)

    q = q_ref[0]
    lane = lax.broadcasted_iota(jnp.int32, (tq, LANE), 1)
    for mi in range(4):
        lo = mi * DIFF_DH
        qs_scr[mi * tq:(mi + 1) * tq] = jnp.where((lane >= lo) & (lane < lo + DIFF_DH), q, jnp.zeros_like(q))

    def block(i, masked):
        k_lo = pl.multiple_of(i * tk, tk)
        kb = k_ref[0, pl.ds(k_lo, tk), :]
        kpos = (k_lo + lax.broadcasted_iota(jnp.int32, (1, tk), 1)).astype(F32)

        def scores(mi):
            return _dot_nt(qs_scr[mi * tq:(mi + 1) * tq], kb)

        def softmax(mi, s_map):
            slope = slope_ref[2 * pr + mi // 2] * LOG2E

            def score_group(g, rg):
                sg = s_map[g * rg:(g + 1) * rg]
                if masked:
                    qp, kp, vis = _chunk_visible(q_lo, k_lo, g, rg, tk)
                    return jnp.where(vis, sg + slope * jnp.minimum(kp, 2 * qp - kp).astype(F32), NEG)
                return sg + slope * kpos

            _softmax_block(score_group, tq, tk, mi * tq, m_scr, al_scr, p_scr, l_scr)

        def values(mi):
            rows = slice(mi * tq, (mi + 1) * tq)
            acc_scr[rows] = acc_scr[rows] * al_scr[rows] + _dot(p_scr[rows], v_ref[0, pl.ds(k_lo, tk), :])

        s0 = scores(0)
        s1 = scores(1)
        softmax(0, s0)
        s2 = scores(2)
        values(0)
        softmax(1, s1)
        s3 = scores(3)
        values(1)
        softmax(2, s2)
        values(2)
        softmax(3, s3)
        values(3)

    _for_blocks(0, n_behind, lambda i: block(i, False))
    _for_blocks(n_behind, n_need, lambda i: block(i, True))

    lam = (jnp.exp(jnp.sum(lq1_ref[...] * lk1_ref[...], axis=1, keepdims=True))
           - jnp.exp(jnp.sum(lq2_ref[...] * lk2_ref[...], axis=1, keepdims=True)) + lam_init)
    first = lane < DIFF_V
    on = [acc_scr[mi * tq:(mi + 1) * tq] / l_scr[mi * tq:(mi + 1) * tq] for mi in range(4)]
    o = jnp.where(first, on[0] - lam * on[1], on[2] - lam * on[3])
    sq = o * o
    ms0 = jnp.sum(jnp.where(first, sq, 0.0), axis=1, keepdims=True) * (1.0 / DIFF_V)
    ms1 = jnp.sum(jnp.where(first, 0.0, sq), axis=1, keepdims=True) * (1.0 / DIFF_V)
    r = jnp.where(first, lax.rsqrt(ms0 + EPS), lax.rsqrt(ms1 + EPS))
    o_ref[0] = ((o * r * sub_ref[...]) * (1.0 - lam_init)).astype(o_ref.dtype)


def _diff_attention(q, k, v, wp, *, tq, tk, q_off, lam_init):
    b, sq, _ = q.shape
    sk = k.shape[1]
    grid, in_specs, out_spec = _attn_specs(b, DIFF_HEADS // 2, sq, sk, tq, LANE, LANE, LANE)
    kern = functools.partial(_diff_kernel, tq=tq, tk=tk, q_off=q_off, nk=sk // tk, lam_init=lam_init)
    small = lambda w: pl.BlockSpec((1, w), lambda bi, p, qi: (0, 0))
    stat = pltpu.VMEM((4 * tq, LANE), F32)
    return pl.pallas_call(
        kern,
        grid=grid,
        in_specs=[pl.BlockSpec(memory_space=pltpu.SMEM),
                  small(DIFF_DH), small(DIFF_DH), small(DIFF_DH), small(DIFF_DH), small(LANE)] + in_specs,
        out_specs=out_spec,
        out_shape=jax.ShapeDtypeStruct((b, sq, DIFF_HEADS * DIFF_V), BF16),
        scratch_shapes=[pltpu.VMEM((4 * tq, LANE), BF16), stat, stat, stat, stat,
                        pltpu.VMEM((4 * tq, tk), BF16)],
        compiler_params=_cparams(("parallel", "parallel", "parallel")),
        name="diff_attn",
    )(wp["slopes"], wp["diff_lq1"], wp["diff_lk1"], wp["diff_lq2"], wp["diff_lk2"], wp["diff_subln2"],
      q, k, v)


def _mem_attn_kernel(q_ref, k_ref, v_ref, o_ref):
    for hh in range(MEM_HEADS):
        sl = slice(hh * MEM_DH, (hh + 1) * MEM_DH)
        s = _dot_nt(q_ref[0, :, sl], k_ref[0, :, sl])
        p = jnp.exp2(s - jnp.max(s, axis=1, keepdims=True))
        o = _dot(p.astype(BF16), v_ref[0, :, sl]) / jnp.sum(p, axis=1, keepdims=True)
        o_ref[0, :, sl] = o.astype(o_ref.dtype)


def _mem_attention(q, k, v, *, tq):
    b, sq, w = q.shape
    nm = k.shape[1]
    return pl.pallas_call(
        _mem_attn_kernel,
        grid=(b, sq // tq),
        in_specs=[pl.BlockSpec((1, tq, w), lambda bi, qi: (bi, qi, 0)),
                  pl.BlockSpec((1, nm, w), lambda bi, qi: (bi, 0, 0)),
                  pl.BlockSpec((1, nm, w), lambda bi, qi: (bi, 0, 0))],
        out_specs=pl.BlockSpec((1, tq, w), lambda bi, qi: (bi, qi, 0)),
        out_shape=jax.ShapeDtypeStruct((b, sq, w), BF16),
        compiler_params=_cparams(("parallel", "parallel")),
        name="mem_attn",
    )(q, k, v)


def _mem_kv_kernel(x_ref, g_ref, w_ref, k_ref, v_ref, kb_ref, vb_ref):
    h = _rms(x_ref[...], g_ref[...]).astype(BF16)
    kv = _dot(h, w_ref[...])
    w = MEM_HEADS * MEM_DH
    k_ref[...] = kv[:, :w]
    v_ref[...] = kv[:, w:]
    kb_ref[...] = kv[:, :w].astype(BF16)
    vb_ref[...] = kv[:, w:].astype(BF16)


def _mem_kv(mem, wp, *, tm):
    n, d = mem.shape
    w = MEM_HEADS * MEM_DH
    row = lambda c: pl.BlockSpec((tm, c), lambda i: (i, 0))
    return pl.pallas_call(
        _mem_kv_kernel,
        grid=(n // tm,),
        in_specs=[row(d), _full_spec((1, d)), _full_spec(wp["w_mem_kv"].shape)],
        out_specs=(row(w), row(w), row(w), row(w)),
        out_shape=(jax.ShapeDtypeStruct((n, w), F32), jax.ShapeDtypeStruct((n, w), F32),
                   jax.ShapeDtypeStruct((n, w), BF16), jax.ShapeDtypeStruct((n, w), BF16)),
        compiler_params=_cparams(("parallel",)),
        name="mem_kv",
    )(mem, wp["norm_mem"], wp["w_mem_kv"])


def _merge_kernel(x_ref, oa_ref, ob_ref, om_ref, g_ref, wg_ref, bg_ref, wa_ref, wb_ref, wm_ref, wo_ref,
                  gf_ref, x1_ref, hn_ref):
    x = x_ref[...]
    d = x.shape[1]
    h = _rms(x, g_ref[...]).astype(BF16)
    gates = jax.nn.sigmoid(_dot(h, wg_ref[...]) + bg_ref[...])
    merged = (gates[:, :d] * _dot(oa_ref[...], wa_ref[...])
              + gates[:, d:2 * d] * _dot(ob_ref[...], wb_ref[...])
              + gates[:, 2 * d:] * _dot(om_ref[...], wm_ref[...]))
    x1 = x + _dot(merged.astype(BF16), wo_ref[...])
    x1_ref[...] = x1
    hn_ref[...] = _rms(x1, gf_ref[...]).astype(BF16)


def _merge(x, oa, ob, om, wp, *, tm):
    n, d = x.shape
    row = lambda c: pl.BlockSpec((tm, c), lambda i: (i, 0))
    names = ("norm_mix", "w_gate", "b_gate", "w_br_a", "w_br_b", "w_br_m", "w_o", "norm_ffn")
    return pl.pallas_call(
        _merge_kernel,
        grid=(n // tm,),
        in_specs=[row(d), row(oa.shape[1]), row(ob.shape[1]), row(om.shape[1])]
                 + [_full_spec(wp[k].shape) for k in names],
        out_specs=(row(d), row(d)),
        out_shape=(jax.ShapeDtypeStruct((n, d), F32), jax.ShapeDtypeStruct((n, d), BF16)),
        compiler_params=_cparams(("parallel",)),
        name="merge",
    )(x, oa, ob, om, *[wp[k] for k in names])


def _top_values(sc, k, with_rank=False):
    vals = []
    cur = sc
    rank = jnp.full(sc.shape, float(k), F32) if with_rank else None
    for r in range(k):
        m = jnp.max(cur, axis=0, keepdims=True)
        vals.append(m)
        hit = cur == m
        if with_rank:
            rank = jnp.where(hit, float(r), rank)
        if r + 1 < k:
            cur = jnp.where(hit, NEG, cur)
    return (vals, rank) if with_rank else vals


def _stack_rows(rows, sub):
    out = jnp.zeros(sub.shape, F32)
    for r, row in enumerate(rows):
        out = jnp.where(sub == r, row, out)
    return out


def _route_kernel(hn_ref, wq_ref, sk_ref, cnt_ref, e1_ref, rank_ref, e2_ref):
    tt = hn_ref.shape[0]
    q = _dot(hn_ref[...], wq_ref[...]).astype(BF16)
    sub = lax.broadcasted_iota(jnp.int32, (PEER_TOPK, tt), 0)
    sub8 = lax.broadcasted_iota(jnp.int32, (8, tt), 0)
    for h in range(PEER_HEADS):
        sc1 = _dot_nt(sk_ref[2 * h], q[:, (2 * h) * PEER_HALF:(2 * h + 1) * PEER_HALF])
        sc2 = _dot_nt(sk_ref[2 * h + 1], q[:, (2 * h + 1) * PEER_HALF:(2 * h + 2) * PEER_HALF])
        t1 = _top_values(sc1, PEER_TOPK)
        t2_rows, rank2 = _top_values(sc2, PEER_TOPK, with_rank=True)
        t2 = _stack_rows(t2_rows, sub)
        t2h = t2[:8]
        cands = []
        for a in range(PEER_TOPK):
            nb = PEER_TOPK // (a + 1)
            c = t1[a] + (t2 if nb > 8 else t2h)
            if nb < c.shape[0]:
                c = jnp.where((sub if nb > 8 else sub8) < nb, c, NEG)
            cands.append(c)
        cand = jnp.concatenate(cands, axis=0)
        tau = _top_values(cand, PEER_TOPK)[-1]
        top = t1[0] + t2[0:1]
        z = jnp.sum(jnp.where(cand >= tau, jnp.exp(cand - top), 0.0), axis=0, keepdims=True)
        cnt = jnp.zeros(sc1.shape, F32)
        for a in range(PEER_TOPK):
            ca = jnp.sum(jnp.where(cands[a] >= tau, 1.0, 0.0), axis=0, keepdims=True)
            cnt = jnp.where(sc1 == t1[a], ca, cnt)
        cnt_ref[h] = cnt
        e1_ref[h] = jnp.exp(sc1 - t1[0]) * (0.5 / z)
        rank_ref[h] = rank2.astype(BF16)
        e2_ref[h] = jnp.exp(sc2 - t2[0:1]).astype(BF16)


def _route(hn, wp, *, tt):
    n, d = hn.shape
    big = pl.BlockSpec((PEER_HEADS, PEER_NKEYS, tt), lambda i: (0, 0, i))
    big_shape = jax.ShapeDtypeStruct((PEER_HEADS, PEER_NKEYS, n), F32)
    big_half = jax.ShapeDtypeStruct((PEER_HEADS, PEER_NKEYS, n), BF16)
    return pl.pallas_call(
        _route_kernel,
        grid=(n // tt,),
        in_specs=[pl.BlockSpec((tt, d), lambda i: (i, 0)), _full_spec(wp["peer_wq"].shape),
                  _full_spec(wp["peer_subkeys"].shape)],
        out_specs=(big, big, big, big),
        out_shape=(big_shape, big_shape, big_half, big_half),
        compiler_params=_cparams(("parallel",)),
        name="peer_route",
    )(hn, wp["peer_wq"], wp["peer_subkeys"])


PAIR = 2 * PEER_NKEYS
SUBROWS = 128


def _peer_kernel(hn_in, u_ref, vt_ref, cnt_ref, e1_ref, rank_in, e2_in, x1_ref, gn_ref, y_ref,
                 acc_scr, hw_scr, rank_ref, e2_ref, hn_ref, *, te, ne):
    j = pl.program_id(1)
    tt = hn_in.shape[0]
    npair = te // PAIR

    @pl.when(j == 0)
    def _():
        rank_ref[...] = rank_in[...]
        e2_ref[...] = e2_in[...]
        hn_ref[...] = hn_in[...]

    def pre_act(p):
        a = _dot_nt(u_ref[p * PAIR:(p + 1) * PAIR, :], hn_ref[...])
        return a * (1.0 + lax.erf(a * math.sqrt(0.5)))

    def down(lo, hi):
        return _dot(vt_ref[:, lo * PAIR:hi * PAIR], hw_scr[lo * PAIR:hi * PAIR, :])

    cuts = sorted({0, 3 * npair // 8, 3 * npair // 4, npair - 1, npair})
    piece_after = {hi: lo for lo, hi in zip(cuts[:-1], cuts[1:])}

    groups = [(tc, sb) for tc in range(tt // LANE) for sb in range(PEER_NKEYS // SUBROWS)]
    zero = jnp.zeros((), BF16)
    act = pre_act(0)
    tot = None
    carry = None
    for p in range(npair):
        nxt = None
        for gi, (tc, sb) in enumerate(groups):
            if gi == 1 and p + 1 < npair:
                nxt = pre_act(p + 1)
            if gi == len(groups) // 2 and p in piece_after:
                part = down(piece_after[p], p)
                tot = part if tot is None else tot + part
            ls = slice(tc * LANE, (tc + 1) * LANE)
            i2 = slice(sb * SUBROWS, (sb + 1) * SUBROWS)
            w = [None, None] if carry is None else [carry * zero, carry * zero]
            for h in range(PEER_HEADS):
                rk = rank_ref[h, i2, ls]
                e2b = e2_ref[h, i2, ls]
                for rr in range(2):
                    i1 = 2 * p + rr
                    cnt = jnp.broadcast_to(cnt_ref[h, i1:i1 + 1, ls], (SUBROWS, LANE)).astype(BF16)
                    e1b = jnp.broadcast_to(e1_ref[h, i1:i1 + 1, ls], (SUBROWS, LANE)).astype(BF16)
                    cw = jnp.where(rk < cnt, e2b, zero) * e1b
                    w[rr] = cw if w[rr] is None else w[rr] + cw
            carry = w[1]
            for rr in range(2):
                r0 = rr * PEER_NKEYS + sb * SUBROWS
                hw_scr[p * PAIR + r0:p * PAIR + r0 + SUBROWS, ls] = w[rr] * act[r0:r0 + SUBROWS, ls].astype(BF16)
        act = nxt
    part = down(piece_after[npair], npair)
    tot = part if tot is None else tot + part

    @pl.when(j == 0)
    def _():
        acc_scr[...] = tot

    @pl.when(j > 0)
    def _():
        acc_scr[...] += tot

    @pl.when(j == ne - 1)
    def _():
        xr = x1_ref[...] + acc_scr[...].T
        y_ref[...] = _rms(xr, gn_ref[...])


def _peer(hn, cnt, e1, rank2, e2, x1, wp, *, tt, te):
    n, d = hn.shape
    ne = wp["peer_u"].shape[0] // te
    big = pl.BlockSpec((PEER_HEADS, PEER_NKEYS, tt), lambda i, j: (0, 0, i))
    rows = pl.BlockSpec((PEER_HEADS, te // PEER_NKEYS, tt), lambda i, j: (0, j, i))
    kern = functools.partial(_peer_kernel, te=te, ne=ne)
    return pl.pallas_call(
        kern,
        grid=(n // tt, ne),
        in_specs=[pl.BlockSpec((tt, d), lambda i, j: (i, 0)),
                  pl.BlockSpec((te, d), lambda i, j: (j, 0)),
                  pl.BlockSpec((d, te), lambda i, j: (0, j)),
                  rows, rows, big, big,
                  pl.BlockSpec((tt, d), lambda i, j: (i, 0)),
                  pl.BlockSpec((1, d), lambda i, j: (0, 0))],
        out_specs=pl.BlockSpec((tt, d), lambda i, j: (i, 0)),
        out_shape=jax.ShapeDtypeStruct((n, d), F32),
        scratch_shapes=[pltpu.VMEM((d, tt), F32), pltpu.VMEM((te, tt), BF16),
                        pltpu.VMEM((PEER_HEADS, PEER_NKEYS, tt), BF16),
                        pltpu.VMEM((PEER_HEADS, PEER_NKEYS, tt), BF16),
                        pltpu.VMEM((tt, d), BF16)],
        compiler_params=_cparams(("parallel", "arbitrary")),
        name="peer_experts",
    )(hn, wp["peer_u"], wp["peer_vt"], cnt, e1, rank2, e2, x1, wp["norm_final"])


def _prep_weights(l, p):
    f = lambda a: a.astype(F32)
    w_in = f(p["w_in"][l])
    o_cq, o_ckv, o_kr = 0, MLA_Q_LORA, MLA_Q_LORA + MLA_KV_LORA
    o_dq = o_kr + MLA_ROPE
    half = MLA_ROPE // 2
    d = w_in.shape[0]
    kr = w_in[:, o_kr:o_dq]
    pad = jnp.zeros((d, LANE - MLA_ROPE), F32)
    w_in2 = jnp.concatenate([
        w_in[:, o_cq:o_kr], kr, pad,
        -kr[:, half:], kr[:, :half], pad,
        w_in[:, o_dq:]], axis=1)
    assert w_in2.shape[1] == _C_END

    w_uq = f(p["w_uq"][l]).reshape(MLA_Q_LORA, MLA_HEADS, MLA_NOPE + MLA_ROPE)
    nope, x1, x2 = w_uq[..., :MLA_NOPE], w_uq[..., MLA_NOPE:MLA_NOPE + half], w_uq[..., MLA_NOPE + half:]
    zpad = jnp.zeros((MLA_Q_LORA, MLA_HEADS, HEAD_W - MLA_NOPE - MLA_ROPE), F32)
    q_slab = jnp.concatenate([nope, x1, x2, zpad], axis=-1).reshape(MLA_Q_LORA, -1)
    r_slab = jnp.concatenate([jnp.zeros_like(nope), -x2, x1, zpad], axis=-1).reshape(MLA_Q_LORA, -1)
    w_uq2 = jnp.concatenate([q_slab, r_slab], axis=1)

    w_uk = f(p["w_uk"][l]).reshape(MLA_KV_LORA, MLA_HEADS, MLA_NOPE)
    w_uk2 = jnp.concatenate([w_uk, jnp.zeros((MLA_KV_LORA, MLA_HEADS, HEAD_W - MLA_NOPE), F32)],
                            axis=-1).reshape(MLA_KV_LORA, -1)
    r_idx = jnp.arange(MLA_ROPE)
    cols = jnp.arange(MLA_HEADS * HEAD_W)
    p_kr = ((cols[None, :] % HEAD_W) == (MLA_NOPE + r_idx[:, None])).astype(F32)

    slopes = jnp.exp2(-8.0 * jnp.arange(1, DIFF_HEADS + 1, dtype=F32) / DIFF_HEADS)
    sub = f(p["diff_subln"][l])
    row = lambda a: f(a).reshape(1, -1)
    return {
        "norm_mix": row(p["norm_mix"][l]), "w_in": w_in2.astype(BF16),
        "mla_q_norm": row(p["mla_q_norm"][l]), "w_uq": w_uq2.astype(BF16),
        "mla_kv_norm": row(p["mla_kv_norm"][l]),
        "w_uk": w_uk2.astype(BF16), "p_kr": p_kr.astype(BF16), "w_uv": p["w_uv"][l].astype(BF16),
        "slopes": slopes,
        "diff_lq1": row(p["diff_lq1"][l]), "diff_lk1": row(p["diff_lk1"][l]),
        "diff_lq2": row(p["diff_lq2"][l]), "diff_lk2": row(p["diff_lk2"][l]),
        "diff_subln2": jnp.concatenate([sub, sub]).reshape(1, -1),
        "norm_mem": row(p["norm_mem"][l]), "w_mem_kv": p["w_mem_kv"][l].astype(BF16),
        "w_gate": p["w_gate"][l].astype(BF16), "b_gate": row(p["b_gate"][l]),
        "w_br_a": p["w_br_a"][l].astype(BF16), "w_br_b": p["w_br_b"][l].astype(BF16),
        "w_br_m": p["w_br_m"][l].astype(BF16), "w_o": p["w_o"][l].astype(BF16),
        "norm_ffn": row(p["norm_ffn"][l]),
        "peer_wq": p["peer_wq"][l].astype(BF16),
        "peer_subkeys": p["peer_subkeys"][l].reshape(PEER_HEADS * 2, PEER_NKEYS, PEER_HALF).astype(BF16),
        "peer_u": p["peer_u"][l].astype(BF16), "peer_vt": p["peer_v"][l].T.astype(BF16),
        "norm_final": row(p["norm_final"]),
    }


def _rope_tables(pos):
    half = MLA_ROPE // 2
    inv = 1.0 / (ROPE_THETA ** (jnp.arange(half, dtype=F32) / half))
    ang = pos.astype(F32)[:, None] * inv[None, :]
    cos, sin = jnp.cos(ang), jnp.sin(ang)
    n = pos.shape[0]
    one = jnp.ones((n, MLA_NOPE), F32)
    zq = jnp.zeros((n, HEAD_W - MLA_NOPE - MLA_ROPE), F32)
    cq = jnp.concatenate([one, cos, cos, zq], axis=1) * MLA_SCALE
    sq = jnp.concatenate([0.0 * one, sin, sin, zq], axis=1) * MLA_SCALE
    zk = jnp.zeros((n, LANE - MLA_ROPE), F32)
    ck = jnp.concatenate([cos, cos, zk], axis=1)
    sk = jnp.concatenate([sin, sin, zk], axis=1)
    return cq, sq, ck, sk


def _layer(x, pos, past, mem_k, mem_v, wp, layer, cfg):
    b, s, d = x.shape
    n = b * s
    xf = x.reshape(n, d)
    tabs = _rope_tables(pos)
    if s % cfg["tm"] == 0:
        pos_blocks = s // cfg["tm"]
    else:
        tabs = tuple(jnp.tile(t, (b, 1)) for t in tabs)
        pos_blocks = n // cfg["tm"]
    ckv, kr, dk, dv, qm, dqb, dkb, dvb, mqb = _proj_in(xf, tabs, wp, tm=cfg["tm"], pos_blocks=pos_blocks)
    new_rows = (ckv.reshape(b, s, -1), kr.reshape(b, s, -1), dk.reshape(b, s, DIFF_HEADS, DIFF_V),
                dv.reshape(b, s, DIFF_HEADS, DIFF_V))
    r3 = lambda a: a.reshape(b, s, -1)
    if past is None:
        ckv_all, kr_all = ckv, kr
        dk_all, dv_all = r3(dkb), r3(dvb)
        kk = s
        q_off = 0
    else:
        p_len = past[0].shape[1]
        kk = p_len + s
        q_off = p_len
        ckv_all = jnp.concatenate([past[0], r3(ckv)], axis=1).reshape(b * kk, -1)
        kr_all = jnp.concatenate([past[1], r3(kr)], axis=1).reshape(b * kk, -1)
        dk_all = jnp.concatenate([past[2].reshape(b, p_len, -1).astype(BF16), r3(dkb)], axis=1)
        dv_all = jnp.concatenate([past[3].reshape(b, p_len, -1).astype(BF16), r3(dvb)], axis=1)
    k_mla, v_mla = _kv_up(ckv_all, kr_all, wp, tm=cfg["tm_kv"])
    att = dict(tk=cfg["tk"] if past is None else kk, q_off=q_off)
    o_a = _mla_attention(r3(qm), k_mla.reshape(b, kk, -1), v_mla.reshape(b, kk, -1), tq=cfg["tq_mla"], **att)
    lam_init = 0.8 - 0.6 * math.exp(-0.3 * layer)
    o_b = _diff_attention(r3(dqb), dk_all, dv_all, wp, lam_init=lam_init, tq=cfg["tq"], **att)
    o_m = _mem_attention(r3(mqb), mem_k, mem_v, tq=cfg["tq"])
    x1, hn = _merge(xf, o_a.reshape(n, -1), o_b.reshape(n, -1), o_m.reshape(n, -1), wp, tm=cfg["tm"])
    return x1, hn, new_rows


def _peer_and_norm(x1, hn, wp, cfg):
    cnt, e1, rank2, e2 = _route(hn, wp, tt=cfg["tt"])
    return _peer(hn, cnt, e1, rank2, e2, x1, wp, tt=cfg["tt"], te=cfg["te"])


_CFG_PROMPT = dict(tm=512, tm_kv=512, tq=512, tq_mla=1024, tk=512, tt=512, te=2048)
_CFG_SAMPLE = dict(tm=256, tm_kv=256, tq=32, tq_mla=32, tk=None, tt=256, te=2048)


def kernel(x_prompt, x_sample, cache_mla_ckv, cache_mla_krope, cache_diff_k, cache_diff_v, cache_mem_k, cache_mem_v, mem_prompt, norm_mix, w_in, mla_q_norm, w_uq, mla_kv_norm, w_uk, w_uv, diff_lq1, diff_lk1, diff_lq2, diff_lk2, diff_subln, norm_mem, w_mem_kv, w_br_a, w_br_b, w_br_m, w_gate, b_gate, w_o, norm_ffn, peer_wq, peer_subkeys, peer_u, peer_v, norm_final):
    params = dict(norm_mix=norm_mix, w_in=w_in, mla_q_norm=mla_q_norm, w_uq=w_uq, mla_kv_norm=mla_kv_norm,
                  w_uk=w_uk, w_uv=w_uv, diff_lq1=diff_lq1, diff_lk1=diff_lk1, diff_lq2=diff_lq2,
                  diff_lk2=diff_lk2, diff_subln=diff_subln, norm_mem=norm_mem, w_mem_kv=w_mem_kv,
                  w_br_a=w_br_a, w_br_b=w_br_b, w_br_m=w_br_m, w_gate=w_gate, b_gate=b_gate, w_o=w_o,
                  norm_ffn=norm_ffn, peer_wq=peer_wq, peer_subkeys=peer_subkeys, peer_u=peer_u,
                  peer_v=peer_v, norm_final=norm_final)
    depth = w_in.shape[0]
    assert depth == 1, "the final norm is fused into the last PEER step of a single layer"
    bp, sp, d = x_prompt.shape
    bs, ss, _ = x_sample.shape
    n_mem = mem_prompt.shape[1]
    past_len = cache_mla_ckv.shape[2]
    pos_p = jnp.arange(sp, dtype=jnp.int32)
    pos_s = past_len + jnp.arange(ss, dtype=jnp.int32)

    l = 0
    wp = _prep_weights(l, params)
    mk, mv, mkb, mvb = _mem_kv(mem_prompt.reshape(bp * n_mem, d), wp, tm=n_mem)
    m3 = lambda a, b: a.reshape(b, n_mem, -1)
    x1p, hnp, rows_p = _layer(x_prompt, pos_p, None, m3(mkb, bp), m3(mvb, bp), wp, l, _CFG_PROMPT)
    past = (cache_mla_ckv[l], cache_mla_krope[l], cache_diff_k[l], cache_diff_v[l])
    x1s, hns, rows_s = _layer(x_sample, pos_s, past, m3(cache_mem_k[l].astype(BF16), bs),
                              m3(cache_mem_v[l].astype(BF16), bs), wp, l, _CFG_SAMPLE)
    y_prompt = _peer_and_norm(x1p, hnp, wp, _CFG_PROMPT).reshape(bp, sp, d)
    y_sample = _peer_and_norm(x1s, hns, wp, _CFG_SAMPLE).reshape(bs, ss, d)
    st = lambda a: a[None]
    mem4 = lambda a: a.reshape(bp, n_mem, MEM_HEADS, MEM_DH)[None]
    return (y_prompt, y_sample,
            st(rows_p[0]), st(rows_p[1]), st(rows_p[2]), st(rows_p[3]),
            mem4(mk), mem4(mv),
            st(rows_s[0]), st(rows_s[1]), st(rows_s[2]), st(rows_s[3]))
```

```python
import functools
import math

import jax
import jax.numpy as jnp
import numpy as np
from jax import lax
from jax.experimental import pallas as pl
from jax.experimental.pallas import tpu as pltpu

F32 = jnp.float32
BF16 = jnp.bfloat16

CHUNK = 64
CHUNK_SHIFT = 6
EPS = 1e-6
NEG = -1e30
MLA_HEADS = 8
MLA_Q_LORA = 384
MLA_KV_LORA = 256
MLA_NOPE = 64
MLA_ROPE = 32
MLA_V = 64
ROPE_THETA = 10000.0
LOG2E = math.log2(math.e)
MLA_SCALE = (MLA_NOPE + MLA_ROPE) ** -0.5 * LOG2E
DIFF_HEADS = 8
DIFF_DH = 32
DIFF_V = 2 * DIFF_DH
DIFF_SCALE = DIFF_DH ** -0.5 * LOG2E
MEM_HEADS = 4
MEM_DH = 128
MEM_SCALE = MEM_DH ** -0.5 * LOG2E
PEER_HEADS = 8
PEER_NKEYS = 128
PEER_HALF = 128
PEER_TOPK = 16
LANE = 128
HEAD_W = 128

VMEM_LIMIT = 56 * 1024 * 1024

_C_CQ = 0
_C_CKV = _C_CQ + MLA_Q_LORA
_C_KR = _C_CKV + MLA_KV_LORA
_C_KRR = _C_KR + LANE
_C_DQ = _C_KRR + LANE
_C_DK = _C_DQ + DIFF_HEADS * DIFF_V
_C_DV = _C_DK + DIFF_HEADS * DIFF_V
_C_MQ = _C_DV + DIFF_HEADS * DIFF_V
_C_END = _C_MQ + MEM_HEADS * MEM_DH


def _cparams(sem):
    return pltpu.CompilerParams(dimension_semantics=sem, vmem_limit_bytes=VMEM_LIMIT)


def _rms(x, g):
    return x * lax.rsqrt(jnp.mean(x * x, axis=-1, keepdims=True) + EPS) * g


def _dot(a, b):
    return jnp.dot(a, b, preferred_element_type=F32)


def _dot_nt(a, b):
    return lax.dot_general(a, b, (((1,), (1,)), ((), ())), preferred_element_type=F32)


def _full_spec(shape):
    nd = len(shape)
    return pl.BlockSpec(shape, lambda *_: (0,) * nd)


def _proj_in_kernel(x_ref, g_ref, win_ref, qn_ref, wuq_ref, kvn_ref, cq_ref, sq_ref, ck_ref, sk_ref,
                    ckv_ref, kr_ref, dk_ref, dv_ref, qm_ref, dqb_ref, dkb_ref, dvb_ref, mqb_ref):
    h = _rms(x_ref[...], g_ref[...]).astype(BF16)
    z = _dot(h, win_ref[...])
    cqn = _rms(z[:, _C_CQ:_C_CKV], qn_ref[...]).astype(BF16)
    q2 = _dot(cqn, wuq_ref[...])
    cq = cq_ref[...]
    sq = sq_ref[...]
    nq = MLA_HEADS * HEAD_W
    for hh in range(MLA_HEADS):
        lo = hh * HEAD_W
        qm_ref[:, lo:lo + HEAD_W] = (q2[:, lo:lo + HEAD_W] * cq
                                     + q2[:, nq + lo:nq + lo + HEAD_W] * sq).astype(BF16)
    ckv_ref[...] = _rms(z[:, _C_CKV:_C_KR], kvn_ref[...])
    kr = z[:, _C_KR:_C_KRR] * ck_ref[...] + z[:, _C_KRR:_C_DQ] * sk_ref[...]
    kr_ref[...] = kr[:, :MLA_ROPE]
    dk = z[:, _C_DK:_C_DV]
    dv = z[:, _C_DV:_C_MQ]
    dk_ref[...] = dk
    dv_ref[...] = dv
    dkb_ref[...] = dk.astype(BF16)
    dvb_ref[...] = dv.astype(BF16)
    dqb_ref[...] = (z[:, _C_DQ:_C_DK] * DIFF_SCALE).astype(BF16)
    mqb_ref[...] = (z[:, _C_MQ:_C_END] * MEM_SCALE).astype(BF16)


def _proj_in(x, tabs, wp, *, tm, pos_blocks):
    n, d = x.shape
    grid = (n // tm,)
    row = lambda w: pl.BlockSpec((tm, w), lambda i: (i, 0))
    tab = pl.BlockSpec((tm, LANE), lambda i: (i % pos_blocks, 0))
    dw = DIFF_HEADS * DIFF_V
    out_shape = (
        jax.ShapeDtypeStruct((n, MLA_KV_LORA), F32),
        jax.ShapeDtypeStruct((n, MLA_ROPE), F32),
        jax.ShapeDtypeStruct((n, dw), F32),
        jax.ShapeDtypeStruct((n, dw), F32),
        jax.ShapeDtypeStruct((n, MLA_HEADS * HEAD_W), BF16),
        jax.ShapeDtypeStruct((n, dw), BF16),
        jax.ShapeDtypeStruct((n, dw), BF16),
        jax.ShapeDtypeStruct((n, dw), BF16),
        jax.ShapeDtypeStruct((n, MEM_HEADS * MEM_DH), BF16),
    )
    return pl.pallas_call(
        _proj_in_kernel,
        grid=grid,
        in_specs=[row(d), _full_spec((1, d)), _full_spec(wp["w_in"].shape), _full_spec((1, MLA_Q_LORA)),
                  _full_spec(wp["w_uq"].shape), _full_spec((1, MLA_KV_LORA)), tab, tab, tab, tab],
        out_specs=(row(MLA_KV_LORA), row(MLA_ROPE), row(dw), row(dw), row(MLA_HEADS * HEAD_W),
                   row(dw), row(dw), row(dw), row(MEM_HEADS * MEM_DH)),
        out_shape=out_shape,
        compiler_params=_cparams(("parallel",)),
        name="proj_in",
    )(x, wp["norm_mix"], wp["w_in"], wp["mla_q_norm"], wp["w_uq"], wp["mla_kv_norm"], *tabs)


def _kv_up_kernel(ckv_ref, kr_ref, wuk_ref, pk_ref, wuv_ref, k_ref, v_ref):
    c = ckv_ref[...].astype(BF16)
    k = _dot(c, wuk_ref[...]) + _dot(kr_ref[...].astype(BF16), pk_ref[...])
    k_ref[...] = k.astype(BF16)
    v_ref[...] = _dot(c, wuv_ref[...]).astype(BF16)


def _kv_up(ckv, kr, wp, *, tm):
    n = ckv.shape[0]
    row = lambda w: pl.BlockSpec((tm, w), lambda i: (i, 0))
    return pl.pallas_call(
        _kv_up_kernel,
        grid=(n // tm,),
        in_specs=[row(MLA_KV_LORA), row(MLA_ROPE), _full_spec(wp["w_uk"].shape),
                  _full_spec(wp["p_kr"].shape), _full_spec(wp["w_uv"].shape)],
        out_specs=(row(MLA_HEADS * HEAD_W), row(MLA_HEADS * MLA_V)),
        out_shape=(jax.ShapeDtypeStruct((n, MLA_HEADS * HEAD_W), BF16),
                   jax.ShapeDtypeStruct((n, MLA_HEADS * MLA_V), BF16)),
        compiler_params=_cparams(("parallel",)),
        name="kv_up",
    )(ckv, kr, wp["w_uk"], wp["p_kr"], wp["w_uv"])


def _lanes(col, n):
    if n == LANE:
        return col
    if n % LANE == 0:
        return jnp.concatenate([col] * (n // LANE), axis=1)
    return jnp.broadcast_to(col[:, :1], (col.shape[0], n))


GROUP_VREGS = 16


def _softmax_block(score_group, tq, tk, row0, m_scr, al_scr, p_scr, l_scr=None):
    rg = min(tq, max(16, (GROUP_VREGS * 8 * LANE // tk) // 16 * 16))
    for g in range(tq // rg):
        r = slice(row0 + g * rg, row0 + (g + 1) * rg)
        m_prev = m_scr[r]
        m_next = jnp.maximum(m_prev, jnp.max(score_group(g, rg), axis=1, keepdims=True))
        al_scr[r] = jnp.exp2(m_prev - m_next)
        m_scr[r] = m_next
    for g in range(tq // rg):
        r = slice(row0 + g * rg, row0 + (g + 1) * rg)
        e = jnp.exp2(score_group(g, rg) - _lanes(m_scr[r], tk))
        p_scr[r] = e.astype(BF16)
        if l_scr is not None:
            l_scr[r] = al_scr[r] * l_scr[r] + jnp.sum(e, axis=1, keepdims=True)


def _accumulate(rows, v, al_scr, l_scr, acc_scr, p_scr):
    pv = _dot(p_scr[rows], jnp.concatenate([v, jnp.ones(v.shape, v.dtype)], axis=1))
    w = v.shape[1]
    acc_scr[rows] = acc_scr[rows] * al_scr[rows] + pv[:, :w]
    l_scr[rows] = l_scr[rows] * al_scr[rows] + pv[:, w:]


def _chunk_visible(q_lo, k_lo, g, rg, tk):
    qp = q_lo + g * rg + lax.broadcasted_iota(jnp.int32, (rg, tk), 0)
    kp = k_lo + lax.broadcasted_iota(jnp.int32, (rg, tk), 1)
    return qp, kp, (kp >> CHUNK_SHIFT) <= (qp >> CHUNK_SHIFT)


def _init_stats(m_scr, l_scr, acc_scr):
    m_scr[...] = jnp.full(m_scr.shape, NEG, F32)
    l_scr[...] = jnp.zeros(l_scr.shape, F32)
    acc_scr[...] = jnp.zeros(acc_scr.shape, F32)


def _block_range(qi, *, tq, tk, q_off, nk):
    q_lo = q_off + qi * tq
    q_hi = q_lo + (tq - 1)
    n_behind = jnp.minimum((q_lo + 1) // tk, nk)
    last_key = ((q_hi >> CHUNK_SHIFT) << CHUNK_SHIFT) + (CHUNK - 1)
    n_need = jnp.minimum(last_key // tk + 1, nk)
    return q_lo, n_behind, n_need


def _for_blocks(lo, hi, fn):
    lax.fori_loop(lo, hi, lambda i, c: (fn(i), c)[1], 0)


def _attn_specs(b, pairs, sq, sk, tq, qw, kw, vw):
    grid = (b, pairs, sq // tq)
    in_specs = [pl.BlockSpec((1, tq, qw), lambda bi, p, qi: (bi, qi, p)),
                pl.BlockSpec((1, sk, kw), lambda bi, p, qi: (bi, 0, p)),
                pl.BlockSpec((1, sk, vw), lambda bi, p, qi: (bi, 0, p))]
    out_spec = pl.BlockSpec((1, tq, vw), lambda bi, p, qi: (bi, qi, p))
    return grid, in_specs, out_spec


def _mla_kernel(q_ref, k_ref, v_ref, o_ref, m_scr, l_scr, al_scr, acc_scr, p_scr, *, tq, tk, q_off, nk):
    qi = pl.program_id(2)
    _init_stats(m_scr, l_scr, acc_scr)
    q_lo, n_behind, n_need = _block_range(qi, tq=tq, tk=tk, q_off=q_off, nk=nk)

    def block(i, masked):
        k_lo = pl.multiple_of(i * tk, tk)
        for hh in range(2):
            hs = slice(hh * HEAD_W, (hh + 1) * HEAD_W)
            s = _dot_nt(q_ref[0, :, hs], k_ref[0, pl.ds(k_lo, tk), hs])

            def score_group(g, rg, s=s):
                sg = s[g * rg:(g + 1) * rg]
                if masked:
                    sg = jnp.where(_chunk_visible(q_lo, k_lo, g, rg, tk)[2], sg, NEG)
                return sg

            _softmax_block(score_group, tq, tk, hh * tq, m_scr, al_scr, p_scr)
        _accumulate(slice(0, 2 * tq), v_ref[0, pl.ds(k_lo, tk), :], al_scr, l_scr, acc_scr, p_scr)

    _for_blocks(0, n_behind, lambda i: block(i, False))
    _for_blocks(n_behind, n_need, lambda i: block(i, True))

    lane = lax.broadcasted_iota(jnp.int32, (tq, LANE), 1)
    o0 = acc_scr[0:tq] / l_scr[0:tq]
    o1 = acc_scr[tq:2 * tq] / l_scr[tq:2 * tq]
    o_ref[0] = jnp.where(lane < MLA_V, o0, o1).astype(o_ref.dtype)


def _mla_attention(q, k, v, *, tq, tk, q_off):
    b, sq, _ = q.shape
    sk = k.shape[1]
    grid, in_specs, out_spec = _attn_specs(b, MLA_HEADS // 2, sq, sk, tq, 2 * HEAD_W, 2 * HEAD_W, 2 * MLA_V)
    kern = functools.partial(_mla_kernel, tq=tq, tk=tk, q_off=q_off, nk=sk // tk)
    stat = pltpu.VMEM((2 * tq, LANE), F32)
    return pl.pallas_call(
        kern,
        grid=grid,
        in_specs=in_specs,
        out_specs=out_spec,
        out_shape=jax.ShapeDtypeStruct((b, sq, MLA_HEADS * MLA_V), BF16),
        scratch_shapes=[stat, stat, stat, stat, pltpu.VMEM((2 * tq, tk), BF16)],
        compiler_params=_cparams(("parallel", "parallel", "parallel")),
        name="mla_attn",
    )(q, k, v)


def _diff_kernel(slope_ref, lq1_ref, lk1_ref, lq2_ref, lk2_ref, sub_ref, q_ref, k_ref, v_ref, o_ref,
                 qs_scr, m_scr, l_scr, al_scr, acc_scr, p_scr, *, tq, tk, q_off, nk, lam_init):
    pr = pl.program_id(1)
    qi = pl.program_id(2)
    _init_stats(m_scr, l_scr, acc_scr)
    q_lo, n_behind, n_need = _block_range(qi, tq=tq, tk=tk, q_off=q_off, nk=nk)

    q = q_ref[0]
    lane = lax.broadcasted_iota(jnp.int32, (tq, LANE), 1)
    for mi in range(4):
        lo = mi * DIFF_DH
        qs_scr[mi * tq:(mi + 1) * tq] = jnp.where((lane >= lo) & (lane < lo + DIFF_DH), q, jnp.zeros_like(q))

    def block(i, masked):
        k_lo = pl.multiple_of(i * tk, tk)
        kb = k_ref[0, pl.ds(k_lo, tk), :]
        kpos = (k_lo + lax.broadcasted_iota(jnp.int32, (1, tk), 1)).astype(F32)

        def scores(mi):
            return _dot_nt(qs_scr[mi * tq:(mi + 1) * tq], kb)

        def softmax(mi, s_map):
            slope = slope_ref[2 * pr + mi // 2] * LOG2E

            def score_group(g, rg):
                sg = s_map[g * rg:(g + 1) * rg]
                if masked:
                    qp, kp, vis = _chunk_visible(q_lo, k_lo, g, rg, tk)
                    return jnp.where(vis, sg + slope * jnp.minimum(kp, 2 * qp - kp).astype(F32), NEG)
                return sg + slope * kpos

            _softmax_block(score_group, tq, tk, mi * tq, m_scr, al_scr, p_scr, l_scr)

        def values(mi):
            rows = slice(mi * tq, (mi + 1) * tq)
            acc_scr[rows] = acc_scr[rows] * al_scr[rows] + _dot(p_scr[rows], v_ref[0, pl.ds(k_lo, tk), :])

        s0 = scores(0)
        s1 = scores(1)
        softmax(0, s0)
        s2 = scores(2)
        values(0)
        softmax(1, s1)
        s3 = scores(3)
        values(1)
        softmax(2, s2)
        values(2)
        softmax(3, s3)
        values(3)

    _for_blocks(0, n_behind, lambda i: block(i, False))
    _for_blocks(n_behind, n_need, lambda i: block(i, True))

    lam = (jnp.exp(jnp.sum(lq1_ref[...] * lk1_ref[...], axis=1, keepdims=True))
           - jnp.exp(jnp.sum(lq2_ref[...] * lk2_ref[...], axis=1, keepdims=True)) + lam_init)
    first = lane < DIFF_V
    on = [acc_scr[mi * tq:(mi + 1) * tq] / l_scr[mi * tq:(mi + 1) * tq] for mi in range(4)]
    o = jnp.where(first, on[0] - lam * on[1], on[2] - lam * on[3])
    sq = o * o
    ms0 = jnp.sum(jnp.where(first, sq, 0.0), axis=1, keepdims=True) * (1.0 / DIFF_V)
    ms1 = jnp.sum(jnp.where(first, 0.0, sq), axis=1, keepdims=True) * (1.0 / DIFF_V)
    r = jnp.where(first, lax.rsqrt(ms0 + EPS), lax.rsqrt(ms1 + EPS))
    o_ref[0] = ((o * r * sub_ref[...]) * (1.0 - lam_init)).astype(o_ref.dtype)


def _diff_attention(q, k, v, wp, *, tq, tk, q_off, lam_init):
    b, sq, _ = q.shape
    sk = k.shape[1]
    grid, in_specs, out_spec = _attn_specs(b, DIFF_HEADS // 2, sq, sk, tq, LANE, LANE, LANE)
    kern = functools.partial(_diff_kernel, tq=tq, tk=tk, q_off=q_off, nk=sk // tk, lam_init=lam_init)
    small = lambda w: pl.BlockSpec((1, w), lambda bi, p, qi: (0, 0))
    stat = pltpu.VMEM((4 * tq, LANE), F32)
    return pl.pallas_call(
        kern,
        grid=grid,
        in_specs=[pl.BlockSpec(memory_space=pltpu.SMEM),
                  small(DIFF_DH), small(DIFF_DH), small(DIFF_DH), small(DIFF_DH), small(LANE)] + in_specs,
        out_specs=out_spec,
        out_shape=jax.ShapeDtypeStruct((b, sq, DIFF_HEADS * DIFF_V), BF16),
        scratch_shapes=[pltpu.VMEM((4 * tq, LANE), BF16), stat, stat, stat, stat,
                        pltpu.VMEM((4 * tq, tk), BF16)],
        compiler_params=_cparams(("parallel", "parallel", "parallel")),
        name="diff_attn",
    )(wp["slopes"], wp["diff_lq1"], wp["diff_lk1"], wp["diff_lq2"], wp["diff_lk2"], wp["diff_subln2"],
      q, k, v)


def _mem_attn_kernel(q_ref, k_ref, v_ref, o_ref):
    for hh in range(MEM_HEADS):
        sl = slice(hh * MEM_DH, (hh + 1) * MEM_DH)
        s = _dot_nt(q_ref[0, :, sl], k_ref[0, :, sl])
        p = jnp.exp2(s - jnp.max(s, axis=1, keepdims=True))
        o = _dot(p.astype(BF16), v_ref[0, :, sl]) / jnp.sum(p, axis=1, keepdims=True)
        o_ref[0, :, sl] = o.astype(o_ref.dtype)


def _mem_attention(q, k, v, *, tq):
    b, sq, w = q.shape
    nm = k.shape[1]
    return pl.pallas_call(
        _mem_attn_kernel,
        grid=(b, sq // tq),
        in_specs=[pl.BlockSpec((1, tq, w), lambda bi, qi: (bi, qi, 0)),
                  pl.BlockSpec((1, nm, w), lambda bi, qi: (bi, 0, 0)),
                  pl.BlockSpec((1, nm, w), lambda bi, qi: (bi, 0, 0))],
        out_specs=pl.BlockSpec((1, tq, w), lambda bi, qi: (bi, qi, 0)),
        out_shape=jax.ShapeDtypeStruct((b, sq, w), BF16),
        compiler_params=_cparams(("parallel", "parallel")),
        name="mem_attn",
    )(q, k, v)


def _mem_kv_kernel(x_ref, g_ref, w_ref, k_ref, v_ref, kb_ref, vb_ref):
    h = _rms(x_ref[...], g_ref[...]).astype(BF16)
    kv = _dot(h, w_ref[...])
    w = MEM_HEADS * MEM_DH
    k_ref[...] = kv[:, :w]
    v_ref[...] = kv[:, w:]
    kb_ref[...] = kv[:, :w].astype(BF16)
    vb_ref[...] = kv[:, w:].astype(BF16)


def _mem_kv(mem, wp, *, tm):
    n, d = mem.shape
    w = MEM_HEADS * MEM_DH
    row = lambda c: pl.BlockSpec((tm, c), lambda i: (i, 0))
    return pl.pallas_call(
        _mem_kv_kernel,
        grid=(n // tm,),
        in_specs=[row(d), _full_spec((1, d)), _full_spec(wp["w_mem_kv"].shape)],
        out_specs=(row(w), row(w), row(w), row(w)),
        out_shape=(jax.ShapeDtypeStruct((n, w), F32), jax.ShapeDtypeStruct((n, w), F32),
                   jax.ShapeDtypeStruct((n, w), BF16), jax.ShapeDtypeStruct((n, w), BF16)),
        compiler_params=_cparams(("parallel",)),
        name="mem_kv",
    )(mem, wp["norm_mem"], wp["w_mem_kv"])


def _merge_kernel(x_ref, oa_ref, ob_ref, om_ref, g_ref, wg_ref, bg_ref, wa_ref, wb_ref, wm_ref, wo_ref,
                  gf_ref, x1_ref, hn_ref):
    x = x_ref[...]
    d = x.shape[1]
    h = _rms(x, g_ref[...]).astype(BF16)
    gates = jax.nn.sigmoid(_dot(h, wg_ref[...]) + bg_ref[...])
    merged = (gates[:, :d] * _dot(oa_ref[...], wa_ref[...])
              + gates[:, d:2 * d] * _dot(ob_ref[...], wb_ref[...])
              + gates[:, 2 * d:] * _dot(om_ref[...], wm_ref[...]))
    x1 = x + _dot(merged.astype(BF16), wo_ref[...])
    x1_ref[...] = x1
    hn_ref[...] = _rms(x1, gf_ref[...]).astype(BF16)


def _merge(x, oa, ob, om, wp, *, tm):
    n, d = x.shape
    row = lambda c: pl.BlockSpec((tm, c), lambda i: (i, 0))
    names = ("norm_mix", "w_gate", "b_gate", "w_br_a", "w_br_b", "w_br_m", "w_o", "norm_ffn")
    return pl.pallas_call(
        _merge_kernel,
        grid=(n // tm,),
        in_specs=[row(d), row(oa.shape[1]), row(ob.shape[1]), row(om.shape[1])]
                 + [_full_spec(wp[k].shape) for k in names],
        out_specs=(row(d), row(d)),
        out_shape=(jax.ShapeDtypeStruct((n, d), F32), jax.ShapeDtypeStruct((n, d), BF16)),
        compiler_params=_cparams(("parallel",)),
        name="merge",
    )(x, oa, ob, om, *[wp[k] for k in names])


def _top_values(sc, k, with_rank=False):
    vals = []
    cur = sc
    rank = jnp.full(sc.shape, float(k), F32) if with_rank else None
    for r in range(k):
        m = jnp.max(cur, axis=0, keepdims=True)
        vals.append(m)
        hit = cur == m
        if with_rank:
            rank = jnp.where(hit, float(r), rank)
        if r + 1 < k:
            cur = jnp.where(hit, NEG, cur)
    return (vals, rank) if with_rank else vals


def _stack_rows(rows, sub):
    out = jnp.zeros(sub.shape, F32)
    for r, row in enumerate(rows):
        out = jnp.where(sub == r, row, out)
    return out


def _route_kernel(hn_ref, wq_ref, sk_ref, cnt_ref, e1_ref, rank_ref, e2_ref):
    tt = hn_ref.shape[0]
    q = _dot(hn_ref[...], wq_ref[...]).astype(BF16)
    sub = lax.broadcasted_iota(jnp.int32, (PEER_TOPK, tt), 0)
    sub8 = lax.broadcasted_iota(jnp.int32, (8, tt), 0)
    for h in range(PEER_HEADS):
        sc1 = _dot_nt(sk_ref[2 * h], q[:, (2 * h) * PEER_HALF:(2 * h + 1) * PEER_HALF])
        sc2 = _dot_nt(sk_ref[2 * h + 1], q[:, (2 * h + 1) * PEER_HALF:(2 * h + 2) * PEER_HALF])
        t1 = _top_values(sc1, PEER_TOPK)
        t2_rows, rank2 = _top_values(sc2, PEER_TOPK, with_rank=True)
        t2 = _stack_rows(t2_rows, sub)
        t2h = t2[:8]
        cands = []
        for a in range(PEER_TOPK):
            nb = PEER_TOPK // (a + 1)
            c = t1[a] + (t2 if nb > 8 else t2h)
            if nb < c.shape[0]:
                c = jnp.where((sub if nb > 8 else sub8) < nb, c, NEG)
            cands.append(c)
        cand = jnp.concatenate(cands, axis=0)
        tau = _top_values(cand, PEER_TOPK)[-1]
        top = t1[0] + t2[0:1]
        z = jnp.sum(jnp.where(cand >= tau, jnp.exp(cand - top), 0.0), axis=0, keepdims=True)
        cnt = jnp.zeros(sc1.shape, F32)
        for a in range(PEER_TOPK):
            ca = jnp.sum(jnp.where(cands[a] >= tau, 1.0, 0.0), axis=0, keepdims=True)
            cnt = jnp.where(sc1 == t1[a], ca, cnt)
        cnt_ref[h] = cnt
        e1_ref[h] = jnp.exp(sc1 - t1[0]) * (0.5 / z)
        rank_ref[h] = rank2.astype(BF16)
        e2_ref[h] = jnp.exp(sc2 - t2[0:1]).astype(BF16)


def _route(hn, wp, *, tt):
    n, d = hn.shape
    big = pl.BlockSpec((PEER_HEADS, PEER_NKEYS, tt), lambda i: (0, 0, i))
    big_shape = jax.ShapeDtypeStruct((PEER_HEADS, PEER_NKEYS, n), F32)
    big_half = jax.ShapeDtypeStruct((PEER_HEADS, PEER_NKEYS, n), BF16)
    return pl.pallas_call(
        _route_kernel,
        grid=(n // tt,),
        in_specs=[pl.BlockSpec((tt, d), lambda i: (i, 0)), _full_spec(wp["peer_wq"].shape),
                  _full_spec(wp["peer_subkeys"].shape)],
        out_specs=(big, big, big, big),
        out_shape=(big_shape, big_shape, big_half, big_half),
        compiler_params=_cparams(("parallel",)),
        name="peer_route",
    )(hn, wp["peer_wq"], wp["peer_subkeys"])


PAIR = 2 * PEER_NKEYS
SUBROWS = 128


def _peer_kernel(hn_in, u_ref, vt_ref, cnt_ref, e1_ref, rank_in, e2_in, x1_ref, gn_ref, y_ref,
                 acc_scr, hw_scr, rank_ref, e2_ref, hn_ref, *, te, ne):
    j = pl.program_id(1)
    tt = hn_in.shape[0]
    npair = te // PAIR

    @pl.when(j == 0)
    def _():
        rank_ref[...] = rank_in[...]
        e2_ref[...] = e2_in[...]
        hn_ref[...] = hn_in[...]

    def pre_act(p):
        a = _dot_nt(u_ref[p * PAIR:(p + 1) * PAIR, :], hn_ref[...])
        return a * (1.0 + lax.erf(a * math.sqrt(0.5)))

    def down(lo, hi):
        return _dot(vt_ref[:, lo * PAIR:hi * PAIR], hw_scr[lo * PAIR:hi * PAIR, :])

    cuts = sorted({0, 3 * npair // 8, 3 * npair // 4, npair - 1, npair})
    piece_after = {hi: lo for lo, hi in zip(cuts[:-1], cuts[1:])}

    groups = [(tc, sb) for tc in range(tt // LANE) for sb in range(PEER_NKEYS // SUBROWS)]
    zero = jnp.zeros((), BF16)
    act = pre_act(0)
    tot = None
    carry = None
    for p in range(npair):
        nxt = None
        for gi, (tc, sb) in enumerate(groups):
            if gi == 1 and p + 1 < npair:
                nxt = pre_act(p + 1)
            if gi == len(groups) // 2 and p in piece_after:
                part = down(piece_after[p], p)
                tot = part if tot is None else tot + part
            ls = slice(tc * LANE, (tc + 1) * LANE)
            i2 = slice(sb * SUBROWS, (sb + 1) * SUBROWS)
            w = [None, None] if carry is None else [carry * zero, carry * zero]
            for h in range(PEER_HEADS):
                rk = rank_ref[h, i2, ls]
                e2b = e2_ref[h, i2, ls]
                for rr in range(2):
                    i1 = 2 * p + rr
                    cnt = jnp.broadcast_to(cnt_ref[h, i1:i1 + 1, ls], (SUBROWS, LANE)).astype(BF16)
                    e1b = jnp.broadcast_to(e1_ref[h, i1:i1 + 1, ls], (SUBROWS, LANE)).astype(BF16)
                    cw = jnp.where(rk < cnt, e2b, zero) * e1b
                    w[rr] = cw if w[rr] is None else w[rr] + cw
            carry = w[1]
            for rr in range(2):
                r0 = rr * PEER_NKEYS + sb * SUBROWS
                hw_scr[p * PAIR + r0:p * PAIR + r0 + SUBROWS, ls] = w[rr] * act[r0:r0 + SUBROWS, ls].astype(BF16)
        act = nxt
    part = down(piece_after[npair], npair)
    tot = part if tot is None else tot + part

    @pl.when(j == 0)
    def _():
        acc_scr[...] = tot

    @pl.when(j > 0)
    def _():
        acc_scr[...] += tot

    @pl.when(j == ne - 1)
    def _():
        xr = x1_ref[...] + acc_scr[...].T
        y_ref[...] = _rms(xr, gn_ref[...])


def _peer(hn, cnt, e1, rank2, e2, x1, wp, *, tt, te):
    n, d = hn.shape
    ne = wp["peer_u"].shape[0] // te
    big = pl.BlockSpec((PEER_HEADS, PEER_NKEYS, tt), lambda i, j: (0, 0, i))
    rows = pl.BlockSpec((PEER_HEADS, te // PEER_NKEYS, tt), lambda i, j: (0, j, i))
    kern = functools.partial(_peer_kernel, te=te, ne=ne)
    return pl.pallas_call(
        kern,
        grid=(n // tt, ne),
        in_specs=[pl.BlockSpec((tt, d), lambda i, j: (i, 0)),
                  pl.BlockSpec((te, d), lambda i, j: (j, 0)),
                  pl.BlockSpec((d, te), lambda i, j: (0, j)),
                  rows, rows, big, big,
                  pl.BlockSpec((tt, d), lambda i, j: (i, 0)),
                  pl.BlockSpec((1, d), lambda i, j: (0, 0))],
        out_specs=pl.BlockSpec((tt, d), lambda i, j: (i, 0)),
        out_shape=jax.ShapeDtypeStruct((n, d), F32),
        scratch_shapes=[pltpu.VMEM((d, tt), F32), pltpu.VMEM((te, tt), BF16),
                        pltpu.VMEM((PEER_HEADS, PEER_NKEYS, tt), BF16),
                        pltpu.VMEM((PEER_HEADS, PEER_NKEYS, tt), BF16),
                        pltpu.VMEM((tt, d), BF16)],
        compiler_params=_cparams(("parallel", "arbitrary")),
        name="peer_experts",
    )(hn, wp["peer_u"], wp["peer_vt"], cnt, e1, rank2, e2, x1, wp["norm_final"])


def _prep_weights(l, p):
    f = lambda a: a.astype(F32)
    w_in = f(p["w_in"][l])
    o_cq, o_ckv, o_kr = 0, MLA_Q_LORA, MLA_Q_LORA + MLA_KV_LORA
    o_dq = o_kr + MLA_ROPE
    half = MLA_ROPE // 2
    d = w_in.shape[0]
    kr = w_in[:, o_kr:o_dq]
    pad = jnp.zeros((d, LANE - MLA_ROPE), F32)
    w_in2 = jnp.concatenate([
        w_in[:, o_cq:o_kr], kr, pad,
        -kr[:, half:], kr[:, :half], pad,
        w_in[:, o_dq:]], axis=1)
    assert w_in2.shape[1] == _C_END

    w_uq = f(p["w_uq"][l]).reshape(MLA_Q_LORA, MLA_HEADS, MLA_NOPE + MLA_ROPE)
    nope, x1, x2 = w_uq[..., :MLA_NOPE], w_uq[..., MLA_NOPE:MLA_NOPE + half], w_uq[..., MLA_NOPE + half:]
    zpad = jnp.zeros((MLA_Q_LORA, MLA_HEADS, HEAD_W - MLA_NOPE - MLA_ROPE), F32)
    q_slab = jnp.concatenate([nope, x1, x2, zpad], axis=-1).reshape(MLA_Q_LORA, -1)
    r_slab = jnp.concatenate([jnp.zeros_like(nope), -x2, x1, zpad], axis=-1).reshape(MLA_Q_LORA, -1)
    w_uq2 = jnp.concatenate([q_slab, r_slab], axis=1)

    w_uk = f(p["w_uk"][l]).reshape(MLA_KV_LORA, MLA_HEADS, MLA_NOPE)
    w_uk2 = jnp.concatenate([w_uk, jnp.zeros((MLA_KV_LORA, MLA_HEADS, HEAD_W - MLA_NOPE), F32)],
                            axis=-1).reshape(MLA_KV_LORA, -1)
    r_idx = jnp.arange(MLA_ROPE)
    cols = jnp.arange(MLA_HEADS * HEAD_W)
    p_kr = ((cols[None, :] % HEAD_W) == (MLA_NOPE + r_idx[:, None])).astype(F32)

    slopes = jnp.exp2(-8.0 * jnp.arange(1, DIFF_HEADS + 1, dtype=F32) / DIFF_HEADS)
    sub = f(p["diff_subln"][l])
    row = lambda a: f(a).reshape(1, -1)
    return {
        "norm_mix": row(p["norm_mix"][l]), "w_in": w_in2.astype(BF16),
        "mla_q_norm": row(p["mla_q_norm"][l]), "w_uq": w_uq2.astype(BF16),
        "mla_kv_norm": row(p["mla_kv_norm"][l]),
        "w_uk": w_uk2.astype(BF16), "p_kr": p_kr.astype(BF16), "w_uv": p["w_uv"][l].astype(BF16),
        "slopes": slopes,
        "diff_lq1": row(p["diff_lq1"][l]), "diff_lk1": row(p["diff_lk1"][l]),
        "diff_lq2": row(p["diff_lq2"][l]), "diff_lk2": row(p["diff_lk2"][l]),
        "diff_subln2": jnp.concatenate([sub, sub]).reshape(1, -1),
        "norm_mem": row(p["norm_mem"][l]), "w_mem_kv": p["w_mem_kv"][l].astype(BF16),
        "w_gate": p["w_gate"][l].astype(BF16), "b_gate": row(p["b_gate"][l]),
        "w_br_a": p["w_br_a"][l].astype(BF16), "w_br_b": p["w_br_b"][l].astype(BF16),
        "w_br_m": p["w_br_m"][l].astype(BF16), "w_o": p["w_o"][l].astype(BF16),
        "norm_ffn": row(p["norm_ffn"][l]),
        "peer_wq": p["peer_wq"][l].astype(BF16),
        "peer_subkeys": p["peer_subkeys"][l].reshape(PEER_HEADS * 2, PEER_NKEYS, PEER_HALF).astype(BF16),
        "peer_u": p["peer_u"][l].astype(BF16), "peer_vt": p["peer_v"][l].T.astype(BF16),
        "norm_final": row(p["norm_final"]),
    }


def _rope_tables(pos):
    half = MLA_ROPE // 2
    inv = (1.0 / (ROPE_THETA ** (np.arange(half, dtype=np.float32) / half))).astype(np.float32)
    ang = pos.astype(np.float32)[:, None] * inv[None, :]
    cos, sin = np.cos(ang), np.sin(ang)
    n = pos.shape[0]
    one = np.ones((n, MLA_NOPE), np.float32)
    zq = np.zeros((n, HEAD_W - MLA_NOPE - MLA_ROPE), np.float32)
    scale = np.float32(MLA_SCALE)
    cq = np.concatenate([one, cos, cos, zq], axis=1) * scale
    sq = np.concatenate([0.0 * one, sin, sin, zq], axis=1) * scale
    zk = np.zeros((n, LANE - MLA_ROPE), np.float32)
    ck = np.concatenate([cos, cos, zk], axis=1)
    sk = np.concatenate([sin, sin, zk], axis=1)
    return cq, sq, ck, sk


def _layer(x, pos, past, mem_k, mem_v, wp, layer, cfg):
    b, s, d = x.shape
    n = b * s
    xf = x.reshape(n, d)
    tabs = _rope_tables(pos)
    if s % cfg["tm"] == 0:
        pos_blocks = s // cfg["tm"]
    else:
        tabs = tuple(np.tile(t, (b, 1)) for t in tabs)
        pos_blocks = n // cfg["tm"]
    ckv, kr, dk, dv, qm, dqb, dkb, dvb, mqb = _proj_in(xf, tabs, wp, tm=cfg["tm"], pos_blocks=pos_blocks)
    new_rows = (ckv.reshape(b, s, -1), kr.reshape(b, s, -1), dk.reshape(b, s, DIFF_HEADS, DIFF_V),
                dv.reshape(b, s, DIFF_HEADS, DIFF_V))
    r3 = lambda a: a.reshape(b, s, -1)
    if past is None:
        ckv_all, kr_all = ckv, kr
        dk_all, dv_all = r3(dkb), r3(dvb)
        kk = s
        q_off = 0
    else:
        p_len = past[0].shape[1]
        kk = p_len + s
        q_off = p_len
        ckv_all = jnp.concatenate([past[0], r3(ckv)], axis=1).reshape(b * kk, -1)
        kr_all = jnp.concatenate([past[1], r3(kr)], axis=1).reshape(b * kk, -1)
        dk_all = jnp.concatenate([past[2].reshape(b, p_len, -1).astype(BF16), r3(dkb)], axis=1)
        dv_all = jnp.concatenate([past[3].reshape(b, p_len, -1).astype(BF16), r3(dvb)], axis=1)
    k_mla, v_mla = _kv_up(ckv_all, kr_all, wp, tm=cfg["tm_kv"])
    att = dict(tk=cfg["tk"] if past is None else kk, q_off=q_off)
    o_a = _mla_attention(r3(qm), k_mla.reshape(b, kk, -1), v_mla.reshape(b, kk, -1), tq=cfg["tq_mla"], **att)
    lam_init = 0.8 - 0.6 * math.exp(-0.3 * layer)
    o_b = _diff_attention(r3(dqb), dk_all, dv_all, wp, lam_init=lam_init, tq=cfg["tq"], **att)
    o_m = _mem_attention(r3(mqb), mem_k, mem_v, tq=cfg["tq"])
    x1, hn = _merge(xf, o_a.reshape(n, -1), o_b.reshape(n, -1), o_m.reshape(n, -1), wp, tm=cfg["tm"])
    return x1, hn, new_rows


def _peer_and_norm(x1, hn, wp, cfg):
    cnt, e1, rank2, e2 = _route(hn, wp, tt=cfg["tt"])
    return _peer(hn, cnt, e1, rank2, e2, x1, wp, tt=cfg["tt"], te=cfg["te"])


_CFG_PROMPT = dict(tm=512, tm_kv=512, tq=512, tq_mla=1024, tk=512, tt=512, te=2048)
_CFG_SAMPLE = dict(tm=256, tm_kv=256, tq=32, tq_mla=32, tk=None, tt=256, te=2048)


def kernel(x_prompt, x_sample, cache_mla_ckv, cache_mla_krope, cache_diff_k, cache_diff_v, cache_mem_k, cache_mem_v, mem_prompt, norm_mix, w_in, mla_q_norm, w_uq, mla_kv_norm, w_uk, w_uv, diff_lq1, diff_lk1, diff_lq2, diff_lk2, diff_subln, norm_mem, w_mem_kv, w_br_a, w_br_b, w_br_m, w_gate, b_gate, w_o, norm_ffn, peer_wq, peer_subkeys, peer_u, peer_v, norm_final):
    params = dict(norm_mix=norm_mix, w_in=w_in, mla_q_norm=mla_q_norm, w_uq=w_uq, mla_kv_norm=mla_kv_norm,
                  w_uk=w_uk, w_uv=w_uv, diff_lq1=diff_lq1, diff_lk1=diff_lk1, diff_lq2=diff_lq2,
                  diff_lk2=diff_lk2, diff_subln=diff_subln, norm_mem=norm_mem, w_mem_kv=w_mem_kv,
                  w_br_a=w_br_a, w_br_b=w_br_b, w_br_m=w_br_m, w_gate=w_gate, b_gate=b_gate, w_o=w_o,
                  norm_ffn=norm_ffn, peer_wq=peer_wq, peer_subkeys=peer_subkeys, peer_u=peer_u,
                  peer_v=peer_v, norm_final=norm_final)
    depth = w_in.shape[0]
    assert depth == 1, "the final norm is fused into the last PEER step of a single layer"
    bp, sp, d = x_prompt.shape
    bs, ss, _ = x_sample.shape
    n_mem = mem_prompt.shape[1]
    past_len = cache_mla_ckv.shape[2]
    pos_p = np.arange(sp, dtype=np.int32)
    pos_s = past_len + np.arange(ss, dtype=np.int32)

    l = 0
    wp = _prep_weights(l, params)
    mk, mv, mkb, mvb = _mem_kv(mem_prompt.reshape(bp * n_mem, d), wp, tm=n_mem)
    m3 = lambda a, b: a.reshape(b, n_mem, -1)
    x1p, hnp, rows_p = _layer(x_prompt, pos_p, None, m3(mkb, bp), m3(mvb, bp), wp, l, _CFG_PROMPT)
    past = (cache_mla_ckv[l], cache_mla_krope[l], cache_diff_k[l], cache_diff_v[l])
    x1s, hns, rows_s = _layer(x_sample, pos_s, past, m3(cache_mem_k[l].astype(BF16), bs),
                              m3(cache_mem_v[l].astype(BF16), bs), wp, l, _CFG_SAMPLE)
    y_prompt = _peer_and_norm(x1p, hnp, wp, _CFG_PROMPT).reshape(bp, sp, d)
    y_sample = _peer_and_norm(x1s, hns, wp, _CFG_SAMPLE).reshape(bs, ss, d)
    st = lambda a: a[None]
    mem4 = lambda a: a.reshape(bp, n_mem, MEM_HEADS, MEM_DH)[None]
    return (y_prompt, y_sample,
            st(rows_p[0]), st(rows_p[1]), st(rows_p[2]), st(rows_p[3]),
            mem4(mk), mem4(mv),
            st(rows_s[0]), st(rows_s[1]), st(rows_s[2]), st(rows_s[3]))
```

```python
import functools
import math

import jax
import jax.numpy as jnp
import numpy as np
from jax import lax
from jax.experimental import pallas as pl
from jax.experimental.pallas import tpu as pltpu

F32 = jnp.float32
BF16 = jnp.bfloat16

CHUNK = 64
CHUNK_SHIFT = 6
EPS = 1e-6
NEG = -1e30
MLA_HEADS = 8
MLA_Q_LORA = 384
MLA_KV_LORA = 256
MLA_NOPE = 64
MLA_ROPE = 32
MLA_V = 64
ROPE_THETA = 10000.0
LOG2E = math.log2(math.e)
MLA_SCALE = (MLA_NOPE + MLA_ROPE) ** -0.5 * LOG2E
DIFF_HEADS = 8
DIFF_DH = 32
DIFF_V = 2 * DIFF_DH
DIFF_SCALE = DIFF_DH ** -0.5 * LOG2E
MEM_HEADS = 4
MEM_DH = 128
MEM_SCALE = MEM_DH ** -0.5 * LOG2E
PEER_HEADS = 8
PEER_NKEYS = 128
PEER_HALF = 128
PEER_TOPK = 16
LANE = 128
HEAD_W = 128

VMEM_LIMIT = 56 * 1024 * 1024

_C_CQ = 0
_C_CKV = _C_CQ + MLA_Q_LORA
_C_KR = _C_CKV + MLA_KV_LORA
_C_KRR = _C_KR + LANE
_C_DQ = _C_KRR + LANE
_C_DK = _C_DQ + DIFF_HEADS * DIFF_V
_C_DV = _C_DK + DIFF_HEADS * DIFF_V
_C_MQ = _C_DV + DIFF_HEADS * DIFF_V
_C_END = _C_MQ + MEM_HEADS * MEM_DH


def _cparams(sem):
    return pltpu.CompilerParams(dimension_semantics=sem, vmem_limit_bytes=VMEM_LIMIT)


def _rms(x, g):
    return x * lax.rsqrt(jnp.mean(x * x, axis=-1, keepdims=True) + EPS) * g


def _dot(a, b):
    return jnp.dot(a, b, preferred_element_type=F32)


def _dot_nt(a, b):
    return lax.dot_general(a, b, (((1,), (1,)), ((), ())), preferred_element_type=F32)


def _full_spec(shape):
    nd = len(shape)
    return pl.BlockSpec(shape, lambda *_: (0,) * nd)


def _proj_in_kernel(x_ref, g_ref, win_ref, qn_ref, wuq_ref, kvn_ref, cq_ref, sq_ref, ck_ref, sk_ref,
                    ckv_ref, kr_ref, dk_ref, dv_ref, qm_ref, dqb_ref, dkb_ref, dvb_ref, mqb_ref):
    h = _rms(x_ref[...], g_ref[...]).astype(BF16)
    z = _dot(h, win_ref[...])
    cqn = _rms(z[:, _C_CQ:_C_CKV], qn_ref[...]).astype(BF16)
    q2 = _dot(cqn, wuq_ref[...])
    cq = cq_ref[...]
    sq = sq_ref[...]
    nq = MLA_HEADS * HEAD_W
    for hh in range(MLA_HEADS):
        lo = hh * HEAD_W
        qm_ref[:, lo:lo + HEAD_W] = (q2[:, lo:lo + HEAD_W] * cq
                                     + q2[:, nq + lo:nq + lo + HEAD_W] * sq).astype(BF16)
    ckv_ref[...] = _rms(z[:, _C_CKV:_C_KR], kvn_ref[...])
    kr = z[:, _C_KR:_C_KRR] * ck_ref[...] + z[:, _C_KRR:_C_DQ] * sk_ref[...]
    kr_ref[...] = kr[:, :MLA_ROPE]
    dk = z[:, _C_DK:_C_DV]
    dv = z[:, _C_DV:_C_MQ]
    dk_ref[...] = dk
    dv_ref[...] = dv
    dkb_ref[...] = dk.astype(BF16)
    dvb_ref[...] = dv.astype(BF16)
    dqb_ref[...] = (z[:, _C_DQ:_C_DK] * DIFF_SCALE).astype(BF16)
    mqb_ref[...] = (z[:, _C_MQ:_C_END] * MEM_SCALE).astype(BF16)


def _proj_in(x, tabs, wp, *, tm, pos_blocks):
    n, d = x.shape
    grid = (n // tm,)
    row = lambda w: pl.BlockSpec((tm, w), lambda i: (i, 0))
    tab = pl.BlockSpec((tm, LANE), lambda i: (i % pos_blocks, 0))
    dw = DIFF_HEADS * DIFF_V
    out_shape = (
        jax.ShapeDtypeStruct((n, MLA_KV_LORA), F32),
        jax.ShapeDtypeStruct((n, MLA_ROPE), F32),
        jax.ShapeDtypeStruct((n, dw), F32),
        jax.ShapeDtypeStruct((n, dw), F32),
        jax.ShapeDtypeStruct((n, MLA_HEADS * HEAD_W), BF16),
        jax.ShapeDtypeStruct((n, dw), BF16),
        jax.ShapeDtypeStruct((n, dw), BF16),
        jax.ShapeDtypeStruct((n, dw), BF16),
        jax.ShapeDtypeStruct((n, MEM_HEADS * MEM_DH), BF16),
    )
    return pl.pallas_call(
        _proj_in_kernel,
        grid=grid,
        in_specs=[row(d), _full_spec((1, d)), _full_spec(wp["w_in"].shape), _full_spec((1, MLA_Q_LORA)),
                  _full_spec(wp["w_uq"].shape), _full_spec((1, MLA_KV_LORA)), tab, tab, tab, tab],
        out_specs=(row(MLA_KV_LORA), row(MLA_ROPE), row(dw), row(dw), row(MLA_HEADS * HEAD_W),
                   row(dw), row(dw), row(dw), row(MEM_HEADS * MEM_DH)),
        out_shape=out_shape,
        compiler_params=_cparams(("parallel",)),
        name="proj_in",
    )(x, wp["norm_mix"], wp["w_in"], wp["mla_q_norm"], wp["w_uq"], wp["mla_kv_norm"], *tabs)


def _kv_up_kernel(ckv_ref, kr_ref, wuk_ref, pk_ref, wuv_ref, k_ref, v_ref):
    c = ckv_ref[...].astype(BF16)
    k = _dot(c, wuk_ref[...]) + _dot(kr_ref[...].astype(BF16), pk_ref[...])
    k_ref[...] = k.astype(BF16)
    v_ref[...] = _dot(c, wuv_ref[...]).astype(BF16)


def _kv_up(ckv, kr, wp, *, tm):
    n = ckv.shape[0]
    row = lambda w: pl.BlockSpec((tm, w), lambda i: (i, 0))
    return pl.pallas_call(
        _kv_up_kernel,
        grid=(n // tm,),
        in_specs=[row(MLA_KV_LORA), row(MLA_ROPE), _full_spec(wp["w_uk"].shape),
                  _full_spec(wp["p_kr"].shape), _full_spec(wp["w_uv"].shape)],
        out_specs=(row(MLA_HEADS * HEAD_W), row(MLA_HEADS * MLA_V)),
        out_shape=(jax.ShapeDtypeStruct((n, MLA_HEADS * HEAD_W), BF16),
                   jax.ShapeDtypeStruct((n, MLA_HEADS * MLA_V), BF16)),
        compiler_params=_cparams(("parallel",)),
        name="kv_up",
    )(ckv, kr, wp["w_uk"], wp["p_kr"], wp["w_uv"])


def _lanes(col, n):
    if n == LANE:
        return col
    if n % LANE == 0:
        return jnp.concatenate([col] * (n // LANE), axis=1)
    return jnp.broadcast_to(col[:, :1], (col.shape[0], n))


GROUP_VREGS = 16


def _softmax_block(score_group, tq, tk, row0, m_scr, al_scr, p_scr, l_scr=None):
    rg = min(tq, max(16, (GROUP_VREGS * 8 * LANE // tk) // 16 * 16))
    for g in range(tq // rg):
        r = slice(row0 + g * rg, row0 + (g + 1) * rg)
        m_prev = m_scr[r]
        m_next = jnp.maximum(m_prev, jnp.max(score_group(g, rg), axis=1, keepdims=True))
        al_scr[r] = jnp.exp2(m_prev - m_next)
        m_scr[r] = m_next
    for g in range(tq // rg):
        r = slice(row0 + g * rg, row0 + (g + 1) * rg)
        e = jnp.exp2(score_group(g, rg) - _lanes(m_scr[r], tk))
        p_scr[r] = e.astype(BF16)
        if l_scr is not None:
            l_scr[r] = al_scr[r] * l_scr[r] + jnp.sum(e, axis=1, keepdims=True)


def _accumulate(rows, v, al_scr, l_scr, acc_scr, p_scr):
    pv = _dot(p_scr[rows], jnp.concatenate([v, jnp.ones(v.shape, v.dtype)], axis=1))
    w = v.shape[1]
    acc_scr[rows] = acc_scr[rows] * al_scr[rows] + pv[:, :w]
    l_scr[rows] = l_scr[rows] * al_scr[rows] + pv[:, w:]


def _chunk_visible(q_lo, k_lo, g, rg, tk):
    qp = q_lo + g * rg + lax.broadcasted_iota(jnp.int32, (rg, tk), 0)
    kp = k_lo + lax.broadcasted_iota(jnp.int32, (rg, tk), 1)
    return qp, kp, (kp >> CHUNK_SHIFT) <= (qp >> CHUNK_SHIFT)


def _init_stats(m_scr, l_scr, acc_scr):
    m_scr[...] = jnp.full(m_scr.shape, NEG, F32)
    l_scr[...] = jnp.zeros(l_scr.shape, F32)
    acc_scr[...] = jnp.zeros(acc_scr.shape, F32)


def _block_range(qi, *, tq, tk, q_off, nk):
    q_lo = q_off + qi * tq
    q_hi = q_lo + (tq - 1)
    n_behind = jnp.minimum((q_lo + 1) // tk, nk)
    last_key = ((q_hi >> CHUNK_SHIFT) << CHUNK_SHIFT) + (CHUNK - 1)
    n_need = jnp.minimum(last_key // tk + 1, nk)
    return q_lo, n_behind, n_need


def _for_blocks(lo, hi, fn):
    lax.fori_loop(lo, hi, lambda i, c: (fn(i), c)[1], 0)


def _attn_specs(b, pairs, sq, sk, tq, qw, kw, vw):
    grid = (b, pairs, sq // tq)
    in_specs = [pl.BlockSpec((1, tq, qw), lambda bi, p, qi: (bi, qi, p)),
                pl.BlockSpec((1, sk, kw), lambda bi, p, qi: (bi, 0, p)),
                pl.BlockSpec((1, sk, vw), lambda bi, p, qi: (bi, 0, p))]
    out_spec = pl.BlockSpec((1, tq, vw), lambda bi, p, qi: (bi, qi, p))
    return grid, in_specs, out_spec


def _mla_kernel(q_ref, k_ref, v_ref, o_ref, m_scr, l_scr, al_scr, acc_scr, p_scr, *, tq, tk, q_off, nk):
    qi = pl.program_id(2)
    _init_stats(m_scr, l_scr, acc_scr)
    q_lo, n_behind, n_need = _block_range(qi, tq=tq, tk=tk, q_off=q_off, nk=nk)

    def block(i, masked):
        k_lo = pl.multiple_of(i * tk, tk)
        for hh in range(2):
            hs = slice(hh * HEAD_W, (hh + 1) * HEAD_W)
            s = _dot_nt(q_ref[0, :, hs], k_ref[0, pl.ds(k_lo, tk), hs])

            def score_group(g, rg, s=s):
                sg = s[g * rg:(g + 1) * rg]
                if masked:
                    sg = jnp.where(_chunk_visible(q_lo, k_lo, g, rg, tk)[2], sg, NEG)
                return sg

            _softmax_block(score_group, tq, tk, hh * tq, m_scr, al_scr, p_scr)
        _accumulate(slice(0, 2 * tq), v_ref[0, pl.ds(k_lo, tk), :], al_scr, l_scr, acc_scr, p_scr)

    _for_blocks(0, n_behind, lambda i: block(i, False))
    _for_blocks(n_behind, n_need, lambda i: block(i, True))

    lane = lax.broadcasted_iota(jnp.int32, (tq, LANE), 1)
    o0 = acc_scr[0:tq] / l_scr[0:tq]
    o1 = acc_scr[tq:2 * tq] / l_scr[tq:2 * tq]
    o_ref[0] = jnp.where(lane < MLA_V, o0, o1).astype(o_ref.dtype)


def _mla_attention(q, k, v, *, tq, tk, q_off):
    b, sq, _ = q.shape
    sk = k.shape[1]
    grid, in_specs, out_spec = _attn_specs(b, MLA_HEADS // 2, sq, sk, tq, 2 * HEAD_W, 2 * HEAD_W, 2 * MLA_V)
    kern = functools.partial(_mla_kernel, tq=tq, tk=tk, q_off=q_off, nk=sk // tk)
    stat = pltpu.VMEM((2 * tq, LANE), F32)
    return pl.pallas_call(
        kern,
        grid=grid,
        in_specs=in_specs,
        out_specs=out_spec,
        out_shape=jax.ShapeDtypeStruct((b, sq, MLA_HEADS * MLA_V), BF16),
        scratch_shapes=[stat, stat, stat, stat, pltpu.VMEM((2 * tq, tk), BF16)],
        compiler_params=_cparams(("parallel", "parallel", "parallel")),
        name="mla_attn",
    )(q, k, v)


def _diff_kernel(slope_ref, lq1_ref, lk1_ref, lq2_ref, lk2_ref, sub_ref, q_ref, k_ref, v_ref, o_ref,
                 qs_scr, m_scr, l_scr, al_scr, acc_scr, p_scr, *, tq, tk, q_off, nk, lam_init):
    pr = pl.program_id(1)
    qi = pl.program_id(2)
    _init_stats(m_scr, l_scr, acc_scr)
    q_lo, n_behind, n_need = _block_range(qi, tq=tq, tk=tk, q_off=q_off, nk=nk)

    q = q_ref[0]
    lane = lax.broadcasted_iota(jnp.int32, (tq, LANE), 1)
    for mi in range(4):
        lo = mi * DIFF_DH
        qs_scr[mi * tq:(mi + 1) * tq] = jnp.where((lane >= lo) & (lane < lo + DIFF_DH), q, jnp.zeros_like(q))

    def block(i, masked):
        k_lo = pl.multiple_of(i * tk, tk)
        kb = k_ref[0, pl.ds(k_lo, tk), :]
        kpos = (k_lo + lax.broadcasted_iota(jnp.int32, (1, tk), 1)).astype(F32)

        def scores(mi):
            return _dot_nt(qs_scr[mi * tq:(mi + 1) * tq], kb)

        def softmax(mi, s_map):
            slope = slope_ref[2 * pr + mi // 2] * LOG2E

            def score_group(g, rg):
                sg = s_map[g * rg:(g + 1) * rg]
                if masked:
                    qp, kp, vis = _chunk_visible(q_lo, k_lo, g, rg, tk)
                    return jnp.where(vis, sg + slope * jnp.minimum(kp, 2 * qp - kp).astype(F32), NEG)
                return sg + slope * kpos

            _softmax_block(score_group, tq, tk, mi * tq, m_scr, al_scr, p_scr, l_scr)

        def values(mi):
            rows = slice(mi * tq, (mi + 1) * tq)
            acc_scr[rows] = acc_scr[rows] * al_scr[rows] + _dot(p_scr[rows], v_ref[0, pl.ds(k_lo, tk), :])

        s0 = scores(0)
        s1 = scores(1)
        softmax(0, s0)
        s2 = scores(2)
        values(0)
        softmax(1, s1)
        s3 = scores(3)
        values(1)
        softmax(2, s2)
        values(2)
        softmax(3, s3)
        values(3)

    _for_blocks(0, n_behind, lambda i: block(i, False))
    _for_blocks(n_behind, n_need, lambda i: block(i, True))

    lam = (jnp.exp(jnp.sum(lq1_ref[...] * lk1_ref[...], axis=1, keepdims=True))
           - jnp.exp(jnp.sum(lq2_ref[...] * lk2_ref[...], axis=1, keepdims=True)) + lam_init)
    first = lane < DIFF_V
    on = [acc_scr[mi * tq:(mi + 1) * tq] / l_scr[mi * tq:(mi + 1) * tq] for mi in range(4)]
    o = jnp.where(first, on[0] - lam * on[1], on[2] - lam * on[3])
    sq = o * o
    ms0 = jnp.sum(jnp.where(first, sq, 0.0), axis=1, keepdims=True) * (1.0 / DIFF_V)
    ms1 = jnp.sum(jnp.where(first, 0.0, sq), axis=1, keepdims=True) * (1.0 / DIFF_V)
    r = jnp.where(first, lax.rsqrt(ms0 + EPS), lax.rsqrt(ms1 + EPS))
    o_ref[0] = ((o * r * sub_ref[...]) * (1.0 - lam_init)).astype(o_ref.dtype)


def _diff_attention(q, k, v, wp, *, tq, tk, q_off, lam_init):
    b, sq, _ = q.shape
    sk = k.shape[1]
    grid, in_specs, out_spec = _attn_specs(b, DIFF_HEADS // 2, sq, sk, tq, LANE, LANE, LANE)
    kern = functools.partial(_diff_kernel, tq=tq, tk=tk, q_off=q_off, nk=sk // tk, lam_init=lam_init)
    small = lambda w: pl.BlockSpec((1, w), lambda bi, p, qi: (0, 0))
    stat = pltpu.VMEM((4 * tq, LANE), F32)
    return pl.pallas_call(
        kern,
        grid=grid,
        in_specs=[pl.BlockSpec(memory_space=pltpu.SMEM),
                  small(DIFF_DH), small(DIFF_DH), small(DIFF_DH), small(DIFF_DH), small(LANE)] + in_specs,
        out_specs=out_spec,
        out_shape=jax.ShapeDtypeStruct((b, sq, DIFF_HEADS * DIFF_V), BF16),
        scratch_shapes=[pltpu.VMEM((4 * tq, LANE), BF16), stat, stat, stat, stat,
                        pltpu.VMEM((4 * tq, tk), BF16)],
        compiler_params=_cparams(("parallel", "parallel", "parallel")),
        name="diff_attn",
    )(wp["slopes"], wp["diff_lq1"], wp["diff_lk1"], wp["diff_lq2"], wp["diff_lk2"], wp["diff_subln2"],
      q, k, v)


def _mem_attn_kernel(q_ref, k_ref, v_ref, o_ref):
    for hh in range(MEM_HEADS):
        sl = slice(hh * MEM_DH, (hh + 1) * MEM_DH)
        s = _dot_nt(q_ref[0, :, sl], k_ref[0, :, sl])
        p = jnp.exp2(s - jnp.max(s, axis=1, keepdims=True))
        o = _dot(p.astype(BF16), v_ref[0, :, sl]) / jnp.sum(p, axis=1, keepdims=True)
        o_ref[0, :, sl] = o.astype(o_ref.dtype)


def _mem_attention(q, k, v, *, tq):
    b, sq, w = q.shape
    nm = k.shape[1]
    return pl.pallas_call(
        _mem_attn_kernel,
        grid=(b, sq // tq),
        in_specs=[pl.BlockSpec((1, tq, w), lambda bi, qi: (bi, qi, 0)),
                  pl.BlockSpec((1, nm, w), lambda bi, qi: (bi, 0, 0)),
                  pl.BlockSpec((1, nm, w), lambda bi, qi: (bi, 0, 0))],
        out_specs=pl.BlockSpec((1, tq, w), lambda bi, qi: (bi, qi, 0)),
        out_shape=jax.ShapeDtypeStruct((b, sq, w), BF16),
        compiler_params=_cparams(("parallel", "parallel")),
        name="mem_attn",
    )(q, k, v)


def _mem_kv_kernel(x_ref, g_ref, w_ref, k_ref, v_ref, kb_ref, vb_ref):
    h = _rms(x_ref[...], g_ref[...]).astype(BF16)
    kv = _dot(h, w_ref[...])
    w = MEM_HEADS * MEM_DH
    k_ref[...] = kv[:, :w]
    v_ref[...] = kv[:, w:]
    kb_ref[...] = kv[:, :w].astype(BF16)
    vb_ref[...] = kv[:, w:].astype(BF16)


def _mem_kv(mem, wp, *, tm):
    n, d = mem.shape
    w = MEM_HEADS * MEM_DH
    row = lambda c: pl.BlockSpec((tm, c), lambda i: (i, 0))
    return pl.pallas_call(
        _mem_kv_kernel,
        grid=(n // tm,),
        in_specs=[row(d), _full_spec((1, d)), _full_spec(wp["w_mem_kv"].shape)],
        out_specs=(row(w), row(w), row(w), row(w)),
        out_shape=(jax.ShapeDtypeStruct((n, w), F32), jax.ShapeDtypeStruct((n, w), F32),
                   jax.ShapeDtypeStruct((n, w), BF16), jax.ShapeDtypeStruct((n, w), BF16)),
        compiler_params=_cparams(("parallel",)),
        name="mem_kv",
    )(mem, wp["norm_mem"], wp["w_mem_kv"])


def _merge_kernel(x_ref, oa_ref, ob_ref, om_ref, g_ref, wg_ref, bg_ref, wa_ref, wb_ref, wm_ref, wo_ref,
                  gf_ref, x1_ref, hn_ref):
    x = x_ref[...]
    d = x.shape[1]
    h = _rms(x, g_ref[...]).astype(BF16)
    gates = jax.nn.sigmoid(_dot(h, wg_ref[...]) + bg_ref[...])
    merged = (gates[:, :d] * _dot(oa_ref[...], wa_ref[...])
              + gates[:, d:2 * d] * _dot(ob_ref[...], wb_ref[...])
              + gates[:, 2 * d:] * _dot(om_ref[...], wm_ref[...]))
    x1 = x + _dot(merged.astype(BF16), wo_ref[...])
    x1_ref[...] = x1
    hn_ref[...] = _rms(x1, gf_ref[...]).astype(BF16)


def _merge(x, oa, ob, om, wp, *, tm):
    n, d = x.shape
    row = lambda c: pl.BlockSpec((tm, c), lambda i: (i, 0))
    names = ("norm_mix", "w_gate", "b_gate", "w_br_a", "w_br_b", "w_br_m", "w_o", "norm_ffn")
    return pl.pallas_call(
        _merge_kernel,
        grid=(n // tm,),
        in_specs=[row(d), row(oa.shape[1]), row(ob.shape[1]), row(om.shape[1])]
                 + [_full_spec(wp[k].shape) for k in names],
        out_specs=(row(d), row(d)),
        out_shape=(jax.ShapeDtypeStruct((n, d), F32), jax.ShapeDtypeStruct((n, d), BF16)),
        compiler_params=_cparams(("parallel",)),
        name="merge",
    )(x, oa, ob, om, *[wp[k] for k in names])


def _top_values(sc, k, with_rank=False):
    vals = []
    cur = sc
    rank = jnp.full(sc.shape, float(k), F32) if with_rank else None
    for r in range(k):
        m = jnp.max(cur, axis=0, keepdims=True)
        vals.append(m)
        hit = cur == m
        if with_rank:
            rank = jnp.where(hit, float(r), rank)
        if r + 1 < k:
            cur = jnp.where(hit, NEG, cur)
    return (vals, rank) if with_rank else vals


def _stack_rows(rows, sub):
    out = jnp.zeros(sub.shape, F32)
    for r, row in enumerate(rows):
        if row is not None:
            out = jnp.where(sub == r, row, out)
    return out


def _packed_candidates(t1, t2, t2h, sub8):
    assert PEER_TOPK == 16
    down = lambda x, k: pltpu.roll(x, k, 0)
    t1_8_13 = _stack_rows([None, None] + t1[8:14], sub8)
    t1_14_15 = _stack_rows(t1[14:16], sub8)
    top2 = t2[0:1]
    slabs = [
        t1[0] + t2h,
        t1[0] + t2[8:16],
        t1[1] + t2h,
        jnp.where(sub8 < 5, t1[2] + t2h, t1[4] + down(t2h, 5)),
        jnp.where(sub8 < 4, t1[3] + t2h,
                  jnp.where(sub8 < 6, t1[5] + down(t2h, 4), t1[6] + down(t2h, 6))),
        jnp.where(sub8 < 2, t1[7] + t2h, t1_8_13 + top2),
        jnp.where(sub8 < 2, t1_14_15 + top2, NEG),
    ]
    return jnp.concatenate(slabs, axis=0)


def _route_kernel(hn_ref, wq_ref, sk_ref, cnt_ref, e1_ref, rank_ref, e2_ref):
    tt = hn_ref.shape[0]
    q = _dot(hn_ref[...], wq_ref[...]).astype(BF16)
    sub = lax.broadcasted_iota(jnp.int32, (PEER_TOPK, tt), 0)
    sub8 = lax.broadcasted_iota(jnp.int32, (8, tt), 0)
    for h in range(PEER_HEADS):
        sc1 = _dot_nt(sk_ref[2 * h], q[:, (2 * h) * PEER_HALF:(2 * h + 1) * PEER_HALF])
        sc2 = _dot_nt(sk_ref[2 * h + 1], q[:, (2 * h + 1) * PEER_HALF:(2 * h + 2) * PEER_HALF])
        t1 = _top_values(sc1, PEER_TOPK)
        t2_rows, rank2 = _top_values(sc2, PEER_TOPK, with_rank=True)
        t2 = _stack_rows(t2_rows, sub)
        t2h = t2[:8]
        cands = []
        for a in range(PEER_TOPK):
            nb = PEER_TOPK // (a + 1)
            c = t1[a] + (t2 if nb > 8 else t2h)
            if nb < c.shape[0]:
                c = jnp.where((sub if nb > 8 else sub8) < nb, c, NEG)
            cands.append(c)
        cand = _packed_candidates(t1, t2, t2h, sub8)
        tau = _top_values(cand, PEER_TOPK)[-1]
        top = t1[0] + t2[0:1]
        z = jnp.sum(jnp.where(cand >= tau, jnp.exp(cand - top), 0.0), axis=0, keepdims=True)
        cnt = jnp.zeros(sc1.shape, F32)
        for a in range(PEER_TOPK):
            ca = jnp.sum(jnp.where(cands[a] >= tau, 1.0, 0.0), axis=0, keepdims=True)
            cnt = jnp.where(sc1 == t1[a], ca, cnt)
        cnt_ref[h] = cnt
        e1_ref[h] = jnp.exp(sc1 - t1[0]) * (0.5 / z)
        rank_ref[h] = rank2.astype(BF16)
        e2_ref[h] = jnp.exp(sc2 - t2[0:1]).astype(BF16)


def _route(hn, wp, *, tt):
    n, d = hn.shape
    big = pl.BlockSpec((PEER_HEADS, PEER_NKEYS, tt), lambda i: (0, 0, i))
    big_shape = jax.ShapeDtypeStruct((PEER_HEADS, PEER_NKEYS, n), F32)
    big_half = jax.ShapeDtypeStruct((PEER_HEADS, PEER_NKEYS, n), BF16)
    return pl.pallas_call(
        _route_kernel,
        grid=(n // tt,),
        in_specs=[pl.BlockSpec((tt, d), lambda i: (i, 0)), _full_spec(wp["peer_wq"].shape),
                  _full_spec(wp["peer_subkeys"].shape)],
        out_specs=(big, big, big, big),
        out_shape=(big_shape, big_shape, big_half, big_half),
        compiler_params=_cparams(("parallel",)),
        name="peer_route",
    )(hn, wp["peer_wq"], wp["peer_subkeys"])


PAIR = 2 * PEER_NKEYS
SUBROWS = 128


def _peer_kernel(hn_in, u_ref, vt_ref, cnt_ref, e1_ref, rank_in, e2_in, x1_ref, gn_ref, y_ref,
                 acc_scr, hw_scr, rank_ref, e2_ref, hn_ref, *, te, ne):
    j = pl.program_id(1)
    tt = hn_in.shape[0]
    npair = te // PAIR

    @pl.when(j == 0)
    def _():
        rank_ref[...] = rank_in[...]
        e2_ref[...] = e2_in[...]
        hn_ref[...] = hn_in[...]

    def pre_act(p):
        a = _dot_nt(u_ref[p * PAIR:(p + 1) * PAIR, :], hn_ref[...])
        return a * (1.0 + lax.erf(a * math.sqrt(0.5)))

    def down(lo, hi):
        return _dot(vt_ref[:, lo * PAIR:hi * PAIR], hw_scr[lo * PAIR:hi * PAIR, :])

    cuts = sorted({0, 3 * npair // 8, 3 * npair // 4, npair - 1, npair})
    piece_after = {hi: lo for lo, hi in zip(cuts[:-1], cuts[1:])}

    groups = [(tc, sb) for tc in range(tt // LANE) for sb in range(PEER_NKEYS // SUBROWS)]
    zero = jnp.zeros((), BF16)
    act = pre_act(0)
    tot = None
    carry = None
    for p in range(npair):
        nxt = None
        for gi, (tc, sb) in enumerate(groups):
            if gi == 1 and p + 1 < npair:
                nxt = pre_act(p + 1)
            if gi == len(groups) // 2 and p in piece_after:
                part = down(piece_after[p], p)
                tot = part if tot is None else tot + part
            ls = slice(tc * LANE, (tc + 1) * LANE)
            i2 = slice(sb * SUBROWS, (sb + 1) * SUBROWS)
            w = [None, None] if carry is None else [carry * zero, carry * zero]
            for h in range(PEER_HEADS):
                rk = rank_ref[h, i2, ls]
                e2b = e2_ref[h, i2, ls]
                for rr in range(2):
                    i1 = 2 * p + rr
                    cnt = jnp.broadcast_to(cnt_ref[h, i1:i1 + 1, ls], (SUBROWS, LANE)).astype(BF16)
                    e1b = jnp.broadcast_to(e1_ref[h, i1:i1 + 1, ls], (SUBROWS, LANE)).astype(BF16)
                    cw = jnp.where(rk < cnt, e2b, zero) * e1b
                    w[rr] = cw if w[rr] is None else w[rr] + cw
            carry = w[1]
            for rr in range(2):
                r0 = rr * PEER_NKEYS + sb * SUBROWS
                hw_scr[p * PAIR + r0:p * PAIR + r0 + SUBROWS, ls] = w[rr] * act[r0:r0 + SUBROWS, ls].astype(BF16)
        act = nxt
    part = down(piece_after[npair], npair)
    tot = part if tot is None else tot + part

    @pl.when(j == 0)
    def _():
        acc_scr[...] = tot

    @pl.when(j > 0)
    def _():
        acc_scr[...] += tot

    @pl.when(j == ne - 1)
    def _():
        xr = x1_ref[...] + acc_scr[...].T
        y_ref[...] = _rms(xr, gn_ref[...])


def _peer(hn, cnt, e1, rank2, e2, x1, wp, *, tt, te):
    n, d = hn.shape
    ne = wp["peer_u"].shape[0] // te
    big = pl.BlockSpec((PEER_HEADS, PEER_NKEYS, tt), lambda i, j: (0, 0, i))
    rows = pl.BlockSpec((PEER_HEADS, te // PEER_NKEYS, tt), lambda i, j: (0, j, i))
    kern = functools.partial(_peer_kernel, te=te, ne=ne)
    return pl.pallas_call(
        kern,
        grid=(n // tt, ne),
        in_specs=[pl.BlockSpec((tt, d), lambda i, j: (i, 0)),
                  pl.BlockSpec((te, d), lambda i, j: (j, 0)),
                  pl.BlockSpec((d, te), lambda i, j: (0, j)),
                  rows, rows, big, big,
                  pl.BlockSpec((tt, d), lambda i, j: (i, 0)),
                  pl.BlockSpec((1, d), lambda i, j: (0, 0))],
        out_specs=pl.BlockSpec((tt, d), lambda i, j: (i, 0)),
        out_shape=jax.ShapeDtypeStruct((n, d), F32),
        scratch_shapes=[pltpu.VMEM((d, tt), F32), pltpu.VMEM((te, tt), BF16),
                        pltpu.VMEM((PEER_HEADS, PEER_NKEYS, tt), BF16),
                        pltpu.VMEM((PEER_HEADS, PEER_NKEYS, tt), BF16),
                        pltpu.VMEM((tt, d), BF16)],
        compiler_params=_cparams(("parallel", "arbitrary")),
        name="peer_experts",
    )(hn, wp["peer_u"], wp["peer_vt"], cnt, e1, rank2, e2, x1, wp["norm_final"])


def _prep_weights(l, p):
    f = lambda a: a.astype(F32)
    w_in = f(p["w_in"][l])
    o_cq, o_ckv, o_kr = 0, MLA_Q_LORA, MLA_Q_LORA + MLA_KV_LORA
    o_dq = o_kr + MLA_ROPE
    half = MLA_ROPE // 2
    d = w_in.shape[0]
    kr = w_in[:, o_kr:o_dq]
    pad = jnp.zeros((d, LANE - MLA_ROPE), F32)
    w_in2 = jnp.concatenate([
        w_in[:, o_cq:o_kr], kr, pad,
        -kr[:, half:], kr[:, :half], pad,
        w_in[:, o_dq:]], axis=1)
    assert w_in2.shape[1] == _C_END

    w_uq = f(p["w_uq"][l]).reshape(MLA_Q_LORA, MLA_HEADS, MLA_NOPE + MLA_ROPE)
    nope, x1, x2 = w_uq[..., :MLA_NOPE], w_uq[..., MLA_NOPE:MLA_NOPE + half], w_uq[..., MLA_NOPE + half:]
    zpad = jnp.zeros((MLA_Q_LORA, MLA_HEADS, HEAD_W - MLA_NOPE - MLA_ROPE), F32)
    q_slab = jnp.concatenate([nope, x1, x2, zpad], axis=-1).reshape(MLA_Q_LORA, -1)
    r_slab = jnp.concatenate([jnp.zeros_like(nope), -x2, x1, zpad], axis=-1).reshape(MLA_Q_LORA, -1)
    w_uq2 = jnp.concatenate([q_slab, r_slab], axis=1)

    w_uk = f(p["w_uk"][l]).reshape(MLA_KV_LORA, MLA_HEADS, MLA_NOPE)
    w_uk2 = jnp.concatenate([w_uk, jnp.zeros((MLA_KV_LORA, MLA_HEADS, HEAD_W - MLA_NOPE), F32)],
                            axis=-1).reshape(MLA_KV_LORA, -1)
    r_idx = jnp.arange(MLA_ROPE)
    cols = jnp.arange(MLA_HEADS * HEAD_W)
    p_kr = ((cols[None, :] % HEAD_W) == (MLA_NOPE + r_idx[:, None])).astype(F32)

    slopes = jnp.exp2(-8.0 * jnp.arange(1, DIFF_HEADS + 1, dtype=F32) / DIFF_HEADS)
    sub = f(p["diff_subln"][l])
    row = lambda a: f(a).reshape(1, -1)
    return {
        "norm_mix": row(p["norm_mix"][l]), "w_in": w_in2.astype(BF16),
        "mla_q_norm": row(p["mla_q_norm"][l]), "w_uq": w_uq2.astype(BF16),
        "mla_kv_norm": row(p["mla_kv_norm"][l]),
        "w_uk": w_uk2.astype(BF16), "p_kr": p_kr.astype(BF16), "w_uv": p["w_uv"][l].astype(BF16),
        "slopes": slopes,
        "diff_lq1": row(p["diff_lq1"][l]), "diff_lk1": row(p["diff_lk1"][l]),
        "diff_lq2": row(p["diff_lq2"][l]), "diff_lk2": row(p["diff_lk2"][l]),
        "diff_subln2": jnp.concatenate([sub, sub]).reshape(1, -1),
        "norm_mem": row(p["norm_mem"][l]), "w_mem_kv": p["w_mem_kv"][l].astype(BF16),
        "w_gate": p["w_gate"][l].astype(BF16), "b_gate": row(p["b_gate"][l]),
        "w_br_a": p["w_br_a"][l].astype(BF16), "w_br_b": p["w_br_b"][l].astype(BF16),
        "w_br_m": p["w_br_m"][l].astype(BF16), "w_o": p["w_o"][l].astype(BF16),
        "norm_ffn": row(p["norm_ffn"][l]),
        "peer_wq": p["peer_wq"][l].astype(BF16),
        "peer_subkeys": p["peer_subkeys"][l].reshape(PEER_HEADS * 2, PEER_NKEYS, PEER_HALF).astype(BF16),
        "peer_u": p["peer_u"][l].astype(BF16), "peer_vt": p["peer_v"][l].T.astype(BF16),
        "norm_final": row(p["norm_final"]),
    }


def _rope_tables(pos):
    half = MLA_ROPE // 2
    inv = (1.0 / (ROPE_THETA ** (np.arange(half, dtype=np.float32) / half))).astype(np.float32)
    ang = pos.astype(np.float32)[:, None] * inv[None, :]
    cos, sin = np.cos(ang), np.sin(ang)
    n = pos.shape[0]
    one = np.ones((n, MLA_NOPE), np.float32)
    zq = np.zeros((n, HEAD_W - MLA_NOPE - MLA_ROPE), np.float32)
    scale = np.float32(MLA_SCALE)
    cq = np.concatenate([one, cos, cos, zq], axis=1) * scale
    sq = np.concatenate([0.0 * one, sin, sin, zq], axis=1) * scale
    zk = np.zeros((n, LANE - MLA_ROPE), np.float32)
    ck = np.concatenate([cos, cos, zk], axis=1)
    sk = np.concatenate([sin, sin, zk], axis=1)
    return cq, sq, ck, sk


def _layer(x, pos, past, mem_k, mem_v, wp, layer, cfg):
    b, s, d = x.shape
    n = b * s
    xf = x.reshape(n, d)
    tabs = _rope_tables(pos)
    if s % cfg["tm"] == 0:
        pos_blocks = s // cfg["tm"]
    else:
        tabs = tuple(np.tile(t, (b, 1)) for t in tabs)
        pos_blocks = n // cfg["tm"]
    ckv, kr, dk, dv, qm, dqb, dkb, dvb, mqb = _proj_in(xf, tabs, wp, tm=cfg["tm"], pos_blocks=pos_blocks)
    new_rows = (ckv.reshape(b, s, -1), kr.reshape(b, s, -1), dk.reshape(b, s, DIFF_HEADS, DIFF_V),
                dv.reshape(b, s, DIFF_HEADS, DIFF_V))
    r3 = lambda a: a.reshape(b, s, -1)
    if past is None:
        ckv_all, kr_all = ckv, kr
        dk_all, dv_all = r3(dkb), r3(dvb)
        kk = s
        q_off = 0
    else:
        p_len = past[0].shape[1]
        kk = p_len + s
        q_off = p_len
        ckv_all = jnp.concatenate([past[0], r3(ckv)], axis=1).reshape(b * kk, -1)
        kr_all = jnp.concatenate([past[1], r3(kr)], axis=1).reshape(b * kk, -1)
        dk_all = jnp.concatenate([past[2].reshape(b, p_len, -1).astype(BF16), r3(dkb)], axis=1)
        dv_all = jnp.concatenate([past[3].reshape(b, p_len, -1).astype(BF16), r3(dvb)], axis=1)
    k_mla, v_mla = _kv_up(ckv_all, kr_all, wp, tm=cfg["tm_kv"])
    att = dict(tk=cfg["tk"] if past is None else kk, q_off=q_off)
    o_a = _mla_attention(r3(qm), k_mla.reshape(b, kk, -1), v_mla.reshape(b, kk, -1), tq=cfg["tq_mla"], **att)
    lam_init = 0.8 - 0.6 * math.exp(-0.3 * layer)
    o_b = _diff_attention(r3(dqb), dk_all, dv_all, wp, lam_init=lam_init, tq=cfg["tq"], **att)
    o_m = _mem_attention(r3(mqb), mem_k, mem_v, tq=cfg["tq"])
    x1, hn = _merge(xf, o_a.reshape(n, -1), o_b.reshape(n, -1), o_m.reshape(n, -1), wp, tm=cfg["tm"])
    return x1, hn, new_rows


def _peer_and_norm(x1, hn, wp, cfg):
    cnt, e1, rank2, e2 = _route(hn, wp, tt=cfg["tt"])
    return _peer(hn, cnt, e1, rank2, e2, x1, wp, tt=cfg["tt"], te=cfg["te"])


_CFG_PROMPT = dict(tm=512, tm_kv=512, tq=512, tq_mla=1024, tk=512, tt=512, te=2048)
_CFG_SAMPLE = dict(tm=256, tm_kv=256, tq=32, tq_mla=32, tk=None, tt=256, te=2048)


def kernel(x_prompt, x_sample, cache_mla_ckv, cache_mla_krope, cache_diff_k, cache_diff_v, cache_mem_k, cache_mem_v, mem_prompt, norm_mix, w_in, mla_q_norm, w_uq, mla_kv_norm, w_uk, w_uv, diff_lq1, diff_lk1, diff_lq2, diff_lk2, diff_subln, norm_mem, w_mem_kv, w_br_a, w_br_b, w_br_m, w_gate, b_gate, w_o, norm_ffn, peer_wq, peer_subkeys, peer_u, peer_v, norm_final):
    params = dict(norm_mix=norm_mix, w_in=w_in, mla_q_norm=mla_q_norm, w_uq=w_uq, mla_kv_norm=mla_kv_norm,
                  w_uk=w_uk, w_uv=w_uv, diff_lq1=diff_lq1, diff_lk1=diff_lk1, diff_lq2=diff_lq2,
                  diff_lk2=diff_lk2, diff_subln=diff_subln, norm_mem=norm_mem, w_mem_kv=w_mem_kv,
                  w_br_a=w_br_a, w_br_b=w_br_b, w_br_m=w_br_m, w_gate=w_gate, b_gate=b_gate, w_o=w_o,
                  norm_ffn=norm_ffn, peer_wq=peer_wq, peer_subkeys=peer_subkeys, peer_u=peer_u,
                  peer_v=peer_v, norm_final=norm_final)
    depth = w_in.shape[0]
    assert depth == 1, "the final norm is fused into the last PEER step of a single layer"
    bp, sp, d = x_prompt.shape
    bs, ss, _ = x_sample.shape
    n_mem = mem_prompt.shape[1]
    past_len = cache_mla_ckv.shape[2]
    pos_p = np.arange(sp, dtype=np.int32)
    pos_s = past_len + np.arange(ss, dtype=np.int32)

    l = 0
    wp = _prep_weights(l, params)
    mk, mv, mkb, mvb = _mem_kv(mem_prompt.reshape(bp * n_mem, d), wp, tm=n_mem)
    m3 = lambda a, b: a.reshape(b, n_mem, -1)
    x1p, hnp, rows_p = _layer(x_prompt, pos_p, None, m3(mkb, bp), m3(mvb, bp), wp, l, _CFG_PROMPT)
    past = (cache_mla_ckv[l], cache_mla_krope[l], cache_diff_k[l], cache_diff_v[l])
    x1s, hns, rows_s = _layer(x_sample, pos_s, past, m3(cache_mem_k[l].astype(BF16), bs),
                              m3(cache_mem_v[l].astype(BF16), bs), wp, l, _CFG_SAMPLE)
    y_prompt = _peer_and_norm(x1p, hnp, wp, _CFG_PROMPT).reshape(bp, sp, d)
    y_sample = _peer_and_norm(x1s, hns, wp, _CFG_SAMPLE).reshape(bs, ss, d)
    st = lambda a: a[None]
    mem4 = lambda a: a.reshape(bp, n_mem, MEM_HEADS, MEM_DH)[None]
    return (y_prompt, y_sample,
            st(rows_p[0]), st(rows_p[1]), st(rows_p[2]), st(rows_p[3]),
            mem4(mk), mem4(mv),
            st(rows_s[0]), st(rows_s[1]), st(rows_s[2]), st(rows_s[3]))
```

```python
import functools
import math

import jax
import jax.numpy as jnp
import numpy as np
from jax import lax
from jax.experimental import pallas as pl
from jax.experimental.pallas import tpu as pltpu

F32 = jnp.float32
BF16 = jnp.bfloat16

CHUNK = 64
CHUNK_SHIFT = 6
EPS = 1e-6
NEG = -1e30
MLA_HEADS = 8
MLA_Q_LORA = 384
MLA_KV_LORA = 256
MLA_NOPE = 64
MLA_ROPE = 32
MLA_V = 64
ROPE_THETA = 10000.0
LOG2E = math.log2(math.e)
MLA_SCALE = (MLA_NOPE + MLA_ROPE) ** -0.5 * LOG2E
DIFF_HEADS = 8
DIFF_DH = 32
DIFF_V = 2 * DIFF_DH
DIFF_SCALE = DIFF_DH ** -0.5 * LOG2E
MEM_HEADS = 4
MEM_DH = 128
MEM_SCALE = MEM_DH ** -0.5 * LOG2E
PEER_HEADS = 8
PEER_NKEYS = 128
PEER_HALF = 128
PEER_TOPK = 16
LANE = 128
HEAD_W = 128

VMEM_LIMIT = 56 * 1024 * 1024

_C_CQ = 0
_C_CKV = _C_CQ + MLA_Q_LORA
_C_KR = _C_CKV + MLA_KV_LORA
_C_KRR = _C_KR + LANE
_C_DQ = _C_KRR + LANE
_C_DK = _C_DQ + DIFF_HEADS * DIFF_V
_C_DV = _C_DK + DIFF_HEADS * DIFF_V
_C_MQ = _C_DV + DIFF_HEADS * DIFF_V
_C_END = _C_MQ + MEM_HEADS * MEM_DH


def _cparams(sem):
    return pltpu.CompilerParams(dimension_semantics=sem, vmem_limit_bytes=VMEM_LIMIT)


def _rms(x, g):
    return x * lax.rsqrt(jnp.mean(x * x, axis=-1, keepdims=True) + EPS) * g


def _dot(a, b):
    return jnp.dot(a, b, preferred_element_type=F32)


def _dot_nt(a, b):
    return lax.dot_general(a, b, (((1,), (1,)), ((), ())), preferred_element_type=F32)


def _full_spec(shape):
    nd = len(shape)
    return pl.BlockSpec(shape, lambda *_: (0,) * nd)


def _proj_in_kernel(x_ref, g_ref, win_ref, qn_ref, wuq_ref, kvn_ref, cq_ref, sq_ref, ck_ref, sk_ref,
                    ckv_ref, kr_ref, dk_ref, dv_ref, qm_ref, dqb_ref, dkb_ref, dvb_ref, mqb_ref):
    h = _rms(x_ref[...], g_ref[...]).astype(BF16)
    z = _dot(h, win_ref[...])
    cqn = _rms(z[:, _C_CQ:_C_CKV], qn_ref[...]).astype(BF16)
    q2 = _dot(cqn, wuq_ref[...])
    cq = cq_ref[...]
    sq = sq_ref[...]
    nq = MLA_HEADS * HEAD_W
    for hh in range(MLA_HEADS):
        lo = hh * HEAD_W
        qm_ref[:, lo:lo + HEAD_W] = (q2[:, lo:lo + HEAD_W] * cq
                                     + q2[:, nq + lo:nq + lo + HEAD_W] * sq).astype(BF16)
    ckv_ref[...] = _rms(z[:, _C_CKV:_C_KR], kvn_ref[...])
    kr = z[:, _C_KR:_C_KRR] * ck_ref[...] + z[:, _C_KRR:_C_DQ] * sk_ref[...]
    kr_ref[...] = kr[:, :MLA_ROPE]
    dk = z[:, _C_DK:_C_DV]
    dv = z[:, _C_DV:_C_MQ]
    dk_ref[...] = dk
    dv_ref[...] = dv
    dkb_ref[...] = dk.astype(BF16)
    dvb_ref[...] = dv.astype(BF16)
    dqb_ref[...] = (z[:, _C_DQ:_C_DK] * DIFF_SCALE).astype(BF16)
    mqb_ref[...] = (z[:, _C_MQ:_C_END] * MEM_SCALE).astype(BF16)


def _proj_in(x, tabs, wp, *, tm, pos_blocks):
    n, d = x.shape
    grid = (n // tm,)
    row = lambda w: pl.BlockSpec((tm, w), lambda i: (i, 0))
    tab = pl.BlockSpec((tm, LANE), lambda i: (i % pos_blocks, 0))
    dw = DIFF_HEADS * DIFF_V
    out_shape = (
        jax.ShapeDtypeStruct((n, MLA_KV_LORA), F32),
        jax.ShapeDtypeStruct((n, MLA_ROPE), F32),
        jax.ShapeDtypeStruct((n, dw), F32),
        jax.ShapeDtypeStruct((n, dw), F32),
        jax.ShapeDtypeStruct((n, MLA_HEADS * HEAD_W), BF16),
        jax.ShapeDtypeStruct((n, dw), BF16),
        jax.ShapeDtypeStruct((n, dw), BF16),
        jax.ShapeDtypeStruct((n, dw), BF16),
        jax.ShapeDtypeStruct((n, MEM_HEADS * MEM_DH), BF16),
    )
    return pl.pallas_call(
        _proj_in_kernel,
        grid=grid,
        in_specs=[row(d), _full_spec((1, d)), _full_spec(wp["w_in"].shape), _full_spec((1, MLA_Q_LORA)),
                  _full_spec(wp["w_uq"].shape), _full_spec((1, MLA_KV_LORA)), tab, tab, tab, tab],
        out_specs=(row(MLA_KV_LORA), row(MLA_ROPE), row(dw), row(dw), row(MLA_HEADS * HEAD_W),
                   row(dw), row(dw), row(dw), row(MEM_HEADS * MEM_DH)),
        out_shape=out_shape,
        compiler_params=_cparams(("parallel",)),
        name="proj_in",
    )(x, wp["norm_mix"], wp["w_in"], wp["mla_q_norm"], wp["w_uq"], wp["mla_kv_norm"], *tabs)


def _kv_up_kernel(ckv_ref, kr_ref, wuk_ref, pk_ref, wuv_ref, k_ref, v_ref):
    c = ckv_ref[...].astype(BF16)
    k = _dot(c, wuk_ref[...]) + _dot(kr_ref[...].astype(BF16), pk_ref[...])
    k_ref[...] = k.astype(BF16)
    v_ref[...] = _dot(c, wuv_ref[...]).astype(BF16)


def _kv_up(ckv, kr, wp, *, tm):
    n = ckv.shape[0]
    row = lambda w: pl.BlockSpec((tm, w), lambda i: (i, 0))
    return pl.pallas_call(
        _kv_up_kernel,
        grid=(n // tm,),
        in_specs=[row(MLA_KV_LORA), row(MLA_ROPE), _full_spec(wp["w_uk"].shape),
                  _full_spec(wp["p_kr"].shape), _full_spec(wp["w_uv"].shape)],
        out_specs=(row(MLA_HEADS * HEAD_W), row(MLA_HEADS * MLA_V)),
        out_shape=(jax.ShapeDtypeStruct((n, MLA_HEADS * HEAD_W), BF16),
                   jax.ShapeDtypeStruct((n, MLA_HEADS * MLA_V), BF16)),
        compiler_params=_cparams(("parallel",)),
        name="kv_up",
    )(ckv, kr, wp["w_uk"], wp["p_kr"], wp["w_uv"])


def _lanes(col, n):
    if n == LANE:
        return col
    if n % LANE == 0:
        return jnp.concatenate([col] * (n // LANE), axis=1)
    return jnp.broadcast_to(col[:, :1], (col.shape[0], n))


GROUP_VREGS = 16


def _softmax_block(score_group, tq, tk, row0, m_scr, al_scr, p_scr, l_scr=None):
    rg = min(tq, max(16, (GROUP_VREGS * 8 * LANE // tk) // 16 * 16))
    for g in range(tq // rg):
        r = slice(row0 + g * rg, row0 + (g + 1) * rg)
        m_prev = m_scr[r]
        m_next = jnp.maximum(m_prev, jnp.max(score_group(g, rg), axis=1, keepdims=True))
        al_scr[r] = jnp.exp2(m_prev - m_next)
        m_scr[r] = m_next
    for g in range(tq // rg):
        r = slice(row0 + g * rg, row0 + (g + 1) * rg)
        e = jnp.exp2(score_group(g, rg) - _lanes(m_scr[r], tk))
        p_scr[r] = e.astype(BF16)
        if l_scr is not None:
            l_scr[r] = al_scr[r] * l_scr[r] + jnp.sum(e, axis=1, keepdims=True)


def _accumulate(rows, v, al_scr, l_scr, acc_scr, p_scr):
    pv = _dot(p_scr[rows], jnp.concatenate([v, jnp.ones(v.shape, v.dtype)], axis=1))
    w = v.shape[1]
    acc_scr[rows] = acc_scr[rows] * al_scr[rows] + pv[:, :w]
    l_scr[rows] = l_scr[rows] * al_scr[rows] + pv[:, w:]


def _chunk_visible(q_lo, k_lo, g, rg, tk):
    qp = q_lo + g * rg + lax.broadcasted_iota(jnp.int32, (rg, tk), 0)
    kp = k_lo + lax.broadcasted_iota(jnp.int32, (rg, tk), 1)
    return qp, kp, (kp >> CHUNK_SHIFT) <= (qp >> CHUNK_SHIFT)


def _init_stats(m_scr, l_scr, acc_scr):
    m_scr[...] = jnp.full(m_scr.shape, NEG, F32)
    l_scr[...] = jnp.zeros(l_scr.shape, F32)
    acc_scr[...] = jnp.zeros(acc_scr.shape, F32)


def _block_range(qi, *, tq, tk, q_off, nk):
    q_lo = q_off + qi * tq
    q_hi = q_lo + (tq - 1)
    n_behind = jnp.minimum((q_lo + 1) // tk, nk)
    last_key = ((q_hi >> CHUNK_SHIFT) << CHUNK_SHIFT) + (CHUNK - 1)
    n_need = jnp.minimum(last_key // tk + 1, nk)
    return q_lo, n_behind, n_need


def _for_blocks(lo, hi, fn):
    lax.fori_loop(lo, hi, lambda i, c: (fn(i), c)[1], 0)


def _attn_specs(b, pairs, sq, sk, tq, qw, kw, vw):
    grid = (b, pairs, sq // tq)
    in_specs = [pl.BlockSpec((1, tq, qw), lambda bi, p, qi: (bi, qi, p)),
                pl.BlockSpec((1, sk, kw), lambda bi, p, qi: (bi, 0, p)),
                pl.BlockSpec((1, sk, vw), lambda bi, p, qi: (bi, 0, p))]
    out_spec = pl.BlockSpec((1, tq, vw), lambda bi, p, qi: (bi, qi, p))
    return grid, in_specs, out_spec


def _mla_kernel(q_ref, k_ref, v_ref, o_ref, m_scr, l_scr, al_scr, acc_scr, p_scr, *, tq, tk, q_off, nk):
    qi = pl.program_id(2)
    _init_stats(m_scr, l_scr, acc_scr)
    q_lo, n_behind, n_need = _block_range(qi, tq=tq, tk=tk, q_off=q_off, nk=nk)

    def block(i, masked):
        k_lo = pl.multiple_of(i * tk, tk)
        for hh in range(2):
            hs = slice(hh * HEAD_W, (hh + 1) * HEAD_W)
            s = _dot_nt(q_ref[0, :, hs], k_ref[0, pl.ds(k_lo, tk), hs])

            def score_group(g, rg, s=s):
                sg = s[g * rg:(g + 1) * rg]
                if masked:
                    sg = jnp.where(_chunk_visible(q_lo, k_lo, g, rg, tk)[2], sg, NEG)
                return sg

            _softmax_block(score_group, tq, tk, hh * tq, m_scr, al_scr, p_scr)
        _accumulate(slice(0, 2 * tq), v_ref[0, pl.ds(k_lo, tk), :], al_scr, l_scr, acc_scr, p_scr)

    _for_blocks(0, n_behind, lambda i: block(i, False))
    _for_blocks(n_behind, n_need, lambda i: block(i, True))

    lane = lax.broadcasted_iota(jnp.int32, (tq, LANE), 1)
    o0 = acc_scr[0:tq] / l_scr[0:tq]
    o1 = acc_scr[tq:2 * tq] / l_scr[tq:2 * tq]
    o_ref[0] = jnp.where(lane < MLA_V, o0, o1).astype(o_ref.dtype)


def _mla_attention(q, k, v, *, tq, tk, q_off):
    b, sq, _ = q.shape
    sk = k.shape[1]
    grid, in_specs, out_spec = _attn_specs(b, MLA_HEADS // 2, sq, sk, tq, 2 * HEAD_W, 2 * HEAD_W, 2 * MLA_V)
    kern = functools.partial(_mla_kernel, tq=tq, tk=tk, q_off=q_off, nk=sk // tk)
    stat = pltpu.VMEM((2 * tq, LANE), F32)
    return pl.pallas_call(
        kern,
        grid=grid,
        in_specs=in_specs,
        out_specs=out_spec,
        out_shape=jax.ShapeDtypeStruct((b, sq, MLA_HEADS * MLA_V), BF16),
        scratch_shapes=[stat, stat, stat, stat, pltpu.VMEM((2 * tq, tk), BF16)],
        compiler_params=_cparams(("parallel", "parallel", "parallel")),
        name="mla_attn",
    )(q, k, v)


def _diff_kernel(slope_ref, lq1_ref, lk1_ref, lq2_ref, lk2_ref, sub_ref, q_ref, k_ref, v_ref, o_ref,
                 qs_scr, m_scr, l_scr, al_scr, acc_scr, p_scr, *, tq, tk, q_off, nk, lam_init):
    pr = pl.program_id(1)
    qi = pl.program_id(2)
    _init_stats(m_scr, l_scr, acc_scr)
    q_lo, n_behind, n_need = _block_range(qi, tq=tq, tk=tk, q_off=q_off, nk=nk)

    q = q_ref[0]
    lane = lax.broadcasted_iota(jnp.int32, (tq, LANE), 1)
    for mi in range(4):
        lo = mi * DIFF_DH
        qs_scr[mi * tq:(mi + 1) * tq] = jnp.where((lane >= lo) & (lane < lo + DIFF_DH), q, jnp.zeros_like(q))

    def block(i, masked):
        k_lo = pl.multiple_of(i * tk, tk)
        kb = k_ref[0, pl.ds(k_lo, tk), :]
        kpos = (k_lo + lax.broadcasted_iota(jnp.int32, (1, tk), 1)).astype(F32)

        def scores(mi):
            return _dot_nt(qs_scr[mi * tq:(mi + 1) * tq], kb)

        def softmax(mi, s_map):
            slope = slope_ref[2 * pr + mi // 2] * LOG2E

            def score_group(g, rg):
                sg = s_map[g * rg:(g + 1) * rg]
                if masked:
                    qp, kp, vis = _chunk_visible(q_lo, k_lo, g, rg, tk)
                    return jnp.where(vis, sg + slope * jnp.minimum(kp, 2 * qp - kp).astype(F32), NEG)
                return sg + slope * kpos

            _softmax_block(score_group, tq, tk, mi * tq, m_scr, al_scr, p_scr, l_scr)

        def values(mi):
            rows = slice(mi * tq, (mi + 1) * tq)
            acc_scr[rows] = acc_scr[rows] * al_scr[rows] + _dot(p_scr[rows], v_ref[0, pl.ds(k_lo, tk), :])

        s0 = scores(0)
        s1 = scores(1)
        softmax(0, s0)
        s2 = scores(2)
        values(0)
        softmax(1, s1)
        s3 = scores(3)
        values(1)
        softmax(2, s2)
        values(2)
        softmax(3, s3)
        values(3)

    _for_blocks(0, n_behind, lambda i: block(i, False))
    _for_blocks(n_behind, n_need, lambda i: block(i, True))

    lam = (jnp.exp(jnp.sum(lq1_ref[...] * lk1_ref[...], axis=1, keepdims=True))
           - jnp.exp(jnp.sum(lq2_ref[...] * lk2_ref[...], axis=1, keepdims=True)) + lam_init)
    first = lane < DIFF_V
    on = [acc_scr[mi * tq:(mi + 1) * tq] / l_scr[mi * tq:(mi + 1) * tq] for mi in range(4)]
    o = jnp.where(first, on[0] - lam * on[1], on[2] - lam * on[3])
    sq = o * o
    ms0 = jnp.sum(jnp.where(first, sq, 0.0), axis=1, keepdims=True) * (1.0 / DIFF_V)
    ms1 = jnp.sum(jnp.where(first, 0.0, sq), axis=1, keepdims=True) * (1.0 / DIFF_V)
    r = jnp.where(first, lax.rsqrt(ms0 + EPS), lax.rsqrt(ms1 + EPS))
    o_ref[0] = ((o * r * sub_ref[...]) * (1.0 - lam_init)).astype(o_ref.dtype)


def _diff_attention(q, k, v, wp, *, tq, tk, q_off, lam_init):
    b, sq, _ = q.shape
    sk = k.shape[1]
    grid, in_specs, out_spec = _attn_specs(b, DIFF_HEADS // 2, sq, sk, tq, LANE, LANE, LANE)
    kern = functools.partial(_diff_kernel, tq=tq, tk=tk, q_off=q_off, nk=sk // tk, lam_init=lam_init)
    small = lambda w: pl.BlockSpec((1, w), lambda bi, p, qi: (0, 0))
    stat = pltpu.VMEM((4 * tq, LANE), F32)
    return pl.pallas_call(
        kern,
        grid=grid,
        in_specs=[pl.BlockSpec(memory_space=pltpu.SMEM),
                  small(DIFF_DH), small(DIFF_DH), small(DIFF_DH), small(DIFF_DH), small(LANE)] + in_specs,
        out_specs=out_spec,
        out_shape=jax.ShapeDtypeStruct((b, sq, DIFF_HEADS * DIFF_V), BF16),
        scratch_shapes=[pltpu.VMEM((4 * tq, LANE), BF16), stat, stat, stat, stat,
                        pltpu.VMEM((4 * tq, tk), BF16)],
        compiler_params=_cparams(("parallel", "parallel", "parallel")),
        name="diff_attn",
    )(wp["slopes"], wp["diff_lq1"], wp["diff_lk1"], wp["diff_lq2"], wp["diff_lk2"], wp["diff_subln2"],
      q, k, v)


def _mem_attn_kernel(q_ref, k_ref, v_ref, o_ref):
    for hh in range(MEM_HEADS):
        sl = slice(hh * MEM_DH, (hh + 1) * MEM_DH)
        s = _dot_nt(q_ref[0, :, sl], k_ref[0, :, sl])
        p = jnp.exp2(s - jnp.max(s, axis=1, keepdims=True))
        o = _dot(p.astype(BF16), v_ref[0, :, sl]) / jnp.sum(p, axis=1, keepdims=True)
        o_ref[0, :, sl] = o.astype(o_ref.dtype)


def _mem_attention(q, k, v, *, tq):
    b, sq, w = q.shape
    nm = k.shape[1]
    return pl.pallas_call(
        _mem_attn_kernel,
        grid=(b, sq // tq),
        in_specs=[pl.BlockSpec((1, tq, w), lambda bi, qi: (bi, qi, 0)),
                  pl.BlockSpec((1, nm, w), lambda bi, qi: (bi, 0, 0)),
                  pl.BlockSpec((1, nm, w), lambda bi, qi: (bi, 0, 0))],
        out_specs=pl.BlockSpec((1, tq, w), lambda bi, qi: (bi, qi, 0)),
        out_shape=jax.ShapeDtypeStruct((b, sq, w), BF16),
        compiler_params=_cparams(("parallel", "parallel")),
        name="mem_attn",
    )(q, k, v)


def _mem_kv_kernel(x_ref, g_ref, w_ref, k_ref, v_ref, kb_ref, vb_ref):
    h = _rms(x_ref[...], g_ref[...]).astype(BF16)
    kv = _dot(h, w_ref[...])
    w = MEM_HEADS * MEM_DH
    k_ref[...] = kv[:, :w]
    v_ref[...] = kv[:, w:]
    kb_ref[...] = kv[:, :w].astype(BF16)
    vb_ref[...] = kv[:, w:].astype(BF16)


def _mem_kv(mem, wp, *, tm):
    n, d = mem.shape
    w = MEM_HEADS * MEM_DH
    row = lambda c: pl.BlockSpec((tm, c), lambda i: (i, 0))
    return pl.pallas_call(
        _mem_kv_kernel,
        grid=(n // tm,),
        in_specs=[row(d), _full_spec((1, d)), _full_spec(wp["w_mem_kv"].shape)],
        out_specs=(row(w), row(w), row(w), row(w)),
        out_shape=(jax.ShapeDtypeStruct((n, w), F32), jax.ShapeDtypeStruct((n, w), F32),
                   jax.ShapeDtypeStruct((n, w), BF16), jax.ShapeDtypeStruct((n, w), BF16)),
        compiler_params=_cparams(("parallel",)),
        name="mem_kv",
    )(mem, wp["norm_mem"], wp["w_mem_kv"])


def _merge_kernel(x_ref, oa_ref, ob_ref, om_ref, g_ref, wg_ref, bg_ref, wa_ref, wb_ref, wm_ref, wo_ref,
                  gf_ref, x1_ref, hn_ref):
    x = x_ref[...]
    d = x.shape[1]
    h = _rms(x, g_ref[...]).astype(BF16)
    gates = jax.nn.sigmoid(_dot(h, wg_ref[...]) + bg_ref[...])
    merged = (gates[:, :d] * _dot(oa_ref[...], wa_ref[...])
              + gates[:, d:2 * d] * _dot(ob_ref[...], wb_ref[...])
              + gates[:, 2 * d:] * _dot(om_ref[...], wm_ref[...]))
    x1 = x + _dot(merged.astype(BF16), wo_ref[...])
    x1_ref[...] = x1
    hn_ref[...] = _rms(x1, gf_ref[...]).astype(BF16)


def _merge(x, oa, ob, om, wp, *, tm):
    n, d = x.shape
    row = lambda c: pl.BlockSpec((tm, c), lambda i: (i, 0))
    names = ("norm_mix", "w_gate", "b_gate", "w_br_a", "w_br_b", "w_br_m", "w_o", "norm_ffn")
    return pl.pallas_call(
        _merge_kernel,
        grid=(n // tm,),
        in_specs=[row(d), row(oa.shape[1]), row(ob.shape[1]), row(om.shape[1])]
                 + [_full_spec(wp[k].shape) for k in names],
        out_specs=(row(d), row(d)),
        out_shape=(jax.ShapeDtypeStruct((n, d), F32), jax.ShapeDtypeStruct((n, d), BF16)),
        compiler_params=_cparams(("parallel",)),
        name="merge",
    )(x, oa, ob, om, *[wp[k] for k in names])


def _top_values(sc, k, with_rank=False):
    vals = []
    cur = sc
    rank = jnp.full(sc.shape, float(k), F32) if with_rank else None
    for r in range(k):
        m = jnp.max(cur, axis=0, keepdims=True)
        vals.append(m)
        hit = cur == m
        if with_rank:
            rank = jnp.where(hit, float(r), rank)
        if r + 1 < k:
            cur = jnp.where(hit, NEG, cur)
    return (vals, rank) if with_rank else vals


def _stack_rows(rows, sub):
    out = jnp.zeros(sub.shape, F32)
    for r, row in enumerate(rows):
        if row is not None:
            out = jnp.where(sub == r, row, out)
    return out


def _packed_candidates(t1, t2, t2h, sub8):
    assert PEER_TOPK == 16
    down = lambda x, k: pltpu.roll(x, k, 0)
    t1_8_13 = _stack_rows([None, None] + t1[8:14], sub8)
    t1_14_15 = _stack_rows(t1[14:16], sub8)
    top2 = t2[0:1]
    slabs = [
        t1[0] + t2h,
        t1[0] + t2[8:16],
        t1[1] + t2h,
        jnp.where(sub8 < 5, t1[2] + t2h, t1[4] + down(t2h, 5)),
        jnp.where(sub8 < 4, t1[3] + t2h,
                  jnp.where(sub8 < 6, t1[5] + down(t2h, 4), t1[6] + down(t2h, 6))),
        jnp.where(sub8 < 2, t1[7] + t2h, t1_8_13 + top2),
        jnp.where(sub8 < 2, t1_14_15 + top2, NEG),
    ]
    return jnp.concatenate(slabs, axis=0)


def _route_kernel(hn_ref, wq_ref, sk_ref, cnt_ref, e1_ref, rank_ref, e2_ref):
    tt = hn_ref.shape[0]
    q = _dot(hn_ref[...], wq_ref[...]).astype(BF16)
    sub = lax.broadcasted_iota(jnp.int32, (PEER_TOPK, tt), 0)
    sub8 = lax.broadcasted_iota(jnp.int32, (8, tt), 0)
    for h in range(PEER_HEADS):
        sc1 = _dot_nt(sk_ref[2 * h], q[:, (2 * h) * PEER_HALF:(2 * h + 1) * PEER_HALF])
        sc2 = _dot_nt(sk_ref[2 * h + 1], q[:, (2 * h + 1) * PEER_HALF:(2 * h + 2) * PEER_HALF])
        t1 = _top_values(sc1, PEER_TOPK)
        t2_rows, rank2 = _top_values(sc2, PEER_TOPK, with_rank=True)
        t2 = _stack_rows(t2_rows, sub)
        t2h = t2[:8]
        cands = []
        for a in range(PEER_TOPK):
            nb = PEER_TOPK // (a + 1)
            c = t1[a] + (t2 if nb > 8 else t2h)
            if nb < c.shape[0]:
                c = jnp.where((sub if nb > 8 else sub8) < nb, c, NEG)
            cands.append(c)
        cand = _packed_candidates(t1, t2, t2h, sub8)
        tau = _top_values(cand, PEER_TOPK)[-1]
        top = t1[0] + t2[0:1]
        z = jnp.sum(jnp.where(cand >= tau, jnp.exp(cand - top), 0.0), axis=0, keepdims=True)
        cnt = jnp.zeros(sc1.shape, F32)
        for a in range(PEER_TOPK):
            ca = jnp.sum(jnp.where(cands[a] >= tau, 1.0, 0.0), axis=0, keepdims=True)
            cnt = jnp.where(sc1 == t1[a], ca, cnt)
        cnt_ref[h] = cnt
        e1_ref[h] = jnp.exp(sc1 - t1[0]) * (0.5 / z)
        rank_ref[h] = rank2.astype(BF16)
        e2_ref[h] = jnp.exp(sc2 - t2[0:1]).astype(BF16)


def _route(hn, wp, *, tt):
    n, d = hn.shape
    big = pl.BlockSpec((PEER_HEADS, PEER_NKEYS, tt), lambda i: (0, 0, i))
    big_shape = jax.ShapeDtypeStruct((PEER_HEADS, PEER_NKEYS, n), F32)
    big_half = jax.ShapeDtypeStruct((PEER_HEADS, PEER_NKEYS, n), BF16)
    return pl.pallas_call(
        _route_kernel,
        grid=(n // tt,),
        in_specs=[pl.BlockSpec((tt, d), lambda i: (i, 0)), _full_spec(wp["peer_wq"].shape),
                  _full_spec(wp["peer_subkeys"].shape)],
        out_specs=(big, big, big, big),
        out_shape=(big_shape, big_shape, big_half, big_half),
        compiler_params=_cparams(("parallel",)),
        name="peer_route",
    )(hn, wp["peer_wq"], wp["peer_subkeys"])


PAIR = 2 * PEER_NKEYS
SUBROWS = 128


def _peer_kernel(hn_in, u_ref, vt_ref, cnt_ref, e1_ref, rank_in, e2_in, x1_ref, gn_ref, y_ref,
                 acc_scr, hw_scr, rank_ref, e2_ref, hn_ref, *, te, ne):
    j = pl.program_id(1)
    tt = hn_in.shape[0]
    npair = te // PAIR

    @pl.when(j == 0)
    def _():
        rank_ref[...] = rank_in[...]
        e2_ref[...] = e2_in[...]
        hn_ref[...] = hn_in[...]

    def pre_act(p):
        a = _dot_nt(u_ref[p * PAIR:(p + 1) * PAIR, :], hn_ref[...])
        return a * (1.0 + lax.erf(a * math.sqrt(0.5)))

    def down(lo, hi):
        return _dot(vt_ref[:, lo * PAIR:hi * PAIR], hw_scr[lo * PAIR:hi * PAIR, :])

    cuts = sorted({0, 3 * npair // 8, 3 * npair // 4, npair - 1, npair})
    piece_after = {hi: lo for lo, hi in zip(cuts[:-1], cuts[1:])}

    groups = [(tc, sb) for tc in range(tt // LANE) for sb in range(PEER_NKEYS // SUBROWS)]
    zero = jnp.zeros((), BF16)
    act = pre_act(0)
    tot = None
    carry = None
    for p in range(npair):
        nxt = None
        for gi, (tc, sb) in enumerate(groups):
            if gi == 1 and p + 1 < npair:
                nxt = pre_act(p + 1)
            if gi == len(groups) // 2 and p in piece_after:
                part = down(piece_after[p], p)
                tot = part if tot is None else tot + part
            ls = slice(tc * LANE, (tc + 1) * LANE)
            i2 = slice(sb * SUBROWS, (sb + 1) * SUBROWS)
            w = [None, None] if carry is None else [carry * zero, carry * zero]
            for h in range(PEER_HEADS):
                rk = rank_ref[h, i2, ls]
                e2b = e2_ref[h, i2, ls]
                for rr in range(2):
                    i1 = 2 * p + rr
                    cnt = jnp.broadcast_to(cnt_ref[h, i1:i1 + 1, ls], (SUBROWS, LANE)).astype(BF16)
                    e1b = jnp.broadcast_to(e1_ref[h, i1:i1 + 1, ls], (SUBROWS, LANE)).astype(BF16)
                    cw = jnp.where(rk < cnt, e2b, zero) * e1b
                    w[rr] = cw if w[rr] is None else w[rr] + cw
            carry = w[1]
            for rr in range(2):
                r0 = rr * PEER_NKEYS + sb * SUBROWS
                hw_scr[p * PAIR + r0:p * PAIR + r0 + SUBROWS, ls] = w[rr] * act[r0:r0 + SUBROWS, ls].astype(BF16)
        act = nxt
    part = down(piece_after[npair], npair)
    tot = part if tot is None else tot + part

    @pl.when(j == 0)
    def _():
        acc_scr[...] = tot

    @pl.when(j > 0)
    def _():
        acc_scr[...] += tot

    @pl.when(j == ne - 1)
    def _():
        xr = x1_ref[...] + acc_scr[...].T
        y_ref[...] = _rms(xr, gn_ref[...])


def _peer(hn, cnt, e1, rank2, e2, x1, wp, *, tt, te):
    n, d = hn.shape
    ne = wp["peer_u"].shape[0] // te
    big = pl.BlockSpec((PEER_HEADS, PEER_NKEYS, tt), lambda i, j: (0, 0, i))
    rows = pl.BlockSpec((PEER_HEADS, te // PEER_NKEYS, tt), lambda i, j: (0, j, i))
    kern = functools.partial(_peer_kernel, te=te, ne=ne)
    return pl.pallas_call(
        kern,
        grid=(n // tt, ne),
        in_specs=[pl.BlockSpec((tt, d), lambda i, j: (i, 0)),
                  pl.BlockSpec((te, d), lambda i, j: (j, 0)),
                  pl.BlockSpec((d, te), lambda i, j: (0, j)),
                  rows, rows, big, big,
                  pl.BlockSpec((tt, d), lambda i, j: (i, 0)),
                  pl.BlockSpec((1, d), lambda i, j: (0, 0))],
        out_specs=pl.BlockSpec((tt, d), lambda i, j: (i, 0)),
        out_shape=jax.ShapeDtypeStruct((n, d), F32),
        scratch_shapes=[pltpu.VMEM((d, tt), F32), pltpu.VMEM((te, tt), BF16),
                        pltpu.VMEM((PEER_HEADS, PEER_NKEYS, tt), BF16),
                        pltpu.VMEM((PEER_HEADS, PEER_NKEYS, tt), BF16),
                        pltpu.VMEM((tt, d), BF16)],
        compiler_params=_cparams(("parallel", "arbitrary")),
        name="peer_experts",
    )(hn, wp["peer_u"], wp["peer_vt"], cnt, e1, rank2, e2, x1, wp["norm_final"])


def _prep_weights(l, p):
    f = lambda a: a.astype(F32)
    w_in = f(p["w_in"][l])
    o_cq, o_ckv, o_kr = 0, MLA_Q_LORA, MLA_Q_LORA + MLA_KV_LORA
    o_dq = o_kr + MLA_ROPE
    half = MLA_ROPE // 2
    d = w_in.shape[0]
    kr = w_in[:, o_kr:o_dq]
    pad = jnp.zeros((d, LANE - MLA_ROPE), F32)
    w_in2 = jnp.concatenate([
        w_in[:, o_cq:o_kr], kr, pad,
        -kr[:, half:], kr[:, :half], pad,
        w_in[:, o_dq:]], axis=1)
    assert w_in2.shape[1] == _C_END

    w_uq = f(p["w_uq"][l]).reshape(MLA_Q_LORA, MLA_HEADS, MLA_NOPE + MLA_ROPE)
    nope, x1, x2 = w_uq[..., :MLA_NOPE], w_uq[..., MLA_NOPE:MLA_NOPE + half], w_uq[..., MLA_NOPE + half:]
    zpad = jnp.zeros((MLA_Q_LORA, MLA_HEADS, HEAD_W - MLA_NOPE - MLA_ROPE), F32)
    q_slab = jnp.concatenate([nope, x1, x2, zpad], axis=-1).reshape(MLA_Q_LORA, -1)
    r_slab = jnp.concatenate([jnp.zeros_like(nope), -x2, x1, zpad], axis=-1).reshape(MLA_Q_LORA, -1)
    w_uq2 = jnp.concatenate([q_slab, r_slab], axis=1)

    w_uk = f(p["w_uk"][l]).reshape(MLA_KV_LORA, MLA_HEADS, MLA_NOPE)
    w_uk2 = jnp.concatenate([w_uk, jnp.zeros((MLA_KV_LORA, MLA_HEADS, HEAD_W - MLA_NOPE), F32)],
                            axis=-1).reshape(MLA_KV_LORA, -1)
    r_idx = jnp.arange(MLA_ROPE)
    cols = jnp.arange(MLA_HEADS * HEAD_W)
    p_kr = ((cols[None, :] % HEAD_W) == (MLA_NOPE + r_idx[:, None])).astype(F32)

    slopes = jnp.exp2(-8.0 * jnp.arange(1, DIFF_HEADS + 1, dtype=F32) / DIFF_HEADS)
    sub = f(p["diff_subln"][l])
    row = lambda a: f(a).reshape(1, -1)
    return {
        "norm_mix": row(p["norm_mix"][l]), "w_in": w_in2.astype(BF16),
        "mla_q_norm": row(p["mla_q_norm"][l]), "w_uq": w_uq2.astype(BF16),
        "mla_kv_norm": row(p["mla_kv_norm"][l]),
        "w_uk": w_uk2.astype(BF16), "p_kr": p_kr.astype(BF16), "w_uv": p["w_uv"][l].astype(BF16),
        "slopes": slopes,
        "diff_lq1": row(p["diff_lq1"][l]), "diff_lk1": row(p["diff_lk1"][l]),
        "diff_lq2": row(p["diff_lq2"][l]), "diff_lk2": row(p["diff_lk2"][l]),
        "diff_subln2": jnp.concatenate([sub, sub]).reshape(1, -1),
        "norm_mem": row(p["norm_mem"][l]), "w_mem_kv": p["w_mem_kv"][l].astype(BF16),
        "w_gate": p["w_gate"][l].astype(BF16), "b_gate": row(p["b_gate"][l]),
        "w_br_a": p["w_br_a"][l].astype(BF16), "w_br_b": p["w_br_b"][l].astype(BF16),
        "w_br_m": p["w_br_m"][l].astype(BF16), "w_o": p["w_o"][l].astype(BF16),
        "norm_ffn": row(p["norm_ffn"][l]),
        "peer_wq": p["peer_wq"][l].astype(BF16),
        "peer_subkeys": p["peer_subkeys"][l].reshape(PEER_HEADS * 2, PEER_NKEYS, PEER_HALF).astype(BF16),
        "peer_u": p["peer_u"][l].astype(BF16), "peer_vt": p["peer_v"][l].T.astype(BF16),
        "norm_final": row(p["norm_final"]),
    }


def _rope_tables(pos):
    half = MLA_ROPE // 2
    inv = (1.0 / (ROPE_THETA ** (np.arange(half, dtype=np.float32) / half))).astype(np.float32)
    ang = pos.astype(np.float32)[:, None] * inv[None, :]
    cos, sin = np.cos(ang), np.sin(ang)
    n = pos.shape[0]
    one = np.ones((n, MLA_NOPE), np.float32)
    zq = np.zeros((n, HEAD_W - MLA_NOPE - MLA_ROPE), np.float32)
    scale = np.float32(MLA_SCALE)
    cq = np.concatenate([one, cos, cos, zq], axis=1) * scale
    sq = np.concatenate([0.0 * one, sin, sin, zq], axis=1) * scale
    zk = np.zeros((n, LANE - MLA_ROPE), np.float32)
    ck = np.concatenate([cos, cos, zk], axis=1)
    sk = np.concatenate([sin, sin, zk], axis=1)
    return cq, sq, ck, sk


def _layer(x, pos, past, mem_k, mem_v, wp, layer, cfg):
    b, s, d = x.shape
    n = b * s
    xf = x.reshape(n, d)
    tabs = _rope_tables(pos)
    if s % cfg["tm"] == 0:
        pos_blocks = s // cfg["tm"]
    else:
        tabs = tuple(np.tile(t, (b, 1)) for t in tabs)
        pos_blocks = n // cfg["tm"]
    ckv, kr, dk, dv, qm, dqb, dkb, dvb, mqb = _proj_in(xf, tabs, wp, tm=cfg["tm"], pos_blocks=pos_blocks)
    new_rows = (ckv.reshape(b, s, -1), kr.reshape(b, s, -1), dk.reshape(b, s, DIFF_HEADS, DIFF_V),
                dv.reshape(b, s, DIFF_HEADS, DIFF_V))
    r3 = lambda a: a.reshape(b, s, -1)
    if past is None:
        ckv_all, kr_all = ckv, kr
        dk_all, dv_all = r3(dkb), r3(dvb)
        kk = s
        q_off = 0
    else:
        p_len = past[0].shape[1]
        kk = p_len + s
        q_off = p_len
        ckv_all = jnp.concatenate([past[0], r3(ckv)], axis=1).reshape(b * kk, -1)
        kr_all = jnp.concatenate([past[1], r3(kr)], axis=1).reshape(b * kk, -1)
        dk_all = jnp.concatenate([past[2].reshape(b, p_len, -1).astype(BF16), r3(dkb)], axis=1)
        dv_all = jnp.concatenate([past[3].reshape(b, p_len, -1).astype(BF16), r3(dvb)], axis=1)
    k_mla, v_mla = _kv_up(ckv_all, kr_all, wp, tm=cfg["tm_kv"])
    att = dict(tk=cfg["tk"] if past is None else kk, q_off=q_off)
    o_a = _mla_attention(r3(qm), k_mla.reshape(b, kk, -1), v_mla.reshape(b, kk, -1), tq=cfg["tq_mla"], **att)
    lam_init = 0.8 - 0.6 * math.exp(-0.3 * layer)
    o_b = _diff_attention(r3(dqb), dk_all, dv_all, wp, lam_init=lam_init, tq=cfg["tq"], **att)
    o_m = _mem_attention(r3(mqb), mem_k, mem_v, tq=cfg["tq"])
    x1, hn = _merge(xf, o_a.reshape(n, -1), o_b.reshape(n, -1), o_m.reshape(n, -1), wp, tm=cfg["tm"])
    return x1, hn, new_rows


def _peer_and_norm(x1, hn, wp, cfg):
    cnt, e1, rank2, e2 = _route(hn, wp, tt=cfg["tt_route"])
    return _peer(hn, cnt, e1, rank2, e2, x1, wp, tt=cfg["tt"], te=cfg["te"])


_CFG_PROMPT = dict(tm=512, tm_kv=512, tq=512, tq_mla=1024, tk=512, tt_route=256, tt=512, te=2048)
_CFG_SAMPLE = dict(tm=256, tm_kv=256, tq=32, tq_mla=32, tk=None, tt_route=256, tt=256, te=2048)


def kernel(x_prompt, x_sample, cache_mla_ckv, cache_mla_krope, cache_diff_k, cache_diff_v, cache_mem_k, cache_mem_v, mem_prompt, norm_mix, w_in, mla_q_norm, w_uq, mla_kv_norm, w_uk, w_uv, diff_lq1, diff_lk1, diff_lq2, diff_lk2, diff_subln, norm_mem, w_mem_kv, w_br_a, w_br_b, w_br_m, w_gate, b_gate, w_o, norm_ffn, peer_wq, peer_subkeys, peer_u, peer_v, norm_final):
    params = dict(norm_mix=norm_mix, w_in=w_in, mla_q_norm=mla_q_norm, w_uq=w_uq, mla_kv_norm=mla_kv_norm,
                  w_uk=w_uk, w_uv=w_uv, diff_lq1=diff_lq1, diff_lk1=diff_lk1, diff_lq2=diff_lq2,
                  diff_lk2=diff_lk2, diff_subln=diff_subln, norm_mem=norm_mem, w_mem_kv=w_mem_kv,
                  w_br_a=w_br_a, w_br_b=w_br_b, w_br_m=w_br_m, w_gate=w_gate, b_gate=b_gate, w_o=w_o,
                  norm_ffn=norm_ffn, peer_wq=peer_wq, peer_subkeys=peer_subkeys, peer_u=peer_u,
                  peer_v=peer_v, norm_final=norm_final)
    depth = w_in.shape[0]
    assert depth == 1, "the final norm is fused into the last PEER step of a single layer"
    bp, sp, d = x_prompt.shape
    bs, ss, _ = x_sample.shape
    n_mem = mem_prompt.shape[1]
    past_len = cache_mla_ckv.shape[2]
    pos_p = np.arange(sp, dtype=np.int32)
    pos_s = past_len + np.arange(ss, dtype=np.int32)

    l = 0
    wp = _prep_weights(l, params)
    mk, mv, mkb, mvb = _mem_kv(mem_prompt.reshape(bp * n_mem, d), wp, tm=n_mem)
    m3 = lambda a, b: a.reshape(b, n_mem, -1)
    x1p, hnp, rows_p = _layer(x_prompt, pos_p, None, m3(mkb, bp), m3(mvb, bp), wp, l, _CFG_PROMPT)
    past = (cache_mla_ckv[l], cache_mla_krope[l], cache_diff_k[l], cache_diff_v[l])
    x1s, hns, rows_s = _layer(x_sample, pos_s, past, m3(cache_mem_k[l].astype(BF16), bs),
                              m3(cache_mem_v[l].astype(BF16), bs), wp, l, _CFG_SAMPLE)
    y_prompt = _peer_and_norm(x1p, hnp, wp, _CFG_PROMPT).reshape(bp, sp, d)
    y_sample = _peer_and_norm(x1s, hns, wp, _CFG_SAMPLE).reshape(bs, ss, d)
    st = lambda a: a[None]
    mem4 = lambda a: a.reshape(bp, n_mem, MEM_HEADS, MEM_DH)[None]
    return (y_prompt, y_sample,
            st(rows_p[0]), st(rows_p[1]), st(rows_p[2]), st(rows_p[3]),
            mem4(mk), mem4(mv),
            st(rows_s[0]), st(rows_s[1]), st(rows_s[2]), st(rows_s[3]))
```

```python
import functools
import math

import jax
import jax.numpy as jnp
import numpy as np
from jax import lax
from jax.experimental import pallas as pl
from jax.experimental.pallas import tpu as pltpu

F32 = jnp.float32
BF16 = jnp.bfloat16

CHUNK = 64
CHUNK_SHIFT = 6
EPS = 1e-6
NEG = -1e30
MLA_HEADS = 8
MLA_Q_LORA = 384
MLA_KV_LORA = 256
MLA_NOPE = 64
MLA_ROPE = 32
MLA_V = 64
ROPE_THETA = 10000.0
LOG2E = math.log2(math.e)
MLA_SCALE = (MLA_NOPE + MLA_ROPE) ** -0.5 * LOG2E
DIFF_HEADS = 8
DIFF_DH = 32
DIFF_V = 2 * DIFF_DH
DIFF_SCALE = DIFF_DH ** -0.5 * LOG2E
MEM_HEADS = 4
MEM_DH = 128
MEM_SCALE = MEM_DH ** -0.5 * LOG2E
PEER_HEADS = 8
PEER_NKEYS = 128
PEER_HALF = 128
PEER_TOPK = 16
LANE = 128
HEAD_W = 128

VMEM_LIMIT = 56 * 1024 * 1024

_C_CQ = 0
_C_CKV = _C_CQ + MLA_Q_LORA
_C_KR = _C_CKV + MLA_KV_LORA
_C_KRR = _C_KR + LANE
_C_DQ = _C_KRR + LANE
_C_DK = _C_DQ + DIFF_HEADS * DIFF_V
_C_DV = _C_DK + DIFF_HEADS * DIFF_V
_C_MQ = _C_DV + DIFF_HEADS * DIFF_V
_C_END = _C_MQ + MEM_HEADS * MEM_DH


def _cparams(sem):
    return pltpu.CompilerParams(dimension_semantics=sem, vmem_limit_bytes=VMEM_LIMIT)


def _rms(x, g):
    return x * lax.rsqrt(jnp.mean(x * x, axis=-1, keepdims=True) + EPS) * g


def _dot(a, b):
    return jnp.dot(a, b, preferred_element_type=F32)


def _dot_nt(a, b):
    return lax.dot_general(a, b, (((1,), (1,)), ((), ())), preferred_element_type=F32)


def _full_spec(shape):
    nd = len(shape)
    return pl.BlockSpec(shape, lambda *_: (0,) * nd)


def _proj_in_kernel(x_ref, g_ref, win_ref, qn_ref, wuq_ref, kvn_ref, cq_ref, sq_ref, ck_ref, sk_ref,
                    ckv_ref, kr_ref, dk_ref, dv_ref, qm_ref, dqb_ref, dkb_ref, dvb_ref, mqb_ref):
    h = _rms(x_ref[...], g_ref[...]).astype(BF16)
    z = _dot(h, win_ref[...])
    cqn = _rms(z[:, _C_CQ:_C_CKV], qn_ref[...]).astype(BF16)
    q2 = _dot(cqn, wuq_ref[...])
    cq = cq_ref[...]
    sq = sq_ref[...]
    nq = MLA_HEADS * HEAD_W
    for hh in range(MLA_HEADS):
        lo = hh * HEAD_W
        qm_ref[:, lo:lo + HEAD_W] = (q2[:, lo:lo + HEAD_W] * cq
                                     + q2[:, nq + lo:nq + lo + HEAD_W] * sq).astype(BF16)
    ckv_ref[...] = _rms(z[:, _C_CKV:_C_KR], kvn_ref[...])
    kr = z[:, _C_KR:_C_KRR] * ck_ref[...] + z[:, _C_KRR:_C_DQ] * sk_ref[...]
    kr_ref[...] = kr[:, :MLA_ROPE]
    dk = z[:, _C_DK:_C_DV]
    dv = z[:, _C_DV:_C_MQ]
    dk_ref[...] = dk.reshape(dk_ref.shape)
    dv_ref[...] = dv.reshape(dv_ref.shape)
    dkb_ref[...] = dk.astype(BF16)
    dvb_ref[...] = dv.astype(BF16)
    dqb_ref[...] = (z[:, _C_DQ:_C_DK] * DIFF_SCALE).astype(BF16)
    mqb_ref[...] = (z[:, _C_MQ:_C_END] * MEM_SCALE).astype(BF16)


def _proj_in(x, tabs, wp, *, tm, pos_blocks):
    n, d = x.shape
    grid = (n // tm,)
    row = lambda w: pl.BlockSpec((tm, w), lambda i: (i, 0))
    tab = pl.BlockSpec((tm, LANE), lambda i: (i % pos_blocks, 0))
    heads = pl.BlockSpec((tm, DIFF_HEADS, DIFF_V), lambda i: (i, 0, 0))
    dw = DIFF_HEADS * DIFF_V
    out_shape = (
        jax.ShapeDtypeStruct((n, MLA_KV_LORA), F32),
        jax.ShapeDtypeStruct((n, MLA_ROPE), F32),
        jax.ShapeDtypeStruct((n, DIFF_HEADS, DIFF_V), F32),
        jax.ShapeDtypeStruct((n, DIFF_HEADS, DIFF_V), F32),
        jax.ShapeDtypeStruct((n, MLA_HEADS * HEAD_W), BF16),
        jax.ShapeDtypeStruct((n, dw), BF16),
        jax.ShapeDtypeStruct((n, dw), BF16),
        jax.ShapeDtypeStruct((n, dw), BF16),
        jax.ShapeDtypeStruct((n, MEM_HEADS * MEM_DH), BF16),
    )
    return pl.pallas_call(
        _proj_in_kernel,
        grid=grid,
        in_specs=[row(d), _full_spec((1, d)), _full_spec(wp["w_in"].shape), _full_spec((1, MLA_Q_LORA)),
                  _full_spec(wp["w_uq"].shape), _full_spec((1, MLA_KV_LORA)), tab, tab, tab, tab],
        out_specs=(row(MLA_KV_LORA), row(MLA_ROPE), heads, heads, row(MLA_HEADS * HEAD_W),
                   row(dw), row(dw), row(dw), row(MEM_HEADS * MEM_DH)),
        out_shape=out_shape,
        compiler_params=_cparams(("parallel",)),
        name="proj_in",
    )(x, wp["norm_mix"], wp["w_in"], wp["mla_q_norm"], wp["w_uq"], wp["mla_kv_norm"], *tabs)


def _kv_up_kernel(ckv_ref, kr_ref, wuk_ref, pk_ref, wuv_ref, k_ref, v_ref):
    c = ckv_ref[...].astype(BF16)
    k = _dot(c, wuk_ref[...]) + _dot(kr_ref[...].astype(BF16), pk_ref[...])
    k_ref[...] = k.astype(BF16)
    v_ref[...] = _dot(c, wuv_ref[...]).astype(BF16)


def _kv_up(ckv, kr, wp, *, tm):
    n = ckv.shape[0]
    row = lambda w: pl.BlockSpec((tm, w), lambda i: (i, 0))
    return pl.pallas_call(
        _kv_up_kernel,
        grid=(n // tm,),
        in_specs=[row(MLA_KV_LORA), row(MLA_ROPE), _full_spec(wp["w_uk"].shape),
                  _full_spec(wp["p_kr"].shape), _full_spec(wp["w_uv"].shape)],
        out_specs=(row(MLA_HEADS * HEAD_W), row(MLA_HEADS * MLA_V)),
        out_shape=(jax.ShapeDtypeStruct((n, MLA_HEADS * HEAD_W), BF16),
                   jax.ShapeDtypeStruct((n, MLA_HEADS * MLA_V), BF16)),
        compiler_params=_cparams(("parallel",)),
        name="kv_up",
    )(ckv, kr, wp["w_uk"], wp["p_kr"], wp["w_uv"])


def _lanes(col, n):
    if n == LANE:
        return col
    if n % LANE == 0:
        return jnp.concatenate([col] * (n // LANE), axis=1)
    return jnp.broadcast_to(col[:, :1], (col.shape[0], n))


GROUP_VREGS = 16


def _softmax_block(score_group, tq, tk, row0, m_scr, al_scr, p_scr, l_scr=None):
    rg = min(tq, max(16, (GROUP_VREGS * 8 * LANE // tk) // 16 * 16))
    for g in range(tq // rg):
        r = slice(row0 + g * rg, row0 + (g + 1) * rg)
        m_prev = m_scr[r]
        m_next = jnp.maximum(m_prev, jnp.max(score_group(g, rg), axis=1, keepdims=True))
        al_scr[r] = jnp.exp2(m_prev - m_next)
        m_scr[r] = m_next
    for g in range(tq // rg):
        r = slice(row0 + g * rg, row0 + (g + 1) * rg)
        e = jnp.exp2(score_group(g, rg) - _lanes(m_scr[r], tk))
        p_scr[r] = e.astype(BF16)
        if l_scr is not None:
            l_scr[r] = al_scr[r] * l_scr[r] + jnp.sum(e, axis=1, keepdims=True)


def _accumulate(rows, v, al_scr, l_scr, acc_scr, p_scr):
    pv = _dot(p_scr[rows], jnp.concatenate([v, jnp.ones(v.shape, v.dtype)], axis=1))
    w = v.shape[1]
    acc_scr[rows] = acc_scr[rows] * al_scr[rows] + pv[:, :w]
    l_scr[rows] = l_scr[rows] * al_scr[rows] + pv[:, w:]


def _chunk_visible(q_lo, k_lo, g, rg, tk):
    qp = q_lo + g * rg + lax.broadcasted_iota(jnp.int32, (rg, tk), 0)
    kp = k_lo + lax.broadcasted_iota(jnp.int32, (rg, tk), 1)
    return qp, kp, (kp >> CHUNK_SHIFT) <= (qp >> CHUNK_SHIFT)


def _init_stats(m_scr, l_scr, acc_scr):
    m_scr[...] = jnp.full(m_scr.shape, NEG, F32)
    l_scr[...] = jnp.zeros(l_scr.shape, F32)
    acc_scr[...] = jnp.zeros(acc_scr.shape, F32)


def _block_range(qi, *, tq, tk, q_off, nk):
    q_lo = q_off + qi * tq
    q_hi = q_lo + (tq - 1)
    n_behind = jnp.minimum((q_lo + 1) // tk, nk)
    last_key = ((q_hi >> CHUNK_SHIFT) << CHUNK_SHIFT) + (CHUNK - 1)
    n_need = jnp.minimum(last_key // tk + 1, nk)
    return q_lo, n_behind, n_need


def _for_blocks(lo, hi, fn):
    lax.fori_loop(lo, hi, lambda i, c: (fn(i), c)[1], 0)


def _attn_specs(b, pairs, sq, sk, tq, qw, kw, vw):
    grid = (b, pairs, sq // tq)
    in_specs = [pl.BlockSpec((1, tq, qw), lambda bi, p, qi: (bi, qi, p)),
                pl.BlockSpec((1, sk, kw), lambda bi, p, qi: (bi, 0, p)),
                pl.BlockSpec((1, sk, vw), lambda bi, p, qi: (bi, 0, p))]
    out_spec = pl.BlockSpec((1, tq, vw), lambda bi, p, qi: (bi, qi, p))
    return grid, in_specs, out_spec


def _mla_kernel(q_ref, k_ref, v_ref, o_ref, m_scr, l_scr, al_scr, acc_scr, p_scr, *, tq, tk, q_off, nk):
    qi = pl.program_id(2)
    _init_stats(m_scr, l_scr, acc_scr)
    q_lo, n_behind, n_need = _block_range(qi, tq=tq, tk=tk, q_off=q_off, nk=nk)

    def block(i, masked):
        k_lo = pl.multiple_of(i * tk, tk)
        for hh in range(2):
            hs = slice(hh * HEAD_W, (hh + 1) * HEAD_W)
            s = _dot_nt(q_ref[0, :, hs], k_ref[0, pl.ds(k_lo, tk), hs])

            def score_group(g, rg, s=s):
                sg = s[g * rg:(g + 1) * rg]
                if masked:
                    sg = jnp.where(_chunk_visible(q_lo, k_lo, g, rg, tk)[2], sg, NEG)
                return sg

            _softmax_block(score_group, tq, tk, hh * tq, m_scr, al_scr, p_scr)
        _accumulate(slice(0, 2 * tq), v_ref[0, pl.ds(k_lo, tk), :], al_scr, l_scr, acc_scr, p_scr)

    _for_blocks(0, n_behind, lambda i: block(i, False))
    _for_blocks(n_behind, n_need, lambda i: block(i, True))

    lane = lax.broadcasted_iota(jnp.int32, (tq, LANE), 1)
    o0 = acc_scr[0:tq] / l_scr[0:tq]
    o1 = acc_scr[tq:2 * tq] / l_scr[tq:2 * tq]
    o_ref[0] = jnp.where(lane < MLA_V, o0, o1).astype(o_ref.dtype)


def _mla_attention(q, k, v, *, tq, tk, q_off):
    b, sq, _ = q.shape
    sk = k.shape[1]
    grid, in_specs, out_spec = _attn_specs(b, MLA_HEADS // 2, sq, sk, tq, 2 * HEAD_W, 2 * HEAD_W, 2 * MLA_V)
    kern = functools.partial(_mla_kernel, tq=tq, tk=tk, q_off=q_off, nk=sk // tk)
    stat = pltpu.VMEM((2 * tq, LANE), F32)
    return pl.pallas_call(
        kern,
        grid=grid,
        in_specs=in_specs,
        out_specs=out_spec,
        out_shape=jax.ShapeDtypeStruct((b, sq, MLA_HEADS * MLA_V), BF16),
        scratch_shapes=[stat, stat, stat, stat, pltpu.VMEM((2 * tq, tk), BF16)],
        compiler_params=_cparams(("parallel", "parallel", "parallel")),
        name="mla_attn",
    )(q, k, v)


def _diff_kernel(slope_ref, lq1_ref, lk1_ref, lq2_ref, lk2_ref, sub_ref, q_ref, k_ref, v_ref, o_ref,
                 qs_scr, m_scr, l_scr, al_scr, acc_scr, p_scr, *, tq, tk, q_off, nk, lam_init):
    pr = pl.program_id(1)
    qi = pl.program_id(2)
    _init_stats(m_scr, l_scr, acc_scr)
    q_lo, n_behind, n_need = _block_range(qi, tq=tq, tk=tk, q_off=q_off, nk=nk)

    q = q_ref[0]
    lane = lax.broadcasted_iota(jnp.int32, (tq, LANE), 1)
    for mi in range(4):
        lo = mi * DIFF_DH
        qs_scr[mi * tq:(mi + 1) * tq] = jnp.where((lane >= lo) & (lane < lo + DIFF_DH), q, jnp.zeros_like(q))

    def block(i, masked):
        k_lo = pl.multiple_of(i * tk, tk)
        kb = k_ref[0, pl.ds(k_lo, tk), :]
        kpos = (k_lo + lax.broadcasted_iota(jnp.int32, (1, tk), 1)).astype(F32)

        def scores(mi):
            return _dot_nt(qs_scr[mi * tq:(mi + 1) * tq], kb)

        def softmax(mi, s_map):
            slope = slope_ref[2 * pr + mi // 2] * LOG2E

            def score_group(g, rg):
                sg = s_map[g * rg:(g + 1) * rg]
                if masked:
                    qp, kp, vis = _chunk_visible(q_lo, k_lo, g, rg, tk)
                    return jnp.where(vis, sg + slope * jnp.minimum(kp, 2 * qp - kp).astype(F32), NEG)
                return sg + slope * kpos

            _softmax_block(score_group, tq, tk, mi * tq, m_scr, al_scr, p_scr, l_scr)

        def values(mi):
            rows = slice(mi * tq, (mi + 1) * tq)
            acc_scr[rows] = acc_scr[rows] * al_scr[rows] + _dot(p_scr[rows], v_ref[0, pl.ds(k_lo, tk), :])

        s0 = scores(0)
        s1 = scores(1)
        softmax(0, s0)
        s2 = scores(2)
        values(0)
        softmax(1, s1)
        s3 = scores(3)
        values(1)
        softmax(2, s2)
        values(2)
        softmax(3, s3)
        values(3)

    _for_blocks(0, n_behind, lambda i: block(i, False))
    _for_blocks(n_behind, n_need, lambda i: block(i, True))

    lam = (jnp.exp(jnp.sum(lq1_ref[...] * lk1_ref[...], axis=1, keepdims=True))
           - jnp.exp(jnp.sum(lq2_ref[...] * lk2_ref[...], axis=1, keepdims=True)) + lam_init)
    first = lane < DIFF_V
    on = [acc_scr[mi * tq:(mi + 1) * tq] / l_scr[mi * tq:(mi + 1) * tq] for mi in range(4)]
    o = jnp.where(first, on[0] - lam * on[1], on[2] - lam * on[3])
    sq = o * o
    ms0 = jnp.sum(jnp.where(first, sq, 0.0), axis=1, keepdims=True) * (1.0 / DIFF_V)
    ms1 = jnp.sum(jnp.where(first, 0.0, sq), axis=1, keepdims=True) * (1.0 / DIFF_V)
    r = jnp.where(first, lax.rsqrt(ms0 + EPS), lax.rsqrt(ms1 + EPS))
    o_ref[0] = ((o * r * sub_ref[...]) * (1.0 - lam_init)).astype(o_ref.dtype)


def _diff_attention(q, k, v, wp, *, tq, tk, q_off, lam_init):
    b, sq, _ = q.shape
    sk = k.shape[1]
    grid, in_specs, out_spec = _attn_specs(b, DIFF_HEADS // 2, sq, sk, tq, LANE, LANE, LANE)
    kern = functools.partial(_diff_kernel, tq=tq, tk=tk, q_off=q_off, nk=sk // tk, lam_init=lam_init)
    small = lambda w: pl.BlockSpec((1, w), lambda bi, p, qi: (0, 0))
    stat = pltpu.VMEM((4 * tq, LANE), F32)
    return pl.pallas_call(
        kern,
        grid=grid,
        in_specs=[pl.BlockSpec(memory_space=pltpu.SMEM),
                  small(DIFF_DH), small(DIFF_DH), small(DIFF_DH), small(DIFF_DH), small(LANE)] + in_specs,
        out_specs=out_spec,
        out_shape=jax.ShapeDtypeStruct((b, sq, DIFF_HEADS * DIFF_V), BF16),
        scratch_shapes=[pltpu.VMEM((4 * tq, LANE), BF16), stat, stat, stat, stat,
                        pltpu.VMEM((4 * tq, tk), BF16)],
        compiler_params=_cparams(("parallel", "parallel", "parallel")),
        name="diff_attn",
    )(wp["slopes"], wp["diff_lq1"], wp["diff_lk1"], wp["diff_lq2"], wp["diff_lk2"], wp["diff_subln2"],
      q, k, v)


def _mem_attn_kernel(q_ref, k_ref, v_ref, o_ref):
    for hh in range(MEM_HEADS):
        sl = slice(hh * MEM_DH, (hh + 1) * MEM_DH)
        s = _dot_nt(q_ref[0, :, sl], k_ref[0, :, sl])
        p = jnp.exp2(s - jnp.max(s, axis=1, keepdims=True))
        o = _dot(p.astype(BF16), v_ref[0, :, sl]) / jnp.sum(p, axis=1, keepdims=True)
        o_ref[0, :, sl] = o.astype(o_ref.dtype)


def _mem_attention(q, k, v, *, tq):
    b, sq, w = q.shape
    nm = k.shape[1]
    return pl.pallas_call(
        _mem_attn_kernel,
        grid=(b, sq // tq),
        in_specs=[pl.BlockSpec((1, tq, w), lambda bi, qi: (bi, qi, 0)),
                  pl.BlockSpec((1, nm, w), lambda bi, qi: (bi, 0, 0)),
                  pl.BlockSpec((1, nm, w), lambda bi, qi: (bi, 0, 0))],
        out_specs=pl.BlockSpec((1, tq, w), lambda bi, qi: (bi, qi, 0)),
        out_shape=jax.ShapeDtypeStruct((b, sq, w), BF16),
        compiler_params=_cparams(("parallel", "parallel")),
        name="mem_attn",
    )(q, k, v)


def _mem_kv_kernel(x_ref, g_ref, w_ref, k_ref, v_ref, kb_ref, vb_ref):
    h = _rms(x_ref[...], g_ref[...]).astype(BF16)
    kv = _dot(h, w_ref[...])
    w = MEM_HEADS * MEM_DH
    k_ref[...] = kv[:, :w]
    v_ref[...] = kv[:, w:]
    kb_ref[...] = kv[:, :w].astype(BF16)
    vb_ref[...] = kv[:, w:].astype(BF16)


def _mem_kv(mem, wp, *, tm):
    n, d = mem.shape
    w = MEM_HEADS * MEM_DH
    row = lambda c: pl.BlockSpec((tm, c), lambda i: (i, 0))
    return pl.pallas_call(
        _mem_kv_kernel,
        grid=(n // tm,),
        in_specs=[row(d), _full_spec((1, d)), _full_spec(wp["w_mem_kv"].shape)],
        out_specs=(row(w), row(w), row(w), row(w)),
        out_shape=(jax.ShapeDtypeStruct((n, w), F32), jax.ShapeDtypeStruct((n, w), F32),
                   jax.ShapeDtypeStruct((n, w), BF16), jax.ShapeDtypeStruct((n, w), BF16)),
        compiler_params=_cparams(("parallel",)),
        name="mem_kv",
    )(mem, wp["norm_mem"], wp["w_mem_kv"])


def _merge_kernel(x_ref, oa_ref, ob_ref, om_ref, g_ref, wg_ref, bg_ref, wa_ref, wb_ref, wm_ref, wo_ref,
                  gf_ref, x1_ref, hn_ref):
    x = x_ref[...]
    d = x.shape[1]
    h = _rms(x, g_ref[...]).astype(BF16)
    gates = jax.nn.sigmoid(_dot(h, wg_ref[...]) + bg_ref[...])
    merged = (gates[:, :d] * _dot(oa_ref[...], wa_ref[...])
              + gates[:, d:2 * d] * _dot(ob_ref[...], wb_ref[...])
              + gates[:, 2 * d:] * _dot(om_ref[...], wm_ref[...]))
    x1 = x + _dot(merged.astype(BF16), wo_ref[...])
    x1_ref[...] = x1
    hn_ref[...] = _rms(x1, gf_ref[...]).astype(BF16)


def _merge(x, oa, ob, om, wp, *, tm):
    n, d = x.shape
    row = lambda c: pl.BlockSpec((tm, c), lambda i: (i, 0))
    names = ("norm_mix", "w_gate", "b_gate", "w_br_a", "w_br_b", "w_br_m", "w_o", "norm_ffn")
    return pl.pallas_call(
        _merge_kernel,
        grid=(n // tm,),
        in_specs=[row(d), row(oa.shape[1]), row(ob.shape[1]), row(om.shape[1])]
                 + [_full_spec(wp[k].shape) for k in names],
        out_specs=(row(d), row(d)),
        out_shape=(jax.ShapeDtypeStruct((n, d), F32), jax.ShapeDtypeStruct((n, d), BF16)),
        compiler_params=_cparams(("parallel",)),
        name="merge",
    )(x, oa, ob, om, *[wp[k] for k in names])


def _top_values(sc, k, with_rank=False):
    vals = []
    cur = sc
    rank = jnp.full(sc.shape, float(k), F32) if with_rank else None
    for r in range(k):
        m = jnp.max(cur, axis=0, keepdims=True)
        vals.append(m)
        hit = cur == m
        if with_rank:
            rank = jnp.where(hit, float(r), rank)
        if r + 1 < k:
            cur = jnp.where(hit, NEG, cur)
    return (vals, rank) if with_rank else vals


def _stack_rows(rows, sub):
    out = jnp.zeros(sub.shape, F32)
    for r, row in enumerate(rows):
        if row is not None:
            out = jnp.where(sub == r, row, out)
    return out


def _packed_candidates(t1, t2, t2h, sub8):
    assert PEER_TOPK == 16
    down = lambda x, k: pltpu.roll(x, k, 0)
    t1_8_13 = _stack_rows([None, None] + t1[8:14], sub8)
    t1_14_15 = _stack_rows(t1[14:16], sub8)
    top2 = t2[0:1]
    slabs = [
        t1[0] + t2h,
        t1[0] + t2[8:16],
        t1[1] + t2h,
        jnp.where(sub8 < 5, t1[2] + t2h, t1[4] + down(t2h, 5)),
        jnp.where(sub8 < 4, t1[3] + t2h,
                  jnp.where(sub8 < 6, t1[5] + down(t2h, 4), t1[6] + down(t2h, 6))),
        jnp.where(sub8 < 2, t1[7] + t2h, t1_8_13 + top2),
        jnp.where(sub8 < 2, t1_14_15 + top2, NEG),
    ]
    return jnp.concatenate(slabs, axis=0)


def _route_kernel(hn_ref, wq_ref, sk_ref, cnt_ref, e1_ref, rank_ref, e2_ref):
    tt = hn_ref.shape[0]
    q = _dot(hn_ref[...], wq_ref[...]).astype(BF16)
    sub = lax.broadcasted_iota(jnp.int32, (PEER_TOPK, tt), 0)
    sub8 = lax.broadcasted_iota(jnp.int32, (8, tt), 0)
    for h in range(PEER_HEADS):
        sc1 = _dot_nt(sk_ref[2 * h], q[:, (2 * h) * PEER_HALF:(2 * h + 1) * PEER_HALF])
        sc2 = _dot_nt(sk_ref[2 * h + 1], q[:, (2 * h + 1) * PEER_HALF:(2 * h + 2) * PEER_HALF])
        t1 = _top_values(sc1, PEER_TOPK)
        t2_rows, rank2 = _top_values(sc2, PEER_TOPK, with_rank=True)
        t2 = _stack_rows(t2_rows, sub)
        t2h = t2[:8]
        cands = []
        for a in range(PEER_TOPK):
            nb = PEER_TOPK // (a + 1)
            c = t1[a] + (t2 if nb > 8 else t2h)
            if nb < c.shape[0]:
                c = jnp.where((sub if nb > 8 else sub8) < nb, c, NEG)
            cands.append(c)
        cand = _packed_candidates(t1, t2, t2h, sub8)
        tau = _top_values(cand, PEER_TOPK)[-1]
        top = t1[0] + t2[0:1]
        z = jnp.sum(jnp.where(cand >= tau, jnp.exp(cand - top), 0.0), axis=0, keepdims=True)
        cnt = jnp.zeros(sc1.shape, F32)
        for a in range(PEER_TOPK):
            ca = jnp.sum(jnp.where(cands[a] >= tau, 1.0, 0.0), axis=0, keepdims=True)
            cnt = jnp.where(sc1 == t1[a], ca, cnt)
        cnt_ref[h] = cnt
        e1_ref[h] = jnp.exp(sc1 - t1[0]) * (0.5 / z)
        rank_ref[h] = rank2.astype(BF16)
        e2_ref[h] = jnp.exp(sc2 - t2[0:1]).astype(BF16)


def _route(hn, wp, *, tt):
    n, d = hn.shape
    big = pl.BlockSpec((PEER_HEADS, PEER_NKEYS, tt), lambda i: (0, 0, i))
    big_shape = jax.ShapeDtypeStruct((PEER_HEADS, PEER_NKEYS, n), F32)
    big_half = jax.ShapeDtypeStruct((PEER_HEADS, PEER_NKEYS, n), BF16)
    return pl.pallas_call(
        _route_kernel,
        grid=(n // tt,),
        in_specs=[pl.BlockSpec((tt, d), lambda i: (i, 0)), _full_spec(wp["peer_wq"].shape),
                  _full_spec(wp["peer_subkeys"].shape)],
        out_specs=(big, big, big, big),
        out_shape=(big_shape, big_shape, big_half, big_half),
        compiler_params=_cparams(("parallel",)),
        name="peer_route",
    )(hn, wp["peer_wq"], wp["peer_subkeys"])


PAIR = 2 * PEER_NKEYS
SUBROWS = 128


def _peer_kernel(hn_in, u_ref, vt_ref, cnt_ref, e1_ref, rank_in, e2_in, x1_ref, gn_ref, y_ref,
                 acc_scr, hw_scr, rank_ref, e2_ref, hn_ref, *, te, ne):
    j = pl.program_id(1)
    tt = hn_in.shape[0]
    npair = te // PAIR

    @pl.when(j == 0)
    def _():
        rank_ref[...] = rank_in[...]
        e2_ref[...] = e2_in[...]
        hn_ref[...] = hn_in[...]

    def pre_act(p):
        a = _dot_nt(u_ref[p * PAIR:(p + 1) * PAIR, :], hn_ref[...])
        return a * (1.0 + lax.erf(a * math.sqrt(0.5)))

    def down(lo, hi):
        return _dot(vt_ref[:, lo * PAIR:hi * PAIR], hw_scr[lo * PAIR:hi * PAIR, :])

    cuts = sorted({0, 3 * npair // 8, 3 * npair // 4, npair - 1, npair})
    piece_after = {hi: lo for lo, hi in zip(cuts[:-1], cuts[1:])}

    groups = [(tc, sb) for tc in range(tt // LANE) for sb in range(PEER_NKEYS // SUBROWS)]
    zero = jnp.zeros((), BF16)
    act = pre_act(0)
    tot = None
    carry = None
    for p in range(npair):
        nxt = None
        for gi, (tc, sb) in enumerate(groups):
            if gi == 1 and p + 1 < npair:
                nxt = pre_act(p + 1)
            if gi == len(groups) // 2 and p in piece_after:
                part = down(piece_after[p], p)
                tot = part if tot is None else tot + part
            ls = slice(tc * LANE, (tc + 1) * LANE)
            i2 = slice(sb * SUBROWS, (sb + 1) * SUBROWS)
            w = [None, None] if carry is None else [carry * zero, carry * zero]
            for h in range(PEER_HEADS):
                rk = rank_ref[h, i2, ls]
                e2b = e2_ref[h, i2, ls]
                for rr in range(2):
                    i1 = 2 * p + rr
                    cnt = jnp.broadcast_to(cnt_ref[h, i1:i1 + 1, ls], (SUBROWS, LANE)).astype(BF16)
                    e1b = jnp.broadcast_to(e1_ref[h, i1:i1 + 1, ls], (SUBROWS, LANE)).astype(BF16)
                    cw = jnp.where(rk < cnt, e2b, zero) * e1b
                    w[rr] = cw if w[rr] is None else w[rr] + cw
            carry = w[1]
            for rr in range(2):
                r0 = rr * PEER_NKEYS + sb * SUBROWS
                hw_scr[p * PAIR + r0:p * PAIR + r0 + SUBROWS, ls] = w[rr] * act[r0:r0 + SUBROWS, ls].astype(BF16)
        act = nxt
    part = down(piece_after[npair], npair)
    tot = part if tot is None else tot + part

    @pl.when(j == 0)
    def _():
        acc_scr[...] = tot

    @pl.when(j > 0)
    def _():
        acc_scr[...] += tot

    @pl.when(j == ne - 1)
    def _():
        xr = x1_ref[...] + acc_scr[...].T
        y_ref[...] = _rms(xr, gn_ref[...])


def _peer(hn, cnt, e1, rank2, e2, x1, wp, *, tt, te):
    n, d = hn.shape
    ne = wp["peer_u"].shape[0] // te
    big = pl.BlockSpec((PEER_HEADS, PEER_NKEYS, tt), lambda i, j: (0, 0, i))
    rows = pl.BlockSpec((PEER_HEADS, te // PEER_NKEYS, tt), lambda i, j: (0, j, i))
    kern = functools.partial(_peer_kernel, te=te, ne=ne)
    return pl.pallas_call(
        kern,
        grid=(n // tt, ne),
        in_specs=[pl.BlockSpec((tt, d), lambda i, j: (i, 0)),
                  pl.BlockSpec((te, d), lambda i, j: (j, 0)),
                  pl.BlockSpec((d, te), lambda i, j: (0, j)),
                  rows, rows, big, big,
                  pl.BlockSpec((tt, d), lambda i, j: (i, 0)),
                  pl.BlockSpec((1, d), lambda i, j: (0, 0))],
        out_specs=pl.BlockSpec((tt, d), lambda i, j: (i, 0)),
        out_shape=jax.ShapeDtypeStruct((n, d), F32),
        scratch_shapes=[pltpu.VMEM((d, tt), F32), pltpu.VMEM((te, tt), BF16),
                        pltpu.VMEM((PEER_HEADS, PEER_NKEYS, tt), BF16),
                        pltpu.VMEM((PEER_HEADS, PEER_NKEYS, tt), BF16),
                        pltpu.VMEM((tt, d), BF16)],
        compiler_params=_cparams(("parallel", "arbitrary")),
        name="peer_experts",
    )(hn, wp["peer_u"], wp["peer_vt"], cnt, e1, rank2, e2, x1, wp["norm_final"])


def _prep_weights(l, p):
    f = lambda a: a.astype(F32)
    w_in = f(p["w_in"][l])
    o_cq, o_ckv, o_kr = 0, MLA_Q_LORA, MLA_Q_LORA + MLA_KV_LORA
    o_dq = o_kr + MLA_ROPE
    half = MLA_ROPE // 2
    d = w_in.shape[0]
    kr = w_in[:, o_kr:o_dq]
    pad = jnp.zeros((d, LANE - MLA_ROPE), F32)
    w_in2 = jnp.concatenate([
        w_in[:, o_cq:o_kr], kr, pad,
        -kr[:, half:], kr[:, :half], pad,
        w_in[:, o_dq:]], axis=1)
    assert w_in2.shape[1] == _C_END

    w_uq = f(p["w_uq"][l]).reshape(MLA_Q_LORA, MLA_HEADS, MLA_NOPE + MLA_ROPE)
    nope, x1, x2 = w_uq[..., :MLA_NOPE], w_uq[..., MLA_NOPE:MLA_NOPE + half], w_uq[..., MLA_NOPE + half:]
    zpad = jnp.zeros((MLA_Q_LORA, MLA_HEADS, HEAD_W - MLA_NOPE - MLA_ROPE), F32)
    q_slab = jnp.concatenate([nope, x1, x2, zpad], axis=-1).reshape(MLA_Q_LORA, -1)
    r_slab = jnp.concatenate([jnp.zeros_like(nope), -x2, x1, zpad], axis=-1).reshape(MLA_Q_LORA, -1)
    w_uq2 = jnp.concatenate([q_slab, r_slab], axis=1)

    w_uk = f(p["w_uk"][l]).reshape(MLA_KV_LORA, MLA_HEADS, MLA_NOPE)
    w_uk2 = jnp.concatenate([w_uk, jnp.zeros((MLA_KV_LORA, MLA_HEADS, HEAD_W - MLA_NOPE), F32)],
                            axis=-1).reshape(MLA_KV_LORA, -1)
    r_idx = jnp.arange(MLA_ROPE)
    cols = jnp.arange(MLA_HEADS * HEAD_W)
    p_kr = ((cols[None, :] % HEAD_W) == (MLA_NOPE + r_idx[:, None])).astype(F32)

    slopes = jnp.exp2(-8.0 * jnp.arange(1, DIFF_HEADS + 1, dtype=F32) / DIFF_HEADS)
    sub = f(p["diff_subln"][l])
    row = lambda a: f(a).reshape(1, -1)
    return {
        "norm_mix": row(p["norm_mix"][l]), "w_in": w_in2.astype(BF16),
        "mla_q_norm": row(p["mla_q_norm"][l]), "w_uq": w_uq2.astype(BF16),
        "mla_kv_norm": row(p["mla_kv_norm"][l]),
        "w_uk": w_uk2.astype(BF16), "p_kr": p_kr.astype(BF16), "w_uv": p["w_uv"][l].astype(BF16),
        "slopes": slopes,
        "diff_lq1": row(p["diff_lq1"][l]), "diff_lk1": row(p["diff_lk1"][l]),
        "diff_lq2": row(p["diff_lq2"][l]), "diff_lk2": row(p["diff_lk2"][l]),
        "diff_subln2": jnp.concatenate([sub, sub]).reshape(1, -1),
        "norm_mem": row(p["norm_mem"][l]), "w_mem_kv": p["w_mem_kv"][l].astype(BF16),
        "w_gate": p["w_gate"][l].astype(BF16), "b_gate": row(p["b_gate"][l]),
        "w_br_a": p["w_br_a"][l].astype(BF16), "w_br_b": p["w_br_b"][l].astype(BF16),
        "w_br_m": p["w_br_m"][l].astype(BF16), "w_o": p["w_o"][l].astype(BF16),
        "norm_ffn": row(p["norm_ffn"][l]),
        "peer_wq": p["peer_wq"][l].astype(BF16),
        "peer_subkeys": p["peer_subkeys"][l].reshape(PEER_HEADS * 2, PEER_NKEYS, PEER_HALF).astype(BF16),
        "peer_u": p["peer_u"][l].astype(BF16), "peer_vt": p["peer_v"][l].T.astype(BF16),
        "norm_final": row(p["norm_final"]),
    }


def _rope_tables(pos):
    half = MLA_ROPE // 2
    inv = (1.0 / (ROPE_THETA ** (np.arange(half, dtype=np.float32) / half))).astype(np.float32)
    ang = pos.astype(np.float32)[:, None] * inv[None, :]
    cos, sin = np.cos(ang), np.sin(ang)
    n = pos.shape[0]
    one = np.ones((n, MLA_NOPE), np.float32)
    zq = np.zeros((n, HEAD_W - MLA_NOPE - MLA_ROPE), np.float32)
    scale = np.float32(MLA_SCALE)
    cq = np.concatenate([one, cos, cos, zq], axis=1) * scale
    sq = np.concatenate([0.0 * one, sin, sin, zq], axis=1) * scale
    zk = np.zeros((n, LANE - MLA_ROPE), np.float32)
    ck = np.concatenate([cos, cos, zk], axis=1)
    sk = np.concatenate([sin, sin, zk], axis=1)
    return cq, sq, ck, sk


def _layer(x, pos, past, mem_k, mem_v, wp, layer, cfg):
    b, s, d = x.shape
    n = b * s
    xf = x.reshape(n, d)
    tabs = _rope_tables(pos)
    if s % cfg["tm"] == 0:
        pos_blocks = s // cfg["tm"]
    else:
        tabs = tuple(np.tile(t, (b, 1)) for t in tabs)
        pos_blocks = n // cfg["tm"]
    ckv, kr, dk, dv, qm, dqb, dkb, dvb, mqb = _proj_in(xf, tabs, wp, tm=cfg["tm"], pos_blocks=pos_blocks)
    new_rows = (ckv.reshape(b, s, -1), kr.reshape(b, s, -1), dk.reshape(b, s, DIFF_HEADS, DIFF_V),
                dv.reshape(b, s, DIFF_HEADS, DIFF_V))
    r3 = lambda a: a.reshape(b, s, -1)
    if past is None:
        ckv_all, kr_all = ckv, kr
        dk_all, dv_all = r3(dkb), r3(dvb)
        kk = s
        q_off = 0
    else:
        p_len = past[0].shape[1]
        kk = p_len + s
        q_off = p_len
        ckv_all = jnp.concatenate([past[0], r3(ckv)], axis=1).reshape(b * kk, -1)
        kr_all = jnp.concatenate([past[1], r3(kr)], axis=1).reshape(b * kk, -1)
        dk_all = jnp.concatenate([past[2].reshape(b, p_len, -1).astype(BF16), r3(dkb)], axis=1)
        dv_all = jnp.concatenate([past[3].reshape(b, p_len, -1).astype(BF16), r3(dvb)], axis=1)
    k_mla, v_mla = _kv_up(ckv_all, kr_all, wp, tm=cfg["tm_kv"])
    att = dict(tk=cfg["tk"] if past is None else kk, q_off=q_off)
    o_a = _mla_attention(r3(qm), k_mla.reshape(b, kk, -1), v_mla.reshape(b, kk, -1), tq=cfg["tq_mla"], **att)
    lam_init = 0.8 - 0.6 * math.exp(-0.3 * layer)
    o_b = _diff_attention(r3(dqb), dk_all, dv_all, wp, lam_init=lam_init, tq=cfg["tq"], **att)
    o_m = _mem_attention(r3(mqb), mem_k, mem_v, tq=cfg["tq"])
    x1, hn = _merge(xf, o_a.reshape(n, -1), o_b.reshape(n, -1), o_m.reshape(n, -1), wp, tm=cfg["tm"])
    return x1, hn, new_rows


def _peer_and_norm(x1, hn, wp, cfg):
    cnt, e1, rank2, e2 = _route(hn, wp, tt=cfg["tt_route"])
    return _peer(hn, cnt, e1, rank2, e2, x1, wp, tt=cfg["tt"], te=cfg["te"])


_CFG_PROMPT = dict(tm=512, tm_kv=512, tq=512, tq_mla=1024, tk=512, tt_route=256, tt=512, te=2048)
_CFG_SAMPLE = dict(tm=256, tm_kv=256, tq=32, tq_mla=32, tk=None, tt_route=256, tt=256, te=2048)


def kernel(x_prompt, x_sample, cache_mla_ckv, cache_mla_krope, cache_diff_k, cache_diff_v, cache_mem_k, cache_mem_v, mem_prompt, norm_mix, w_in, mla_q_norm, w_uq, mla_kv_norm, w_uk, w_uv, diff_lq1, diff_lk1, diff_lq2, diff_lk2, diff_subln, norm_mem, w_mem_kv, w_br_a, w_br_b, w_br_m, w_gate, b_gate, w_o, norm_ffn, peer_wq, peer_subkeys, peer_u, peer_v, norm_final):
    params = dict(norm_mix=norm_mix, w_in=w_in, mla_q_norm=mla_q_norm, w_uq=w_uq, mla_kv_norm=mla_kv_norm,
                  w_uk=w_uk, w_uv=w_uv, diff_lq1=diff_lq1, diff_lk1=diff_lk1, diff_lq2=diff_lq2,
                  diff_lk2=diff_lk2, diff_subln=diff_subln, norm_mem=norm_mem, w_mem_kv=w_mem_kv,
                  w_br_a=w_br_a, w_br_b=w_br_b, w_br_m=w_br_m, w_gate=w_gate, b_gate=b_gate, w_o=w_o,
                  norm_ffn=norm_ffn, peer_wq=peer_wq, peer_subkeys=peer_subkeys, peer_u=peer_u,
                  peer_v=peer_v, norm_final=norm_final)
    depth = w_in.shape[0]
    assert depth == 1, "the final norm is fused into the last PEER step of a single layer"
    bp, sp, d = x_prompt.shape
    bs, ss, _ = x_sample.shape
    n_mem = mem_prompt.shape[1]
    past_len = cache_mla_ckv.shape[2]
    pos_p = np.arange(sp, dtype=np.int32)
    pos_s = past_len + np.arange(ss, dtype=np.int32)

    l = 0
    wp = _prep_weights(l, params)
    mk, mv, mkb, mvb = _mem_kv(mem_prompt.reshape(bp * n_mem, d), wp, tm=n_mem)
    m3 = lambda a, b: a.reshape(b, n_mem, -1)
    x1p, hnp, rows_p = _layer(x_prompt, pos_p, None, m3(mkb, bp), m3(mvb, bp), wp, l, _CFG_PROMPT)
    past = (cache_mla_ckv[l], cache_mla_krope[l], cache_diff_k[l], cache_diff_v[l])
    x1s, hns, rows_s = _layer(x_sample, pos_s, past, m3(cache_mem_k[l].astype(BF16), bs),
                              m3(cache_mem_v[l].astype(BF16), bs), wp, l, _CFG_SAMPLE)
    y_prompt = _peer_and_norm(x1p, hnp, wp, _CFG_PROMPT).reshape(bp, sp, d)
    y_sample = _peer_and_norm(x1s, hns, wp, _CFG_SAMPLE).reshape(bs, ss, d)
    st = lambda a: a[None]
    mem4 = lambda a: a.reshape(bp, n_mem, MEM_HEADS, MEM_DH)[None]
    return (y_prompt, y_sample,
            st(rows_p[0]), st(rows_p[1]), st(rows_p[2]), st(rows_p[3]),
            mem4(mk), mem4(mv),
            st(rows_s[0]), st(rows_s[1]), st(rows_s[2]), st(rows_s[3]))
```

```python
import functools
import math

import jax
import jax.numpy as jnp
import numpy as np
from jax import lax
from jax.experimental import pallas as pl
from jax.experimental.pallas import tpu as pltpu

F32 = jnp.float32
BF16 = jnp.bfloat16

CHUNK = 64
CHUNK_SHIFT = 6
EPS = 1e-6
NEG = -1e30
MLA_HEADS = 8
MLA_Q_LORA = 384
MLA_KV_LORA = 256
MLA_NOPE = 64
MLA_ROPE = 32
MLA_V = 64
ROPE_THETA = 10000.0
LOG2E = math.log2(math.e)
MLA_SCALE = (MLA_NOPE + MLA_ROPE) ** -0.5 * LOG2E
DIFF_HEADS = 8
DIFF_DH = 32
DIFF_V = 2 * DIFF_DH
DIFF_SCALE = DIFF_DH ** -0.5 * LOG2E
MEM_HEADS = 4
MEM_DH = 128
MEM_SCALE = MEM_DH ** -0.5 * LOG2E
PEER_HEADS = 8
PEER_NKEYS = 128
PEER_HALF = 128
PEER_TOPK = 16
LANE = 128
HEAD_W = 128

VMEM_LIMIT = 56 * 1024 * 1024

_C_CQ = 0
_C_CKV = _C_CQ + MLA_Q_LORA
_C_KR = _C_CKV + MLA_KV_LORA
_C_KRR = _C_KR + LANE
_C_DQ = _C_KRR + LANE
_C_DK = _C_DQ + DIFF_HEADS * DIFF_V
_C_DV = _C_DK + DIFF_HEADS * DIFF_V
_C_MQ = _C_DV + DIFF_HEADS * DIFF_V
_C_END = _C_MQ + MEM_HEADS * MEM_DH


def _cparams(sem):
    return pltpu.CompilerParams(dimension_semantics=sem, vmem_limit_bytes=VMEM_LIMIT)


def _rms(x, g):
    return x * lax.rsqrt(jnp.mean(x * x, axis=-1, keepdims=True) + EPS) * g


def _dot(a, b):
    return jnp.dot(a, b, preferred_element_type=F32)


def _dot_nt(a, b):
    return lax.dot_general(a, b, (((1,), (1,)), ((), ())), preferred_element_type=F32)


def _full_spec(shape):
    nd = len(shape)
    return pl.BlockSpec(shape, lambda *_: (0,) * nd)


def _proj_in_kernel(x_ref, g_ref, win_ref, qn_ref, wuq_ref, kvn_ref, cq_ref, sq_ref, ck_ref, sk_ref,
                    ckv_ref, kr_ref, dk_ref, dv_ref, qm_ref, dqb_ref, dkb_ref, dvb_ref, mqb_ref):
    h = _rms(x_ref[...], g_ref[...]).astype(BF16)
    z = _dot(h, win_ref[...])
    cqn = _rms(z[:, _C_CQ:_C_CKV], qn_ref[...]).astype(BF16)
    q2 = _dot(cqn, wuq_ref[...])
    cq = cq_ref[...]
    sq = sq_ref[...]
    nq = MLA_HEADS * HEAD_W
    for hh in range(MLA_HEADS):
        lo = hh * HEAD_W
        qm_ref[:, lo:lo + HEAD_W] = (q2[:, lo:lo + HEAD_W] * cq
                                     + q2[:, nq + lo:nq + lo + HEAD_W] * sq).astype(BF16)
    ckv_ref[...] = _rms(z[:, _C_CKV:_C_KR], kvn_ref[...])
    kr = z[:, _C_KR:_C_KRR] * ck_ref[...] + z[:, _C_KRR:_C_DQ] * sk_ref[...]
    kr_ref[...] = kr[:, :MLA_ROPE]
    dk = z[:, _C_DK:_C_DV]
    dv = z[:, _C_DV:_C_MQ]
    dk_ref[...] = dk.reshape(dk_ref.shape)
    dv_ref[...] = dv.reshape(dv_ref.shape)
    dkb_ref[...] = dk.astype(BF16)
    dvb_ref[...] = dv.astype(BF16)
    dqb_ref[...] = (z[:, _C_DQ:_C_DK] * DIFF_SCALE).astype(BF16)
    mqb_ref[...] = (z[:, _C_MQ:_C_END] * MEM_SCALE).astype(BF16)


def _proj_in(x, tabs, wp, *, tm, pos_blocks):
    n, d = x.shape
    grid = (n // tm,)
    row = lambda w: pl.BlockSpec((tm, w), lambda i: (i, 0))
    tab = pl.BlockSpec((tm, LANE), lambda i: (i % pos_blocks, 0))
    heads = pl.BlockSpec((tm, DIFF_HEADS, DIFF_V), lambda i: (i, 0, 0))
    dw = DIFF_HEADS * DIFF_V
    out_shape = (
        jax.ShapeDtypeStruct((n, MLA_KV_LORA), F32),
        jax.ShapeDtypeStruct((n, MLA_ROPE), F32),
        jax.ShapeDtypeStruct((n, DIFF_HEADS, DIFF_V), F32),
        jax.ShapeDtypeStruct((n, DIFF_HEADS, DIFF_V), F32),
        jax.ShapeDtypeStruct((n, MLA_HEADS * HEAD_W), BF16),
        jax.ShapeDtypeStruct((n, dw), BF16),
        jax.ShapeDtypeStruct((n, dw), BF16),
        jax.ShapeDtypeStruct((n, dw), BF16),
        jax.ShapeDtypeStruct((n, MEM_HEADS * MEM_DH), BF16),
    )
    return pl.pallas_call(
        _proj_in_kernel,
        grid=grid,
        in_specs=[row(d), _full_spec((1, d)), _full_spec(wp["w_in"].shape), _full_spec((1, MLA_Q_LORA)),
                  _full_spec(wp["w_uq"].shape), _full_spec((1, MLA_KV_LORA)), tab, tab, tab, tab],
        out_specs=(row(MLA_KV_LORA), row(MLA_ROPE), heads, heads, row(MLA_HEADS * HEAD_W),
                   row(dw), row(dw), row(dw), row(MEM_HEADS * MEM_DH)),
        out_shape=out_shape,
        compiler_params=_cparams(("parallel",)),
        name="proj_in",
    )(x, wp["norm_mix"], wp["w_in"], wp["mla_q_norm"], wp["w_uq"], wp["mla_kv_norm"], *tabs)


def _kv_up_kernel(ckv_ref, kr_ref, wuk_ref, pk_ref, wuv_ref, k_ref, v_ref):
    c = ckv_ref[...].astype(BF16)
    k = _dot(c, wuk_ref[...]) + _dot(kr_ref[...].astype(BF16), pk_ref[...])
    k_ref[...] = k.astype(BF16)
    v_ref[...] = _dot(c, wuv_ref[...]).astype(BF16)


def _kv_up(ckv, kr, wp, *, tm):
    n = ckv.shape[0]
    row = lambda w: pl.BlockSpec((tm, w), lambda i: (i, 0))
    return pl.pallas_call(
        _kv_up_kernel,
        grid=(n // tm,),
        in_specs=[row(MLA_KV_LORA), row(MLA_ROPE), _full_spec(wp["w_uk"].shape),
                  _full_spec(wp["p_kr"].shape), _full_spec(wp["w_uv"].shape)],
        out_specs=(row(MLA_HEADS * HEAD_W), row(MLA_HEADS * MLA_V)),
        out_shape=(jax.ShapeDtypeStruct((n, MLA_HEADS * HEAD_W), BF16),
                   jax.ShapeDtypeStruct((n, MLA_HEADS * MLA_V), BF16)),
        compiler_params=_cparams(("parallel",)),
        name="kv_up",
    )(ckv, kr, wp["w_uk"], wp["p_kr"], wp["w_uv"])


def _lanes(col, n):
    if n == LANE:
        return col
    if n % LANE == 0:
        return jnp.concatenate([col] * (n // LANE), axis=1)
    return jnp.broadcast_to(col[:, :1], (col.shape[0], n))


GROUP_VREGS = 16


def _softmax_block(score_group, tq, tk, row0, m_scr, al_scr, p_scr, l_scr=None):
    rg = min(tq, max(16, (GROUP_VREGS * 8 * LANE // tk) // 16 * 16))
    for g in range(tq // rg):
        r = slice(row0 + g * rg, row0 + (g + 1) * rg)
        m_prev = m_scr[r]
        m_next = jnp.maximum(m_prev, jnp.max(score_group(g, rg), axis=1, keepdims=True))
        al_scr[r] = jnp.exp2(m_prev - m_next)
        m_scr[r] = m_next
    for g in range(tq // rg):
        r = slice(row0 + g * rg, row0 + (g + 1) * rg)
        e = jnp.exp2(score_group(g, rg) - _lanes(m_scr[r], tk))
        p_scr[r] = e.astype(BF16)
        if l_scr is not None:
            l_scr[r] = al_scr[r] * l_scr[r] + jnp.sum(e, axis=1, keepdims=True)


def _accumulate(rows, v, al_scr, l_scr, acc_scr, p_scr):
    pv = _dot(p_scr[rows], jnp.concatenate([v, jnp.ones(v.shape, v.dtype)], axis=1))
    w = v.shape[1]
    acc_scr[rows] = acc_scr[rows] * al_scr[rows] + pv[:, :w]
    l_scr[rows] = l_scr[rows] * al_scr[rows] + pv[:, w:]


def _chunk_visible(q_lo, k_lo, g, rg, tk):
    qp = q_lo + g * rg + lax.broadcasted_iota(jnp.int32, (rg, tk), 0)
    kp = k_lo + lax.broadcasted_iota(jnp.int32, (rg, tk), 1)
    return qp, kp, (kp >> CHUNK_SHIFT) <= (qp >> CHUNK_SHIFT)


def _init_stats(m_scr, l_scr, acc_scr):
    m_scr[...] = jnp.full(m_scr.shape, NEG, F32)
    l_scr[...] = jnp.zeros(l_scr.shape, F32)
    acc_scr[...] = jnp.zeros(acc_scr.shape, F32)


def _block_range(qi, *, tq, tk, q_off, nk):
    q_lo = q_off + qi * tq
    q_hi = q_lo + (tq - 1)
    n_behind = jnp.minimum((q_lo + 1) // tk, nk)
    last_key = ((q_hi >> CHUNK_SHIFT) << CHUNK_SHIFT) + (CHUNK - 1)
    n_need = jnp.minimum(last_key // tk + 1, nk)
    return q_lo, n_behind, n_need


def _for_blocks(lo, hi, fn):
    lax.fori_loop(lo, hi, lambda i, c: (fn(i), c)[1], 0)


def _attn_specs(b, pairs, sq, sk, tq, qw, kw, vw):
    grid = (b, pairs, sq // tq)
    in_specs = [pl.BlockSpec((1, tq, qw), lambda bi, p, qi: (bi, qi, p)),
                pl.BlockSpec((1, sk, kw), lambda bi, p, qi: (bi, 0, p)),
                pl.BlockSpec((1, sk, vw), lambda bi, p, qi: (bi, 0, p))]
    out_spec = pl.BlockSpec((1, tq, vw), lambda bi, p, qi: (bi, qi, p))
    return grid, in_specs, out_spec


def _mla_kernel(q_ref, k_ref, v_ref, o_ref, m_scr, l_scr, al_scr, acc_scr, p_scr, *, tq, tk, q_off, nk):
    qi = pl.program_id(2)
    _init_stats(m_scr, l_scr, acc_scr)
    q_lo, n_behind, n_need = _block_range(qi, tq=tq, tk=tk, q_off=q_off, nk=nk)

    def block(i, masked):
        k_lo = pl.multiple_of(i * tk, tk)
        for hh in range(2):
            hs = slice(hh * HEAD_W, (hh + 1) * HEAD_W)
            s = _dot_nt(q_ref[0, :, hs], k_ref[0, pl.ds(k_lo, tk), hs])

            def score_group(g, rg, s=s):
                sg = s[g * rg:(g + 1) * rg]
                if masked:
                    sg = jnp.where(_chunk_visible(q_lo, k_lo, g, rg, tk)[2], sg, NEG)
                return sg

            _softmax_block(score_group, tq, tk, hh * tq, m_scr, al_scr, p_scr)
        _accumulate(slice(0, 2 * tq), v_ref[0, pl.ds(k_lo, tk), :], al_scr, l_scr, acc_scr, p_scr)

    _for_blocks(0, n_behind, lambda i: block(i, False))
    _for_blocks(n_behind, n_need, lambda i: block(i, True))

    lane = lax.broadcasted_iota(jnp.int32, (tq, LANE), 1)
    o0 = acc_scr[0:tq] / l_scr[0:tq]
    o1 = acc_scr[tq:2 * tq] / l_scr[tq:2 * tq]
    o_ref[0] = jnp.where(lane < MLA_V, o0, o1).astype(o_ref.dtype)


def _mla_attention(q, k, v, *, tq, tk, q_off):
    b, sq, _ = q.shape
    sk = k.shape[1]
    grid, in_specs, out_spec = _attn_specs(b, MLA_HEADS // 2, sq, sk, tq, 2 * HEAD_W, 2 * HEAD_W, 2 * MLA_V)
    kern = functools.partial(_mla_kernel, tq=tq, tk=tk, q_off=q_off, nk=sk // tk)
    stat = pltpu.VMEM((2 * tq, LANE), F32)
    return pl.pallas_call(
        kern,
        grid=grid,
        in_specs=in_specs,
        out_specs=out_spec,
        out_shape=jax.ShapeDtypeStruct((b, sq, MLA_HEADS * MLA_V), BF16),
        scratch_shapes=[stat, stat, stat, stat, pltpu.VMEM((2 * tq, tk), BF16)],
        compiler_params=_cparams(("parallel", "parallel", "parallel")),
        name="mla_attn",
    )(q, k, v)


def _diff_kernel(slope_ref, lq1_ref, lk1_ref, lq2_ref, lk2_ref, sub_ref, q_ref, k_ref, v_ref, o_ref,
                 qs_scr, m_scr, l_scr, al_scr, acc_scr, p_scr, *, tq, tk, q_off, nk, lam_init):
    pr = pl.program_id(1)
    qi = pl.program_id(2)
    _init_stats(m_scr, l_scr, acc_scr)
    q_lo, n_behind, n_need = _block_range(qi, tq=tq, tk=tk, q_off=q_off, nk=nk)

    q = q_ref[0]
    lane = lax.broadcasted_iota(jnp.int32, (tq, LANE), 1)
    for mi in range(4):
        lo = mi * DIFF_DH
        qs_scr[mi * tq:(mi + 1) * tq] = jnp.where((lane >= lo) & (lane < lo + DIFF_DH), q, jnp.zeros_like(q))

    def block(i, masked):
        k_lo = pl.multiple_of(i * tk, tk)
        kb = k_ref[0, pl.ds(k_lo, tk), :]
        kpos = (k_lo + lax.broadcasted_iota(jnp.int32, (1, tk), 1)).astype(F32)

        def scores(mi):
            return _dot_nt(qs_scr[mi * tq:(mi + 1) * tq], kb)

        def softmax(mi, s_map):
            slope = slope_ref[2 * pr + mi // 2] * LOG2E

            def score_group(g, rg):
                sg = s_map[g * rg:(g + 1) * rg]
                if masked:
                    qp, kp, vis = _chunk_visible(q_lo, k_lo, g, rg, tk)
                    return jnp.where(vis, sg + slope * jnp.minimum(kp, 2 * qp - kp).astype(F32), NEG)
                return sg + slope * kpos

            _softmax_block(score_group, tq, tk, mi * tq, m_scr, al_scr, p_scr, l_scr)

        def values(mi):
            rows = slice(mi * tq, (mi + 1) * tq)
            acc_scr[rows] = acc_scr[rows] * al_scr[rows] + _dot(p_scr[rows], v_ref[0, pl.ds(k_lo, tk), :])

        s0 = scores(0)
        s1 = scores(1)
        softmax(0, s0)
        s2 = scores(2)
        values(0)
        softmax(1, s1)
        s3 = scores(3)
        values(1)
        softmax(2, s2)
        values(2)
        softmax(3, s3)
        values(3)

    _for_blocks(0, n_behind, lambda i: block(i, False))
    _for_blocks(n_behind, n_need, lambda i: block(i, True))

    lam = (jnp.exp(jnp.sum(lq1_ref[...] * lk1_ref[...], axis=1, keepdims=True))
           - jnp.exp(jnp.sum(lq2_ref[...] * lk2_ref[...], axis=1, keepdims=True)) + lam_init)
    first = lane < DIFF_V
    on = [acc_scr[mi * tq:(mi + 1) * tq] / l_scr[mi * tq:(mi + 1) * tq] for mi in range(4)]
    o = jnp.where(first, on[0] - lam * on[1], on[2] - lam * on[3])
    sq = o * o
    ms0 = jnp.sum(jnp.where(first, sq, 0.0), axis=1, keepdims=True) * (1.0 / DIFF_V)
    ms1 = jnp.sum(jnp.where(first, 0.0, sq), axis=1, keepdims=True) * (1.0 / DIFF_V)
    r = jnp.where(first, lax.rsqrt(ms0 + EPS), lax.rsqrt(ms1 + EPS))
    o_ref[0] = ((o * r * sub_ref[...]) * (1.0 - lam_init)).astype(o_ref.dtype)


def _diff_attention(q, k, v, wp, *, tq, tk, q_off, lam_init):
    b, sq, _ = q.shape
    sk = k.shape[1]
    grid, in_specs, out_spec = _attn_specs(b, DIFF_HEADS // 2, sq, sk, tq, LANE, LANE, LANE)
    kern = functools.partial(_diff_kernel, tq=tq, tk=tk, q_off=q_off, nk=sk // tk, lam_init=lam_init)
    small = lambda w: pl.BlockSpec((1, w), lambda bi, p, qi: (0, 0))
    stat = pltpu.VMEM((4 * tq, LANE), F32)
    return pl.pallas_call(
        kern,
        grid=grid,
        in_specs=[pl.BlockSpec(memory_space=pltpu.SMEM),
                  small(DIFF_DH), small(DIFF_DH), small(DIFF_DH), small(DIFF_DH), small(LANE)] + in_specs,
        out_specs=out_spec,
        out_shape=jax.ShapeDtypeStruct((b, sq, DIFF_HEADS * DIFF_V), BF16),
        scratch_shapes=[pltpu.VMEM((4 * tq, LANE), BF16), stat, stat, stat, stat,
                        pltpu.VMEM((4 * tq, tk), BF16)],
        compiler_params=_cparams(("parallel", "parallel", "parallel")),
        name="diff_attn",
    )(wp["slopes"], wp["diff_lq1"], wp["diff_lk1"], wp["diff_lq2"], wp["diff_lk2"], wp["diff_subln2"],
      q, k, v)


def _mem_attn_kernel(q_ref, k_ref, v_ref, o_ref):
    for hh in range(MEM_HEADS):
        sl = slice(hh * MEM_DH, (hh + 1) * MEM_DH)
        s = _dot_nt(q_ref[0, :, sl], k_ref[0, :, sl])
        p = jnp.exp2(s - jnp.max(s, axis=1, keepdims=True))
        o = _dot(p.astype(BF16), v_ref[0, :, sl]) / jnp.sum(p, axis=1, keepdims=True)
        o_ref[0, :, sl] = o.astype(o_ref.dtype)


def _mem_attention(q, k, v, *, tq):
    b, sq, w = q.shape
    nm = k.shape[1]
    return pl.pallas_call(
        _mem_attn_kernel,
        grid=(b, sq // tq),
        in_specs=[pl.BlockSpec((1, tq, w), lambda bi, qi: (bi, qi, 0)),
                  pl.BlockSpec((1, nm, w), lambda bi, qi: (bi, 0, 0)),
                  pl.BlockSpec((1, nm, w), lambda bi, qi: (bi, 0, 0))],
        out_specs=pl.BlockSpec((1, tq, w), lambda bi, qi: (bi, qi, 0)),
        out_shape=jax.ShapeDtypeStruct((b, sq, w), BF16),
        compiler_params=_cparams(("parallel", "parallel")),
        name="mem_attn",
    )(q, k, v)


def _mem_kv_kernel(x_ref, g_ref, w_ref, k_ref, v_ref, kb_ref, vb_ref):
    h = _rms(x_ref[...], g_ref[...]).astype(BF16)
    kv = _dot(h, w_ref[...])
    w = MEM_HEADS * MEM_DH
    k_ref[...] = kv[:, :w]
    v_ref[...] = kv[:, w:]
    kb_ref[...] = kv[:, :w].astype(BF16)
    vb_ref[...] = kv[:, w:].astype(BF16)


def _mem_kv(mem, wp, *, tm):
    n, d = mem.shape
    w = MEM_HEADS * MEM_DH
    row = lambda c: pl.BlockSpec((tm, c), lambda i: (i, 0))
    return pl.pallas_call(
        _mem_kv_kernel,
        grid=(n // tm,),
        in_specs=[row(d), _full_spec((1, d)), _full_spec(wp["w_mem_kv"].shape)],
        out_specs=(row(w), row(w), row(w), row(w)),
        out_shape=(jax.ShapeDtypeStruct((n, w), F32), jax.ShapeDtypeStruct((n, w), F32),
                   jax.ShapeDtypeStruct((n, w), BF16), jax.ShapeDtypeStruct((n, w), BF16)),
        compiler_params=_cparams(("parallel",)),
        name="mem_kv",
    )(mem, wp["norm_mem"], wp["w_mem_kv"])


def _merge_kernel(x_ref, oa_ref, ob_ref, om_ref, g_ref, wg_ref, bg_ref, wa_ref, wb_ref, wm_ref, wo_ref,
                  gf_ref, x1_ref, hn_ref):
    x = x_ref[...]
    d = x.shape[1]
    h = _rms(x, g_ref[...]).astype(BF16)
    gates = jax.nn.sigmoid(_dot(h, wg_ref[...]) + bg_ref[...])
    merged = (gates[:, :d] * _dot(oa_ref[...], wa_ref[...])
              + gates[:, d:2 * d] * _dot(ob_ref[...], wb_ref[...])
              + gates[:, 2 * d:] * _dot(om_ref[...], wm_ref[...]))
    x1 = x + _dot(merged.astype(BF16), wo_ref[...])
    x1_ref[...] = x1
    hn_ref[...] = _rms(x1, gf_ref[...]).astype(BF16)


def _merge(x, oa, ob, om, wp, *, tm):
    n, d = x.shape
    row = lambda c: pl.BlockSpec((tm, c), lambda i: (i, 0))
    names = ("norm_mix", "w_gate", "b_gate", "w_br_a", "w_br_b", "w_br_m", "w_o", "norm_ffn")
    return pl.pallas_call(
        _merge_kernel,
        grid=(n // tm,),
        in_specs=[row(d), row(oa.shape[1]), row(ob.shape[1]), row(om.shape[1])]
                 + [_full_spec(wp[k].shape) for k in names],
        out_specs=(row(d), row(d)),
        out_shape=(jax.ShapeDtypeStruct((n, d), F32), jax.ShapeDtypeStruct((n, d), BF16)),
        compiler_params=_cparams(("parallel",)),
        name="merge",
    )(x, oa, ob, om, *[wp[k] for k in names])


def _top_values(sc, k, with_rank=False):
    vals = []
    cur = sc
    rank = jnp.full(sc.shape, float(k), F32) if with_rank else None
    for r in range(k):
        m = jnp.max(cur, axis=0, keepdims=True)
        vals.append(m)
        hit = cur == m
        if with_rank:
            rank = jnp.where(hit, float(r), rank)
        if r + 1 < k:
            cur = jnp.where(hit, NEG, cur)
    return (vals, rank) if with_rank else vals


def _stack_rows(rows, sub):
    out = jnp.zeros(sub.shape, F32)
    for r, row in enumerate(rows):
        if row is not None:
            out = jnp.where(sub == r, row, out)
    return out


def _packed_candidates(t1, t2, t2h, sub8):
    assert PEER_TOPK == 16
    down = lambda x, k: pltpu.roll(x, k, 0)
    t1_8_13 = _stack_rows([None, None] + t1[8:14], sub8)
    t1_14_15 = _stack_rows(t1[14:16], sub8)
    top2 = t2[0:1]
    slabs = [
        t1[0] + t2h,
        t1[0] + t2[8:16],
        t1[1] + t2h,
        jnp.where(sub8 < 5, t1[2] + t2h, t1[4] + down(t2h, 5)),
        jnp.where(sub8 < 4, t1[3] + t2h,
                  jnp.where(sub8 < 6, t1[5] + down(t2h, 4), t1[6] + down(t2h, 6))),
        jnp.where(sub8 < 2, t1[7] + t2h, t1_8_13 + top2),
        jnp.where(sub8 < 2, t1_14_15 + top2, NEG),
    ]
    return jnp.concatenate(slabs, axis=0)


EXACT_RANKS = 4
CNT_CAP = PEER_TOPK // (EXACT_RANKS + 1)


def _route_kernel(hn_ref, wq_ref, sk_ref, cnt_ref, e1_ref, rank_ref, e2_ref):
    tt = hn_ref.shape[0]
    q = _dot(hn_ref[...], wq_ref[...]).astype(BF16)
    sub = lax.broadcasted_iota(jnp.int32, (PEER_TOPK, tt), 0)
    sub8 = lax.broadcasted_iota(jnp.int32, (8, tt), 0)
    for h in range(PEER_HEADS):
        sc1 = _dot_nt(sk_ref[2 * h], q[:, (2 * h) * PEER_HALF:(2 * h + 1) * PEER_HALF])
        sc2 = _dot_nt(sk_ref[2 * h + 1], q[:, (2 * h + 1) * PEER_HALF:(2 * h + 2) * PEER_HALF])
        t1 = _top_values(sc1, PEER_TOPK)
        t2_rows, rank2 = _top_values(sc2, PEER_TOPK, with_rank=True)
        t2 = _stack_rows(t2_rows, sub)
        t2h = t2[:8]
        cands = []
        for a in range(PEER_TOPK):
            nb = PEER_TOPK // (a + 1)
            c = t1[a] + (t2 if nb > 8 else t2h)
            if nb < c.shape[0]:
                c = jnp.where((sub if nb > 8 else sub8) < nb, c, NEG)
            cands.append(c)
        cand = _packed_candidates(t1, t2, t2h, sub8)
        tau = _top_values(cand, PEER_TOPK)[-1]
        top = t1[0] + t2[0:1]
        z = jnp.sum(jnp.where(cand >= tau, jnp.exp(cand - top), 0.0), axis=0, keepdims=True)
        counts = [jnp.sum(jnp.where(cands[a] >= tau, 1.0, 0.0), axis=0, keepdims=True) for a in range(PEER_TOPK)]
        cnt = jnp.zeros(sc1.shape, F32)
        for c in range(1, CNT_CAP + 1):
            theta = jnp.full(tau.shape, -NEG, F32)
            for a in range(PEER_TOPK):
                theta = jnp.where(counts[a] >= c, t1[a], theta)
            cnt = jnp.where(sc1 >= theta, float(c), cnt)
        for a in range(EXACT_RANKS):
            cnt = jnp.where(sc1 == t1[a], counts[a], cnt)
        cnt_ref[h] = cnt
        e1_ref[h] = jnp.exp(sc1 - t1[0]) * (0.5 / z)
        rank_ref[h] = rank2.astype(BF16)
        e2_ref[h] = jnp.exp(sc2 - t2[0:1]).astype(BF16)


def _route(hn, wp, *, tt):
    n, d = hn.shape
    big = pl.BlockSpec((PEER_HEADS, PEER_NKEYS, tt), lambda i: (0, 0, i))
    big_shape = jax.ShapeDtypeStruct((PEER_HEADS, PEER_NKEYS, n), F32)
    big_half = jax.ShapeDtypeStruct((PEER_HEADS, PEER_NKEYS, n), BF16)
    return pl.pallas_call(
        _route_kernel,
        grid=(n // tt,),
        in_specs=[pl.BlockSpec((tt, d), lambda i: (i, 0)), _full_spec(wp["peer_wq"].shape),
                  _full_spec(wp["peer_subkeys"].shape)],
        out_specs=(big, big, big, big),
        out_shape=(big_shape, big_shape, big_half, big_half),
        compiler_params=_cparams(("parallel",)),
        name="peer_route",
    )(hn, wp["peer_wq"], wp["peer_subkeys"])


PAIR = 2 * PEER_NKEYS
SUBROWS = 128


def _peer_kernel(hn_in, u_ref, vt_ref, cnt_ref, e1_ref, rank_in, e2_in, x1_ref, gn_ref, y_ref,
                 acc_scr, hw_scr, rank_ref, e2_ref, hn_ref, *, te, ne):
    j = pl.program_id(1)
    tt = hn_in.shape[0]
    npair = te // PAIR

    @pl.when(j == 0)
    def _():
        rank_ref[...] = rank_in[...]
        e2_ref[...] = e2_in[...]
        hn_ref[...] = hn_in[...]

    def pre_act(p):
        a = _dot_nt(u_ref[p * PAIR:(p + 1) * PAIR, :], hn_ref[...])
        return a * (1.0 + lax.erf(a * math.sqrt(0.5)))

    def down(lo, hi):
        return _dot(vt_ref[:, lo * PAIR:hi * PAIR], hw_scr[lo * PAIR:hi * PAIR, :])

    cuts = sorted({0, 3 * npair // 8, 3 * npair // 4, npair - 1, npair})
    piece_after = {hi: lo for lo, hi in zip(cuts[:-1], cuts[1:])}

    groups = [(tc, sb) for tc in range(tt // LANE) for sb in range(PEER_NKEYS // SUBROWS)]
    zero = jnp.zeros((), BF16)
    act = pre_act(0)
    tot = None
    carry = None
    for p in range(npair):
        nxt = None
        for gi, (tc, sb) in enumerate(groups):
            if gi == 1 and p + 1 < npair:
                nxt = pre_act(p + 1)
            if gi == len(groups) // 2 and p in piece_after:
                part = down(piece_after[p], p)
                tot = part if tot is None else tot + part
            ls = slice(tc * LANE, (tc + 1) * LANE)
            i2 = slice(sb * SUBROWS, (sb + 1) * SUBROWS)
            w = [None, None] if carry is None else [carry * zero, carry * zero]
            for h in range(PEER_HEADS):
                rk = rank_ref[h, i2, ls]
                e2b = e2_ref[h, i2, ls]
                for rr in range(2):
                    i1 = 2 * p + rr
                    cnt = jnp.broadcast_to(cnt_ref[h, i1:i1 + 1, ls], (SUBROWS, LANE)).astype(BF16)
                    e1b = jnp.broadcast_to(e1_ref[h, i1:i1 + 1, ls], (SUBROWS, LANE)).astype(BF16)
                    cw = jnp.where(rk < cnt, e2b, zero) * e1b
                    w[rr] = cw if w[rr] is None else w[rr] + cw
            carry = w[1]
            for rr in range(2):
                r0 = rr * PEER_NKEYS + sb * SUBROWS
                hw_scr[p * PAIR + r0:p * PAIR + r0 + SUBROWS, ls] = w[rr] * act[r0:r0 + SUBROWS, ls].astype(BF16)
        act = nxt
    part = down(piece_after[npair], npair)
    tot = part if tot is None else tot + part

    @pl.when(j == 0)
    def _():
        acc_scr[...] = tot

    @pl.when(j > 0)
    def _():
        acc_scr[...] += tot

    @pl.when(j == ne - 1)
    def _():
        xr = x1_ref[...] + acc_scr[...].T
        y_ref[...] = _rms(xr, gn_ref[...])


def _peer(hn, cnt, e1, rank2, e2, x1, wp, *, tt, te):
    n, d = hn.shape
    ne = wp["peer_u"].shape[0] // te
    big = pl.BlockSpec((PEER_HEADS, PEER_NKEYS, tt), lambda i, j: (0, 0, i))
    rows = pl.BlockSpec((PEER_HEADS, te // PEER_NKEYS, tt), lambda i, j: (0, j, i))
    kern = functools.partial(_peer_kernel, te=te, ne=ne)
    return pl.pallas_call(
        kern,
        grid=(n // tt, ne),
        in_specs=[pl.BlockSpec((tt, d), lambda i, j: (i, 0)),
                  pl.BlockSpec((te, d), lambda i, j: (j, 0)),
                  pl.BlockSpec((d, te), lambda i, j: (0, j)),
                  rows, rows, big, big,
                  pl.BlockSpec((tt, d), lambda i, j: (i, 0)),
                  pl.BlockSpec((1, d), lambda i, j: (0, 0))],
        out_specs=pl.BlockSpec((tt, d), lambda i, j: (i, 0)),
        out_shape=jax.ShapeDtypeStruct((n, d), F32),
        scratch_shapes=[pltpu.VMEM((d, tt), F32), pltpu.VMEM((te, tt), BF16),
                        pltpu.VMEM((PEER_HEADS, PEER_NKEYS, tt), BF16),
                        pltpu.VMEM((PEER_HEADS, PEER_NKEYS, tt), BF16),
                        pltpu.VMEM((tt, d), BF16)],
        compiler_params=_cparams(("parallel", "arbitrary")),
        name="peer_experts",
    )(hn, wp["peer_u"], wp["peer_vt"], cnt, e1, rank2, e2, x1, wp["norm_final"])


def _prep_weights(l, p):
    f = lambda a: a.astype(F32)
    w_in = f(p["w_in"][l])
    o_cq, o_ckv, o_kr = 0, MLA_Q_LORA, MLA_Q_LORA + MLA_KV_LORA
    o_dq = o_kr + MLA_ROPE
    half = MLA_ROPE // 2
    d = w_in.shape[0]
    kr = w_in[:, o_kr:o_dq]
    pad = jnp.zeros((d, LANE - MLA_ROPE), F32)
    w_in2 = jnp.concatenate([
        w_in[:, o_cq:o_kr], kr, pad,
        -kr[:, half:], kr[:, :half], pad,
        w_in[:, o_dq:]], axis=1)
    assert w_in2.shape[1] == _C_END

    w_uq = f(p["w_uq"][l]).reshape(MLA_Q_LORA, MLA_HEADS, MLA_NOPE + MLA_ROPE)
    nope, x1, x2 = w_uq[..., :MLA_NOPE], w_uq[..., MLA_NOPE:MLA_NOPE + half], w_uq[..., MLA_NOPE + half:]
    zpad = jnp.zeros((MLA_Q_LORA, MLA_HEADS, HEAD_W - MLA_NOPE - MLA_ROPE), F32)
    q_slab = jnp.concatenate([nope, x1, x2, zpad], axis=-1).reshape(MLA_Q_LORA, -1)
    r_slab = jnp.concatenate([jnp.zeros_like(nope), -x2, x1, zpad], axis=-1).reshape(MLA_Q_LORA, -1)
    w_uq2 = jnp.concatenate([q_slab, r_slab], axis=1)

    w_uk = f(p["w_uk"][l]).reshape(MLA_KV_LORA, MLA_HEADS, MLA_NOPE)
    w_uk2 = jnp.concatenate([w_uk, jnp.zeros((MLA_KV_LORA, MLA_HEADS, HEAD_W - MLA_NOPE), F32)],
                            axis=-1).reshape(MLA_KV_LORA, -1)
    r_idx = jnp.arange(MLA_ROPE)
    cols = jnp.arange(MLA_HEADS * HEAD_W)
    p_kr = ((cols[None, :] % HEAD_W) == (MLA_NOPE + r_idx[:, None])).astype(F32)

    slopes = jnp.exp2(-8.0 * jnp.arange(1, DIFF_HEADS + 1, dtype=F32) / DIFF_HEADS)
    sub = f(p["diff_subln"][l])
    row = lambda a: f(a).reshape(1, -1)
    return {
        "norm_mix": row(p["norm_mix"][l]), "w_in": w_in2.astype(BF16),
        "mla_q_norm": row(p["mla_q_norm"][l]), "w_uq": w_uq2.astype(BF16),
        "mla_kv_norm": row(p["mla_kv_norm"][l]),
        "w_uk": w_uk2.astype(BF16), "p_kr": p_kr.astype(BF16), "w_uv": p["w_uv"][l].astype(BF16),
        "slopes": slopes,
        "diff_lq1": row(p["diff_lq1"][l]), "diff_lk1": row(p["diff_lk1"][l]),
        "diff_lq2": row(p["diff_lq2"][l]), "diff_lk2": row(p["diff_lk2"][l]),
        "diff_subln2": jnp.concatenate([sub, sub]).reshape(1, -1),
        "norm_mem": row(p["norm_mem"][l]), "w_mem_kv": p["w_mem_kv"][l].astype(BF16),
        "w_gate": p["w_gate"][l].astype(BF16), "b_gate": row(p["b_gate"][l]),
        "w_br_a": p["w_br_a"][l].astype(BF16), "w_br_b": p["w_br_b"][l].astype(BF16),
        "w_br_m": p["w_br_m"][l].astype(BF16), "w_o": p["w_o"][l].astype(BF16),
        "norm_ffn": row(p["norm_ffn"][l]),
        "peer_wq": p["peer_wq"][l].astype(BF16),
        "peer_subkeys": p["peer_subkeys"][l].reshape(PEER_HEADS * 2, PEER_NKEYS, PEER_HALF).astype(BF16),
        "peer_u": p["peer_u"][l].astype(BF16), "peer_vt": p["peer_v"][l].T.astype(BF16),
        "norm_final": row(p["norm_final"]),
    }


def _rope_tables(pos):
    half = MLA_ROPE // 2
    inv = (1.0 / (ROPE_THETA ** (np.arange(half, dtype=np.float32) / half))).astype(np.float32)
    ang = pos.astype(np.float32)[:, None] * inv[None, :]
    cos, sin = np.cos(ang), np.sin(ang)
    n = pos.shape[0]
    one = np.ones((n, MLA_NOPE), np.float32)
    zq = np.zeros((n, HEAD_W - MLA_NOPE - MLA_ROPE), np.float32)
    scale = np.float32(MLA_SCALE)
    cq = np.concatenate([one, cos, cos, zq], axis=1) * scale
    sq = np.concatenate([0.0 * one, sin, sin, zq], axis=1) * scale
    zk = np.zeros((n, LANE - MLA_ROPE), np.float32)
    ck = np.concatenate([cos, cos, zk], axis=1)
    sk = np.concatenate([sin, sin, zk], axis=1)
    return cq, sq, ck, sk


def _layer(x, pos, past, mem_k, mem_v, wp, layer, cfg):
    b, s, d = x.shape
    n = b * s
    xf = x.reshape(n, d)
    tabs = _rope_tables(pos)
    if s % cfg["tm"] == 0:
        pos_blocks = s // cfg["tm"]
    else:
        tabs = tuple(np.tile(t, (b, 1)) for t in tabs)
        pos_blocks = n // cfg["tm"]
    ckv, kr, dk, dv, qm, dqb, dkb, dvb, mqb = _proj_in(xf, tabs, wp, tm=cfg["tm"], pos_blocks=pos_blocks)
    new_rows = (ckv.reshape(b, s, -1), kr.reshape(b, s, -1), dk.reshape(b, s, DIFF_HEADS, DIFF_V),
                dv.reshape(b, s, DIFF_HEADS, DIFF_V))
    r3 = lambda a: a.reshape(b, s, -1)
    if past is None:
        ckv_all, kr_all = ckv, kr
        dk_all, dv_all = r3(dkb), r3(dvb)
        kk = s
        q_off = 0
    else:
        p_len = past[0].shape[1]
        kk = p_len + s
        q_off = p_len
        ckv_all = jnp.concatenate([past[0], r3(ckv)], axis=1).reshape(b * kk, -1)
        kr_all = jnp.concatenate([past[1], r3(kr)], axis=1).reshape(b * kk, -1)
        dk_all = jnp.concatenate([past[2].reshape(b, p_len, -1).astype(BF16), r3(dkb)], axis=1)
        dv_all = jnp.concatenate([past[3].reshape(b, p_len, -1).astype(BF16), r3(dvb)], axis=1)
    k_mla, v_mla = _kv_up(ckv_all, kr_all, wp, tm=cfg["tm_kv"])
    att = dict(tk=cfg["tk"] if past is None else kk, q_off=q_off)
    o_a = _mla_attention(r3(qm), k_mla.reshape(b, kk, -1), v_mla.reshape(b, kk, -1), tq=cfg["tq_mla"], **att)
    lam_init = 0.8 - 0.6 * math.exp(-0.3 * layer)
    o_b = _diff_attention(r3(dqb), dk_all, dv_all, wp, lam_init=lam_init, tq=cfg["tq"], **att)
    o_m = _mem_attention(r3(mqb), mem_k, mem_v, tq=cfg["tq"])
    x1, hn = _merge(xf, o_a.reshape(n, -1), o_b.reshape(n, -1), o_m.reshape(n, -1), wp, tm=cfg["tm"])
    return x1, hn, new_rows


def _peer_and_norm(x1, hn, wp, cfg):
    cnt, e1, rank2, e2 = _route(hn, wp, tt=cfg["tt_route"])
    return _peer(hn, cnt, e1, rank2, e2, x1, wp, tt=cfg["tt"], te=cfg["te"])


_CFG_PROMPT = dict(tm=512, tm_kv=512, tq=512, tq_mla=1024, tk=512, tt_route=256, tt=512, te=2048)
_CFG_SAMPLE = dict(tm=256, tm_kv=256, tq=32, tq_mla=32, tk=None, tt_route=256, tt=256, te=2048)


def kernel(x_prompt, x_sample, cache_mla_ckv, cache_mla_krope, cache_diff_k, cache_diff_v, cache_mem_k, cache_mem_v, mem_prompt, norm_mix, w_in, mla_q_norm, w_uq, mla_kv_norm, w_uk, w_uv, diff_lq1, diff_lk1, diff_lq2, diff_lk2, diff_subln, norm_mem, w_mem_kv, w_br_a, w_br_b, w_br_m, w_gate, b_gate, w_o, norm_ffn, peer_wq, peer_subkeys, peer_u, peer_v, norm_final):
    params = dict(norm_mix=norm_mix, w_in=w_in, mla_q_norm=mla_q_norm, w_uq=w_uq, mla_kv_norm=mla_kv_norm,
                  w_uk=w_uk, w_uv=w_uv, diff_lq1=diff_lq1, diff_lk1=diff_lk1, diff_lq2=diff_lq2,
                  diff_lk2=diff_lk2, diff_subln=diff_subln, norm_mem=norm_mem, w_mem_kv=w_mem_kv,
                  w_br_a=w_br_a, w_br_b=w_br_b, w_br_m=w_br_m, w_gate=w_gate, b_gate=b_gate, w_o=w_o,
                  norm_ffn=norm_ffn, peer_wq=peer_wq, peer_subkeys=peer_subkeys, peer_u=peer_u,
                  peer_v=peer_v, norm_final=norm_final)
    depth = w_in.shape[0]
    assert depth == 1, "the final norm is fused into the last PEER step of a single layer"
    bp, sp, d = x_prompt.shape
    bs, ss, _ = x_sample.shape
    n_mem = mem_prompt.shape[1]
    past_len = cache_mla_ckv.shape[2]
    pos_p = np.arange(sp, dtype=np.int32)
    pos_s = past_len + np.arange(ss, dtype=np.int32)

    l = 0
    wp = _prep_weights(l, params)
    mk, mv, mkb, mvb = _mem_kv(mem_prompt.reshape(bp * n_mem, d), wp, tm=n_mem)
    m3 = lambda a, b: a.reshape(b, n_mem, -1)
    x1p, hnp, rows_p = _layer(x_prompt, pos_p, None, m3(mkb, bp), m3(mvb, bp), wp, l, _CFG_PROMPT)
    past = (cache_mla_ckv[l], cache_mla_krope[l], cache_diff_k[l], cache_diff_v[l])
    x1s, hns, rows_s = _layer(x_sample, pos_s, past, m3(cache_mem_k[l].astype(BF16), bs),
                              m3(cache_mem_v[l].astype(BF16), bs), wp, l, _CFG_SAMPLE)
    y_prompt = _peer_and_norm(x1p, hnp, wp, _CFG_PROMPT).reshape(bp, sp, d)
    y_sample = _peer_and_norm(x1s, hns, wp, _CFG_SAMPLE).reshape(bs, ss, d)
    st = lambda a: a[None]
    mem4 = lambda a: a.reshape(bp, n_mem, MEM_HEADS, MEM_DH)[None]
    return (y_prompt, y_sample,
            st(rows_p[0]), st(rows_p[1]), st(rows_p[2]), st(rows_p[3]),
            mem4(mk), mem4(mv),
            st(rows_s[0]), st(rows_s[1]), st(rows_s[2]), st(rows_s[3]))
```

```python
import functools
import math

import jax
import jax.numpy as jnp
import numpy as np
from jax import lax
from jax.experimental import pallas as pl
from jax.experimental.pallas import tpu as pltpu

F32 = jnp.float32
BF16 = jnp.bfloat16

CHUNK = 64
CHUNK_SHIFT = 6
EPS = 1e-6
NEG = -1e30
MLA_HEADS = 8
MLA_Q_LORA = 384
MLA_KV_LORA = 256
MLA_NOPE = 64
MLA_ROPE = 32
MLA_V = 64
ROPE_THETA = 10000.0
LOG2E = math.log2(math.e)
MLA_SCALE = (MLA_NOPE + MLA_ROPE) ** -0.5 * LOG2E
DIFF_HEADS = 8
DIFF_DH = 32
DIFF_V = 2 * DIFF_DH
DIFF_SCALE = DIFF_DH ** -0.5 * LOG2E
MEM_HEADS = 4
MEM_DH = 128
MEM_SCALE = MEM_DH ** -0.5 * LOG2E
PEER_HEADS = 8
PEER_NKEYS = 128
PEER_HALF = 128
PEER_TOPK = 16
LANE = 128
HEAD_W = 128

VMEM_LIMIT = 56 * 1024 * 1024

_C_CQ = 0
_C_CKV = _C_CQ + MLA_Q_LORA
_C_KR = _C_CKV + MLA_KV_LORA
_C_KRR = _C_KR + LANE
_C_DQ = _C_KRR + LANE
_C_DK = _C_DQ + DIFF_HEADS * DIFF_V
_C_DV = _C_DK + DIFF_HEADS * DIFF_V
_C_MQ = _C_DV + DIFF_HEADS * DIFF_V
_C_END = _C_MQ + MEM_HEADS * MEM_DH


def _cparams(sem):
    return pltpu.CompilerParams(dimension_semantics=sem, vmem_limit_bytes=VMEM_LIMIT)


def _rms(x, g):
    return x * lax.rsqrt(jnp.mean(x * x, axis=-1, keepdims=True) + EPS) * g


def _dot(a, b):
    return jnp.dot(a, b, preferred_element_type=F32)


def _dot_nt(a, b):
    return lax.dot_general(a, b, (((1,), (1,)), ((), ())), preferred_element_type=F32)


def _full_spec(shape):
    nd = len(shape)
    return pl.BlockSpec(shape, lambda *_: (0,) * nd)


def _proj_in_kernel(x_ref, g_ref, win_ref, qn_ref, wuq_ref, kvn_ref, cq_ref, sq_ref, ck_ref, sk_ref,
                    ckv_ref, kr_ref, dk_ref, dv_ref, qm_ref, dqb_ref, dkb_ref, dvb_ref, mqb_ref):
    h = _rms(x_ref[...], g_ref[...]).astype(BF16)
    z = _dot(h, win_ref[...])
    cqn = _rms(z[:, _C_CQ:_C_CKV], qn_ref[...]).astype(BF16)
    q2 = _dot(cqn, wuq_ref[...])
    cq = cq_ref[...]
    sq = sq_ref[...]
    nq = MLA_HEADS * HEAD_W
    for hh in range(MLA_HEADS):
        lo = hh * HEAD_W
        qm_ref[:, lo:lo + HEAD_W] = (q2[:, lo:lo + HEAD_W] * cq
                                     + q2[:, nq + lo:nq + lo + HEAD_W] * sq).astype(BF16)
    ckv_ref[...] = _rms(z[:, _C_CKV:_C_KR], kvn_ref[...])
    kr = z[:, _C_KR:_C_KRR] * ck_ref[...] + z[:, _C_KRR:_C_DQ] * sk_ref[...]
    kr_ref[...] = kr[:, :MLA_ROPE]
    dk = z[:, _C_DK:_C_DV]
    dv = z[:, _C_DV:_C_MQ]
    dk_ref[...] = dk.reshape(dk_ref.shape)
    dv_ref[...] = dv.reshape(dv_ref.shape)
    dkb_ref[...] = dk.astype(BF16)
    dvb_ref[...] = dv.astype(BF16)
    dqb_ref[...] = (z[:, _C_DQ:_C_DK] * DIFF_SCALE).astype(BF16)
    mqb_ref[...] = (z[:, _C_MQ:_C_END] * MEM_SCALE).astype(BF16)


def _proj_in(x, tabs, wp, *, tm, pos_blocks):
    n, d = x.shape
    grid = (n // tm,)
    row = lambda w: pl.BlockSpec((tm, w), lambda i: (i, 0))
    tab = pl.BlockSpec((tm, LANE), lambda i: (i % pos_blocks, 0))
    heads = pl.BlockSpec((tm, DIFF_HEADS, DIFF_V), lambda i: (i, 0, 0))
    dw = DIFF_HEADS * DIFF_V
    out_shape = (
        jax.ShapeDtypeStruct((n, MLA_KV_LORA), F32),
        jax.ShapeDtypeStruct((n, MLA_ROPE), F32),
        jax.ShapeDtypeStruct((n, DIFF_HEADS, DIFF_V), F32),
        jax.ShapeDtypeStruct((n, DIFF_HEADS, DIFF_V), F32),
        jax.ShapeDtypeStruct((n, MLA_HEADS * HEAD_W), BF16),
        jax.ShapeDtypeStruct((n, dw), BF16),
        jax.ShapeDtypeStruct((n, dw), BF16),
        jax.ShapeDtypeStruct((n, dw), BF16),
        jax.ShapeDtypeStruct((n, MEM_HEADS * MEM_DH), BF16),
    )
    return pl.pallas_call(
        _proj_in_kernel,
        grid=grid,
        in_specs=[row(d), _full_spec((1, d)), _full_spec(wp["w_in"].shape), _full_spec((1, MLA_Q_LORA)),
                  _full_spec(wp["w_uq"].shape), _full_spec((1, MLA_KV_LORA)), tab, tab, tab, tab],
        out_specs=(row(MLA_KV_LORA), row(MLA_ROPE), heads, heads, row(MLA_HEADS * HEAD_W),
                   row(dw), row(dw), row(dw), row(MEM_HEADS * MEM_DH)),
        out_shape=out_shape,
        compiler_params=_cparams(("parallel",)),
        name="proj_in",
    )(x, wp["norm_mix"], wp["w_in"], wp["mla_q_norm"], wp["w_uq"], wp["mla_kv_norm"], *tabs)


def _kv_up_kernel(ckv_ref, kr_ref, wuk_ref, pk_ref, wuv_ref, k_ref, v_ref):
    c = ckv_ref[...].astype(BF16)
    k = _dot(c, wuk_ref[...]) + _dot(kr_ref[...].astype(BF16), pk_ref[...])
    k_ref[...] = k.astype(BF16)
    v_ref[...] = _dot(c, wuv_ref[...]).astype(BF16)


def _kv_up(ckv, kr, wp, *, tm):
    n = ckv.shape[0]
    row = lambda w: pl.BlockSpec((tm, w), lambda i: (i, 0))
    return pl.pallas_call(
        _kv_up_kernel,
        grid=(n // tm,),
        in_specs=[row(MLA_KV_LORA), row(MLA_ROPE), _full_spec(wp["w_uk"].shape),
                  _full_spec(wp["p_kr"].shape), _full_spec(wp["w_uv"].shape)],
        out_specs=(row(MLA_HEADS * HEAD_W), row(MLA_HEADS * MLA_V)),
        out_shape=(jax.ShapeDtypeStruct((n, MLA_HEADS * HEAD_W), BF16),
                   jax.ShapeDtypeStruct((n, MLA_HEADS * MLA_V), BF16)),
        compiler_params=_cparams(("parallel",)),
        name="kv_up",
    )(ckv, kr, wp["w_uk"], wp["p_kr"], wp["w_uv"])


def _lanes(col, n):
    if n == LANE:
        return col
    if n % LANE == 0:
        return jnp.concatenate([col] * (n // LANE), axis=1)
    return jnp.broadcast_to(col[:, :1], (col.shape[0], n))


GROUP_VREGS = 16


def _softmax_block(score_group, tq, tk, row0, m_scr, al_scr, p_scr, l_scr=None):
    rg = min(tq, max(16, (GROUP_VREGS * 8 * LANE // tk) // 16 * 16))
    for g in range(tq // rg):
        r = slice(row0 + g * rg, row0 + (g + 1) * rg)
        m_prev = m_scr[r]
        m_next = jnp.maximum(m_prev, jnp.max(score_group(g, rg), axis=1, keepdims=True))
        al_scr[r] = jnp.exp2(m_prev - m_next)
        m_scr[r] = m_next
    for g in range(tq // rg):
        r = slice(row0 + g * rg, row0 + (g + 1) * rg)
        e = jnp.exp2(score_group(g, rg) - _lanes(m_scr[r], tk))
        p_scr[r] = e.astype(BF16)
        if l_scr is not None:
            l_scr[r] = al_scr[r] * l_scr[r] + jnp.sum(e, axis=1, keepdims=True)


def _accumulate(rows, v, al_scr, l_scr, acc_scr, p_scr):
    pv = _dot(p_scr[rows], jnp.concatenate([v, jnp.ones(v.shape, v.dtype)], axis=1))
    w = v.shape[1]
    acc_scr[rows] = acc_scr[rows] * al_scr[rows] + pv[:, :w]
    l_scr[rows] = l_scr[rows] * al_scr[rows] + pv[:, w:]


def _chunk_visible(q_lo, k_lo, g, rg, tk):
    qp = q_lo + g * rg + lax.broadcasted_iota(jnp.int32, (rg, tk), 0)
    kp = k_lo + lax.broadcasted_iota(jnp.int32, (rg, tk), 1)
    return qp, kp, (kp >> CHUNK_SHIFT) <= (qp >> CHUNK_SHIFT)


def _init_stats(m_scr, l_scr, acc_scr):
    m_scr[...] = jnp.full(m_scr.shape, NEG, F32)
    l_scr[...] = jnp.zeros(l_scr.shape, F32)
    acc_scr[...] = jnp.zeros(acc_scr.shape, F32)


def _block_range(qi, *, tq, tk, q_off, nk):
    q_lo = q_off + qi * tq
    q_hi = q_lo + (tq - 1)
    n_behind = jnp.minimum((q_lo + 1) // tk, nk)
    last_key = ((q_hi >> CHUNK_SHIFT) << CHUNK_SHIFT) + (CHUNK - 1)
    n_need = jnp.minimum(last_key // tk + 1, nk)
    return q_lo, n_behind, n_need


def _for_blocks(lo, hi, fn):
    lax.fori_loop(lo, hi, lambda i, c: (fn(i), c)[1], 0)


def _attn_specs(b, pairs, sq, sk, tq, qw, kw, vw):
    grid = (b, pairs, sq // tq)
    in_specs = [pl.BlockSpec((1, tq, qw), lambda bi, p, qi: (bi, qi, p)),
                pl.BlockSpec((1, sk, kw), lambda bi, p, qi: (bi, 0, p)),
                pl.BlockSpec((1, sk, vw), lambda bi, p, qi: (bi, 0, p))]
    out_spec = pl.BlockSpec((1, tq, vw), lambda bi, p, qi: (bi, qi, p))
    return grid, in_specs, out_spec


def _mla_kernel(q_ref, k_ref, v_ref, o_ref, m_scr, l_scr, al_scr, acc_scr, p_scr, *, tq, tk, q_off, nk):
    qi = pl.program_id(2)
    _init_stats(m_scr, l_scr, acc_scr)
    q_lo, n_behind, n_need = _block_range(qi, tq=tq, tk=tk, q_off=q_off, nk=nk)

    def block(i, masked):
        k_lo = pl.multiple_of(i * tk, tk)
        for hh in range(2):
            hs = slice(hh * HEAD_W, (hh + 1) * HEAD_W)
            s = _dot_nt(q_ref[0, :, hs], k_ref[0, pl.ds(k_lo, tk), hs])

            def score_group(g, rg, s=s):
                sg = s[g * rg:(g + 1) * rg]
                if masked:
                    sg = jnp.where(_chunk_visible(q_lo, k_lo, g, rg, tk)[2], sg, NEG)
                return sg

            _softmax_block(score_group, tq, tk, hh * tq, m_scr, al_scr, p_scr)
        _accumulate(slice(0, 2 * tq), v_ref[0, pl.ds(k_lo, tk), :], al_scr, l_scr, acc_scr, p_scr)

    _for_blocks(0, n_behind, lambda i: block(i, False))
    _for_blocks(n_behind, n_need, lambda i: block(i, True))

    lane = lax.broadcasted_iota(jnp.int32, (tq, LANE), 1)
    o0 = acc_scr[0:tq] / l_scr[0:tq]
    o1 = acc_scr[tq:2 * tq] / l_scr[tq:2 * tq]
    o_ref[0] = jnp.where(lane < MLA_V, o0, o1).astype(o_ref.dtype)


def _mla_attention(q, k, v, *, tq, tk, q_off):
    b, sq, _ = q.shape
    sk = k.shape[1]
    grid, in_specs, out_spec = _attn_specs(b, MLA_HEADS // 2, sq, sk, tq, 2 * HEAD_W, 2 * HEAD_W, 2 * MLA_V)
    kern = functools.partial(_mla_kernel, tq=tq, tk=tk, q_off=q_off, nk=sk // tk)
    stat = pltpu.VMEM((2 * tq, LANE), F32)
    return pl.pallas_call(
        kern,
        grid=grid,
        in_specs=in_specs,
        out_specs=out_spec,
        out_shape=jax.ShapeDtypeStruct((b, sq, MLA_HEADS * MLA_V), BF16),
        scratch_shapes=[stat, stat, stat, stat, pltpu.VMEM((2 * tq, tk), BF16)],
        compiler_params=_cparams(("parallel", "parallel", "parallel")),
        name="mla_attn",
    )(q, k, v)


def _diff_kernel(slope_ref, lq1_ref, lk1_ref, lq2_ref, lk2_ref, sub_ref, q_ref, k_ref, v_ref, o_ref,
                 qs_scr, m_scr, l_scr, al_scr, acc_scr, p_scr, *, tq, tk, q_off, nk, lam_init):
    pr = pl.program_id(1)
    qi = pl.program_id(2)
    _init_stats(m_scr, l_scr, acc_scr)
    q_lo, n_behind, n_need = _block_range(qi, tq=tq, tk=tk, q_off=q_off, nk=nk)

    q = q_ref[0]
    lane = lax.broadcasted_iota(jnp.int32, (tq, LANE), 1)
    for mi in range(4):
        lo = mi * DIFF_DH
        qs_scr[mi * tq:(mi + 1) * tq] = jnp.where((lane >= lo) & (lane < lo + DIFF_DH), q, jnp.zeros_like(q))

    def block(i, masked):
        k_lo = pl.multiple_of(i * tk, tk)
        kb = k_ref[0, pl.ds(k_lo, tk), :]
        kpos = (k_lo + lax.broadcasted_iota(jnp.int32, (1, tk), 1)).astype(F32)

        def scores(mi):
            return _dot_nt(qs_scr[mi * tq:(mi + 1) * tq], kb)

        def softmax(mi, s_map):
            slope = slope_ref[2 * pr + mi // 2] * LOG2E

            def score_group(g, rg):
                sg = s_map[g * rg:(g + 1) * rg]
                if masked:
                    qp, kp, vis = _chunk_visible(q_lo, k_lo, g, rg, tk)
                    return jnp.where(vis, sg + slope * jnp.minimum(kp, 2 * qp - kp).astype(F32), NEG)
                return sg + slope * kpos

            _softmax_block(score_group, tq, tk, mi * tq, m_scr, al_scr, p_scr, l_scr)

        def values(mi):
            rows = slice(mi * tq, (mi + 1) * tq)
            acc_scr[rows] = acc_scr[rows] * al_scr[rows] + _dot(p_scr[rows], v_ref[0, pl.ds(k_lo, tk), :])

        s0 = scores(0)
        s1 = scores(1)
        softmax(0, s0)
        s2 = scores(2)
        values(0)
        softmax(1, s1)
        s3 = scores(3)
        values(1)
        softmax(2, s2)
        values(2)
        softmax(3, s3)
        values(3)

    _for_blocks(0, n_behind, lambda i: block(i, False))
    _for_blocks(n_behind, n_need, lambda i: block(i, True))

    lam = (jnp.exp(jnp.sum(lq1_ref[...] * lk1_ref[...], axis=1, keepdims=True))
           - jnp.exp(jnp.sum(lq2_ref[...] * lk2_ref[...], axis=1, keepdims=True)) + lam_init)
    first = lane < DIFF_V
    on = [acc_scr[mi * tq:(mi + 1) * tq] / l_scr[mi * tq:(mi + 1) * tq] for mi in range(4)]
    o = jnp.where(first, on[0] - lam * on[1], on[2] - lam * on[3])
    sq = o * o
    ms0 = jnp.sum(jnp.where(first, sq, 0.0), axis=1, keepdims=True) * (1.0 / DIFF_V)
    ms1 = jnp.sum(jnp.where(first, 0.0, sq), axis=1, keepdims=True) * (1.0 / DIFF_V)
    r = jnp.where(first, lax.rsqrt(ms0 + EPS), lax.rsqrt(ms1 + EPS))
    o_ref[0] = ((o * r * sub_ref[...]) * (1.0 - lam_init)).astype(o_ref.dtype)


def _diff_attention(q, k, v, wp, *, tq, tk, q_off, lam_init):
    b, sq, _ = q.shape
    sk = k.shape[1]
    grid, in_specs, out_spec = _attn_specs(b, DIFF_HEADS // 2, sq, sk, tq, LANE, LANE, LANE)
    kern = functools.partial(_diff_kernel, tq=tq, tk=tk, q_off=q_off, nk=sk // tk, lam_init=lam_init)
    small = lambda w: pl.BlockSpec((1, w), lambda bi, p, qi: (0, 0))
    stat = pltpu.VMEM((4 * tq, LANE), F32)
    return pl.pallas_call(
        kern,
        grid=grid,
        in_specs=[pl.BlockSpec(memory_space=pltpu.SMEM),
                  small(DIFF_DH), small(DIFF_DH), small(DIFF_DH), small(DIFF_DH), small(LANE)] + in_specs,
        out_specs=out_spec,
        out_shape=jax.ShapeDtypeStruct((b, sq, DIFF_HEADS * DIFF_V), BF16),
        scratch_shapes=[pltpu.VMEM((4 * tq, LANE), BF16), stat, stat, stat, stat,
                        pltpu.VMEM((4 * tq, tk), BF16)],
        compiler_params=_cparams(("parallel", "parallel", "parallel")),
        name="diff_attn",
    )(wp["slopes"], wp["diff_lq1"], wp["diff_lk1"], wp["diff_lq2"], wp["diff_lk2"], wp["diff_subln2"],
      q, k, v)


def _mem_attn_kernel(q_ref, k_ref, v_ref, o_ref):
    for hh in range(MEM_HEADS):
        sl = slice(hh * MEM_DH, (hh + 1) * MEM_DH)
        s = _dot_nt(q_ref[0, :, sl], k_ref[0, :, sl])
        p = jnp.exp2(s - jnp.max(s, axis=1, keepdims=True))
        o = _dot(p.astype(BF16), v_ref[0, :, sl]) / jnp.sum(p, axis=1, keepdims=True)
        o_ref[0, :, sl] = o.astype(o_ref.dtype)


def _mem_attention(q, k, v, *, tq):
    b, sq, w = q.shape
    nm = k.shape[1]
    return pl.pallas_call(
        _mem_attn_kernel,
        grid=(b, sq // tq),
        in_specs=[pl.BlockSpec((1, tq, w), lambda bi, qi: (bi, qi, 0)),
                  pl.BlockSpec((1, nm, w), lambda bi, qi: (bi, 0, 0)),
                  pl.BlockSpec((1, nm, w), lambda bi, qi: (bi, 0, 0))],
        out_specs=pl.BlockSpec((1, tq, w), lambda bi, qi: (bi, qi, 0)),
        out_shape=jax.ShapeDtypeStruct((b, sq, w), BF16),
        compiler_params=_cparams(("parallel", "parallel")),
        name="mem_attn",
    )(q, k, v)


def _mem_kv_kernel(x_ref, g_ref, w_ref, k_ref, v_ref, kb_ref, vb_ref):
    h = _rms(x_ref[...], g_ref[...]).astype(BF16)
    kv = _dot(h, w_ref[...])
    w = MEM_HEADS * MEM_DH
    k_ref[...] = kv[:, :w]
    v_ref[...] = kv[:, w:]
    kb_ref[...] = kv[:, :w].astype(BF16)
    vb_ref[...] = kv[:, w:].astype(BF16)


def _mem_kv(mem, wp, *, tm):
    n, d = mem.shape
    w = MEM_HEADS * MEM_DH
    row = lambda c: pl.BlockSpec((tm, c), lambda i: (i, 0))
    return pl.pallas_call(
        _mem_kv_kernel,
        grid=(n // tm,),
        in_specs=[row(d), _full_spec((1, d)), _full_spec(wp["w_mem_kv"].shape)],
        out_specs=(row(w), row(w), row(w), row(w)),
        out_shape=(jax.ShapeDtypeStruct((n, w), F32), jax.ShapeDtypeStruct((n, w), F32),
                   jax.ShapeDtypeStruct((n, w), BF16), jax.ShapeDtypeStruct((n, w), BF16)),
        compiler_params=_cparams(("parallel",)),
        name="mem_kv",
    )(mem, wp["norm_mem"], wp["w_mem_kv"])


def _merge_kernel(x_ref, oa_ref, ob_ref, om_ref, g_ref, wg_ref, bg_ref, wa_ref, wb_ref, wm_ref, wo_ref,
                  gf_ref, x1_ref, hn_ref):
    x = x_ref[...]
    d = x.shape[1]
    h = _rms(x, g_ref[...]).astype(BF16)
    gates = jax.nn.sigmoid(_dot(h, wg_ref[...]) + bg_ref[...])
    merged = (gates[:, :d] * _dot(oa_ref[...], wa_ref[...])
              + gates[:, d:2 * d] * _dot(ob_ref[...], wb_ref[...])
              + gates[:, 2 * d:] * _dot(om_ref[...], wm_ref[...]))
    x1 = x + _dot(merged.astype(BF16), wo_ref[...])
    x1_ref[...] = x1
    hn_ref[...] = _rms(x1, gf_ref[...]).astype(BF16)


def _merge(x, oa, ob, om, wp, *, tm):
    n, d = x.shape
    row = lambda c: pl.BlockSpec((tm, c), lambda i: (i, 0))
    names = ("norm_mix", "w_gate", "b_gate", "w_br_a", "w_br_b", "w_br_m", "w_o", "norm_ffn")
    return pl.pallas_call(
        _merge_kernel,
        grid=(n // tm,),
        in_specs=[row(d), row(oa.shape[1]), row(ob.shape[1]), row(om.shape[1])]
                 + [_full_spec(wp[k].shape) for k in names],
        out_specs=(row(d), row(d)),
        out_shape=(jax.ShapeDtypeStruct((n, d), F32), jax.ShapeDtypeStruct((n, d), BF16)),
        compiler_params=_cparams(("parallel",)),
        name="merge",
    )(x, oa, ob, om, *[wp[k] for k in names])


def _top_values(sc, k, with_rank=False):
    vals = []
    cur = sc
    rank = jnp.full(sc.shape, float(k), F32) if with_rank else None
    for r in range(k):
        m = jnp.max(cur, axis=0, keepdims=True)
        vals.append(m)
        hit = cur == m
        if with_rank:
            rank = jnp.where(hit, float(r), rank)
        if r + 1 < k:
            cur = jnp.where(hit, NEG, cur)
    return (vals, rank) if with_rank else vals


def _stack_rows(rows, sub):
    out = jnp.zeros(sub.shape, F32)
    for r, row in enumerate(rows):
        if row is not None:
            out = jnp.where(sub == r, row, out)
    return out


def _packed_candidates(t1, t2, t2h, sub8):
    assert PEER_TOPK == 16
    down = lambda x, k: pltpu.roll(x, k, 0)
    t1_8_13 = _stack_rows([None, None] + t1[8:14], sub8)
    t1_14_15 = _stack_rows(t1[14:16], sub8)
    top2 = t2[0:1]
    slabs = [
        t1[0] + t2h,
        t1[0] + t2[8:16],
        t1[1] + t2h,
        jnp.where(sub8 < 5, t1[2] + t2h, t1[4] + down(t2h, 5)),
        jnp.where(sub8 < 4, t1[3] + t2h,
                  jnp.where(sub8 < 6, t1[5] + down(t2h, 4), t1[6] + down(t2h, 6))),
        jnp.where(sub8 < 2, t1[7] + t2h, t1_8_13 + top2),
        jnp.where(sub8 < 2, t1_14_15 + top2, NEG),
    ]
    return jnp.concatenate(slabs, axis=0)


EXACT_RANKS = 4
CNT_CAP = PEER_TOPK // (EXACT_RANKS + 1)


def _route_kernel(hn_ref, wq_ref, sk_ref, cnt_ref, e1_ref, rank_ref, e2_ref):
    tt = hn_ref.shape[0]
    q = _dot(hn_ref[...], wq_ref[...]).astype(BF16)
    sub = lax.broadcasted_iota(jnp.int32, (PEER_TOPK, tt), 0)
    sub8 = lax.broadcasted_iota(jnp.int32, (8, tt), 0)
    for h in range(PEER_HEADS):
        sc1 = _dot_nt(sk_ref[2 * h], q[:, (2 * h) * PEER_HALF:(2 * h + 1) * PEER_HALF])
        sc2 = _dot_nt(sk_ref[2 * h + 1], q[:, (2 * h + 1) * PEER_HALF:(2 * h + 2) * PEER_HALF])
        t1 = _top_values(sc1, PEER_TOPK)
        t2_rows, rank2 = _top_values(sc2, PEER_TOPK, with_rank=True)
        t2 = _stack_rows(t2_rows, sub)
        t2h = t2[:8]
        cands = []
        for a in range(PEER_TOPK):
            nb = PEER_TOPK // (a + 1)
            c = t1[a] + (t2 if nb > 8 else t2h)
            if nb < c.shape[0]:
                c = jnp.where((sub if nb > 8 else sub8) < nb, c, NEG)
            cands.append(c)
        cand = _packed_candidates(t1, t2, t2h, sub8)
        tau = _top_values(cand, PEER_TOPK)[-1]
        top = t1[0] + t2[0:1]
        z = jnp.sum(jnp.where(cand >= tau, jnp.exp(cand - top), 0.0), axis=0, keepdims=True)
        counts = [jnp.sum(jnp.where(cands[a] >= tau, 1.0, 0.0), axis=0, keepdims=True) for a in range(PEER_TOPK)]
        cnt = jnp.zeros(sc1.shape, F32)
        for c in range(1, CNT_CAP + 1):
            theta = jnp.full(tau.shape, -NEG, F32)
            for a in range(PEER_TOPK):
                theta = jnp.where(counts[a] >= c, t1[a], theta)
            cnt = jnp.where(sc1 >= theta, float(c), cnt)
        for a in range(EXACT_RANKS):
            cnt = jnp.where(sc1 == t1[a], counts[a], cnt)
        cnt_ref[h] = cnt
        e1_ref[h] = jnp.exp(sc1 - t1[0]) * (0.5 / z)
        rank_ref[h] = rank2.astype(BF16)
        e2_ref[h] = jnp.exp(sc2 - t2[0:1]).astype(BF16)


def _route(hn, wp, *, tt):
    n, d = hn.shape
    big = pl.BlockSpec((PEER_HEADS, PEER_NKEYS, tt), lambda i: (0, 0, i))
    big_shape = jax.ShapeDtypeStruct((PEER_HEADS, PEER_NKEYS, n), F32)
    big_half = jax.ShapeDtypeStruct((PEER_HEADS, PEER_NKEYS, n), BF16)
    return pl.pallas_call(
        _route_kernel,
        grid=(n // tt,),
        in_specs=[pl.BlockSpec((tt, d), lambda i: (i, 0)), _full_spec(wp["peer_wq"].shape),
                  _full_spec(wp["peer_subkeys"].shape)],
        out_specs=(big, big, big, big),
        out_shape=(big_shape, big_shape, big_half, big_half),
        compiler_params=_cparams(("parallel",)),
        name="peer_route",
    )(hn, wp["peer_wq"], wp["peer_subkeys"])


PAIR = 2 * PEER_NKEYS
SUBROWS = 128


def _peer_kernel(hn_in, u_ref, vt_ref, cnt_ref, e1_ref, rank_in, e2_in, x1_ref, gn_ref, y_ref,
                 acc_scr, hw_scr, rank_ref, e2_ref, hn_ref, *, te, ne):
    j = pl.program_id(1)
    tt = hn_in.shape[0]
    npair = te // PAIR

    @pl.when(j == 0)
    def _():
        rank_ref[...] = rank_in[...]
        e2_ref[...] = e2_in[...]
        hn_ref[...] = hn_in[...]

    def pre_act(p):
        a = _dot_nt(u_ref[p * PAIR:(p + 1) * PAIR, :], hn_ref[...])
        return a * (1.0 + lax.erf(a * math.sqrt(0.5)))

    def down(lo, hi):
        return _dot(vt_ref[:, lo * PAIR:hi * PAIR], hw_scr[lo * PAIR:hi * PAIR, :])

    cuts = sorted({0, 3 * npair // 8, 3 * npair // 4, npair - 1, npair})
    piece_after = {hi: lo for lo, hi in zip(cuts[:-1], cuts[1:])}

    groups = [(tc, sb) for tc in range(tt // LANE) for sb in range(PEER_NKEYS // SUBROWS)]
    zero = jnp.zeros((), BF16)
    act = pre_act(0)
    tot = None
    carry = None
    for p in range(npair):
        nxt = None
        for gi, (tc, sb) in enumerate(groups):
            if gi == 1 and p + 1 < npair:
                nxt = pre_act(p + 1)
            if gi == len(groups) // 2 and p in piece_after:
                part = down(piece_after[p], p)
                tot = part if tot is None else tot + part
            ls = slice(tc * LANE, (tc + 1) * LANE)
            i2 = slice(sb * SUBROWS, (sb + 1) * SUBROWS)
            w = [None, None] if carry is None else [carry * zero, carry * zero]
            for h in range(PEER_HEADS):
                rk = rank_ref[h, i2, ls]
                e2b = e2_ref[h, i2, ls]
                for rr in range(2):
                    i1 = 2 * p + rr
                    cnt = jnp.broadcast_to(cnt_ref[h, i1:i1 + 1, ls], (SUBROWS, LANE)).astype(BF16)
                    e1b = jnp.broadcast_to(e1_ref[h, i1:i1 + 1, ls], (SUBROWS, LANE)).astype(BF16)
                    cw = jnp.where(rk < cnt, e2b, zero) * e1b
                    w[rr] = cw if w[rr] is None else w[rr] + cw
            carry = w[1]
            for rr in range(2):
                r0 = rr * PEER_NKEYS + sb * SUBROWS
                hw_scr[p * PAIR + r0:p * PAIR + r0 + SUBROWS, ls] = w[rr] * act[r0:r0 + SUBROWS, ls].astype(BF16)
        act = nxt
    part = down(piece_after[npair], npair)
    tot = part if tot is None else tot + part

    @pl.when(j == 0)
    def _():
        acc_scr[...] = tot

    @pl.when(j > 0)
    def _():
        acc_scr[...] += tot

    @pl.when(j == ne - 1)
    def _():
        xr = x1_ref[...] + acc_scr[...].T
        y_ref[...] = _rms(xr, gn_ref[...])


def _peer(hn, cnt, e1, rank2, e2, x1, wp, *, tt, te):
    n, d = hn.shape
    ne = wp["peer_u"].shape[0] // te
    big = pl.BlockSpec((PEER_HEADS, PEER_NKEYS, tt), lambda i, j: (0, 0, i))
    rows = pl.BlockSpec((PEER_HEADS, te // PEER_NKEYS, tt), lambda i, j: (0, j, i))
    kern = functools.partial(_peer_kernel, te=te, ne=ne)
    return pl.pallas_call(
        kern,
        grid=(n // tt, ne),
        in_specs=[pl.BlockSpec((tt, d), lambda i, j: (i, 0)),
                  pl.BlockSpec((te, d), lambda i, j: (j, 0)),
                  pl.BlockSpec((d, te), lambda i, j: (0, j)),
                  rows, rows, big, big,
                  pl.BlockSpec((tt, d), lambda i, j: (i, 0)),
                  pl.BlockSpec((1, d), lambda i, j: (0, 0))],
        out_specs=pl.BlockSpec((tt, d), lambda i, j: (i, 0)),
        out_shape=jax.ShapeDtypeStruct((n, d), F32),
        scratch_shapes=[pltpu.VMEM((d, tt), F32), pltpu.VMEM((te, tt), BF16),
                        pltpu.VMEM((PEER_HEADS, PEER_NKEYS, tt), BF16),
                        pltpu.VMEM((PEER_HEADS, PEER_NKEYS, tt), BF16),
                        pltpu.VMEM((tt, d), BF16)],
        compiler_params=_cparams(("parallel", "arbitrary")),
        name="peer_experts",
    )(hn, wp["peer_u"], wp["peer_vt"], cnt, e1, rank2, e2, x1, wp["norm_final"])


def _prep_weights(l, p):
    f = lambda a: a.astype(F32)
    w_in = f(p["w_in"][l])
    o_cq, o_ckv, o_kr = 0, MLA_Q_LORA, MLA_Q_LORA + MLA_KV_LORA
    o_dq = o_kr + MLA_ROPE
    half = MLA_ROPE // 2
    d = w_in.shape[0]
    kr = w_in[:, o_kr:o_dq]
    pad = jnp.zeros((d, LANE - MLA_ROPE), F32)
    w_in2 = jnp.concatenate([
        w_in[:, o_cq:o_kr], kr, pad,
        -kr[:, half:], kr[:, :half], pad,
        w_in[:, o_dq:]], axis=1)
    assert w_in2.shape[1] == _C_END

    w_uq = f(p["w_uq"][l]).reshape(MLA_Q_LORA, MLA_HEADS, MLA_NOPE + MLA_ROPE)
    nope, x1, x2 = w_uq[..., :MLA_NOPE], w_uq[..., MLA_NOPE:MLA_NOPE + half], w_uq[..., MLA_NOPE + half:]
    zpad = jnp.zeros((MLA_Q_LORA, MLA_HEADS, HEAD_W - MLA_NOPE - MLA_ROPE), F32)
    q_slab = jnp.concatenate([nope, x1, x2, zpad], axis=-1).reshape(MLA_Q_LORA, -1)
    r_slab = jnp.concatenate([jnp.zeros_like(nope), -x2, x1, zpad], axis=-1).reshape(MLA_Q_LORA, -1)
    w_uq2 = jnp.concatenate([q_slab, r_slab], axis=1)

    w_uk = f(p["w_uk"][l]).reshape(MLA_KV_LORA, MLA_HEADS, MLA_NOPE)
    w_uk2 = jnp.concatenate([w_uk, jnp.zeros((MLA_KV_LORA, MLA_HEADS, HEAD_W - MLA_NOPE), F32)],
                            axis=-1).reshape(MLA_KV_LORA, -1)
    r_idx = jnp.arange(MLA_ROPE)
    cols = jnp.arange(MLA_HEADS * HEAD_W)
    p_kr = ((cols[None, :] % HEAD_W) == (MLA_NOPE + r_idx[:, None])).astype(F32)

    slopes = jnp.exp2(-8.0 * jnp.arange(1, DIFF_HEADS + 1, dtype=F32) / DIFF_HEADS)
    sub = f(p["diff_subln"][l])
    row = lambda a: f(a).reshape(1, -1)
    return {
        "norm_mix": row(p["norm_mix"][l]), "w_in": w_in2.astype(BF16),
        "mla_q_norm": row(p["mla_q_norm"][l]), "w_uq": w_uq2.astype(BF16),
        "mla_kv_norm": row(p["mla_kv_norm"][l]),
        "w_uk": w_uk2.astype(BF16), "p_kr": p_kr.astype(BF16), "w_uv": p["w_uv"][l].astype(BF16),
        "slopes": slopes,
        "diff_lq1": row(p["diff_lq1"][l]), "diff_lk1": row(p["diff_lk1"][l]),
        "diff_lq2": row(p["diff_lq2"][l]), "diff_lk2": row(p["diff_lk2"][l]),
        "diff_subln2": jnp.concatenate([sub, sub]).reshape(1, -1),
        "norm_mem": row(p["norm_mem"][l]), "w_mem_kv": p["w_mem_kv"][l].astype(BF16),
        "w_gate": p["w_gate"][l].astype(BF16), "b_gate": row(p["b_gate"][l]),
        "w_br_a": p["w_br_a"][l].astype(BF16), "w_br_b": p["w_br_b"][l].astype(BF16),
        "w_br_m": p["w_br_m"][l].astype(BF16), "w_o": p["w_o"][l].astype(BF16),
        "norm_ffn": row(p["norm_ffn"][l]),
        "peer_wq": p["peer_wq"][l].astype(BF16),
        "peer_subkeys": p["peer_subkeys"][l].reshape(PEER_HEADS * 2, PEER_NKEYS, PEER_HALF).astype(BF16),
        "peer_u": p["peer_u"][l].astype(BF16), "peer_vt": p["peer_v"][l].T.astype(BF16),
        "norm_final": row(p["norm_final"]),
    }


def _rope_tables(pos):
    half = MLA_ROPE // 2
    inv = 1.0 / (ROPE_THETA ** (np.arange(half, dtype=np.float64) / half))
    ang = pos.astype(np.float64)[:, None] * inv[None, :]
    cos, sin = np.cos(ang), np.sin(ang)
    n = pos.shape[0]
    one = np.ones((n, MLA_NOPE))
    zq = np.zeros((n, HEAD_W - MLA_NOPE - MLA_ROPE))
    cq = np.concatenate([one, cos, cos, zq], axis=1) * MLA_SCALE
    sq = np.concatenate([0.0 * one, sin, sin, zq], axis=1) * MLA_SCALE
    zk = np.zeros((n, LANE - MLA_ROPE))
    ck = np.concatenate([cos, cos, zk], axis=1)
    sk = np.concatenate([sin, sin, zk], axis=1)
    return tuple(t.astype(np.float32) for t in (cq, sq, ck, sk))


def _layer(x, pos, past, mem_k, mem_v, wp, layer, cfg):
    b, s, d = x.shape
    n = b * s
    xf = x.reshape(n, d)
    tabs = _rope_tables(pos)
    if s % cfg["tm"] == 0:
        pos_blocks = s // cfg["tm"]
    else:
        tabs = tuple(np.tile(t, (b, 1)) for t in tabs)
        pos_blocks = n // cfg["tm"]
    ckv, kr, dk, dv, qm, dqb, dkb, dvb, mqb = _proj_in(xf, tabs, wp, tm=cfg["tm"], pos_blocks=pos_blocks)
    new_rows = (ckv.reshape(b, s, -1), kr.reshape(b, s, -1), dk.reshape(b, s, DIFF_HEADS, DIFF_V),
                dv.reshape(b, s, DIFF_HEADS, DIFF_V))
    r3 = lambda a: a.reshape(b, s, -1)
    if past is None:
        ckv_all, kr_all = ckv, kr
        dk_all, dv_all = r3(dkb), r3(dvb)
        kk = s
        q_off = 0
    else:
        p_len = past[0].shape[1]
        kk = p_len + s
        q_off = p_len
        ckv_all = jnp.concatenate([past[0], r3(ckv)], axis=1).reshape(b * kk, -1)
        kr_all = jnp.concatenate([past[1], r3(kr)], axis=1).reshape(b * kk, -1)
        dk_all = jnp.concatenate([past[2].reshape(b, p_len, -1).astype(BF16), r3(dkb)], axis=1)
        dv_all = jnp.concatenate([past[3].reshape(b, p_len, -1).astype(BF16), r3(dvb)], axis=1)
    k_mla, v_mla = _kv_up(ckv_all, kr_all, wp, tm=cfg["tm_kv"])
    att = dict(tk=cfg["tk"] if past is None else kk, q_off=q_off)
    o_a = _mla_attention(r3(qm), k_mla.reshape(b, kk, -1), v_mla.reshape(b, kk, -1), tq=cfg["tq_mla"], **att)
    lam_init = 0.8 - 0.6 * math.exp(-0.3 * layer)
    o_b = _diff_attention(r3(dqb), dk_all, dv_all, wp, lam_init=lam_init, tq=cfg["tq"], **att)
    o_m = _mem_attention(r3(mqb), mem_k, mem_v, tq=cfg["tq"])
    x1, hn = _merge(xf, o_a.reshape(n, -1), o_b.reshape(n, -1), o_m.reshape(n, -1), wp, tm=cfg["tm"])
    return x1, hn, new_rows


def _peer_and_norm(x1, hn, wp, cfg):
    cnt, e1, rank2, e2 = _route(hn, wp, tt=cfg["tt_route"])
    return _peer(hn, cnt, e1, rank2, e2, x1, wp, tt=cfg["tt"], te=cfg["te"])


_CFG_PROMPT = dict(tm=512, tm_kv=512, tq=512, tq_mla=1024, tk=512, tt_route=256, tt=512, te=2048)
_CFG_SAMPLE = dict(tm=256, tm_kv=256, tq=32, tq_mla=32, tk=None, tt_route=256, tt=256, te=2048)


def kernel(x_prompt, x_sample, cache_mla_ckv, cache_mla_krope, cache_diff_k, cache_diff_v, cache_mem_k, cache_mem_v, mem_prompt, norm_mix, w_in, mla_q_norm, w_uq, mla_kv_norm, w_uk, w_uv, diff_lq1, diff_lk1, diff_lq2, diff_lk2, diff_subln, norm_mem, w_mem_kv, w_br_a, w_br_b, w_br_m, w_gate, b_gate, w_o, norm_ffn, peer_wq, peer_subkeys, peer_u, peer_v, norm_final):
    params = dict(norm_mix=norm_mix, w_in=w_in, mla_q_norm=mla_q_norm, w_uq=w_uq, mla_kv_norm=mla_kv_norm,
                  w_uk=w_uk, w_uv=w_uv, diff_lq1=diff_lq1, diff_lk1=diff_lk1, diff_lq2=diff_lq2,
                  diff_lk2=diff_lk2, diff_subln=diff_subln, norm_mem=norm_mem, w_mem_kv=w_mem_kv,
                  w_br_a=w_br_a, w_br_b=w_br_b, w_br_m=w_br_m, w_gate=w_gate, b_gate=b_gate, w_o=w_o,
                  norm_ffn=norm_ffn, peer_wq=peer_wq, peer_subkeys=peer_subkeys, peer_u=peer_u,
                  peer_v=peer_v, norm_final=norm_final)
    depth = w_in.shape[0]
    assert depth == 1, "the final norm is fused into the last PEER step of a single layer"
    bp, sp, d = x_prompt.shape
    bs, ss, _ = x_sample.shape
    n_mem = mem_prompt.shape[1]
    past_len = cache_mla_ckv.shape[2]
    pos_p = np.arange(sp, dtype=np.int32)
    pos_s = past_len + np.arange(ss, dtype=np.int32)

    l = 0
    wp = _prep_weights(l, params)
    mk, mv, mkb, mvb = _mem_kv(mem_prompt.reshape(bp * n_mem, d), wp, tm=n_mem)
    m3 = lambda a, b: a.reshape(b, n_mem, -1)
    x1p, hnp, rows_p = _layer(x_prompt, pos_p, None, m3(mkb, bp), m3(mvb, bp), wp, l, _CFG_PROMPT)
    past = (cache_mla_ckv[l], cache_mla_krope[l], cache_diff_k[l], cache_diff_v[l])
    x1s, hns, rows_s = _layer(x_sample, pos_s, past, m3(cache_mem_k[l].astype(BF16), bs),
                              m3(cache_mem_v[l].astype(BF16), bs), wp, l, _CFG_SAMPLE)
    y_prompt = _peer_and_norm(x1p, hnp, wp, _CFG_PROMPT).reshape(bp, sp, d)
    y_sample = _peer_and_norm(x1s, hns, wp, _CFG_SAMPLE).reshape(bs, ss, d)
    st = lambda a: a[None]
    mem4 = lambda a: a.reshape(bp, n_mem, MEM_HEADS, MEM_DH)[None]
    return (y_prompt, y_sample,
            st(rows_p[0]), st(rows_p[1]), st(rows_p[2]), st(rows_p[3]),
            mem4(mk), mem4(mv),
            st(rows_s[0]), st(rows_s[1]), st(rows_s[2]), st(rows_s[3]))
```

```python
import functools
import math

import jax
import jax.numpy as jnp
import numpy as np
from jax import lax
from jax.experimental import pallas as pl
from jax.experimental.pallas import tpu as pltpu

F32 = jnp.float32
BF16 = jnp.bfloat16

CHUNK = 64
CHUNK_SHIFT = 6
EPS = 1e-6
NEG = -1e30
MLA_HEADS = 8
MLA_Q_LORA = 384
MLA_KV_LORA = 256
MLA_NOPE = 64
MLA_ROPE = 32
MLA_V = 64
ROPE_THETA = 10000.0
LOG2E = math.log2(math.e)
MLA_SCALE = (MLA_NOPE + MLA_ROPE) ** -0.5 * LOG2E
DIFF_HEADS = 8
DIFF_DH = 32
DIFF_V = 2 * DIFF_DH
DIFF_SCALE = DIFF_DH ** -0.5 * LOG2E
MEM_HEADS = 4
MEM_DH = 128
MEM_SCALE = MEM_DH ** -0.5 * LOG2E
PEER_HEADS = 8
PEER_NKEYS = 128
PEER_HALF = 128
PEER_TOPK = 16
LANE = 128
HEAD_W = 128

VMEM_LIMIT = 56 * 1024 * 1024

_C_CQ = 0
_C_CKV = _C_CQ + MLA_Q_LORA
_C_KR = _C_CKV + MLA_KV_LORA
_C_KRR = _C_KR + LANE
_C_DQ = _C_KRR + LANE
_C_DK = _C_DQ + DIFF_HEADS * DIFF_V
_C_DV = _C_DK + DIFF_HEADS * DIFF_V
_C_MQ = _C_DV + DIFF_HEADS * DIFF_V
_C_END = _C_MQ + MEM_HEADS * MEM_DH


def _cparams(sem):
    return pltpu.CompilerParams(dimension_semantics=sem, vmem_limit_bytes=VMEM_LIMIT)


def _rms(x, g):
    return x * lax.rsqrt(jnp.mean(x * x, axis=-1, keepdims=True) + EPS) * g


def _dot(a, b):
    return jnp.dot(a, b, preferred_element_type=F32)


def _dot_nt(a, b):
    return lax.dot_general(a, b, (((1,), (1,)), ((), ())), preferred_element_type=F32)


def _full_spec(shape):
    nd = len(shape)
    return pl.BlockSpec(shape, lambda *_: (0,) * nd)


def _proj_in_kernel(x_ref, g_ref, win_ref, qn_ref, wuq_ref, kvn_ref, cq_ref, sq_ref, ck_ref, sk_ref,
                    ckv_ref, kr_ref, dk_ref, dv_ref, qm_ref, dqb_ref, dkb_ref, dvb_ref, mqb_ref):
    h = _rms(x_ref[...], g_ref[...]).astype(BF16)
    z = _dot(h, win_ref[...])
    cqn = _rms(z[:, _C_CQ:_C_CKV], qn_ref[...]).astype(BF16)
    q2 = _dot(cqn, wuq_ref[...])
    cq = cq_ref[...]
    sq = sq_ref[...]
    nq = MLA_HEADS * HEAD_W
    for hh in range(MLA_HEADS):
        lo = hh * HEAD_W
        qm_ref[:, lo:lo + HEAD_W] = (q2[:, lo:lo + HEAD_W] * cq
                                     + q2[:, nq + lo:nq + lo + HEAD_W] * sq).astype(BF16)
    ckv_ref[...] = _rms(z[:, _C_CKV:_C_KR], kvn_ref[...])
    kr = z[:, _C_KR:_C_KRR] * ck_ref[...] + z[:, _C_KRR:_C_DQ] * sk_ref[...]
    kr_ref[...] = kr[:, :MLA_ROPE]
    dk = z[:, _C_DK:_C_DV]
    dv = z[:, _C_DV:_C_MQ]
    dk_ref[...] = dk.reshape(dk_ref.shape)
    dv_ref[...] = dv.reshape(dv_ref.shape)
    dkb_ref[...] = dk.astype(BF16)
    dvb_ref[...] = dv.astype(BF16)
    dqb_ref[...] = (z[:, _C_DQ:_C_DK] * DIFF_SCALE).astype(BF16)
    mqb_ref[...] = (z[:, _C_MQ:_C_END] * MEM_SCALE).astype(BF16)


def _proj_in(x, tabs, wp, *, tm, pos_blocks):
    n, d = x.shape
    grid = (n // tm,)
    row = lambda w: pl.BlockSpec((tm, w), lambda i: (i, 0))
    tab = pl.BlockSpec((tm, LANE), lambda i: (i % pos_blocks, 0))
    heads = pl.BlockSpec((tm, DIFF_HEADS, DIFF_V), lambda i: (i, 0, 0))
    dw = DIFF_HEADS * DIFF_V
    out_shape = (
        jax.ShapeDtypeStruct((n, MLA_KV_LORA), F32),
        jax.ShapeDtypeStruct((n, MLA_ROPE), F32),
        jax.ShapeDtypeStruct((n, DIFF_HEADS, DIFF_V), F32),
        jax.ShapeDtypeStruct((n, DIFF_HEADS, DIFF_V), F32),
        jax.ShapeDtypeStruct((n, MLA_HEADS * HEAD_W), BF16),
        jax.ShapeDtypeStruct((n, dw), BF16),
        jax.ShapeDtypeStruct((n, dw), BF16),
        jax.ShapeDtypeStruct((n, dw), BF16),
        jax.ShapeDtypeStruct((n, MEM_HEADS * MEM_DH), BF16),
    )
    return pl.pallas_call(
        _proj_in_kernel,
        grid=grid,
        in_specs=[row(d), _full_spec((1, d)), _full_spec(wp["w_in"].shape), _full_spec((1, MLA_Q_LORA)),
                  _full_spec(wp["w_uq"].shape), _full_spec((1, MLA_KV_LORA)), tab, tab, tab, tab],
        out_specs=(row(MLA_KV_LORA), row(MLA_ROPE), heads, heads, row(MLA_HEADS * HEAD_W),
                   row(dw), row(dw), row(dw), row(MEM_HEADS * MEM_DH)),
        out_shape=out_shape,
        compiler_params=_cparams(("parallel",)),
        name="proj_in",
    )(x, wp["norm_mix"], wp["w_in"], wp["mla_q_norm"], wp["w_uq"], wp["mla_kv_norm"], *tabs)


def _kv_up_kernel(ckv_ref, kr_ref, wuk_ref, pk_ref, wuv_ref, k_ref, v_ref):
    c = ckv_ref[...].astype(BF16)
    k = _dot(c, wuk_ref[...]) + _dot(kr_ref[...].astype(BF16), pk_ref[...])
    k_ref[...] = k.astype(BF16)
    v_ref[...] = _dot(c, wuv_ref[...]).astype(BF16)


def _kv_up(ckv, kr, wp, *, tm):
    n = ckv.shape[0]
    row = lambda w: pl.BlockSpec((tm, w), lambda i: (i, 0))
    return pl.pallas_call(
        _kv_up_kernel,
        grid=(n // tm,),
        in_specs=[row(MLA_KV_LORA), row(MLA_ROPE), _full_spec(wp["w_uk"].shape),
                  _full_spec(wp["p_kr"].shape), _full_spec(wp["w_uv"].shape)],
        out_specs=(row(MLA_HEADS * HEAD_W), row(MLA_HEADS * MLA_V)),
        out_shape=(jax.ShapeDtypeStruct((n, MLA_HEADS * HEAD_W), BF16),
                   jax.ShapeDtypeStruct((n, MLA_HEADS * MLA_V), BF16)),
        compiler_params=_cparams(("parallel",)),
        name="kv_up",
    )(ckv, kr, wp["w_uk"], wp["p_kr"], wp["w_uv"])


def _lanes(col, n):
    if n == LANE:
        return col
    if n % LANE == 0:
        return jnp.concatenate([col] * (n // LANE), axis=1)
    return jnp.broadcast_to(col[:, :1], (col.shape[0], n))


GROUP_VREGS = 16


def _softmax_block(score_group, tq, tk, row0, m_scr, al_scr, p_scr, l_scr=None):
    rg = min(tq, max(16, (GROUP_VREGS * 8 * LANE // tk) // 16 * 16))
    for g in range(tq // rg):
        r = slice(row0 + g * rg, row0 + (g + 1) * rg)
        m_prev = m_scr[r]
        m_next = jnp.maximum(m_prev, jnp.max(score_group(g, rg), axis=1, keepdims=True))
        al_scr[r] = jnp.exp2(m_prev - m_next)
        m_scr[r] = m_next
    for g in range(tq // rg):
        r = slice(row0 + g * rg, row0 + (g + 1) * rg)
        e = jnp.exp2(score_group(g, rg) - _lanes(m_scr[r], tk))
        p_scr[r] = e.astype(BF16)
        if l_scr is not None:
            l_scr[r] = al_scr[r] * l_scr[r] + jnp.sum(e, axis=1, keepdims=True)


def _accumulate(rows, v, al_scr, l_scr, acc_scr, p_scr):
    pv = _dot(p_scr[rows], jnp.concatenate([v, jnp.ones(v.shape, v.dtype)], axis=1))
    w = v.shape[1]
    acc_scr[rows] = acc_scr[rows] * al_scr[rows] + pv[:, :w]
    l_scr[rows] = l_scr[rows] * al_scr[rows] + pv[:, w:]


def _chunk_visible(q_lo, k_lo, g, rg, tk):
    qp = q_lo + g * rg + lax.broadcasted_iota(jnp.int32, (rg, tk), 0)
    kp = k_lo + lax.broadcasted_iota(jnp.int32, (rg, tk), 1)
    return qp, kp, (kp >> CHUNK_SHIFT) <= (qp >> CHUNK_SHIFT)


def _init_stats(m_scr, l_scr, acc_scr):
    m_scr[...] = jnp.full(m_scr.shape, NEG, F32)
    l_scr[...] = jnp.zeros(l_scr.shape, F32)
    acc_scr[...] = jnp.zeros(acc_scr.shape, F32)


def _block_range(qi, *, tq, tk, q_off, nk):
    q_lo = q_off + qi * tq
    q_hi = q_lo + (tq - 1)
    n_behind = jnp.minimum((q_lo + 1) // tk, nk)
    last_key = ((q_hi >> CHUNK_SHIFT) << CHUNK_SHIFT) + (CHUNK - 1)
    n_need = jnp.minimum(last_key // tk + 1, nk)
    return q_lo, n_behind, n_need


def _for_blocks(lo, hi, fn):
    lax.fori_loop(lo, hi, lambda i, c: (fn(i), c)[1], 0)


def _attn_specs(b, pairs, sq, sk, tq, qw, kw, vw):
    grid = (b, pairs, sq // tq)
    in_specs = [pl.BlockSpec((1, tq, qw), lambda bi, p, qi: (bi, qi, p)),
                pl.BlockSpec((1, sk, kw), lambda bi, p, qi: (bi, 0, p)),
                pl.BlockSpec((1, sk, vw), lambda bi, p, qi: (bi, 0, p))]
    out_spec = pl.BlockSpec((1, tq, vw), lambda bi, p, qi: (bi, qi, p))
    return grid, in_specs, out_spec


def _mla_kernel(q_ref, k_ref, v_ref, o_ref, m_scr, l_scr, al_scr, acc_scr, p_scr, *, tq, tk, q_off, nk):
    qi = pl.program_id(2)
    _init_stats(m_scr, l_scr, acc_scr)
    q_lo, n_behind, n_need = _block_range(qi, tq=tq, tk=tk, q_off=q_off, nk=nk)

    def block(i, masked):
        k_lo = pl.multiple_of(i * tk, tk)
        for hh in range(2):
            hs = slice(hh * HEAD_W, (hh + 1) * HEAD_W)
            s = _dot_nt(q_ref[0, :, hs], k_ref[0, pl.ds(k_lo, tk), hs])

            def score_group(g, rg, s=s):
                sg = s[g * rg:(g + 1) * rg]
                if masked:
                    sg = jnp.where(_chunk_visible(q_lo, k_lo, g, rg, tk)[2], sg, NEG)
                return sg

            _softmax_block(score_group, tq, tk, hh * tq, m_scr, al_scr, p_scr)
        _accumulate(slice(0, 2 * tq), v_ref[0, pl.ds(k_lo, tk), :], al_scr, l_scr, acc_scr, p_scr)

    _for_blocks(0, n_behind, lambda i: block(i, False))
    _for_blocks(n_behind, n_need, lambda i: block(i, True))

    lane = lax.broadcasted_iota(jnp.int32, (tq, LANE), 1)
    o0 = acc_scr[0:tq] / l_scr[0:tq]
    o1 = acc_scr[tq:2 * tq] / l_scr[tq:2 * tq]
    o_ref[0] = jnp.where(lane < MLA_V, o0, o1).astype(o_ref.dtype)


def _mla_attention(q, k, v, *, tq, tk, q_off):
    b, sq, _ = q.shape
    sk = k.shape[1]
    grid, in_specs, out_spec = _attn_specs(b, MLA_HEADS // 2, sq, sk, tq, 2 * HEAD_W, 2 * HEAD_W, 2 * MLA_V)
    kern = functools.partial(_mla_kernel, tq=tq, tk=tk, q_off=q_off, nk=sk // tk)
    stat = pltpu.VMEM((2 * tq, LANE), F32)
    return pl.pallas_call(
        kern,
        grid=grid,
        in_specs=in_specs,
        out_specs=out_spec,
        out_shape=jax.ShapeDtypeStruct((b, sq, MLA_HEADS * MLA_V), BF16),
        scratch_shapes=[stat, stat, stat, stat, pltpu.VMEM((2 * tq, tk), BF16)],
        compiler_params=_cparams(("parallel", "parallel", "parallel")),
        name="mla_attn",
    )(q, k, v)


def _diff_kernel(slope_ref, lq1_ref, lk1_ref, lq2_ref, lk2_ref, sub_ref, q_ref, k_ref, v_ref, o_ref,
                 qs_scr, m_scr, l_scr, al_scr, acc_scr, p_scr, *, tq, tk, q_off, nk, lam_init):
    pr = pl.program_id(1)
    qi = pl.program_id(2)
    _init_stats(m_scr, l_scr, acc_scr)
    q_lo, n_behind, n_need = _block_range(qi, tq=tq, tk=tk, q_off=q_off, nk=nk)

    q = q_ref[0]
    lane = lax.broadcasted_iota(jnp.int32, (tq, LANE), 1)
    for mi in range(4):
        lo = mi * DIFF_DH
        qs_scr[mi * tq:(mi + 1) * tq] = jnp.where((lane >= lo) & (lane < lo + DIFF_DH), q, jnp.zeros_like(q))

    def block(i, masked):
        k_lo = pl.multiple_of(i * tk, tk)
        kb = k_ref[0, pl.ds(k_lo, tk), :]
        kpos = (k_lo + lax.broadcasted_iota(jnp.int32, (1, tk), 1)).astype(F32)

        def scores(mi):
            return _dot_nt(qs_scr[mi * tq:(mi + 1) * tq], kb)

        def softmax(mi, s_map):
            slope = slope_ref[2 * pr + mi // 2] * LOG2E

            def score_group(g, rg):
                sg = s_map[g * rg:(g + 1) * rg]
                if masked:
                    qp, kp, vis = _chunk_visible(q_lo, k_lo, g, rg, tk)
                    return jnp.where(vis, sg + slope * jnp.minimum(kp, 2 * qp - kp).astype(F32), NEG)
                return sg + slope * kpos

            _softmax_block(score_group, tq, tk, mi * tq, m_scr, al_scr, p_scr, l_scr)

        def values(mi):
            rows = slice(mi * tq, (mi + 1) * tq)
            acc_scr[rows] = acc_scr[rows] * al_scr[rows] + _dot(p_scr[rows], v_ref[0, pl.ds(k_lo, tk), :])

        s0 = scores(0)
        s1 = scores(1)
        softmax(0, s0)
        s2 = scores(2)
        values(0)
        softmax(1, s1)
        s3 = scores(3)
        values(1)
        softmax(2, s2)
        values(2)
        softmax(3, s3)
        values(3)

    _for_blocks(0, n_behind, lambda i: block(i, False))
    _for_blocks(n_behind, n_need, lambda i: block(i, True))

    lam = (jnp.exp(jnp.sum(lq1_ref[...] * lk1_ref[...], axis=1, keepdims=True))
           - jnp.exp(jnp.sum(lq2_ref[...] * lk2_ref[...], axis=1, keepdims=True)) + lam_init)
    first = lane < DIFF_V
    on = [acc_scr[mi * tq:(mi + 1) * tq] / l_scr[mi * tq:(mi + 1) * tq] for mi in range(4)]
    o = jnp.where(first, on[0] - lam * on[1], on[2] - lam * on[3])
    sq = o * o
    ms0 = jnp.sum(jnp.where(first, sq, 0.0), axis=1, keepdims=True) * (1.0 / DIFF_V)
    ms1 = jnp.sum(jnp.where(first, 0.0, sq), axis=1, keepdims=True) * (1.0 / DIFF_V)
    r = jnp.where(first, lax.rsqrt(ms0 + EPS), lax.rsqrt(ms1 + EPS))
    o_ref[0] = ((o * r * sub_ref[...]) * (1.0 - lam_init)).astype(o_ref.dtype)


def _diff_attention(q, k, v, wp, *, tq, tk, q_off, lam_init):
    b, sq, _ = q.shape
    sk = k.shape[1]
    grid, in_specs, out_spec = _attn_specs(b, DIFF_HEADS // 2, sq, sk, tq, LANE, LANE, LANE)
    kern = functools.partial(_diff_kernel, tq=tq, tk=tk, q_off=q_off, nk=sk // tk, lam_init=lam_init)
    small = lambda w: pl.BlockSpec((1, w), lambda bi, p, qi: (0, 0))
    stat = pltpu.VMEM((4 * tq, LANE), F32)
    return pl.pallas_call(
        kern,
        grid=grid,
        in_specs=[pl.BlockSpec(memory_space=pltpu.SMEM),
                  small(DIFF_DH), small(DIFF_DH), small(DIFF_DH), small(DIFF_DH), small(LANE)] + in_specs,
        out_specs=out_spec,
        out_shape=jax.ShapeDtypeStruct((b, sq, DIFF_HEADS * DIFF_V), BF16),
        scratch_shapes=[pltpu.VMEM((4 * tq, LANE), BF16), stat, stat, stat, stat,
                        pltpu.VMEM((4 * tq, tk), BF16)],
        compiler_params=_cparams(("parallel", "parallel", "parallel")),
        name="diff_attn",
    )(wp["slopes"], wp["diff_lq1"], wp["diff_lk1"], wp["diff_lq2"], wp["diff_lk2"], wp["diff_subln2"],
      q, k, v)


def _mem_attn_kernel(q_ref, k_ref, v_ref, o_ref):
    for hh in range(MEM_HEADS):
        sl = slice(hh * MEM_DH, (hh + 1) * MEM_DH)
        s = _dot_nt(q_ref[0, :, sl], k_ref[0, :, sl])
        p = jnp.exp2(s - jnp.max(s, axis=1, keepdims=True))
        o = _dot(p.astype(BF16), v_ref[0, :, sl]) / jnp.sum(p, axis=1, keepdims=True)
        o_ref[0, :, sl] = o.astype(o_ref.dtype)


def _mem_attention(q, k, v, *, tq):
    b, sq, w = q.shape
    nm = k.shape[1]
    return pl.pallas_call(
        _mem_attn_kernel,
        grid=(b, sq // tq),
        in_specs=[pl.BlockSpec((1, tq, w), lambda bi, qi: (bi, qi, 0)),
                  pl.BlockSpec((1, nm, w), lambda bi, qi: (bi, 0, 0)),
                  pl.BlockSpec((1, nm, w), lambda bi, qi: (bi, 0, 0))],
        out_specs=pl.BlockSpec((1, tq, w), lambda bi, qi: (bi, qi, 0)),
        out_shape=jax.ShapeDtypeStruct((b, sq, w), BF16),
        compiler_params=_cparams(("parallel", "parallel")),
        name="mem_attn",
    )(q, k, v)


def _mem_kv_kernel(x_ref, g_ref, w_ref, k_ref, v_ref, kb_ref, vb_ref):
    h = _rms(x_ref[...], g_ref[...]).astype(BF16)
    kv = _dot(h, w_ref[...])
    w = MEM_HEADS * MEM_DH
    k_ref[...] = kv[:, :w]
    v_ref[...] = kv[:, w:]
    kb_ref[...] = kv[:, :w].astype(BF16)
    vb_ref[...] = kv[:, w:].astype(BF16)


def _mem_kv(mem, wp, *, tm):
    n, d = mem.shape
    w = MEM_HEADS * MEM_DH
    row = lambda c: pl.BlockSpec((tm, c), lambda i: (i, 0))
    return pl.pallas_call(
        _mem_kv_kernel,
        grid=(n // tm,),
        in_specs=[row(d), _full_spec((1, d)), _full_spec(wp["w_mem_kv"].shape)],
        out_specs=(row(w), row(w), row(w), row(w)),
        out_shape=(jax.ShapeDtypeStruct((n, w), F32), jax.ShapeDtypeStruct((n, w), F32),
                   jax.ShapeDtypeStruct((n, w), BF16), jax.ShapeDtypeStruct((n, w), BF16)),
        compiler_params=_cparams(("parallel",)),
        name="mem_kv",
    )(mem, wp["norm_mem"], wp["w_mem_kv"])


def _merge_kernel(x_ref, oa_ref, ob_ref, om_ref, g_ref, wg_ref, bg_ref, wa_ref, wb_ref, wm_ref, wo_ref,
                  gf_ref, x1_ref, hn_ref):
    x = x_ref[...]
    d = x.shape[1]
    h = _rms(x, g_ref[...]).astype(BF16)
    gates = jax.nn.sigmoid(_dot(h, wg_ref[...]) + bg_ref[...])
    merged = (gates[:, :d] * _dot(oa_ref[...], wa_ref[...])
              + gates[:, d:2 * d] * _dot(ob_ref[...], wb_ref[...])
              + gates[:, 2 * d:] * _dot(om_ref[...], wm_ref[...]))
    x1 = x + _dot(merged.astype(BF16), wo_ref[...])
    x1_ref[...] = x1
    hn_ref[...] = _rms(x1, gf_ref[...]).astype(BF16)


def _merge(x, oa, ob, om, wp, *, tm):
    n, d = x.shape
    row = lambda c: pl.BlockSpec((tm, c), lambda i: (i, 0))
    names = ("norm_mix", "w_gate", "b_gate", "w_br_a", "w_br_b", "w_br_m", "w_o", "norm_ffn")
    return pl.pallas_call(
        _merge_kernel,
        grid=(n // tm,),
        in_specs=[row(d), row(oa.shape[1]), row(ob.shape[1]), row(om.shape[1])]
                 + [_full_spec(wp[k].shape) for k in names],
        out_specs=(row(d), row(d)),
        out_shape=(jax.ShapeDtypeStruct((n, d), F32), jax.ShapeDtypeStruct((n, d), BF16)),
        compiler_params=_cparams(("parallel",)),
        name="merge",
    )(x, oa, ob, om, *[wp[k] for k in names])


def _top_values(sc, k, with_rank=False):
    vals = []
    cur = sc
    rank = jnp.full(sc.shape, float(k), F32) if with_rank else None
    for r in range(k):
        m = jnp.max(cur, axis=0, keepdims=True)
        vals.append(m)
        hit = cur == m
        if with_rank:
            rank = jnp.where(hit, float(r), rank)
        if r + 1 < k:
            cur = jnp.where(hit, NEG, cur)
    return (vals, rank) if with_rank else vals


def _stack_rows(rows, sub):
    out = jnp.zeros(sub.shape, F32)
    for r, row in enumerate(rows):
        if row is not None:
            out = jnp.where(sub == r, row, out)
    return out


def _packed_candidates(t1, t2, t2h, sub8):
    assert PEER_TOPK == 16
    down = lambda x, k: pltpu.roll(x, k, 0)
    t1_8_13 = _stack_rows([None, None] + t1[8:14], sub8)
    t1_14_15 = _stack_rows(t1[14:16], sub8)
    top2 = t2[0:1]
    slabs = [
        t1[0] + t2h,
        t1[0] + t2[8:16],
        t1[1] + t2h,
        jnp.where(sub8 < 5, t1[2] + t2h, t1[4] + down(t2h, 5)),
        jnp.where(sub8 < 4, t1[3] + t2h,
                  jnp.where(sub8 < 6, t1[5] + down(t2h, 4), t1[6] + down(t2h, 6))),
        jnp.where(sub8 < 2, t1[7] + t2h, t1_8_13 + top2),
        jnp.where(sub8 < 2, t1_14_15 + top2, NEG),
    ]
    return jnp.concatenate(slabs, axis=0)


EXACT_RANKS = 4
CNT_CAP = PEER_TOPK // (EXACT_RANKS + 1)


def _route_kernel(hn_ref, wq_ref, sk_ref, cnt_ref, e1_ref, rank_ref, e2_ref):
    tt = hn_ref.shape[0]
    q = _dot(hn_ref[...], wq_ref[...]).astype(BF16)
    sub = lax.broadcasted_iota(jnp.int32, (PEER_TOPK, tt), 0)
    sub8 = lax.broadcasted_iota(jnp.int32, (8, tt), 0)
    for h in range(PEER_HEADS):
        sc1 = _dot_nt(sk_ref[2 * h], q[:, (2 * h) * PEER_HALF:(2 * h + 1) * PEER_HALF])
        sc2 = _dot_nt(sk_ref[2 * h + 1], q[:, (2 * h + 1) * PEER_HALF:(2 * h + 2) * PEER_HALF])
        t1 = _top_values(sc1, PEER_TOPK)
        t2_rows, rank2 = _top_values(sc2, PEER_TOPK, with_rank=True)
        t2 = _stack_rows(t2_rows, sub)
        t2h = t2[:8]
        cands = []
        for a in range(PEER_TOPK):
            nb = PEER_TOPK // (a + 1)
            c = t1[a] + (t2 if nb > 8 else t2h)
            if nb < c.shape[0]:
                c = jnp.where((sub if nb > 8 else sub8) < nb, c, NEG)
            cands.append(c)
        cand = _packed_candidates(t1, t2, t2h, sub8)
        tau = _top_values(cand, PEER_TOPK)[-1]
        top = t1[0] + t2[0:1]
        z = jnp.sum(jnp.where(cand >= tau, jnp.exp(cand - top), 0.0), axis=0, keepdims=True)
        counts = [jnp.sum(jnp.where(cands[a] >= tau, 1.0, 0.0), axis=0, keepdims=True) for a in range(PEER_TOPK)]
        cnt = jnp.zeros(sc1.shape, F32)
        for c in range(1, CNT_CAP + 1):
            theta = jnp.full(tau.shape, -NEG, F32)
            for a in range(PEER_TOPK):
                theta = jnp.where(counts[a] >= c, t1[a], theta)
            cnt = jnp.where(sc1 >= theta, float(c), cnt)
        for a in range(EXACT_RANKS):
            cnt = jnp.where(sc1 == t1[a], counts[a], cnt)
        cnt_ref[h] = cnt
        e1_ref[h] = jnp.exp(sc1 - t1[0]) * (0.5 / z)
        rank_ref[h] = rank2.astype(BF16)
        e2_ref[h] = jnp.exp(sc2 - t2[0:1]).astype(BF16)


def _route(hn, wp, *, tt):
    n, d = hn.shape
    big = pl.BlockSpec((PEER_HEADS, PEER_NKEYS, tt), lambda i: (0, 0, i))
    big_shape = jax.ShapeDtypeStruct((PEER_HEADS, PEER_NKEYS, n), F32)
    big_half = jax.ShapeDtypeStruct((PEER_HEADS, PEER_NKEYS, n), BF16)
    return pl.pallas_call(
        _route_kernel,
        grid=(n // tt,),
        in_specs=[pl.BlockSpec((tt, d), lambda i: (i, 0)), _full_spec(wp["peer_wq"].shape),
                  _full_spec(wp["peer_subkeys"].shape)],
        out_specs=(big, big, big, big),
        out_shape=(big_shape, big_shape, big_half, big_half),
        compiler_params=_cparams(("parallel",)),
        name="peer_route",
    )(hn, wp["peer_wq"], wp["peer_subkeys"])


PAIR = 2 * PEER_NKEYS
SUBROWS = 128


def _peer_kernel(hn_in, u_ref, vt_ref, cnt_ref, e1_ref, rank_in, e2_in, x1_ref, gn_ref, y_ref,
                 acc_scr, hw_scr, rank_ref, e2_ref, hn_ref, *, te, ne):
    j = pl.program_id(1)
    tt = hn_in.shape[0]
    npair = te // PAIR

    @pl.when(j == 0)
    def _():
        rank_ref[...] = rank_in[...]
        e2_ref[...] = e2_in[...]
        hn_ref[...] = hn_in[...]

    def pre_act(p):
        a = _dot_nt(u_ref[p * PAIR:(p + 1) * PAIR, :], hn_ref[...])
        return a * (1.0 + lax.erf(a * math.sqrt(0.5)))

    def down(lo, hi):
        return _dot(vt_ref[:, lo * PAIR:hi * PAIR], hw_scr[lo * PAIR:hi * PAIR, :])

    cuts = sorted({0, 3 * npair // 8, 3 * npair // 4, npair - 1, npair})
    piece_after = {hi: lo for lo, hi in zip(cuts[:-1], cuts[1:])}

    groups = [(tc, sb) for tc in range(tt // LANE) for sb in range(PEER_NKEYS // SUBROWS)]
    zero = jnp.zeros((), BF16)
    act = pre_act(0)
    tot = None
    carry = None
    for p in range(npair):
        nxt = None
        for gi, (tc, sb) in enumerate(groups):
            if gi == 1 and p + 1 < npair:
                nxt = pre_act(p + 1)
            if gi == len(groups) // 2 and p in piece_after:
                part = down(piece_after[p], p)
                tot = part if tot is None else tot + part
            ls = slice(tc * LANE, (tc + 1) * LANE)
            i2 = slice(sb * SUBROWS, (sb + 1) * SUBROWS)
            w = [None, None] if carry is None else [carry * zero, carry * zero]
            for h in range(PEER_HEADS):
                rk = rank_ref[h, i2, ls]
                e2b = e2_ref[h, i2, ls]
                for rr in range(2):
                    i1 = 2 * p + rr
                    cnt = jnp.broadcast_to(cnt_ref[h, i1:i1 + 1, ls], (SUBROWS, LANE)).astype(BF16)
                    e1b = jnp.broadcast_to(e1_ref[h, i1:i1 + 1, ls], (SUBROWS, LANE)).astype(BF16)
                    cw = jnp.where(rk < cnt, e2b, zero) * e1b
                    w[rr] = cw if w[rr] is None else w[rr] + cw
            carry = w[1]
            for rr in range(2):
                r0 = rr * PEER_NKEYS + sb * SUBROWS
                hw_scr[p * PAIR + r0:p * PAIR + r0 + SUBROWS, ls] = w[rr] * act[r0:r0 + SUBROWS, ls].astype(BF16)
        act = nxt
    part = down(piece_after[npair], npair)
    tot = part if tot is None else tot + part

    @pl.when(j == 0)
    def _():
        acc_scr[...] = tot

    @pl.when(j > 0)
    def _():
        acc_scr[...] += tot

    @pl.when(j == ne - 1)
    def _():
        xr = x1_ref[...] + acc_scr[...].T
        y_ref[...] = _rms(xr, gn_ref[...])


def _peer(hn, cnt, e1, rank2, e2, x1, wp, *, tt, te):
    n, d = hn.shape
    ne = wp["peer_u"].shape[0] // te
    big = pl.BlockSpec((PEER_HEADS, PEER_NKEYS, tt), lambda i, j: (0, 0, i))
    rows = pl.BlockSpec((PEER_HEADS, te // PEER_NKEYS, tt), lambda i, j: (0, j, i))
    kern = functools.partial(_peer_kernel, te=te, ne=ne)
    return pl.pallas_call(
        kern,
        grid=(n // tt, ne),
        in_specs=[pl.BlockSpec((tt, d), lambda i, j: (i, 0)),
                  pl.BlockSpec((te, d), lambda i, j: (j, 0)),
                  pl.BlockSpec((d, te), lambda i, j: (0, j)),
                  rows, rows, big, big,
                  pl.BlockSpec((tt, d), lambda i, j: (i, 0)),
                  pl.BlockSpec((1, d), lambda i, j: (0, 0))],
        out_specs=pl.BlockSpec((tt, d), lambda i, j: (i, 0)),
        out_shape=jax.ShapeDtypeStruct((n, d), F32),
        scratch_shapes=[pltpu.VMEM((d, tt), F32), pltpu.VMEM((te, tt), BF16),
                        pltpu.VMEM((PEER_HEADS, PEER_NKEYS, tt), BF16),
                        pltpu.VMEM((PEER_HEADS, PEER_NKEYS, tt), BF16),
                        pltpu.VMEM((tt, d), BF16)],
        compiler_params=_cparams(("parallel", "arbitrary")),
        name="peer_experts",
    )(hn, wp["peer_u"], wp["peer_vt"], cnt, e1, rank2, e2, x1, wp["norm_final"])


def _prep_weights(l, p):
    f = lambda a: a.astype(F32)
    w_in = f(p["w_in"][l])
    o_cq, o_ckv, o_kr = 0, MLA_Q_LORA, MLA_Q_LORA + MLA_KV_LORA
    o_dq = o_kr + MLA_ROPE
    half = MLA_ROPE // 2
    d = w_in.shape[0]
    kr = w_in[:, o_kr:o_dq]
    pad = jnp.zeros((d, LANE - MLA_ROPE), F32)
    w_in2 = jnp.concatenate([
        w_in[:, o_cq:o_kr], kr, pad,
        -kr[:, half:], kr[:, :half], pad,
        w_in[:, o_dq:]], axis=1)
    assert w_in2.shape[1] == _C_END

    w_uq = f(p["w_uq"][l]).reshape(MLA_Q_LORA, MLA_HEADS, MLA_NOPE + MLA_ROPE)
    nope, x1, x2 = w_uq[..., :MLA_NOPE], w_uq[..., MLA_NOPE:MLA_NOPE + half], w_uq[..., MLA_NOPE + half:]
    zpad = jnp.zeros((MLA_Q_LORA, MLA_HEADS, HEAD_W - MLA_NOPE - MLA_ROPE), F32)
    q_slab = jnp.concatenate([nope, x1, x2, zpad], axis=-1).reshape(MLA_Q_LORA, -1)
    r_slab = jnp.concatenate([jnp.zeros_like(nope), -x2, x1, zpad], axis=-1).reshape(MLA_Q_LORA, -1)
    w_uq2 = jnp.concatenate([q_slab, r_slab], axis=1)

    w_uk = f(p["w_uk"][l]).reshape(MLA_KV_LORA, MLA_HEADS, MLA_NOPE)
    w_uk2 = jnp.concatenate([w_uk, jnp.zeros((MLA_KV_LORA, MLA_HEADS, HEAD_W - MLA_NOPE), F32)],
                            axis=-1).reshape(MLA_KV_LORA, -1)
    r_idx = jnp.arange(MLA_ROPE)
    cols = jnp.arange(MLA_HEADS * HEAD_W)
    p_kr = ((cols[None, :] % HEAD_W) == (MLA_NOPE + r_idx[:, None])).astype(F32)

    slopes = jnp.exp2(-8.0 * jnp.arange(1, DIFF_HEADS + 1, dtype=F32) / DIFF_HEADS)
    sub = f(p["diff_subln"][l])
    row = lambda a: f(a).reshape(1, -1)
    return {
        "norm_mix": row(p["norm_mix"][l]), "w_in": w_in2.astype(BF16),
        "mla_q_norm": row(p["mla_q_norm"][l]), "w_uq": w_uq2.astype(BF16),
        "mla_kv_norm": row(p["mla_kv_norm"][l]),
        "w_uk": w_uk2.astype(BF16), "p_kr": p_kr.astype(BF16), "w_uv": p["w_uv"][l].astype(BF16),
        "slopes": slopes,
        "diff_lq1": row(p["diff_lq1"][l]), "diff_lk1": row(p["diff_lk1"][l]),
        "diff_lq2": row(p["diff_lq2"][l]), "diff_lk2": row(p["diff_lk2"][l]),
        "diff_subln2": jnp.concatenate([sub, sub]).reshape(1, -1),
        "norm_mem": row(p["norm_mem"][l]), "w_mem_kv": p["w_mem_kv"][l].astype(BF16),
        "w_gate": p["w_gate"][l].astype(BF16), "b_gate": row(p["b_gate"][l]),
        "w_br_a": p["w_br_a"][l].astype(BF16), "w_br_b": p["w_br_b"][l].astype(BF16),
        "w_br_m": p["w_br_m"][l].astype(BF16), "w_o": p["w_o"][l].astype(BF16),
        "norm_ffn": row(p["norm_ffn"][l]),
        "peer_wq": p["peer_wq"][l].astype(BF16),
        "peer_subkeys": p["peer_subkeys"][l].reshape(PEER_HEADS * 2, PEER_NKEYS, PEER_HALF).astype(BF16),
        "peer_u": p["peer_u"][l].astype(BF16), "peer_vt": p["peer_v"][l].T.astype(BF16),
        "norm_final": row(p["norm_final"]),
    }


def _rope_tables(pos):
    half = MLA_ROPE // 2
    inv = 1.0 / (ROPE_THETA ** (np.arange(half, dtype=np.float64) / half))
    ang = pos.astype(np.float64)[:, None] * inv[None, :]
    cos, sin = np.cos(ang), np.sin(ang)
    n = pos.shape[0]
    one = np.ones((n, MLA_NOPE))
    zq = np.zeros((n, HEAD_W - MLA_NOPE - MLA_ROPE))
    cq = np.concatenate([one, cos, cos, zq], axis=1) * MLA_SCALE
    sq = np.concatenate([0.0 * one, sin, sin, zq], axis=1) * MLA_SCALE
    zk = np.zeros((n, LANE - MLA_ROPE))
    ck = np.concatenate([cos, cos, zk], axis=1)
    sk = np.concatenate([sin, sin, zk], axis=1)
    return tuple(t.astype(np.float32) for t in (cq, sq, ck, sk))


def _layer(x, pos, past, mem_k, mem_v, wp, layer, cfg):
    b, s, d = x.shape
    n = b * s
    xf = x.reshape(n, d)
    tabs = _rope_tables(pos)
    if s % cfg["tm"] == 0:
        pos_blocks = s // cfg["tm"]
    else:
        tabs = tuple(np.tile(t, (b, 1)) for t in tabs)
        pos_blocks = n // cfg["tm"]
    ckv, kr, dk, dv, qm, dqb, dkb, dvb, mqb = _proj_in(xf, tabs, wp, tm=cfg["tm"], pos_blocks=pos_blocks)
    new_rows = (ckv.reshape(b, s, -1), kr.reshape(b, s, -1), dk.reshape(b, s, DIFF_HEADS, DIFF_V),
                dv.reshape(b, s, DIFF_HEADS, DIFF_V))
    r3 = lambda a: a.reshape(b, s, -1)
    if past is None:
        ckv_all, kr_all = ckv, kr
        dk_all, dv_all = r3(dkb), r3(dvb)
        kk = s
        q_off = 0
    else:
        p_len = past[0].shape[1]
        kk = p_len + s
        q_off = p_len
        ckv_all = jnp.concatenate([past[0], r3(ckv)], axis=1).reshape(b * kk, -1)
        kr_all = jnp.concatenate([past[1], r3(kr)], axis=1).reshape(b * kk, -1)
        dk_all = jnp.concatenate([past[2].reshape(b, p_len, -1).astype(BF16), r3(dkb)], axis=1)
        dv_all = jnp.concatenate([past[3].reshape(b, p_len, -1).astype(BF16), r3(dvb)], axis=1)
    k_mla, v_mla = _kv_up(ckv_all, kr_all, wp, tm=cfg["tm_kv"])
    tk = lambda name: cfg[name] if past is None else kk
    o_a = _mla_attention(r3(qm), k_mla.reshape(b, kk, -1), v_mla.reshape(b, kk, -1), tq=cfg["tq_mla"],
                         tk=tk("tk_mla"), q_off=q_off)
    lam_init = 0.8 - 0.6 * math.exp(-0.3 * layer)
    o_b = _diff_attention(r3(dqb), dk_all, dv_all, wp, lam_init=lam_init, tq=cfg["tq"], tk=tk("tk"), q_off=q_off)
    o_m = _mem_attention(r3(mqb), mem_k, mem_v, tq=cfg["tq"])
    x1, hn = _merge(xf, o_a.reshape(n, -1), o_b.reshape(n, -1), o_m.reshape(n, -1), wp, tm=cfg["tm"])
    return x1, hn, new_rows


def _peer_and_norm(x1, hn, wp, cfg):
    cnt, e1, rank2, e2 = _route(hn, wp, tt=cfg["tt_route"])
    return _peer(hn, cnt, e1, rank2, e2, x1, wp, tt=cfg["tt"], te=cfg["te"])


_CFG_PROMPT = dict(tm=512, tm_kv=2048, tq=512, tk=512, tq_mla=1024, tk_mla=1024, tt_route=256, tt=512, te=2048)
_CFG_SAMPLE = dict(tm=256, tm_kv=256, tq=32, tk=None, tq_mla=32, tk_mla=None, tt_route=256, tt=256, te=2048)


def kernel(x_prompt, x_sample, cache_mla_ckv, cache_mla_krope, cache_diff_k, cache_diff_v, cache_mem_k, cache_mem_v, mem_prompt, norm_mix, w_in, mla_q_norm, w_uq, mla_kv_norm, w_uk, w_uv, diff_lq1, diff_lk1, diff_lq2, diff_lk2, diff_subln, norm_mem, w_mem_kv, w_br_a, w_br_b, w_br_m, w_gate, b_gate, w_o, norm_ffn, peer_wq, peer_subkeys, peer_u, peer_v, norm_final):
    params = dict(norm_mix=norm_mix, w_in=w_in, mla_q_norm=mla_q_norm, w_uq=w_uq, mla_kv_norm=mla_kv_norm,
                  w_uk=w_uk, w_uv=w_uv, diff_lq1=diff_lq1, diff_lk1=diff_lk1, diff_lq2=diff_lq2,
                  diff_lk2=diff_lk2, diff_subln=diff_subln, norm_mem=norm_mem, w_mem_kv=w_mem_kv,
                  w_br_a=w_br_a, w_br_b=w_br_b, w_br_m=w_br_m, w_gate=w_gate, b_gate=b_gate, w_o=w_o,
                  norm_ffn=norm_ffn, peer_wq=peer_wq, peer_subkeys=peer_subkeys, peer_u=peer_u,
                  peer_v=peer_v, norm_final=norm_final)
    depth = w_in.shape[0]
    assert depth == 1, "the final norm is fused into the last PEER step of a single layer"
    bp, sp, d = x_prompt.shape
    bs, ss, _ = x_sample.shape
    n_mem = mem_prompt.shape[1]
    past_len = cache_mla_ckv.shape[2]
    pos_p = np.arange(sp, dtype=np.int32)
    pos_s = past_len + np.arange(ss, dtype=np.int32)

    l = 0
    wp = _prep_weights(l, params)
    mk, mv, mkb, mvb = _mem_kv(mem_prompt.reshape(bp * n_mem, d), wp, tm=n_mem)
    m3 = lambda a, b: a.reshape(b, n_mem, -1)
    x1p, hnp, rows_p = _layer(x_prompt, pos_p, None, m3(mkb, bp), m3(mvb, bp), wp, l, _CFG_PROMPT)
    past = (cache_mla_ckv[l], cache_mla_krope[l], cache_diff_k[l], cache_diff_v[l])
    x1s, hns, rows_s = _layer(x_sample, pos_s, past, m3(cache_mem_k[l].astype(BF16), bs),
                              m3(cache_mem_v[l].astype(BF16), bs), wp, l, _CFG_SAMPLE)
    y_prompt = _peer_and_norm(x1p, hnp, wp, _CFG_PROMPT).reshape(bp, sp, d)
    y_sample = _peer_and_norm(x1s, hns, wp, _CFG_SAMPLE).reshape(bs, ss, d)
    st = lambda a: a[None]
    mem4 = lambda a: a.reshape(bp, n_mem, MEM_HEADS, MEM_DH)[None]
    return (y_prompt, y_sample,
            st(rows_p[0]), st(rows_p[1]), st(rows_p[2]), st(rows_p[3]),
            mem4(mk), mem4(mv),
            st(rows_s[0]), st(rows_s[1]), st(rows_s[2]), st(rows_s[3]))
```

```python
import functools
import math

import jax
import jax.numpy as jnp
import numpy as np
from jax import lax
from jax.experimental import pallas as pl
from jax.experimental.pallas import tpu as pltpu

F32 = jnp.float32
BF16 = jnp.bfloat16

CHUNK = 64
CHUNK_SHIFT = 6
EPS = 1e-6
NEG = -1e30
MLA_HEADS = 8
MLA_Q_LORA = 384
MLA_KV_LORA = 256
MLA_NOPE = 64
MLA_ROPE = 32
MLA_V = 64
ROPE_THETA = 10000.0
LOG2E = math.log2(math.e)
MLA_SCALE = (MLA_NOPE + MLA_ROPE) ** -0.5 * LOG2E
DIFF_HEADS = 8
DIFF_DH = 32
DIFF_V = 2 * DIFF_DH
DIFF_SCALE = DIFF_DH ** -0.5 * LOG2E
MEM_HEADS = 4
MEM_DH = 128
MEM_SCALE = MEM_DH ** -0.5 * LOG2E
PEER_HEADS = 8
PEER_NKEYS = 128
PEER_HALF = 128
PEER_TOPK = 16
LANE = 128
HEAD_W = 128

VMEM_LIMIT = 56 * 1024 * 1024

_C_CQ = 0
_C_CKV = _C_CQ + MLA_Q_LORA
_C_KR = _C_CKV + MLA_KV_LORA
_C_KRR = _C_KR + LANE
_C_DQ = _C_KRR + LANE
_C_DK = _C_DQ + DIFF_HEADS * DIFF_V
_C_DV = _C_DK + DIFF_HEADS * DIFF_V
_C_MQ = _C_DV + DIFF_HEADS * DIFF_V
_C_END = _C_MQ + MEM_HEADS * MEM_DH


def _cparams(sem):
    return pltpu.CompilerParams(dimension_semantics=sem, vmem_limit_bytes=VMEM_LIMIT)


def _rms(x, g):
    return x * lax.rsqrt(jnp.mean(x * x, axis=-1, keepdims=True) + EPS) * g


def _dot(a, b):
    return jnp.dot(a, b, preferred_element_type=F32)


def _dot_nt(a, b):
    return lax.dot_general(a, b, (((1,), (1,)), ((), ())), preferred_element_type=F32)


def _full_spec(shape):
    nd = len(shape)
    return pl.BlockSpec(shape, lambda *_: (0,) * nd)


def _proj_in_kernel(x_ref, g_ref, win_ref, qn_ref, wuq_ref, kvn_ref, cq_ref, sq_ref, ck_ref, sk_ref,
                    ckv_ref, kr_ref, dk_ref, dv_ref, qm_ref, dqb_ref, dkb_ref, dvb_ref, mqb_ref):
    h = _rms(x_ref[...], g_ref[...]).astype(BF16)
    z = _dot(h, win_ref[...])
    cqn = _rms(z[:, _C_CQ:_C_CKV], qn_ref[...]).astype(BF16)
    q2 = _dot(cqn, wuq_ref[...])
    cq = cq_ref[...]
    sq = sq_ref[...]
    nq = MLA_HEADS * HEAD_W
    for hh in range(MLA_HEADS):
        lo = hh * HEAD_W
        qm_ref[:, lo:lo + HEAD_W] = (q2[:, lo:lo + HEAD_W] * cq
                                     + q2[:, nq + lo:nq + lo + HEAD_W] * sq).astype(BF16)
    ckv_ref[...] = _rms(z[:, _C_CKV:_C_KR], kvn_ref[...])
    kr = z[:, _C_KR:_C_KRR] * ck_ref[...] + z[:, _C_KRR:_C_DQ] * sk_ref[...]
    kr_ref[...] = kr[:, :MLA_ROPE]
    dk = z[:, _C_DK:_C_DV]
    dv = z[:, _C_DV:_C_MQ]
    dk_ref[...] = dk.reshape(dk_ref.shape)
    dv_ref[...] = dv.reshape(dv_ref.shape)
    dkb_ref[...] = dk.astype(BF16)
    dvb_ref[...] = dv.astype(BF16)
    dqb_ref[...] = (z[:, _C_DQ:_C_DK] * DIFF_SCALE).astype(BF16)
    mqb_ref[...] = (z[:, _C_MQ:_C_END] * MEM_SCALE).astype(BF16)


def _proj_in(x, tabs, wp, *, tm, pos_blocks):
    n, d = x.shape
    grid = (n // tm,)
    row = lambda w: pl.BlockSpec((tm, w), lambda i: (i, 0))
    tab = pl.BlockSpec((tm, LANE), lambda i: (i % pos_blocks, 0))
    heads = pl.BlockSpec((tm, DIFF_HEADS, DIFF_V), lambda i: (i, 0, 0))
    dw = DIFF_HEADS * DIFF_V
    out_shape = (
        jax.ShapeDtypeStruct((n, MLA_KV_LORA), F32),
        jax.ShapeDtypeStruct((n, MLA_ROPE), F32),
        jax.ShapeDtypeStruct((n, DIFF_HEADS, DIFF_V), F32),
        jax.ShapeDtypeStruct((n, DIFF_HEADS, DIFF_V), F32),
        jax.ShapeDtypeStruct((n, MLA_HEADS * HEAD_W), BF16),
        jax.ShapeDtypeStruct((n, dw), BF16),
        jax.ShapeDtypeStruct((n, dw), BF16),
        jax.ShapeDtypeStruct((n, dw), BF16),
        jax.ShapeDtypeStruct((n, MEM_HEADS * MEM_DH), BF16),
    )
    return pl.pallas_call(
        _proj_in_kernel,
        grid=grid,
        in_specs=[row(d), _full_spec((1, d)), _full_spec(wp["w_in"].shape), _full_spec((1, MLA_Q_LORA)),
                  _full_spec(wp["w_uq"].shape), _full_spec((1, MLA_KV_LORA)), tab, tab, tab, tab],
        out_specs=(row(MLA_KV_LORA), row(MLA_ROPE), heads, heads, row(MLA_HEADS * HEAD_W),
                   row(dw), row(dw), row(dw), row(MEM_HEADS * MEM_DH)),
        out_shape=out_shape,
        compiler_params=_cparams(("parallel",)),
        name="proj_in",
    )(x, wp["norm_mix"], wp["w_in"], wp["mla_q_norm"], wp["w_uq"], wp["mla_kv_norm"], *tabs)


def _kv_up_kernel(ckv_ref, kr_ref, wuk_ref, pk_ref, wuv_ref, k_ref, v_ref):
    c = ckv_ref[...].astype(BF16)
    k = _dot(c, wuk_ref[...]) + _dot(kr_ref[...].astype(BF16), pk_ref[...])
    k_ref[...] = k.astype(BF16)
    v_ref[...] = _dot(c, wuv_ref[...]).astype(BF16)


def _kv_up(ckv, kr, wp, *, tm):
    n = ckv.shape[0]
    row = lambda w: pl.BlockSpec((tm, w), lambda i: (i, 0))
    return pl.pallas_call(
        _kv_up_kernel,
        grid=(n // tm,),
        in_specs=[row(MLA_KV_LORA), row(MLA_ROPE), _full_spec(wp["w_uk"].shape),
                  _full_spec(wp["p_kr"].shape), _full_spec(wp["w_uv"].shape)],
        out_specs=(row(MLA_HEADS * HEAD_W), row(MLA_HEADS * MLA_V)),
        out_shape=(jax.ShapeDtypeStruct((n, MLA_HEADS * HEAD_W), BF16),
                   jax.ShapeDtypeStruct((n, MLA_HEADS * MLA_V), BF16)),
        compiler_params=_cparams(("parallel",)),
        name="kv_up",
    )(ckv, kr, wp["w_uk"], wp["p_kr"], wp["w_uv"])


def _lanes(col, n):
    if n == LANE:
        return col
    if n % LANE == 0:
        return jnp.concatenate([col] * (n // LANE), axis=1)
    return jnp.broadcast_to(col[:, :1], (col.shape[0], n))


GROUP_VREGS = 16


def _softmax_block(score_group, tq, tk, row0, m_scr, al_scr, p_scr, l_scr=None):
    rg = min(tq, max(16, (GROUP_VREGS * 8 * LANE // tk) // 16 * 16))
    for g in range(tq // rg):
        r = slice(row0 + g * rg, row0 + (g + 1) * rg)
        m_prev = m_scr[r]
        m_next = jnp.maximum(m_prev, jnp.max(score_group(g, rg), axis=1, keepdims=True))
        al_scr[r] = jnp.exp2(m_prev - m_next)
        m_scr[r] = m_next
    for g in range(tq // rg):
        r = slice(row0 + g * rg, row0 + (g + 1) * rg)
        e = jnp.exp2(score_group(g, rg) - _lanes(m_scr[r], tk))
        p_scr[r, :tk] = e.astype(BF16)
        if l_scr is not None:
            l_scr[r] = al_scr[r] * l_scr[r] + jnp.sum(e, axis=1, keepdims=True)


def _accumulate(rows, v, al_scr, l_scr, acc_scr, p_scr):
    pv = _dot(p_scr[rows, :v.shape[0]], jnp.concatenate([v, jnp.ones(v.shape, v.dtype)], axis=1))
    w = v.shape[1]
    acc_scr[rows] = acc_scr[rows] * al_scr[rows] + pv[:, :w]
    l_scr[rows] = l_scr[rows] * al_scr[rows] + pv[:, w:]


def _chunk_visible(q_lo, k_lo, g, rg, tk):
    qp = q_lo + g * rg + lax.broadcasted_iota(jnp.int32, (rg, tk), 0)
    kp = k_lo + lax.broadcasted_iota(jnp.int32, (rg, tk), 1)
    return qp, kp, (kp >> CHUNK_SHIFT) <= (qp >> CHUNK_SHIFT)


def _init_stats(m_scr, l_scr, acc_scr):
    m_scr[...] = jnp.full(m_scr.shape, NEG, F32)
    l_scr[...] = jnp.zeros(l_scr.shape, F32)
    acc_scr[...] = jnp.zeros(acc_scr.shape, F32)


def _block_range(qi, *, tq, tk, q_off, nk):
    q_lo = q_off + qi * tq
    q_hi = q_lo + (tq - 1)
    n_behind = jnp.minimum((q_lo + 1) // tk, nk)
    last_key = ((q_hi >> CHUNK_SHIFT) << CHUNK_SHIFT) + (CHUNK - 1)
    n_need = jnp.minimum(last_key // tk + 1, nk)
    return q_lo, n_behind, n_need


def _for_blocks(lo, hi, fn):
    lax.fori_loop(lo, hi, lambda i, c: (fn(i), c)[1], 0)


def _run_blocks(qi, block, *, tq, tk, q_off, nk):
    _, n_behind, n_need = _block_range(qi, tq=tq, tk=tk, q_off=q_off, nk=nk)
    _for_blocks(0, n_behind, lambda i: block(pl.multiple_of(i * tk, tk), False, tk, 0, tq))
    if tq == tk and q_off % tq == 0 and tq % (2 * CHUNK) == 0:
        half = tk // 2
        k_lo = pl.multiple_of(n_behind * tk, tk)
        block(k_lo, True, half, 0, tq)
        block(pl.multiple_of(k_lo + half, half), True, half, half, half)
    else:
        _for_blocks(n_behind, n_need, lambda i: block(pl.multiple_of(i * tk, tk), True, tk, 0, tq))


def _attn_specs(b, pairs, sq, sk, tq, qw, kw, vw):
    grid = (b, pairs, sq // tq)
    in_specs = [pl.BlockSpec((1, tq, qw), lambda bi, p, qi: (bi, qi, p)),
                pl.BlockSpec((1, sk, kw), lambda bi, p, qi: (bi, 0, p)),
                pl.BlockSpec((1, sk, vw), lambda bi, p, qi: (bi, 0, p))]
    out_spec = pl.BlockSpec((1, tq, vw), lambda bi, p, qi: (bi, qi, p))
    return grid, in_specs, out_spec


def _mla_kernel(q_ref, k_ref, v_ref, o_ref, m_scr, l_scr, al_scr, acc_scr, p_scr, *, tq, tk, q_off, nk):
    qi = pl.program_id(2)
    _init_stats(m_scr, l_scr, acc_scr)
    q_lo = q_off + qi * tq

    def block(k_lo, masked, tkk, r0, nr):
        for hh in range(2):
            hs = slice(hh * HEAD_W, (hh + 1) * HEAD_W)
            s = _dot_nt(q_ref[0, r0:r0 + nr, hs], k_ref[0, pl.ds(k_lo, tkk), hs])

            def score_group(g, rg, s=s):
                sg = s[g * rg:(g + 1) * rg]
                if masked:
                    sg = jnp.where(_chunk_visible(q_lo + r0, k_lo, g, rg, tkk)[2], sg, NEG)
                return sg

            _softmax_block(score_group, nr, tkk, hh * tq + r0, m_scr, al_scr, p_scr)
        v = v_ref[0, pl.ds(k_lo, tkk), :]
        if nr == tq:
            _accumulate(slice(0, 2 * tq), v, al_scr, l_scr, acc_scr, p_scr)
        else:
            for hh in range(2):
                _accumulate(slice(hh * tq + r0, hh * tq + r0 + nr), v, al_scr, l_scr, acc_scr, p_scr)

    _run_blocks(qi, block, tq=tq, tk=tk, q_off=q_off, nk=nk)

    lane = lax.broadcasted_iota(jnp.int32, (tq, LANE), 1)
    o0 = acc_scr[0:tq] / l_scr[0:tq]
    o1 = acc_scr[tq:2 * tq] / l_scr[tq:2 * tq]
    o_ref[0] = jnp.where(lane < MLA_V, o0, o1).astype(o_ref.dtype)


def _mla_attention(q, k, v, *, tq, tk, q_off):
    b, sq, _ = q.shape
    sk = k.shape[1]
    grid, in_specs, out_spec = _attn_specs(b, MLA_HEADS // 2, sq, sk, tq, 2 * HEAD_W, 2 * HEAD_W, 2 * MLA_V)
    kern = functools.partial(_mla_kernel, tq=tq, tk=tk, q_off=q_off, nk=sk // tk)
    stat = pltpu.VMEM((2 * tq, LANE), F32)
    return pl.pallas_call(
        kern,
        grid=grid,
        in_specs=in_specs,
        out_specs=out_spec,
        out_shape=jax.ShapeDtypeStruct((b, sq, MLA_HEADS * MLA_V), BF16),
        scratch_shapes=[stat, stat, stat, stat, pltpu.VMEM((2 * tq, tk), BF16)],
        compiler_params=_cparams(("parallel", "parallel", "parallel")),
        name="mla_attn",
    )(q, k, v)


def _diff_kernel(slope_ref, lq1_ref, lk1_ref, lq2_ref, lk2_ref, sub_ref, q_ref, k_ref, v_ref, o_ref,
                 qs_scr, m_scr, l_scr, al_scr, acc_scr, p_scr, *, tq, tk, q_off, nk, lam_init):
    pr = pl.program_id(1)
    qi = pl.program_id(2)
    _init_stats(m_scr, l_scr, acc_scr)
    q_lo = q_off + qi * tq

    q = q_ref[0]
    lane = lax.broadcasted_iota(jnp.int32, (tq, LANE), 1)
    for mi in range(4):
        lo = mi * DIFF_DH
        qs_scr[mi * tq:(mi + 1) * tq] = jnp.where((lane >= lo) & (lane < lo + DIFF_DH), q, jnp.zeros_like(q))

    def block(k_lo, masked, tkk, r0, nr):
        kb = k_ref[0, pl.ds(k_lo, tkk), :]
        kpos = (k_lo + lax.broadcasted_iota(jnp.int32, (1, tkk), 1)).astype(F32)

        def scores(mi):
            return _dot_nt(qs_scr[mi * tq + r0:mi * tq + r0 + nr], kb)

        def softmax(mi, s_map):
            slope = slope_ref[2 * pr + mi // 2] * LOG2E

            def score_group(g, rg):
                sg = s_map[g * rg:(g + 1) * rg]
                if masked:
                    qp, kp, vis = _chunk_visible(q_lo + r0, k_lo, g, rg, tkk)
                    return jnp.where(vis, sg + slope * jnp.minimum(kp, 2 * qp - kp).astype(F32), NEG)
                return sg + slope * kpos

            _softmax_block(score_group, nr, tkk, mi * tq + r0, m_scr, al_scr, p_scr, l_scr)

        def values(mi):
            rows = slice(mi * tq + r0, mi * tq + r0 + nr)
            acc_scr[rows] = acc_scr[rows] * al_scr[rows] + _dot(p_scr[rows, :tkk], v_ref[0, pl.ds(k_lo, tkk), :])

        s0 = scores(0)
        s1 = scores(1)
        softmax(0, s0)
        s2 = scores(2)
        values(0)
        softmax(1, s1)
        s3 = scores(3)
        values(1)
        softmax(2, s2)
        values(2)
        softmax(3, s3)
        values(3)

    _run_blocks(qi, block, tq=tq, tk=tk, q_off=q_off, nk=nk)

    lam = (jnp.exp(jnp.sum(lq1_ref[...] * lk1_ref[...], axis=1, keepdims=True))
           - jnp.exp(jnp.sum(lq2_ref[...] * lk2_ref[...], axis=1, keepdims=True)) + lam_init)
    first = lane < DIFF_V
    on = [acc_scr[mi * tq:(mi + 1) * tq] / l_scr[mi * tq:(mi + 1) * tq] for mi in range(4)]
    o = jnp.where(first, on[0] - lam * on[1], on[2] - lam * on[3])
    sq = o * o
    ms0 = jnp.sum(jnp.where(first, sq, 0.0), axis=1, keepdims=True) * (1.0 / DIFF_V)
    ms1 = jnp.sum(jnp.where(first, 0.0, sq), axis=1, keepdims=True) * (1.0 / DIFF_V)
    r = jnp.where(first, lax.rsqrt(ms0 + EPS), lax.rsqrt(ms1 + EPS))
    o_ref[0] = ((o * r * sub_ref[...]) * (1.0 - lam_init)).astype(o_ref.dtype)


def _diff_attention(q, k, v, wp, *, tq, tk, q_off, lam_init):
    b, sq, _ = q.shape
    sk = k.shape[1]
    grid, in_specs, out_spec = _attn_specs(b, DIFF_HEADS // 2, sq, sk, tq, LANE, LANE, LANE)
    kern = functools.partial(_diff_kernel, tq=tq, tk=tk, q_off=q_off, nk=sk // tk, lam_init=lam_init)
    small = lambda w: pl.BlockSpec((1, w), lambda bi, p, qi: (0, 0))
    stat = pltpu.VMEM((4 * tq, LANE), F32)
    return pl.pallas_call(
        kern,
        grid=grid,
        in_specs=[pl.BlockSpec(memory_space=pltpu.SMEM),
                  small(DIFF_DH), small(DIFF_DH), small(DIFF_DH), small(DIFF_DH), small(LANE)] + in_specs,
        out_specs=out_spec,
        out_shape=jax.ShapeDtypeStruct((b, sq, DIFF_HEADS * DIFF_V), BF16),
        scratch_shapes=[pltpu.VMEM((4 * tq, LANE), BF16), stat, stat, stat, stat,
                        pltpu.VMEM((4 * tq, tk), BF16)],
        compiler_params=_cparams(("parallel", "parallel", "parallel")),
        name="diff_attn",
    )(wp["slopes"], wp["diff_lq1"], wp["diff_lk1"], wp["diff_lq2"], wp["diff_lk2"], wp["diff_subln2"],
      q, k, v)


def _mem_attn_kernel(q_ref, k_ref, v_ref, o_ref):
    for hh in range(MEM_HEADS):
        sl = slice(hh * MEM_DH, (hh + 1) * MEM_DH)
        s = _dot_nt(q_ref[0, :, sl], k_ref[0, :, sl])
        p = jnp.exp2(s - jnp.max(s, axis=1, keepdims=True))
        o = _dot(p.astype(BF16), v_ref[0, :, sl]) / jnp.sum(p, axis=1, keepdims=True)
        o_ref[0, :, sl] = o.astype(o_ref.dtype)


def _mem_attention(q, k, v, *, tq):
    b, sq, w = q.shape
    nm = k.shape[1]
    return pl.pallas_call(
        _mem_attn_kernel,
        grid=(b, sq // tq),
        in_specs=[pl.BlockSpec((1, tq, w), lambda bi, qi: (bi, qi, 0)),
                  pl.BlockSpec((1, nm, w), lambda bi, qi: (bi, 0, 0)),
                  pl.BlockSpec((1, nm, w), lambda bi, qi: (bi, 0, 0))],
        out_specs=pl.BlockSpec((1, tq, w), lambda bi, qi: (bi, qi, 0)),
        out_shape=jax.ShapeDtypeStruct((b, sq, w), BF16),
        compiler_params=_cparams(("parallel", "parallel")),
        name="mem_attn",
    )(q, k, v)


def _mem_kv_kernel(x_ref, g_ref, w_ref, k_ref, v_ref, kb_ref, vb_ref):
    h = _rms(x_ref[...], g_ref[...]).astype(BF16)
    kv = _dot(h, w_ref[...])
    w = MEM_HEADS * MEM_DH
    k_ref[...] = kv[:, :w]
    v_ref[...] = kv[:, w:]
    kb_ref[...] = kv[:, :w].astype(BF16)
    vb_ref[...] = kv[:, w:].astype(BF16)


def _mem_kv(mem, wp, *, tm):
    n, d = mem.shape
    w = MEM_HEADS * MEM_DH
    row = lambda c: pl.BlockSpec((tm, c), lambda i: (i, 0))
    return pl.pallas_call(
        _mem_kv_kernel,
        grid=(n // tm,),
        in_specs=[row(d), _full_spec((1, d)), _full_spec(wp["w_mem_kv"].shape)],
        out_specs=(row(w), row(w), row(w), row(w)),
        out_shape=(jax.ShapeDtypeStruct((n, w), F32), jax.ShapeDtypeStruct((n, w), F32),
                   jax.ShapeDtypeStruct((n, w), BF16), jax.ShapeDtypeStruct((n, w), BF16)),
        compiler_params=_cparams(("parallel",)),
        name="mem_kv",
    )(mem, wp["norm_mem"], wp["w_mem_kv"])


def _merge_kernel(x_ref, oa_ref, ob_ref, om_ref, g_ref, wg_ref, bg_ref, wa_ref, wb_ref, wm_ref, wo_ref,
                  gf_ref, x1_ref, hn_ref):
    x = x_ref[...]
    d = x.shape[1]
    h = _rms(x, g_ref[...]).astype(BF16)
    gates = jax.nn.sigmoid(_dot(h, wg_ref[...]) + bg_ref[...])
    merged = (gates[:, :d] * _dot(oa_ref[...], wa_ref[...])
              + gates[:, d:2 * d] * _dot(ob_ref[...], wb_ref[...])
              + gates[:, 2 * d:] * _dot(om_ref[...], wm_ref[...]))
    x1 = x + _dot(merged.astype(BF16), wo_ref[...])
    x1_ref[...] = x1
    hn_ref[...] = _rms(x1, gf_ref[...]).astype(BF16)


def _merge(x, oa, ob, om, wp, *, tm):
    n, d = x.shape
    row = lambda c: pl.BlockSpec((tm, c), lambda i: (i, 0))
    names = ("norm_mix", "w_gate", "b_gate", "w_br_a", "w_br_b", "w_br_m", "w_o", "norm_ffn")
    return pl.pallas_call(
        _merge_kernel,
        grid=(n // tm,),
        in_specs=[row(d), row(oa.shape[1]), row(ob.shape[1]), row(om.shape[1])]
                 + [_full_spec(wp[k].shape) for k in names],
        out_specs=(row(d), row(d)),
        out_shape=(jax.ShapeDtypeStruct((n, d), F32), jax.ShapeDtypeStruct((n, d), BF16)),
        compiler_params=_cparams(("parallel",)),
        name="merge",
    )(x, oa, ob, om, *[wp[k] for k in names])


def _top_values(sc, k, with_rank=False):
    vals = []
    cur = sc
    rank = jnp.full(sc.shape, float(k), F32) if with_rank else None
    for r in range(k):
        m = jnp.max(cur, axis=0, keepdims=True)
        vals.append(m)
        hit = cur == m
        if with_rank:
            rank = jnp.where(hit, float(r), rank)
        if r + 1 < k:
            cur = jnp.where(hit, NEG, cur)
    return (vals, rank) if with_rank else vals


def _stack_rows(rows, sub):
    out = jnp.zeros(sub.shape, F32)
    for r, row in enumerate(rows):
        if row is not None:
            out = jnp.where(sub == r, row, out)
    return out


def _packed_candidates(t1, t2, t2h, sub8):
    assert PEER_TOPK == 16
    down = lambda x, k: pltpu.roll(x, k, 0)
    t1_8_13 = _stack_rows([None, None] + t1[8:14], sub8)
    t1_14_15 = _stack_rows(t1[14:16], sub8)
    top2 = t2[0:1]
    slabs = [
        t1[0] + t2h,
        t1[0] + t2[8:16],
        t1[1] + t2h,
        jnp.where(sub8 < 5, t1[2] + t2h, t1[4] + down(t2h, 5)),
        jnp.where(sub8 < 4, t1[3] + t2h,
                  jnp.where(sub8 < 6, t1[5] + down(t2h, 4), t1[6] + down(t2h, 6))),
        jnp.where(sub8 < 2, t1[7] + t2h, t1_8_13 + top2),
        jnp.where(sub8 < 2, t1_14_15 + top2, NEG),
    ]
    return jnp.concatenate(slabs, axis=0)


EXACT_RANKS = 4
CNT_CAP = PEER_TOPK // (EXACT_RANKS + 1)


def _route_kernel(hn_ref, wq_ref, sk_ref, cnt_ref, e1_ref, rank_ref, e2_ref):
    tt = hn_ref.shape[0]
    q = _dot(hn_ref[...], wq_ref[...]).astype(BF16)
    sub = lax.broadcasted_iota(jnp.int32, (PEER_TOPK, tt), 0)
    sub8 = lax.broadcasted_iota(jnp.int32, (8, tt), 0)
    for h in range(PEER_HEADS):
        sc1 = _dot_nt(sk_ref[2 * h], q[:, (2 * h) * PEER_HALF:(2 * h + 1) * PEER_HALF])
        sc2 = _dot_nt(sk_ref[2 * h + 1], q[:, (2 * h + 1) * PEER_HALF:(2 * h + 2) * PEER_HALF])
        t1 = _top_values(sc1, PEER_TOPK)
        t2_rows, rank2 = _top_values(sc2, PEER_TOPK, with_rank=True)
        t2 = _stack_rows(t2_rows, sub)
        t2h = t2[:8]
        cands = []
        for a in range(PEER_TOPK):
            nb = PEER_TOPK // (a + 1)
            c = t1[a] + (t2 if nb > 8 else t2h)
            if nb < c.shape[0]:
                c = jnp.where((sub if nb > 8 else sub8) < nb, c, NEG)
            cands.append(c)
        cand = _packed_candidates(t1, t2, t2h, sub8)
        tau = _top_values(cand, PEER_TOPK)[-1]
        top = t1[0] + t2[0:1]
        z = jnp.sum(jnp.where(cand >= tau, jnp.exp(cand - top), 0.0), axis=0, keepdims=True)
        counts = [jnp.sum(jnp.where(cands[a] >= tau, 1.0, 0.0), axis=0, keepdims=True) for a in range(PEER_TOPK)]
        cnt = jnp.zeros(sc1.shape, F32)
        for c in range(1, CNT_CAP + 1):
            theta = jnp.full(tau.shape, -NEG, F32)
            for a in range(PEER_TOPK):
                theta = jnp.where(counts[a] >= c, t1[a], theta)
            cnt = jnp.where(sc1 >= theta, float(c), cnt)
        for a in range(EXACT_RANKS):
            cnt = jnp.where(sc1 == t1[a], counts[a], cnt)
        cnt_ref[h] = cnt
        e1_ref[h] = jnp.exp(sc1 - t1[0]) * (0.5 / z)
        rank_ref[h] = rank2.astype(BF16)
        e2_ref[h] = jnp.exp(sc2 - t2[0:1]).astype(BF16)


def _route(hn, wp, *, tt):
    n, d = hn.shape
    big = pl.BlockSpec((PEER_HEADS, PEER_NKEYS, tt), lambda i: (0, 0, i))
    big_shape = jax.ShapeDtypeStruct((PEER_HEADS, PEER_NKEYS, n), F32)
    big_half = jax.ShapeDtypeStruct((PEER_HEADS, PEER_NKEYS, n), BF16)
    return pl.pallas_call(
        _route_kernel,
        grid=(n // tt,),
        in_specs=[pl.BlockSpec((tt, d), lambda i: (i, 0)), _full_spec(wp["peer_wq"].shape),
                  _full_spec(wp["peer_subkeys"].shape)],
        out_specs=(big, big, big, big),
        out_shape=(big_shape, big_shape, big_half, big_half),
        compiler_params=_cparams(("parallel",)),
        name="peer_route",
    )(hn, wp["peer_wq"], wp["peer_subkeys"])


PAIR = 2 * PEER_NKEYS
SUBROWS = 128


def _peer_kernel(hn_in, u_ref, vt_ref, cnt_ref, e1_ref, rank_in, e2_in, x1_ref, gn_ref, y_ref,
                 acc_scr, hw_scr, rank_ref, e2_ref, hn_ref, *, te, ne):
    j = pl.program_id(1)
    tt = hn_in.shape[0]
    npair = te // PAIR

    @pl.when(j == 0)
    def _():
        rank_ref[...] = rank_in[...]
        e2_ref[...] = e2_in[...]
        hn_ref[...] = hn_in[...]

    def pre_act(p):
        a = _dot_nt(u_ref[p * PAIR:(p + 1) * PAIR, :], hn_ref[...])
        return a * (1.0 + lax.erf(a * math.sqrt(0.5)))

    def down(lo, hi):
        return _dot(vt_ref[:, lo * PAIR:hi * PAIR], hw_scr[lo * PAIR:hi * PAIR, :])

    cuts = sorted({0, 3 * npair // 8, 3 * npair // 4, npair - 1, npair})
    piece_after = {hi: lo for lo, hi in zip(cuts[:-1], cuts[1:])}

    groups = [(tc, sb) for tc in range(tt // LANE) for sb in range(PEER_NKEYS // SUBROWS)]
    zero = jnp.zeros((), BF16)
    act = pre_act(0)
    tot = None
    carry = None
    for p in range(npair):
        nxt = None
        for gi, (tc, sb) in enumerate(groups):
            if gi == 1 and p + 1 < npair:
                nxt = pre_act(p + 1)
            if gi == len(groups) // 2 and p in piece_after:
                part = down(piece_after[p], p)
                tot = part if tot is None else tot + part
            ls = slice(tc * LANE, (tc + 1) * LANE)
            i2 = slice(sb * SUBROWS, (sb + 1) * SUBROWS)
            w = [None, None] if carry is None else [carry * zero, carry * zero]
            for h in range(PEER_HEADS):
                rk = rank_ref[h, i2, ls]
                e2b = e2_ref[h, i2, ls]
                for rr in range(2):
                    i1 = 2 * p + rr
                    cnt = jnp.broadcast_to(cnt_ref[h, i1:i1 + 1, ls], (SUBROWS, LANE)).astype(BF16)
                    e1b = jnp.broadcast_to(e1_ref[h, i1:i1 + 1, ls], (SUBROWS, LANE)).astype(BF16)
                    cw = jnp.where(rk < cnt, e2b, zero) * e1b
                    w[rr] = cw if w[rr] is None else w[rr] + cw
            carry = w[1]
            for rr in range(2):
                r0 = rr * PEER_NKEYS + sb * SUBROWS
                hw_scr[p * PAIR + r0:p * PAIR + r0 + SUBROWS, ls] = w[rr] * act[r0:r0 + SUBROWS, ls].astype(BF16)
        act = nxt
    part = down(piece_after[npair], npair)
    tot = part if tot is None else tot + part

    @pl.when(j == 0)
    def _():
        acc_scr[...] = tot

    @pl.when(j > 0)
    def _():
        acc_scr[...] += tot

    @pl.when(j == ne - 1)
    def _():
        xr = x1_ref[...] + acc_scr[...].T
        y_ref[...] = _rms(xr, gn_ref[...])


def _peer(hn, cnt, e1, rank2, e2, x1, wp, *, tt, te):
    n, d = hn.shape
    ne = wp["peer_u"].shape[0] // te
    big = pl.BlockSpec((PEER_HEADS, PEER_NKEYS, tt), lambda i, j: (0, 0, i))
    rows = pl.BlockSpec((PEER_HEADS, te // PEER_NKEYS, tt), lambda i, j: (0, j, i))
    kern = functools.partial(_peer_kernel, te=te, ne=ne)
    return pl.pallas_call(
        kern,
        grid=(n // tt, ne),
        in_specs=[pl.BlockSpec((tt, d), lambda i, j: (i, 0)),
                  pl.BlockSpec((te, d), lambda i, j: (j, 0)),
                  pl.BlockSpec((d, te), lambda i, j: (0, j)),
                  rows, rows, big, big,
                  pl.BlockSpec((tt, d), lambda i, j: (i, 0)),
                  pl.BlockSpec((1, d), lambda i, j: (0, 0))],
        out_specs=pl.BlockSpec((tt, d), lambda i, j: (i, 0)),
        out_shape=jax.ShapeDtypeStruct((n, d), F32),
        scratch_shapes=[pltpu.VMEM((d, tt), F32), pltpu.VMEM((te, tt), BF16),
                        pltpu.VMEM((PEER_HEADS, PEER_NKEYS, tt), BF16),
                        pltpu.VMEM((PEER_HEADS, PEER_NKEYS, tt), BF16),
                        pltpu.VMEM((tt, d), BF16)],
        compiler_params=_cparams(("parallel", "arbitrary")),
        name="peer_experts",
    )(hn, wp["peer_u"], wp["peer_vt"], cnt, e1, rank2, e2, x1, wp["norm_final"])


def _prep_weights(l, p):
    f = lambda a: a.astype(F32)
    w_in = f(p["w_in"][l])
    o_cq, o_ckv, o_kr = 0, MLA_Q_LORA, MLA_Q_LORA + MLA_KV_LORA
    o_dq = o_kr + MLA_ROPE
    half = MLA_ROPE // 2
    d = w_in.shape[0]
    kr = w_in[:, o_kr:o_dq]
    pad = jnp.zeros((d, LANE - MLA_ROPE), F32)
    w_in2 = jnp.concatenate([
        w_in[:, o_cq:o_kr], kr, pad,
        -kr[:, half:], kr[:, :half], pad,
        w_in[:, o_dq:]], axis=1)
    assert w_in2.shape[1] == _C_END

    w_uq = f(p["w_uq"][l]).reshape(MLA_Q_LORA, MLA_HEADS, MLA_NOPE + MLA_ROPE)
    nope, x1, x2 = w_uq[..., :MLA_NOPE], w_uq[..., MLA_NOPE:MLA_NOPE + half], w_uq[..., MLA_NOPE + half:]
    zpad = jnp.zeros((MLA_Q_LORA, MLA_HEADS, HEAD_W - MLA_NOPE - MLA_ROPE), F32)
    q_slab = jnp.concatenate([nope, x1, x2, zpad], axis=-1).reshape(MLA_Q_LORA, -1)
    r_slab = jnp.concatenate([jnp.zeros_like(nope), -x2, x1, zpad], axis=-1).reshape(MLA_Q_LORA, -1)
    w_uq2 = jnp.concatenate([q_slab, r_slab], axis=1)

    w_uk = f(p["w_uk"][l]).reshape(MLA_KV_LORA, MLA_HEADS, MLA_NOPE)
    w_uk2 = jnp.concatenate([w_uk, jnp.zeros((MLA_KV_LORA, MLA_HEADS, HEAD_W - MLA_NOPE), F32)],
                            axis=-1).reshape(MLA_KV_LORA, -1)
    r_idx = jnp.arange(MLA_ROPE)
    cols = jnp.arange(MLA_HEADS * HEAD_W)
    p_kr = ((cols[None, :] % HEAD_W) == (MLA_NOPE + r_idx[:, None])).astype(F32)

    slopes = jnp.exp2(-8.0 * jnp.arange(1, DIFF_HEADS + 1, dtype=F32) / DIFF_HEADS)
    sub = f(p["diff_subln"][l])
    row = lambda a: f(a).reshape(1, -1)
    return {
        "norm_mix": row(p["norm_mix"][l]), "w_in": w_in2.astype(BF16),
        "mla_q_norm": row(p["mla_q_norm"][l]), "w_uq": w_uq2.astype(BF16),
        "mla_kv_norm": row(p["mla_kv_norm"][l]),
        "w_uk": w_uk2.astype(BF16), "p_kr": p_kr.astype(BF16), "w_uv": p["w_uv"][l].astype(BF16),
        "slopes": slopes,
        "diff_lq1": row(p["diff_lq1"][l]), "diff_lk1": row(p["diff_lk1"][l]),
        "diff_lq2": row(p["diff_lq2"][l]), "diff_lk2": row(p["diff_lk2"][l]),
        "diff_subln2": jnp.concatenate([sub, sub]).reshape(1, -1),
        "norm_mem": row(p["norm_mem"][l]), "w_mem_kv": p["w_mem_kv"][l].astype(BF16),
        "w_gate": p["w_gate"][l].astype(BF16), "b_gate": row(p["b_gate"][l]),
        "w_br_a": p["w_br_a"][l].astype(BF16), "w_br_b": p["w_br_b"][l].astype(BF16),
        "w_br_m": p["w_br_m"][l].astype(BF16), "w_o": p["w_o"][l].astype(BF16),
        "norm_ffn": row(p["norm_ffn"][l]),
        "peer_wq": p["peer_wq"][l].astype(BF16),
        "peer_subkeys": p["peer_subkeys"][l].reshape(PEER_HEADS * 2, PEER_NKEYS, PEER_HALF).astype(BF16),
        "peer_u": p["peer_u"][l].astype(BF16), "peer_vt": p["peer_v"][l].T.astype(BF16),
        "norm_final": row(p["norm_final"]),
    }


def _rope_tables(pos):
    half = MLA_ROPE // 2
    inv = 1.0 / (ROPE_THETA ** (np.arange(half, dtype=np.float64) / half))
    ang = pos.astype(np.float64)[:, None] * inv[None, :]
    cos, sin = np.cos(ang), np.sin(ang)
    n = pos.shape[0]
    one = np.ones((n, MLA_NOPE))
    zq = np.zeros((n, HEAD_W - MLA_NOPE - MLA_ROPE))
    cq = np.concatenate([one, cos, cos, zq], axis=1) * MLA_SCALE
    sq = np.concatenate([0.0 * one, sin, sin, zq], axis=1) * MLA_SCALE
    zk = np.zeros((n, LANE - MLA_ROPE))
    ck = np.concatenate([cos, cos, zk], axis=1)
    sk = np.concatenate([sin, sin, zk], axis=1)
    return tuple(t.astype(np.float32) for t in (cq, sq, ck, sk))


def _layer(x, pos, past, mem_k, mem_v, wp, layer, cfg):
    b, s, d = x.shape
    n = b * s
    xf = x.reshape(n, d)
    tabs = _rope_tables(pos)
    if s % cfg["tm"] == 0:
        pos_blocks = s // cfg["tm"]
    else:
        tabs = tuple(np.tile(t, (b, 1)) for t in tabs)
        pos_blocks = n // cfg["tm"]
    ckv, kr, dk, dv, qm, dqb, dkb, dvb, mqb = _proj_in(xf, tabs, wp, tm=cfg["tm"], pos_blocks=pos_blocks)
    new_rows = (ckv.reshape(b, s, -1), kr.reshape(b, s, -1), dk.reshape(b, s, DIFF_HEADS, DIFF_V),
                dv.reshape(b, s, DIFF_HEADS, DIFF_V))
    r3 = lambda a: a.reshape(b, s, -1)
    if past is None:
        ckv_all, kr_all = ckv, kr
        dk_all, dv_all = r3(dkb), r3(dvb)
        kk = s
        q_off = 0
    else:
        p_len = past[0].shape[1]
        kk = p_len + s
        q_off = p_len
        ckv_all = jnp.concatenate([past[0], r3(ckv)], axis=1).reshape(b * kk, -1)
        kr_all = jnp.concatenate([past[1], r3(kr)], axis=1).reshape(b * kk, -1)
        dk_all = jnp.concatenate([past[2].reshape(b, p_len, -1).astype(BF16), r3(dkb)], axis=1)
        dv_all = jnp.concatenate([past[3].reshape(b, p_len, -1).astype(BF16), r3(dvb)], axis=1)
    k_mla, v_mla = _kv_up(ckv_all, kr_all, wp, tm=cfg["tm_kv"])
    tk = lambda name: cfg[name] if past is None else kk
    o_a = _mla_attention(r3(qm), k_mla.reshape(b, kk, -1), v_mla.reshape(b, kk, -1), tq=cfg["tq_mla"],
                         tk=tk("tk_mla"), q_off=q_off)
    lam_init = 0.8 - 0.6 * math.exp(-0.3 * layer)
    o_b = _diff_attention(r3(dqb), dk_all, dv_all, wp, lam_init=lam_init, tq=cfg["tq"], tk=tk("tk"), q_off=q_off)
    o_m = _mem_attention(r3(mqb), mem_k, mem_v, tq=cfg["tq"])
    x1, hn = _merge(xf, o_a.reshape(n, -1), o_b.reshape(n, -1), o_m.reshape(n, -1), wp, tm=cfg["tm"])
    return x1, hn, new_rows


def _peer_and_norm(x1, hn, wp, cfg):
    cnt, e1, rank2, e2 = _route(hn, wp, tt=cfg["tt_route"])
    return _peer(hn, cnt, e1, rank2, e2, x1, wp, tt=cfg["tt"], te=cfg["te"])


_CFG_PROMPT = dict(tm=512, tm_kv=2048, tq=512, tk=512, tq_mla=1024, tk_mla=1024, tt_route=256, tt=512, te=2048)
_CFG_SAMPLE = dict(tm=256, tm_kv=256, tq=32, tk=None, tq_mla=32, tk_mla=None, tt_route=256, tt=256, te=2048)


def kernel(x_prompt, x_sample, cache_mla_ckv, cache_mla_krope, cache_diff_k, cache_diff_v, cache_mem_k, cache_mem_v, mem_prompt, norm_mix, w_in, mla_q_norm, w_uq, mla_kv_norm, w_uk, w_uv, diff_lq1, diff_lk1, diff_lq2, diff_lk2, diff_subln, norm_mem, w_mem_kv, w_br_a, w_br_b, w_br_m, w_gate, b_gate, w_o, norm_ffn, peer_wq, peer_subkeys, peer_u, peer_v, norm_final):
    params = dict(norm_mix=norm_mix, w_in=w_in, mla_q_norm=mla_q_norm, w_uq=w_uq, mla_kv_norm=mla_kv_norm,
                  w_uk=w_uk, w_uv=w_uv, diff_lq1=diff_lq1, diff_lk1=diff_lk1, diff_lq2=diff_lq2,
                  diff_lk2=diff_lk2, diff_subln=diff_subln, norm_mem=norm_mem, w_mem_kv=w_mem_kv,
                  w_br_a=w_br_a, w_br_b=w_br_b, w_br_m=w_br_m, w_gate=w_gate, b_gate=b_gate, w_o=w_o,
                  norm_ffn=norm_ffn, peer_wq=peer_wq, peer_subkeys=peer_subkeys, peer_u=peer_u,
                  peer_v=peer_v, norm_final=norm_final)
    depth = w_in.shape[0]
    assert depth == 1, "the final norm is fused into the last PEER step of a single layer"
    bp, sp, d = x_prompt.shape
    bs, ss, _ = x_sample.shape
    n_mem = mem_prompt.shape[1]
    past_len = cache_mla_ckv.shape[2]
    pos_p = np.arange(sp, dtype=np.int32)
    pos_s = past_len + np.arange(ss, dtype=np.int32)

    l = 0
    wp = _prep_weights(l, params)
    mk, mv, mkb, mvb = _mem_kv(mem_prompt.reshape(bp * n_mem, d), wp, tm=n_mem)
    m3 = lambda a, b: a.reshape(b, n_mem, -1)
    x1p, hnp, rows_p = _layer(x_prompt, pos_p, None, m3(mkb, bp), m3(mvb, bp), wp, l, _CFG_PROMPT)
    past = (cache_mla_ckv[l], cache_mla_krope[l], cache_diff_k[l], cache_diff_v[l])
    x1s, hns, rows_s = _layer(x_sample, pos_s, past, m3(cache_mem_k[l].astype(BF16), bs),
                              m3(cache_mem_v[l].astype(BF16), bs), wp, l, _CFG_SAMPLE)
    y_prompt = _peer_and_norm(x1p, hnp, wp, _CFG_PROMPT).reshape(bp, sp, d)
    y_sample = _peer_and_norm(x1s, hns, wp, _CFG_SAMPLE).reshape(bs, ss, d)
    st = lambda a: a[None]
    mem4 = lambda a: a.reshape(bp, n_mem, MEM_HEADS, MEM_DH)[None]
    return (y_prompt, y_sample,
            st(rows_p[0]), st(rows_p[1]), st(rows_p[2]), st(rows_p[3]),
            mem4(mk), mem4(mv),
            st(rows_s[0]), st(rows_s[1]), st(rows_s[2]), st(rows_s[3]))
```

```python
import functools
import math

import jax
import jax.numpy as jnp
import numpy as np
from jax import lax
from jax.experimental import pallas as pl
from jax.experimental.pallas import tpu as pltpu

F32 = jnp.float32
BF16 = jnp.bfloat16

CHUNK = 64
CHUNK_SHIFT = 6
EPS = 1e-6
NEG = -1e30
MLA_HEADS = 8
MLA_Q_LORA = 384
MLA_KV_LORA = 256
MLA_NOPE = 64
MLA_ROPE = 32
MLA_V = 64
ROPE_THETA = 10000.0
LOG2E = math.log2(math.e)
MLA_SCALE = (MLA_NOPE + MLA_ROPE) ** -0.5 * LOG2E
DIFF_HEADS = 8
DIFF_DH = 32
DIFF_V = 2 * DIFF_DH
DIFF_SCALE = DIFF_DH ** -0.5 * LOG2E
MEM_HEADS = 4
MEM_DH = 128
MEM_SCALE = MEM_DH ** -0.5 * LOG2E
PEER_HEADS = 8
PEER_NKEYS = 128
PEER_HALF = 128
PEER_TOPK = 16
LANE = 128
HEAD_W = 128

VMEM_LIMIT = 56 * 1024 * 1024

_C_CQ = 0
_C_CKV = _C_CQ + MLA_Q_LORA
_C_KR = _C_CKV + MLA_KV_LORA
_C_KRR = _C_KR + LANE
_C_DQ = _C_KRR + LANE
_C_DK = _C_DQ + DIFF_HEADS * DIFF_V
_C_DV = _C_DK + DIFF_HEADS * DIFF_V
_C_MQ = _C_DV + DIFF_HEADS * DIFF_V
_C_END = _C_MQ + MEM_HEADS * MEM_DH


def _cparams(sem):
    return pltpu.CompilerParams(dimension_semantics=sem, vmem_limit_bytes=VMEM_LIMIT)


def _rms(x, g):
    return x * lax.rsqrt(jnp.mean(x * x, axis=-1, keepdims=True) + EPS) * g


def _dot(a, b):
    return jnp.dot(a, b, preferred_element_type=F32)


def _dot_nt(a, b):
    return lax.dot_general(a, b, (((1,), (1,)), ((), ())), preferred_element_type=F32)


def _full_spec(shape):
    nd = len(shape)
    return pl.BlockSpec(shape, lambda *_: (0,) * nd)


def _proj_in_kernel(x_ref, g_ref, win_ref, qn_ref, wuq_ref, kvn_ref, wuk_ref, pk_ref, wuv_ref,
                    cq_ref, sq_ref, ck_ref, sk_ref,
                    ckv_ref, kr_ref, dk_ref, dv_ref, qm_ref, dqb_ref, dkb_ref, dvb_ref, mqb_ref, km_ref, vm_ref):
    h = _rms(x_ref[...], g_ref[...]).astype(BF16)
    z = _dot(h, win_ref[...])
    cqn = _rms(z[:, _C_CQ:_C_CKV], qn_ref[...]).astype(BF16)
    q2 = _dot(cqn, wuq_ref[...])
    cq = cq_ref[...]
    sq = sq_ref[...]
    nq = MLA_HEADS * HEAD_W
    for hh in range(MLA_HEADS):
        lo = hh * HEAD_W
        qm_ref[:, lo:lo + HEAD_W] = (q2[:, lo:lo + HEAD_W] * cq
                                     + q2[:, nq + lo:nq + lo + HEAD_W] * sq).astype(BF16)
    ckv = _rms(z[:, _C_CKV:_C_KR], kvn_ref[...])
    ckv_ref[...] = ckv
    kr = (z[:, _C_KR:_C_KRR] * ck_ref[...] + z[:, _C_KRR:_C_DQ] * sk_ref[...])[:, :MLA_ROPE]
    kr_ref[...] = kr
    c = ckv.astype(BF16)
    km_ref[...] = (_dot(c, wuk_ref[...]) + _dot(kr.astype(BF16), pk_ref[...])).astype(BF16)
    vm_ref[...] = _dot(c, wuv_ref[...]).astype(BF16)
    dk = z[:, _C_DK:_C_DV]
    dv = z[:, _C_DV:_C_MQ]
    dk_ref[...] = dk.reshape(dk_ref.shape)
    dv_ref[...] = dv.reshape(dv_ref.shape)
    dkb_ref[...] = dk.astype(BF16)
    dvb_ref[...] = dv.astype(BF16)
    dqb_ref[...] = (z[:, _C_DQ:_C_DK] * DIFF_SCALE).astype(BF16)
    mqb_ref[...] = (z[:, _C_MQ:_C_END] * MEM_SCALE).astype(BF16)


def _proj_in(x, tabs, wp, *, tm, pos_blocks):
    n, d = x.shape
    grid = (n // tm,)
    row = lambda w: pl.BlockSpec((tm, w), lambda i: (i, 0))
    tab = pl.BlockSpec((tm, LANE), lambda i: (i % pos_blocks, 0))
    heads = pl.BlockSpec((tm, DIFF_HEADS, DIFF_V), lambda i: (i, 0, 0))
    dw = DIFF_HEADS * DIFF_V
    out_shape = (
        jax.ShapeDtypeStruct((n, MLA_KV_LORA), F32),
        jax.ShapeDtypeStruct((n, MLA_ROPE), F32),
        jax.ShapeDtypeStruct((n, DIFF_HEADS, DIFF_V), F32),
        jax.ShapeDtypeStruct((n, DIFF_HEADS, DIFF_V), F32),
        jax.ShapeDtypeStruct((n, MLA_HEADS * HEAD_W), BF16),
        jax.ShapeDtypeStruct((n, dw), BF16),
        jax.ShapeDtypeStruct((n, dw), BF16),
        jax.ShapeDtypeStruct((n, dw), BF16),
        jax.ShapeDtypeStruct((n, MEM_HEADS * MEM_DH), BF16),
        jax.ShapeDtypeStruct((n, MLA_HEADS * HEAD_W), BF16),
        jax.ShapeDtypeStruct((n, MLA_HEADS * MLA_V), BF16),
    )
    return pl.pallas_call(
        _proj_in_kernel,
        grid=grid,
        in_specs=[row(d), _full_spec((1, d)), _full_spec(wp["w_in"].shape), _full_spec((1, MLA_Q_LORA)),
                  _full_spec(wp["w_uq"].shape), _full_spec((1, MLA_KV_LORA)), _full_spec(wp["w_uk"].shape),
                  _full_spec(wp["p_kr"].shape), _full_spec(wp["w_uv"].shape), tab, tab, tab, tab],
        out_specs=(row(MLA_KV_LORA), row(MLA_ROPE), heads, heads, row(MLA_HEADS * HEAD_W),
                   row(dw), row(dw), row(dw), row(MEM_HEADS * MEM_DH), row(MLA_HEADS * HEAD_W),
                   row(MLA_HEADS * MLA_V)),
        out_shape=out_shape,
        compiler_params=_cparams(("parallel",)),
        name="proj_in",
    )(x, wp["norm_mix"], wp["w_in"], wp["mla_q_norm"], wp["w_uq"], wp["mla_kv_norm"], wp["w_uk"], wp["p_kr"],
      wp["w_uv"], *tabs)


def _kv_up_kernel(ckv_ref, kr_ref, wuk_ref, pk_ref, wuv_ref, k_ref, v_ref):
    c = ckv_ref[...].astype(BF16)
    k = _dot(c, wuk_ref[...]) + _dot(kr_ref[...].astype(BF16), pk_ref[...])
    k_ref[...] = k.astype(BF16)
    v_ref[...] = _dot(c, wuv_ref[...]).astype(BF16)


def _kv_up(ckv, kr, wp, *, tm):
    n = ckv.shape[0]
    row = lambda w: pl.BlockSpec((tm, w), lambda i: (i, 0))
    return pl.pallas_call(
        _kv_up_kernel,
        grid=(n // tm,),
        in_specs=[row(MLA_KV_LORA), row(MLA_ROPE), _full_spec(wp["w_uk"].shape),
                  _full_spec(wp["p_kr"].shape), _full_spec(wp["w_uv"].shape)],
        out_specs=(row(MLA_HEADS * HEAD_W), row(MLA_HEADS * MLA_V)),
        out_shape=(jax.ShapeDtypeStruct((n, MLA_HEADS * HEAD_W), BF16),
                   jax.ShapeDtypeStruct((n, MLA_HEADS * MLA_V), BF16)),
        compiler_params=_cparams(("parallel",)),
        name="kv_up",
    )(ckv, kr, wp["w_uk"], wp["p_kr"], wp["w_uv"])


def _lanes(col, n):
    if n == LANE:
        return col
    if n % LANE == 0:
        return jnp.concatenate([col] * (n // LANE), axis=1)
    return jnp.broadcast_to(col[:, :1], (col.shape[0], n))


GROUP_VREGS = 16


def _softmax_block(score_group, tq, tk, row0, m_scr, al_scr, p_scr, l_scr=None):
    rg = min(tq, max(16, (GROUP_VREGS * 8 * LANE // tk) // 16 * 16))
    for g in range(tq // rg):
        r = slice(row0 + g * rg, row0 + (g + 1) * rg)
        m_prev = m_scr[r]
        m_next = jnp.maximum(m_prev, jnp.max(score_group(g, rg), axis=1, keepdims=True))
        al_scr[r] = jnp.exp2(m_prev - m_next)
        m_scr[r] = m_next
    for g in range(tq // rg):
        r = slice(row0 + g * rg, row0 + (g + 1) * rg)
        e = jnp.exp2(score_group(g, rg) - _lanes(m_scr[r], tk))
        p_scr[r, :tk] = e.astype(BF16)
        if l_scr is not None:
            l_scr[r] = al_scr[r] * l_scr[r] + jnp.sum(e, axis=1, keepdims=True)


def _accumulate(rows, v, al_scr, l_scr, acc_scr, p_scr):
    pv = _dot(p_scr[rows, :v.shape[0]], jnp.concatenate([v, jnp.ones(v.shape, v.dtype)], axis=1))
    w = v.shape[1]
    acc_scr[rows] = acc_scr[rows] * al_scr[rows] + pv[:, :w]
    l_scr[rows] = l_scr[rows] * al_scr[rows] + pv[:, w:]


def _chunk_visible(q_lo, k_lo, g, rg, tk):
    qp = q_lo + g * rg + lax.broadcasted_iota(jnp.int32, (rg, tk), 0)
    kp = k_lo + lax.broadcasted_iota(jnp.int32, (rg, tk), 1)
    return qp, kp, (kp >> CHUNK_SHIFT) <= (qp >> CHUNK_SHIFT)


def _init_stats(m_scr, l_scr, acc_scr):
    m_scr[...] = jnp.full(m_scr.shape, NEG, F32)
    l_scr[...] = jnp.zeros(l_scr.shape, F32)
    acc_scr[...] = jnp.zeros(acc_scr.shape, F32)


def _block_range(qi, *, tq, tk, q_off, nk):
    q_lo = q_off + qi * tq
    q_hi = q_lo + (tq - 1)
    n_behind = jnp.minimum((q_lo + 1) // tk, nk)
    last_key = ((q_hi >> CHUNK_SHIFT) << CHUNK_SHIFT) + (CHUNK - 1)
    n_need = jnp.minimum(last_key // tk + 1, nk)
    return q_lo, n_behind, n_need


def _for_blocks(lo, hi, fn):
    lax.fori_loop(lo, hi, lambda i, c: (fn(i), c)[1], 0)


def _run_blocks(qi, block, *, tq, tk, q_off, nk):
    _, n_behind, n_need = _block_range(qi, tq=tq, tk=tk, q_off=q_off, nk=nk)
    _for_blocks(0, n_behind, lambda i: block(pl.multiple_of(i * tk, tk), False, tk, 0, tq))
    if tq == tk and q_off % tq == 0 and tq % (2 * CHUNK) == 0:
        half = tk // 2
        k_lo = pl.multiple_of(n_behind * tk, tk)
        block(k_lo, True, half, 0, tq)
        block(pl.multiple_of(k_lo + half, half), True, half, half, half)
    else:
        _for_blocks(n_behind, n_need, lambda i: block(pl.multiple_of(i * tk, tk), True, tk, 0, tq))


def _attn_specs(b, pairs, sq, sk, tq, qw, kw, vw):
    grid = (b, pairs, sq // tq)
    in_specs = [pl.BlockSpec((1, tq, qw), lambda bi, p, qi: (bi, qi, p)),
                pl.BlockSpec((1, sk, kw), lambda bi, p, qi: (bi, 0, p)),
                pl.BlockSpec((1, sk, vw), lambda bi, p, qi: (bi, 0, p))]
    out_spec = pl.BlockSpec((1, tq, vw), lambda bi, p, qi: (bi, qi, p))
    return grid, in_specs, out_spec


def _mla_kernel(q_ref, k_ref, v_ref, o_ref, m_scr, l_scr, al_scr, acc_scr, p_scr, *, tq, tk, q_off, nk):
    qi = pl.program_id(2)
    _init_stats(m_scr, l_scr, acc_scr)
    q_lo = q_off + qi * tq

    def block(k_lo, masked, tkk, r0, nr):
        for hh in range(2):
            hs = slice(hh * HEAD_W, (hh + 1) * HEAD_W)
            s = _dot_nt(q_ref[0, r0:r0 + nr, hs], k_ref[0, pl.ds(k_lo, tkk), hs])

            def score_group(g, rg, s=s):
                sg = s[g * rg:(g + 1) * rg]
                if masked:
                    sg = jnp.where(_chunk_visible(q_lo + r0, k_lo, g, rg, tkk)[2], sg, NEG)
                return sg

            _softmax_block(score_group, nr, tkk, hh * tq + r0, m_scr, al_scr, p_scr)
        v = v_ref[0, pl.ds(k_lo, tkk), :]
        if nr == tq:
            _accumulate(slice(0, 2 * tq), v, al_scr, l_scr, acc_scr, p_scr)
        else:
            for hh in range(2):
                _accumulate(slice(hh * tq + r0, hh * tq + r0 + nr), v, al_scr, l_scr, acc_scr, p_scr)

    _run_blocks(qi, block, tq=tq, tk=tk, q_off=q_off, nk=nk)

    lane = lax.broadcasted_iota(jnp.int32, (tq, LANE), 1)
    o0 = acc_scr[0:tq] / l_scr[0:tq]
    o1 = acc_scr[tq:2 * tq] / l_scr[tq:2 * tq]
    o_ref[0] = jnp.where(lane < MLA_V, o0, o1).astype(o_ref.dtype)


def _mla_attention(q, k, v, *, tq, tk, q_off):
    b, sq, _ = q.shape
    sk = k.shape[1]
    grid, in_specs, out_spec = _attn_specs(b, MLA_HEADS // 2, sq, sk, tq, 2 * HEAD_W, 2 * HEAD_W, 2 * MLA_V)
    kern = functools.partial(_mla_kernel, tq=tq, tk=tk, q_off=q_off, nk=sk // tk)
    stat = pltpu.VMEM((2 * tq, LANE), F32)
    return pl.pallas_call(
        kern,
        grid=grid,
        in_specs=in_specs,
        out_specs=out_spec,
        out_shape=jax.ShapeDtypeStruct((b, sq, MLA_HEADS * MLA_V), BF16),
        scratch_shapes=[stat, stat, stat, stat, pltpu.VMEM((2 * tq, tk), BF16)],
        compiler_params=_cparams(("parallel", "parallel", "parallel")),
        name="mla_attn",
    )(q, k, v)


def _diff_kernel(slope_ref, lq1_ref, lk1_ref, lq2_ref, lk2_ref, sub_ref, q_ref, k_ref, v_ref, o_ref,
                 qs_scr, m_scr, l_scr, al_scr, acc_scr, p_scr, *, tq, tk, q_off, nk, lam_init):
    pr = pl.program_id(1)
    qi = pl.program_id(2)
    _init_stats(m_scr, l_scr, acc_scr)
    q_lo = q_off + qi * tq

    q = q_ref[0]
    lane = lax.broadcasted_iota(jnp.int32, (tq, LANE), 1)
    for mi in range(4):
        lo = mi * DIFF_DH
        qs_scr[mi * tq:(mi + 1) * tq] = jnp.where((lane >= lo) & (lane < lo + DIFF_DH), q, jnp.zeros_like(q))

    def block(k_lo, masked, tkk, r0, nr):
        kb = k_ref[0, pl.ds(k_lo, tkk), :]
        kpos = (k_lo + lax.broadcasted_iota(jnp.int32, (1, tkk), 1)).astype(F32)

        def scores(mi):
            return _dot_nt(qs_scr[mi * tq + r0:mi * tq + r0 + nr], kb)

        def softmax(mi, s_map):
            slope = slope_ref[2 * pr + mi // 2] * LOG2E

            def score_group(g, rg):
                sg = s_map[g * rg:(g + 1) * rg]
                if masked:
                    qp, kp, vis = _chunk_visible(q_lo + r0, k_lo, g, rg, tkk)
                    return jnp.where(vis, sg + slope * jnp.minimum(kp, 2 * qp - kp).astype(F32), NEG)
                return sg + slope * kpos

            _softmax_block(score_group, nr, tkk, mi * tq + r0, m_scr, al_scr, p_scr, l_scr)

        def values(mi):
            rows = slice(mi * tq + r0, mi * tq + r0 + nr)
            acc_scr[rows] = acc_scr[rows] * al_scr[rows] + _dot(p_scr[rows, :tkk], v_ref[0, pl.ds(k_lo, tkk), :])

        s0 = scores(0)
        s1 = scores(1)
        softmax(0, s0)
        s2 = scores(2)
        values(0)
        softmax(1, s1)
        s3 = scores(3)
        values(1)
        softmax(2, s2)
        values(2)
        softmax(3, s3)
        values(3)

    _run_blocks(qi, block, tq=tq, tk=tk, q_off=q_off, nk=nk)

    lam = (jnp.exp(jnp.sum(lq1_ref[...] * lk1_ref[...], axis=1, keepdims=True))
           - jnp.exp(jnp.sum(lq2_ref[...] * lk2_ref[...], axis=1, keepdims=True)) + lam_init)
    first = lane < DIFF_V
    on = [acc_scr[mi * tq:(mi + 1) * tq] / l_scr[mi * tq:(mi + 1) * tq] for mi in range(4)]
    o = jnp.where(first, on[0] - lam * on[1], on[2] - lam * on[3])
    sq = o * o
    ms0 = jnp.sum(jnp.where(first, sq, 0.0), axis=1, keepdims=True) * (1.0 / DIFF_V)
    ms1 = jnp.sum(jnp.where(first, 0.0, sq), axis=1, keepdims=True) * (1.0 / DIFF_V)
    r = jnp.where(first, lax.rsqrt(ms0 + EPS), lax.rsqrt(ms1 + EPS))
    o_ref[0] = ((o * r * sub_ref[...]) * (1.0 - lam_init)).astype(o_ref.dtype)


def _diff_attention(q, k, v, wp, *, tq, tk, q_off, lam_init):
    b, sq, _ = q.shape
    sk = k.shape[1]
    grid, in_specs, out_spec = _attn_specs(b, DIFF_HEADS // 2, sq, sk, tq, LANE, LANE, LANE)
    kern = functools.partial(_diff_kernel, tq=tq, tk=tk, q_off=q_off, nk=sk // tk, lam_init=lam_init)
    small = lambda w: pl.BlockSpec((1, w), lambda bi, p, qi: (0, 0))
    stat = pltpu.VMEM((4 * tq, LANE), F32)
    return pl.pallas_call(
        kern,
        grid=grid,
        in_specs=[pl.BlockSpec(memory_space=pltpu.SMEM),
                  small(DIFF_DH), small(DIFF_DH), small(DIFF_DH), small(DIFF_DH), small(LANE)] + in_specs,
        out_specs=out_spec,
        out_shape=jax.ShapeDtypeStruct((b, sq, DIFF_HEADS * DIFF_V), BF16),
        scratch_shapes=[pltpu.VMEM((4 * tq, LANE), BF16), stat, stat, stat, stat,
                        pltpu.VMEM((4 * tq, tk), BF16)],
        compiler_params=_cparams(("parallel", "parallel", "parallel")),
        name="diff_attn",
    )(wp["slopes"], wp["diff_lq1"], wp["diff_lk1"], wp["diff_lq2"], wp["diff_lk2"], wp["diff_subln2"],
      q, k, v)


def _mem_attn_kernel(q_ref, k_ref, v_ref, o_ref):
    for hh in range(MEM_HEADS):
        sl = slice(hh * MEM_DH, (hh + 1) * MEM_DH)
        s = _dot_nt(q_ref[0, :, sl], k_ref[0, :, sl])
        p = jnp.exp2(s - jnp.max(s, axis=1, keepdims=True))
        o = _dot(p.astype(BF16), v_ref[0, :, sl]) / jnp.sum(p, axis=1, keepdims=True)
        o_ref[0, :, sl] = o.astype(o_ref.dtype)


def _mem_attention(q, k, v, *, tq):
    b, sq, w = q.shape
    nm = k.shape[1]
    return pl.pallas_call(
        _mem_attn_kernel,
        grid=(b, sq // tq),
        in_specs=[pl.BlockSpec((1, tq, w), lambda bi, qi: (bi, qi, 0)),
                  pl.BlockSpec((1, nm, w), lambda bi, qi: (bi, 0, 0)),
                  pl.BlockSpec((1, nm, w), lambda bi, qi: (bi, 0, 0))],
        out_specs=pl.BlockSpec((1, tq, w), lambda bi, qi: (bi, qi, 0)),
        out_shape=jax.ShapeDtypeStruct((b, sq, w), BF16),
        compiler_params=_cparams(("parallel", "parallel")),
        name="mem_attn",
    )(q, k, v)


def _mem_kv_kernel(x_ref, g_ref, w_ref, k_ref, v_ref, kb_ref, vb_ref):
    h = _rms(x_ref[...], g_ref[...]).astype(BF16)
    kv = _dot(h, w_ref[...])
    w = MEM_HEADS * MEM_DH
    k_ref[...] = kv[:, :w]
    v_ref[...] = kv[:, w:]
    kb_ref[...] = kv[:, :w].astype(BF16)
    vb_ref[...] = kv[:, w:].astype(BF16)


def _mem_kv(mem, wp, *, tm):
    n, d = mem.shape
    w = MEM_HEADS * MEM_DH
    row = lambda c: pl.BlockSpec((tm, c), lambda i: (i, 0))
    return pl.pallas_call(
        _mem_kv_kernel,
        grid=(n // tm,),
        in_specs=[row(d), _full_spec((1, d)), _full_spec(wp["w_mem_kv"].shape)],
        out_specs=(row(w), row(w), row(w), row(w)),
        out_shape=(jax.ShapeDtypeStruct((n, w), F32), jax.ShapeDtypeStruct((n, w), F32),
                   jax.ShapeDtypeStruct((n, w), BF16), jax.ShapeDtypeStruct((n, w), BF16)),
        compiler_params=_cparams(("parallel",)),
        name="mem_kv",
    )(mem, wp["norm_mem"], wp["w_mem_kv"])


def _merge_kernel(x_ref, oa_ref, ob_ref, om_ref, g_ref, wg_ref, bg_ref, wa_ref, wb_ref, wm_ref, wo_ref,
                  gf_ref, x1_ref, hn_ref):
    x = x_ref[...]
    d = x.shape[1]
    h = _rms(x, g_ref[...]).astype(BF16)
    gates = jax.nn.sigmoid(_dot(h, wg_ref[...]) + bg_ref[...])
    merged = (gates[:, :d] * _dot(oa_ref[...], wa_ref[...])
              + gates[:, d:2 * d] * _dot(ob_ref[...], wb_ref[...])
              + gates[:, 2 * d:] * _dot(om_ref[...], wm_ref[...]))
    x1 = x + _dot(merged.astype(BF16), wo_ref[...])
    x1_ref[...] = x1
    hn_ref[...] = _rms(x1, gf_ref[...]).astype(BF16)


def _merge(x, oa, ob, om, wp, *, tm):
    n, d = x.shape
    row = lambda c: pl.BlockSpec((tm, c), lambda i: (i, 0))
    names = ("norm_mix", "w_gate", "b_gate", "w_br_a", "w_br_b", "w_br_m", "w_o", "norm_ffn")
    return pl.pallas_call(
        _merge_kernel,
        grid=(n // tm,),
        in_specs=[row(d), row(oa.shape[1]), row(ob.shape[1]), row(om.shape[1])]
                 + [_full_spec(wp[k].shape) for k in names],
        out_specs=(row(d), row(d)),
        out_shape=(jax.ShapeDtypeStruct((n, d), F32), jax.ShapeDtypeStruct((n, d), BF16)),
        compiler_params=_cparams(("parallel",)),
        name="merge",
    )(x, oa, ob, om, *[wp[k] for k in names])


def _top_values(sc, k, with_rank=False):
    vals = []
    cur = sc
    rank = jnp.full(sc.shape, float(k), F32) if with_rank else None
    for r in range(k):
        m = jnp.max(cur, axis=0, keepdims=True)
        vals.append(m)
        hit = cur == m
        if with_rank:
            rank = jnp.where(hit, float(r), rank)
        if r + 1 < k:
            cur = jnp.where(hit, NEG, cur)
    return (vals, rank) if with_rank else vals


def _stack_rows(rows, sub):
    out = jnp.zeros(sub.shape, F32)
    for r, row in enumerate(rows):
        if row is not None:
            out = jnp.where(sub == r, row, out)
    return out


def _packed_candidates(t1, t2, t2h, sub8):
    assert PEER_TOPK == 16
    down = lambda x, k: pltpu.roll(x, k, 0)
    t1_8_13 = _stack_rows([None, None] + t1[8:14], sub8)
    t1_14_15 = _stack_rows(t1[14:16], sub8)
    top2 = t2[0:1]
    slabs = [
        t1[0] + t2h,
        t1[0] + t2[8:16],
        t1[1] + t2h,
        jnp.where(sub8 < 5, t1[2] + t2h, t1[4] + down(t2h, 5)),
        jnp.where(sub8 < 4, t1[3] + t2h,
                  jnp.where(sub8 < 6, t1[5] + down(t2h, 4), t1[6] + down(t2h, 6))),
        jnp.where(sub8 < 2, t1[7] + t2h, t1_8_13 + top2),
        jnp.where(sub8 < 2, t1_14_15 + top2, NEG),
    ]
    return jnp.concatenate(slabs, axis=0)


EXACT_RANKS = 4
CNT_CAP = PEER_TOPK // (EXACT_RANKS + 1)


def _route_kernel(hn_ref, wq_ref, sk_ref, cnt_ref, e1_ref, rank_ref, e2_ref):
    tt = hn_ref.shape[0]
    q = _dot(hn_ref[...], wq_ref[...]).astype(BF16)
    sub = lax.broadcasted_iota(jnp.int32, (PEER_TOPK, tt), 0)
    sub8 = lax.broadcasted_iota(jnp.int32, (8, tt), 0)
    for h in range(PEER_HEADS):
        sc1 = _dot_nt(sk_ref[2 * h], q[:, (2 * h) * PEER_HALF:(2 * h + 1) * PEER_HALF])
        sc2 = _dot_nt(sk_ref[2 * h + 1], q[:, (2 * h + 1) * PEER_HALF:(2 * h + 2) * PEER_HALF])
        t1 = _top_values(sc1, PEER_TOPK)
        t2_rows, rank2 = _top_values(sc2, PEER_TOPK, with_rank=True)
        t2 = _stack_rows(t2_rows, sub)
        t2h = t2[:8]
        cands = []
        for a in range(PEER_TOPK):
            nb = PEER_TOPK // (a + 1)
            c = t1[a] + (t2 if nb > 8 else t2h)
            if nb < c.shape[0]:
                c = jnp.where((sub if nb > 8 else sub8) < nb, c, NEG)
            cands.append(c)
        cand = _packed_candidates(t1, t2, t2h, sub8)
        tau = _top_values(cand, PEER_TOPK)[-1]
        top = t1[0] + t2[0:1]
        z = jnp.sum(jnp.where(cand >= tau, jnp.exp(cand - top), 0.0), axis=0, keepdims=True)
        counts = [jnp.sum(jnp.where(cands[a] >= tau, 1.0, 0.0), axis=0, keepdims=True) for a in range(PEER_TOPK)]
        cnt = jnp.zeros(sc1.shape, F32)
        for c in range(1, CNT_CAP + 1):
            theta = jnp.full(tau.shape, -NEG, F32)
            for a in range(PEER_TOPK):
                theta = jnp.where(counts[a] >= c, t1[a], theta)
            cnt = jnp.where(sc1 >= theta, float(c), cnt)
        for a in range(EXACT_RANKS):
            cnt = jnp.where(sc1 == t1[a], counts[a], cnt)
        cnt_ref[h] = cnt
        e1_ref[h] = jnp.exp(sc1 - t1[0]) * (0.5 / z)
        rank_ref[h] = rank2.astype(BF16)
        e2_ref[h] = jnp.exp(sc2 - t2[0:1]).astype(BF16)


def _route(hn, wp, *, tt):
    n, d = hn.shape
    big = pl.BlockSpec((PEER_HEADS, PEER_NKEYS, tt), lambda i: (0, 0, i))
    big_shape = jax.ShapeDtypeStruct((PEER_HEADS, PEER_NKEYS, n), F32)
    big_half = jax.ShapeDtypeStruct((PEER_HEADS, PEER_NKEYS, n), BF16)
    return pl.pallas_call(
        _route_kernel,
        grid=(n // tt,),
        in_specs=[pl.BlockSpec((tt, d), lambda i: (i, 0)), _full_spec(wp["peer_wq"].shape),
                  _full_spec(wp["peer_subkeys"].shape)],
        out_specs=(big, big, big, big),
        out_shape=(big_shape, big_shape, big_half, big_half),
        compiler_params=_cparams(("parallel",)),
        name="peer_route",
    )(hn, wp["peer_wq"], wp["peer_subkeys"])


PAIR = 2 * PEER_NKEYS
SUBROWS = 128


def _peer_kernel(hn_in, u_ref, vt_ref, cnt_ref, e1_ref, rank_in, e2_in, x1_ref, gn_ref, y_ref,
                 acc_scr, hw_scr, rank_ref, e2_ref, hn_ref, *, te, ne):
    j = pl.program_id(1)
    tt = hn_in.shape[0]
    npair = te // PAIR

    @pl.when(j == 0)
    def _():
        rank_ref[...] = rank_in[...]
        e2_ref[...] = e2_in[...]
        hn_ref[...] = hn_in[...]

    def pre_act(p):
        a = _dot_nt(u_ref[p * PAIR:(p + 1) * PAIR, :], hn_ref[...])
        return a * (1.0 + lax.erf(a * math.sqrt(0.5)))

    def down(lo, hi):
        return _dot(vt_ref[:, lo * PAIR:hi * PAIR], hw_scr[lo * PAIR:hi * PAIR, :])

    cuts = sorted({0, 3 * npair // 8, 3 * npair // 4, npair - 1, npair})
    piece_after = {hi: lo for lo, hi in zip(cuts[:-1], cuts[1:])}

    groups = [(tc, sb) for tc in range(tt // LANE) for sb in range(PEER_NKEYS // SUBROWS)]
    zero = jnp.zeros((), BF16)
    act = pre_act(0)
    tot = None
    carry = None
    for p in range(npair):
        nxt = None
        for gi, (tc, sb) in enumerate(groups):
            if gi == 1 and p + 1 < npair:
                nxt = pre_act(p + 1)
            if gi == len(groups) // 2 and p in piece_after:
                part = down(piece_after[p], p)
                tot = part if tot is None else tot + part
            ls = slice(tc * LANE, (tc + 1) * LANE)
            i2 = slice(sb * SUBROWS, (sb + 1) * SUBROWS)
            w = [None, None] if carry is None else [carry * zero, carry * zero]
            for h in range(PEER_HEADS):
                rk = rank_ref[h, i2, ls]
                e2b = e2_ref[h, i2, ls]
                for rr in range(2):
                    i1 = 2 * p + rr
                    cnt = jnp.broadcast_to(cnt_ref[h, i1:i1 + 1, ls], (SUBROWS, LANE)).astype(BF16)
                    e1b = jnp.broadcast_to(e1_ref[h, i1:i1 + 1, ls], (SUBROWS, LANE)).astype(BF16)
                    cw = jnp.where(rk < cnt, e2b, zero) * e1b
                    w[rr] = cw if w[rr] is None else w[rr] + cw
            carry = w[1]
            for rr in range(2):
                r0 = rr * PEER_NKEYS + sb * SUBROWS
                hw_scr[p * PAIR + r0:p * PAIR + r0 + SUBROWS, ls] = w[rr] * act[r0:r0 + SUBROWS, ls].astype(BF16)
        act = nxt
    part = down(piece_after[npair], npair)
    tot = part if tot is None else tot + part

    @pl.when(j == 0)
    def _():
        acc_scr[...] = tot

    @pl.when(j > 0)
    def _():
        acc_scr[...] += tot

    @pl.when(j == ne - 1)
    def _():
        xr = x1_ref[...] + acc_scr[...].T
        y_ref[...] = _rms(xr, gn_ref[...])


def _peer(hn, cnt, e1, rank2, e2, x1, wp, *, tt, te):
    n, d = hn.shape
    ne = wp["peer_u"].shape[0] // te
    big = pl.BlockSpec((PEER_HEADS, PEER_NKEYS, tt), lambda i, j: (0, 0, i))
    rows = pl.BlockSpec((PEER_HEADS, te // PEER_NKEYS, tt), lambda i, j: (0, j, i))
    kern = functools.partial(_peer_kernel, te=te, ne=ne)
    return pl.pallas_call(
        kern,
        grid=(n // tt, ne),
        in_specs=[pl.BlockSpec((tt, d), lambda i, j: (i, 0)),
                  pl.BlockSpec((te, d), lambda i, j: (j, 0)),
                  pl.BlockSpec((d, te), lambda i, j: (0, j)),
                  rows, rows, big, big,
                  pl.BlockSpec((tt, d), lambda i, j: (i, 0)),
                  pl.BlockSpec((1, d), lambda i, j: (0, 0))],
        out_specs=pl.BlockSpec((tt, d), lambda i, j: (i, 0)),
        out_shape=jax.ShapeDtypeStruct((n, d), F32),
        scratch_shapes=[pltpu.VMEM((d, tt), F32), pltpu.VMEM((te, tt), BF16),
                        pltpu.VMEM((PEER_HEADS, PEER_NKEYS, tt), BF16),
                        pltpu.VMEM((PEER_HEADS, PEER_NKEYS, tt), BF16),
                        pltpu.VMEM((tt, d), BF16)],
        compiler_params=_cparams(("parallel", "arbitrary")),
        name="peer_experts",
    )(hn, wp["peer_u"], wp["peer_vt"], cnt, e1, rank2, e2, x1, wp["norm_final"])


def _prep_weights(l, p):
    f = lambda a: a.astype(F32)
    w_in = f(p["w_in"][l])
    o_cq, o_ckv, o_kr = 0, MLA_Q_LORA, MLA_Q_LORA + MLA_KV_LORA
    o_dq = o_kr + MLA_ROPE
    half = MLA_ROPE // 2
    d = w_in.shape[0]
    kr = w_in[:, o_kr:o_dq]
    pad = jnp.zeros((d, LANE - MLA_ROPE), F32)
    w_in2 = jnp.concatenate([
        w_in[:, o_cq:o_kr], kr, pad,
        -kr[:, half:], kr[:, :half], pad,
        w_in[:, o_dq:]], axis=1)
    assert w_in2.shape[1] == _C_END

    w_uq = f(p["w_uq"][l]).reshape(MLA_Q_LORA, MLA_HEADS, MLA_NOPE + MLA_ROPE)
    nope, x1, x2 = w_uq[..., :MLA_NOPE], w_uq[..., MLA_NOPE:MLA_NOPE + half], w_uq[..., MLA_NOPE + half:]
    zpad = jnp.zeros((MLA_Q_LORA, MLA_HEADS, HEAD_W - MLA_NOPE - MLA_ROPE), F32)
    q_slab = jnp.concatenate([nope, x1, x2, zpad], axis=-1).reshape(MLA_Q_LORA, -1)
    r_slab = jnp.concatenate([jnp.zeros_like(nope), -x2, x1, zpad], axis=-1).reshape(MLA_Q_LORA, -1)
    w_uq2 = jnp.concatenate([q_slab, r_slab], axis=1)

    w_uk = f(p["w_uk"][l]).reshape(MLA_KV_LORA, MLA_HEADS, MLA_NOPE)
    w_uk2 = jnp.concatenate([w_uk, jnp.zeros((MLA_KV_LORA, MLA_HEADS, HEAD_W - MLA_NOPE), F32)],
                            axis=-1).reshape(MLA_KV_LORA, -1)
    r_idx = jnp.arange(MLA_ROPE)
    cols = jnp.arange(MLA_HEADS * HEAD_W)
    p_kr = ((cols[None, :] % HEAD_W) == (MLA_NOPE + r_idx[:, None])).astype(F32)

    slopes = jnp.exp2(-8.0 * jnp.arange(1, DIFF_HEADS + 1, dtype=F32) / DIFF_HEADS)
    sub = f(p["diff_subln"][l])
    row = lambda a: f(a).reshape(1, -1)
    return {
        "norm_mix": row(p["norm_mix"][l]), "w_in": w_in2.astype(BF16),
        "mla_q_norm": row(p["mla_q_norm"][l]), "w_uq": w_uq2.astype(BF16),
        "mla_kv_norm": row(p["mla_kv_norm"][l]),
        "w_uk": w_uk2.astype(BF16), "p_kr": p_kr.astype(BF16), "w_uv": p["w_uv"][l].astype(BF16),
        "slopes": slopes,
        "diff_lq1": row(p["diff_lq1"][l]), "diff_lk1": row(p["diff_lk1"][l]),
        "diff_lq2": row(p["diff_lq2"][l]), "diff_lk2": row(p["diff_lk2"][l]),
        "diff_subln2": jnp.concatenate([sub, sub]).reshape(1, -1),
        "norm_mem": row(p["norm_mem"][l]), "w_mem_kv": p["w_mem_kv"][l].astype(BF16),
        "w_gate": p["w_gate"][l].astype(BF16), "b_gate": row(p["b_gate"][l]),
        "w_br_a": p["w_br_a"][l].astype(BF16), "w_br_b": p["w_br_b"][l].astype(BF16),
        "w_br_m": p["w_br_m"][l].astype(BF16), "w_o": p["w_o"][l].astype(BF16),
        "norm_ffn": row(p["norm_ffn"][l]),
        "peer_wq": p["peer_wq"][l].astype(BF16),
        "peer_subkeys": p["peer_subkeys"][l].reshape(PEER_HEADS * 2, PEER_NKEYS, PEER_HALF).astype(BF16),
        "peer_u": p["peer_u"][l].astype(BF16), "peer_vt": p["peer_v"][l].T.astype(BF16),
        "norm_final": row(p["norm_final"]),
    }


def _rope_tables(pos):
    half = MLA_ROPE // 2
    inv = 1.0 / (ROPE_THETA ** (np.arange(half, dtype=np.float64) / half))
    ang = pos.astype(np.float64)[:, None] * inv[None, :]
    cos, sin = np.cos(ang), np.sin(ang)
    n = pos.shape[0]
    one = np.ones((n, MLA_NOPE))
    zq = np.zeros((n, HEAD_W - MLA_NOPE - MLA_ROPE))
    cq = np.concatenate([one, cos, cos, zq], axis=1) * MLA_SCALE
    sq = np.concatenate([0.0 * one, sin, sin, zq], axis=1) * MLA_SCALE
    zk = np.zeros((n, LANE - MLA_ROPE))
    ck = np.concatenate([cos, cos, zk], axis=1)
    sk = np.concatenate([sin, sin, zk], axis=1)
    return tuple(t.astype(np.float32) for t in (cq, sq, ck, sk))


def _layer(x, pos, past, mem_k, mem_v, wp, layer, cfg):
    b, s, d = x.shape
    n = b * s
    xf = x.reshape(n, d)
    tabs = _rope_tables(pos)
    if s % cfg["tm"] == 0:
        pos_blocks = s // cfg["tm"]
    else:
        tabs = tuple(np.tile(t, (b, 1)) for t in tabs)
        pos_blocks = n // cfg["tm"]
    ckv, kr, dk, dv, qm, dqb, dkb, dvb, mqb, k_mla, v_mla = _proj_in(xf, tabs, wp, tm=cfg["tm"],
                                                                     pos_blocks=pos_blocks)
    new_rows = (ckv.reshape(b, s, -1), kr.reshape(b, s, -1), dk.reshape(b, s, DIFF_HEADS, DIFF_V),
                dv.reshape(b, s, DIFF_HEADS, DIFF_V))
    r3 = lambda a: a.reshape(b, s, -1)
    if past is None:
        dk_all, dv_all = r3(dkb), r3(dvb)
        kk = s
        q_off = 0
    else:
        p_len = past[0].shape[1]
        kk = p_len + s
        q_off = p_len
        ckv_all = jnp.concatenate([past[0], r3(ckv)], axis=1).reshape(b * kk, -1)
        kr_all = jnp.concatenate([past[1], r3(kr)], axis=1).reshape(b * kk, -1)
        dk_all = jnp.concatenate([past[2].reshape(b, p_len, -1).astype(BF16), r3(dkb)], axis=1)
        dv_all = jnp.concatenate([past[3].reshape(b, p_len, -1).astype(BF16), r3(dvb)], axis=1)
        k_mla, v_mla = _kv_up(ckv_all, kr_all, wp, tm=cfg["tm_kv"])
    tk = lambda name: cfg[name] if past is None else kk
    o_a = _mla_attention(r3(qm), k_mla.reshape(b, kk, -1), v_mla.reshape(b, kk, -1), tq=cfg["tq_mla"],
                         tk=tk("tk_mla"), q_off=q_off)
    lam_init = 0.8 - 0.6 * math.exp(-0.3 * layer)
    o_b = _diff_attention(r3(dqb), dk_all, dv_all, wp, lam_init=lam_init, tq=cfg["tq"], tk=tk("tk"), q_off=q_off)
    o_m = _mem_attention(r3(mqb), mem_k, mem_v, tq=cfg["tq"])
    x1, hn = _merge(xf, o_a.reshape(n, -1), o_b.reshape(n, -1), o_m.reshape(n, -1), wp, tm=cfg["tm"])
    return x1, hn, new_rows


def _peer_and_norm(x1, hn, wp, cfg):
    cnt, e1, rank2, e2 = _route(hn, wp, tt=cfg["tt_route"])
    return _peer(hn, cnt, e1, rank2, e2, x1, wp, tt=cfg["tt"], te=cfg["te"])


_CFG_PROMPT = dict(tm=512, tm_kv=2048, tq=512, tk=512, tq_mla=1024, tk_mla=1024, tt_route=256, tt=512, te=2048)
_CFG_SAMPLE = dict(tm=256, tm_kv=256, tq=32, tk=None, tq_mla=32, tk_mla=None, tt_route=256, tt=256, te=2048)


def kernel(x_prompt, x_sample, cache_mla_ckv, cache_mla_krope, cache_diff_k, cache_diff_v, cache_mem_k, cache_mem_v, mem_prompt, norm_mix, w_in, mla_q_norm, w_uq, mla_kv_norm, w_uk, w_uv, diff_lq1, diff_lk1, diff_lq2, diff_lk2, diff_subln, norm_mem, w_mem_kv, w_br_a, w_br_b, w_br_m, w_gate, b_gate, w_o, norm_ffn, peer_wq, peer_subkeys, peer_u, peer_v, norm_final):
    params = dict(norm_mix=norm_mix, w_in=w_in, mla_q_norm=mla_q_norm, w_uq=w_uq, mla_kv_norm=mla_kv_norm,
                  w_uk=w_uk, w_uv=w_uv, diff_lq1=diff_lq1, diff_lk1=diff_lk1, diff_lq2=diff_lq2,
                  diff_lk2=diff_lk2, diff_subln=diff_subln, norm_mem=norm_mem, w_mem_kv=w_mem_kv,
                  w_br_a=w_br_a, w_br_b=w_br_b, w_br_m=w_br_m, w_gate=w_gate, b_gate=b_gate, w_o=w_o,
                  norm_ffn=norm_ffn, peer_wq=peer_wq, peer_subkeys=peer_subkeys, peer_u=peer_u,
                  peer_v=peer_v, norm_final=norm_final)
    depth = w_in.shape[0]
    assert depth == 1, "the final norm is fused into the last PEER step of a single layer"
    bp, sp, d = x_prompt.shape
    bs, ss, _ = x_sample.shape
    n_mem = mem_prompt.shape[1]
    past_len = cache_mla_ckv.shape[2]
    pos_p = np.arange(sp, dtype=np.int32)
    pos_s = past_len + np.arange(ss, dtype=np.int32)

    l = 0
    wp = _prep_weights(l, params)
    mk, mv, mkb, mvb = _mem_kv(mem_prompt.reshape(bp * n_mem, d), wp, tm=n_mem)
    m3 = lambda a, b: a.reshape(b, n_mem, -1)
    x1p, hnp, rows_p = _layer(x_prompt, pos_p, None, m3(mkb, bp), m3(mvb, bp), wp, l, _CFG_PROMPT)
    past = (cache_mla_ckv[l], cache_mla_krope[l], cache_diff_k[l], cache_diff_v[l])
    x1s, hns, rows_s = _layer(x_sample, pos_s, past, m3(cache_mem_k[l].astype(BF16), bs),
                              m3(cache_mem_v[l].astype(BF16), bs), wp, l, _CFG_SAMPLE)
    y_prompt = _peer_and_norm(x1p, hnp, wp, _CFG_PROMPT).reshape(bp, sp, d)
    y_sample = _peer_and_norm(x1s, hns, wp, _CFG_SAMPLE).reshape(bs, ss, d)
    st = lambda a: a[None]
    mem4 = lambda a: a.reshape(bp, n_mem, MEM_HEADS, MEM_DH)[None]
    return (y_prompt, y_sample,
            st(rows_p[0]), st(rows_p[1]), st(rows_p[2]), st(rows_p[3]),
            mem4(mk), mem4(mv),
            st(rows_s[0]), st(rows_s[1]), st(rows_s[2]), st(rows_s[3]))
```
